```python
import jax
import jax.numpy as jnp
from jax import lax
import numpy as np

D_MODEL = 2048
BATCH = 4
SEQ = 2048
DEPTH = 2

GRID_W = 64
CTX_LEN = 256

NA_HEADS = 16
NA_HEAD_DIM = 64
NA_WIN_R_MAX = 8
NA_WIN_C = 16
NA_W = NA_HEADS * NA_HEAD_DIM

GLA_HEADS = 4
GLA_DK = 128
GLA_DV = 256
GLA_QK = GLA_HEADS * GLA_DK
GLA_V = GLA_HEADS * GLA_DV
GLA_GATE_RANK = 16
GLA_GATE_TAU = 16.0
GLA_CHUNK = 64

RW_HEADS = 16
RW_HEAD = 64
RW_W = RW_HEADS * RW_HEAD
RW_DECAY_RANK = 96
RW_A_RANK = 96
RW_GATE_RANK = 256
RW_SPLITS = (RW_W, RW_W, RW_W, 2 * RW_DECAY_RANK, 2 * RW_A_RANK, RW_GATE_RANK)
RW_IN = sum(RW_SPLITS)
RW_GN_EPS = 64e-5

N_BRANCH = 3
BRANCH_W = 1024
IN_SPLITS = (NA_W, NA_W, NA_W, GLA_QK, GLA_QK, GLA_V, GLA_V, 2 * GLA_GATE_RANK, RW_IN, N_BRANCH * D_MODEL)
D_IN = sum(IN_SPLITS)

N_EXPERTS = 16
N_GROUPS = 4
EXPERTS_PER_GROUP = N_EXPERTS // N_GROUPS
TOP_K = 2
D_EXPERT = 512

ROPE_BASE = 10000.0
EPS = 1e-6
F32 = jnp.float32

kernel_name = 'hybrid_na_gla_rwkv7_grouped_moe_dit'


def split_at(z, sizes):
    return jnp.split(z, np.cumsum(sizes)[:-1].tolist(), axis=-1)


def rms_norm(x, g):
    xf = x.astype(F32)
    y = xf * lax.rsqrt(jnp.mean(xf * xf, axis=-1, keepdims=True) + EPS)
    return (y * g.astype(F32)).astype(x.dtype)


def axial_angles(pos, dim):
    inv = ROPE_BASE ** (-jnp.arange(0, dim, 2, dtype=F32) / dim)
    ang = pos.astype(F32)[:, None] * inv[None, :]
    return jnp.cos(ang), jnp.sin(ang)


def rotate_half(x, cos, sin):
    x1, x2 = jnp.split(x, 2, axis=-1)
    cos = cos[None, :, None, :]
    sin = sin[None, :, None, :]
    return jnp.concatenate([x1 * cos - x2 * sin, x1 * sin + x2 * cos], axis=-1)


def rope_2d(x, rows, cols):
    half = x.shape[-1] // 2
    xr, xc = jnp.split(x.astype(F32), 2, axis=-1)
    xr = rotate_half(xr, *axial_angles(rows, half))
    xc = rotate_half(xc, *axial_angles(cols, half))
    return jnp.concatenate([xr, xc], axis=-1).astype(x.dtype)


def flip(t):
    return jnp.flip(t, axis=1)


def na_mixer(zl, zc, q_norm, k_norm, rpb, with_ctx):
    B, S, _ = zl[0].shape
    scale = NA_HEAD_DIM ** -0.5

    def heads(t):
        return t.reshape(t.shape[0], t.shape[1], NA_HEADS, NA_HEAD_DIM)

    ql = rms_norm(heads(zl[0]), q_norm) * scale
    kl = rms_norm(heads(zl[1]), k_norm)
    vl = heads(zl[2])
    qc = rms_norm(heads(zc[0]), q_norm) * scale
    kc = rms_norm(heads(zc[1]), k_norm)
    vc = heads(zc[2])

    n_rows = S // GRID_W
    win_r = min(NA_WIN_R_MAX, n_rows)
    n_loc = win_r * NA_WIN_C

    def grid(t):
        return t.reshape(B, n_rows, GRID_W, NA_HEADS, NA_HEAD_DIM)

    qg, kg, vg = grid(ql), grid(kl), grid(vl)
    col = np.arange(GRID_W)
    col_start = np.clip(col - NA_WIN_C // 2, 0, GRID_W - NA_WIN_C)
    col_idx = col_start[:, None] + np.arange(NA_WIN_C)[None, :]
    dc_idx = col_idx - col[:, None] + NA_WIN_C - 1
    rpb_c = rpb[:, :, dc_idx]

    def row_block(r):
        rs = jnp.clip(r - win_r // 2, 0, n_rows - win_r)
        q_r = lax.dynamic_index_in_dim(qg, r, axis=1, keepdims=False)
        k_win = lax.dynamic_slice_in_dim(kg, rs, win_r, axis=1)[:, :, col_idx]
        v_win = lax.dynamic_slice_in_dim(vg, rs, win_r, axis=1)[:, :, col_idx]
        dr_idx = rs + jnp.arange(win_r) - r + NA_WIN_R_MAX - 1
        bias = jnp.take(rpb_c, dr_idx, axis=1).transpose(0, 2, 1, 3)
        s_loc = jnp.einsum('bqhd,bnqjhd->bhqnj', q_r, k_win).astype(F32) + bias.astype(F32)[None]
        s_ctx = jnp.einsum('bqhd,blhd->bhql', q_r, kc).astype(F32)
        scores = jnp.concatenate([s_loc.reshape(B, NA_HEADS, GRID_W, n_loc), s_ctx], axis=-1)
        p = jax.nn.softmax(scores, axis=-1).astype(vl.dtype)
        p_loc = p[..., :n_loc].reshape(B, NA_HEADS, GRID_W, win_r, NA_WIN_C)
        return (jnp.einsum('bhqnj,bnqjhd->bqhd', p_loc, v_win)
                + jnp.einsum('bhql,blhd->bqhd', p[..., n_loc:], vc))

    o = lax.map(row_block, jnp.arange(n_rows))
    y = o.transpose(1, 0, 2, 3, 4).reshape(B, S, NA_W)
    yc = None
    if with_ctx:
        pc = jax.nn.softmax(jnp.einsum('bqhd,blhd->bhql', qc, kc).astype(F32), axis=-1).astype(vc.dtype)
        yc = jnp.einsum('bhql,blhd->bqhd', pc, vc).reshape(B, -1, NA_W)
    return y, yc


def gla_scan(q, k, v, log_a, s0):
    q, k, v, log_a = (t.astype(F32) for t in (q, k, v, log_a))
    B, T, H, dk = q.shape
    dv = v.shape[-1]
    C = GLA_CHUNK
    N = T // C

    def chunks(t):
        return t.reshape(B, N, C, H, t.shape[-1])

    q, k, v, log_a = chunks(q), chunks(k), chunks(v), chunks(log_a)
    b = jnp.cumsum(log_a, axis=2)
    b_last = b[:, :, -1:]
    b_ref = b[:, :, C // 2:C // 2 + 1]
    qi = q * jnp.exp(b - b_ref)
    kj = k * jnp.exp(b_ref - b)
    att = jnp.einsum('bnihd,bnjhd->bnhij', qi, kj)
    att = jnp.where(jnp.tril(jnp.ones((C, C), dtype=bool)), att, 0.0)
    y = jnp.einsum('bnhij,bnjhv->bnihv', att, v)
    kv = jnp.einsum('bnjhd,bnjhv->bnhdv', k * jnp.exp(b_last - b), v)
    decay = jnp.exp(b_last[:, :, 0])

    def step(s, inp):
        dec, kv_n = inp
        return dec[..., None] * s + kv_n, s

    s_fin, s_prev = lax.scan(step, s0, (jnp.moveaxis(decay, 1, 0), jnp.moveaxis(kv, 1, 0)))
    s_prev = jnp.moveaxis(s_prev, 0, 1)
    y = y + jnp.einsum('bnihd,bnhdv->bnihv', q * jnp.exp(b), s_prev)
    return y.reshape(B, T, H, dv), s_fin


def gla_mixer(zl, zc, rows, cols, gate_w2, gate_b, norm_g, with_ctx):
    def prep(z, pos):
        q, k, v, r, gd = z
        B, T, _ = q.shape
        q = q.reshape(B, T, GLA_HEADS, GLA_DK) * GLA_DK ** -0.5
        k = k.reshape(B, T, GLA_HEADS, GLA_DK)
        if pos is not None:
            q = rope_2d(q, pos[0], pos[1])
            k = rope_2d(k, pos[0], pos[1])
        v = v.reshape(B, T, GLA_HEADS, GLA_DV)
        gds = jnp.split(gd, 2, axis=-1)
        log_a = []
        for d in range(2):
            pre = (gds[d] @ gate_w2[d] + gate_b[d]).astype(F32)
            log_a.append((jax.nn.log_sigmoid(pre) / GLA_GATE_TAU).reshape(B, T, GLA_HEADS, GLA_DK))
        return q, k, v, r, log_a

    ql, kl, vl, rl, lal = prep(zl, (rows, cols))
    qc, kc, vc, rc, lac = prep(zc, None)
    B = ql.shape[0]
    s0 = jnp.zeros((B, GLA_HEADS, GLA_DK, GLA_DV), F32)
    yc_f, sc_f = gla_scan(qc, kc, vc, lac[0], s0)
    yc_b, sc_b = gla_scan(flip(qc), flip(kc), flip(vc), flip(lac[1]), s0)
    yl_f, _ = gla_scan(ql, kl, vl, lal[0], sc_f)
    yl_b, _ = gla_scan(flip(ql), flip(kl), flip(vl), flip(lal[1]), sc_b)

    def finish(y_f, y_b_rev, r):
        y = rms_norm(y_f + flip(y_b_rev), norm_g)
        return (y.reshape(r.shape) * jax.nn.silu(r.astype(F32))).astype(r.dtype)

    y = finish(yl_f, yl_b, rl)
    yc = finish(yc_f, yc_b, rc) if with_ctx else None
    return y, yc


def token_shift(z, mu):
    prev = jnp.pad(z[:, :-1], ((0, 0), (1, 0), (0, 0)))
    nxt = jnp.pad(z[:, 1:], ((0, 0), (0, 1), (0, 0)))
    return z + mu * (0.5 * (prev + nxt) - z)


def rwkv_scan(r, w, k, v, kk, a, s0):
    def step(s, inp):
        r_t, w_t, k_t, v_t, kk_t, a_t = inp
        sa = jnp.einsum('bhvk,bhk->bhv', s, kk_t)
        s = (s * w_t[:, :, None, :] - sa[..., None] * (kk_t * a_t)[:, :, None, :]
             + v_t[..., None] * k_t[:, :, None, :])
        return s, jnp.einsum('bhvk,bhk->bhv', s, r_t)

    xs = tuple(jnp.moveaxis(t, 1, 0) for t in (r, w, k, v, kk, a))
    s_fin, ys = lax.scan(step, s0, xs)
    return jnp.moveaxis(ys, 0, 1), s_fin


def rwkv_prep(z, mu, w0, w2, a0, a2, g2, k_k, k_a):
    B, T, _ = z.shape
    z = token_shift(z, mu)
    r, k, v, wd, ad, gd = split_at(z, RW_SPLITS)

    def heads(t):
        return t.reshape(B, T, RW_HEADS, RW_HEAD).astype(F32)

    kk = heads(k * k_k)
    kk = kk * lax.rsqrt(jnp.sum(kk * kk, axis=-1, keepdims=True) + EPS)
    wds = jnp.split(wd, 2, axis=-1)
    ads = jnp.split(ad, 2, axis=-1)
    per_dir = []
    for d in range(2):
        w_log = (-jax.nn.softplus(-(w0[d] + jnp.tanh(wds[d]) @ w2[d])) - 0.5).astype(F32)
        decay = jnp.exp(-jnp.exp(w_log))
        a = jax.nn.sigmoid(a0[d] + ads[d] @ a2[d])
        k_d = k * (1.0 + (a - 1.0) * k_a)
        per_dir.append((heads(decay), heads(k_d), heads(a)))
    g = jax.nn.sigmoid(gd) @ g2
    return heads(r), heads(k), heads(v), kk, per_dir, g


def rwkv_mixer(zl, zc, mu, w0, w2, a0, a2, g2, k_k, k_a, r_k, gn_g, gn_b, with_ctx):
    rl, kl, vl, kkl, dl, gl = rwkv_prep(zl, mu, w0, w2, a0, a2, g2, k_k, k_a)
    rc, kc, vc, kkc, dcx, gc = rwkv_prep(zc, mu, w0, w2, a0, a2, g2, k_k, k_a)
    B = rl.shape[0]
    s0 = jnp.zeros((B, RW_HEADS, RW_HEAD, RW_HEAD), F32)
    yc_f, sc_f = rwkv_scan(rc, dcx[0][0], dcx[0][1], vc, kkc, dcx[0][2], s0)
    yc_b, sc_b = rwkv_scan(flip(rc), flip(dcx[1][0]), flip(dcx[1][1]), flip(vc), flip(kkc), flip(dcx[1][2]), s0)
    yl_f, _ = rwkv_scan(rl, dl[0][0], dl[0][1], vl, kkl, dl[0][2], sc_f)
    yl_b, _ = rwkv_scan(flip(rl), flip(dl[1][0]), flip(dl[1][1]), flip(vl), flip(kkl), flip(dl[1][2]), sc_b)
    gamma = gn_g.astype(F32).reshape(RW_HEADS, RW_HEAD)
    beta = gn_b.astype(F32).reshape(RW_HEADS, RW_HEAD)

    def finish(y_f, y_b_rev, r, k, v, g):
        y = y_f + flip(y_b_rev)
        mean = jnp.mean(y, axis=-1, keepdims=True)
        var = jnp.mean(jnp.square(y - mean), axis=-1, keepdims=True)
        y = (y - mean) * lax.rsqrt(var + RW_GN_EPS) * gamma + beta
        y = y + jnp.sum(r * k * r_k.astype(F32), axis=-1, keepdims=True) * v
        return (y.reshape(g.shape) * g.astype(F32)).astype(g.dtype)

    y = finish(yl_f, yl_b, rl, kl, vl, gl)
    yc = finish(yc_f, yc_b, rc, kc, vc, gc) if with_ctx else None
    return y, yc


def merge_branches(ys, gates, w_branch, w_out):
    y = jnp.stack(ys, axis=-2)
    proj = jnp.einsum('btiw,iwd->btid', y, w_branch)
    g = jax.nn.sigmoid(gates.reshape(gates.shape[:-1] + (N_BRANCH, D_MODEL)))
    return jnp.sum(g * proj, axis=-2) @ w_out


def token_mixers(h, hc, rows, cols, p, with_ctx):
    zl = split_at(h @ p['w_in'], IN_SPLITS)
    zc = split_at(hc @ p['w_in'], IN_SPLITS)
    y_na, yc_na = na_mixer(zl[0:3], zc[0:3], p['na_q_norm'], p['na_k_norm'], p['na_rpb'], with_ctx)
    y_gla, yc_gla = gla_mixer(zl[3:8], zc[3:8], rows, cols, p['gla_gate_w2'], p['gla_gate_b'],
                              p['gla_norm_g'], with_ctx)
    y_rw, yc_rw = rwkv_mixer(zl[8], zc[8], p['rw_mu'], p['rw_w0'], p['rw_w2'], p['rw_a0'], p['rw_a2'],
                             p['rw_g2'], p['rw_k_k'], p['rw_k_a'], p['rw_r_k'], p['rw_gn_g'],
                             p['rw_gn_b'], with_ctx)
    y = merge_branches((y_na, y_gla, y_rw), zl[9], p['w_branch'], p['w_out'])
    yc = merge_branches((yc_na, yc_gla, yc_rw), zc[9], p['w_branch'], p['w_out']) if with_ctx else None
    return y, yc


def moe_ffn(h, router_w, router_bias, w_gate, w_up, w_down):
    shp = h.shape
    t = h.reshape(-1, shp[-1])
    s = jax.nn.sigmoid(t.astype(F32) @ router_w.astype(F32))
    biased = s + router_bias.astype(F32)
    grp = biased.reshape(-1, N_GROUPS, EXPERTS_PER_GROUP)
    grp_score = jnp.sum(lax.top_k(grp, TOP_K)[0], axis=-1)
    g_sel = jnp.argmax(grp_score, axis=-1)
    in_group = (jnp.arange(N_EXPERTS) // EXPERTS_PER_GROUP)[None, :] == g_sel[:, None]
    _, idx = lax.top_k(jnp.where(in_group, biased, -jnp.inf), TOP_K)
    w_sel = jnp.take_along_axis(s, idx, axis=-1)
    w_sel = w_sel / jnp.sum(w_sel, axis=-1, keepdims=True)
    comb = jnp.sum(jax.nn.one_hot(idx, N_EXPERTS, dtype=F32) * w_sel[..., None], axis=1)
    act = jax.nn.silu(jnp.einsum('td,edf->tef', t, w_gate)) * jnp.einsum('td,edf->tef', t, w_up)
    y = jnp.einsum('tef,efd->td', act * comb[..., None].astype(act.dtype), w_down)
    return y.reshape(shp)


def setup_inputs(seed: int = 0) -> dict:
    key = jax.random.key(seed)
    ks = iter(jax.random.split(key, 48))
    L, D = DEPTH, D_MODEL

    def nrm(shape, scale):
        return scale * jax.random.normal(next(ks), shape, F32)

    def gain(shape):
        return 1.0 + nrm(shape, 0.02)

    return {
        'x': nrm((BATCH, SEQ, D), 1.0),
        'c': nrm((BATCH, D), 1.0),
        'ctx': nrm((BATCH, CTX_LEN, D), 1.0),
        'c_ctx': nrm((D,), 1.0),
        'ada_w': nrm((L, D, 6 * D), 0.5 * D ** -0.5),
        'ada_b': nrm((L, 6 * D), 0.02),
        'norm1_g': gain((L, D)),
        'norm2_g': gain((L, D)),
        'w_in': nrm((L, D, D_IN), D ** -0.5),
        'na_q_norm': gain((L, NA_HEAD_DIM)),
        'na_k_norm': gain((L, NA_HEAD_DIM)),
        'na_rpb': nrm((L, NA_HEADS, 2 * NA_WIN_R_MAX - 1, 2 * NA_WIN_C - 1), 0.1),
        'gla_gate_w2': nrm((L, 2, GLA_GATE_RANK, GLA_QK), GLA_GATE_RANK ** -0.5),
        'gla_gate_b': nrm((L, 2, GLA_QK), 0.1),
        'gla_norm_g': gain((L, GLA_DV)),
        'rw_mu': jax.random.uniform(next(ks), (L, RW_IN), F32),
        'rw_w0': jax.random.uniform(next(ks), (L, 2, RW_W), F32, -6.0, -1.0),
        'rw_w2': nrm((L, 2, RW_DECAY_RANK, RW_W), 0.1 * RW_DECAY_RANK ** -0.5),
        'rw_a0': nrm((L, 2, RW_W), 0.1),
        'rw_a2': nrm((L, 2, RW_A_RANK, RW_W), 0.3 * RW_A_RANK ** -0.5),
        'rw_g2': nrm((L, RW_GATE_RANK, RW_W), RW_GATE_RANK ** -0.5),
        'rw_k_k': 0.85 + nrm((L, RW_W), 0.05),
        'rw_k_a': 1.0 + nrm((L, RW_W), 0.05),
        'rw_r_k': nrm((L, RW_HEADS, RW_HEAD), 0.1),
        'rw_gn_g': gain((L, RW_W)),
        'rw_gn_b': nrm((L, RW_W), 0.02),
        'w_branch': nrm((L, N_BRANCH, BRANCH_W, D), BRANCH_W ** -0.5),
        'w_out': nrm((L, D, D), D ** -0.5),
        'router_w': nrm((D, N_EXPERTS), D ** -0.5),
        'router_bias': nrm((N_EXPERTS,), 0.01),
        'moe_w_gate': nrm((L, N_EXPERTS, D, D_EXPERT), D ** -0.5),
        'moe_w_up': nrm((L, N_EXPERTS, D, D_EXPERT), D ** -0.5),
        'moe_w_down': nrm((L, N_EXPERTS, D_EXPERT, D), D_EXPERT ** -0.5),
    }


def reference(x, c, ctx, c_ctx, ada_w, ada_b, norm1_g, norm2_g, w_in, na_q_norm, na_k_norm, na_rpb,
              gla_gate_w2, gla_gate_b, gla_norm_g, rw_mu, rw_w0, rw_w2, rw_a0, rw_a2, rw_g2,
              rw_k_k, rw_k_a, rw_r_k, rw_gn_g, rw_gn_b, w_branch, w_out, router_w, router_bias,
              moe_w_gate, moe_w_up, moe_w_down):
    S = x.shape[1]
    t = jnp.arange(S, dtype=jnp.int32)
    rows = t // GRID_W
    cols = t % GRID_W
    for l in range(DEPTH):
        with_ctx = l < DEPTH - 1
        p = {
            'w_in': w_in[l], 'na_q_norm': na_q_norm[l], 'na_k_norm': na_k_norm[l], 'na_rpb': na_rpb[l],
            'gla_gate_w2': gla_gate_w2[l], 'gla_gate_b': gla_gate_b[l], 'gla_norm_g': gla_norm_g[l],
            'rw_mu': rw_mu[l], 'rw_w0': rw_w0[l], 'rw_w2': rw_w2[l], 'rw_a0': rw_a0[l], 'rw_a2': rw_a2[l],
            'rw_g2': rw_g2[l], 'rw_k_k': rw_k_k[l], 'rw_k_a': rw_k_a[l], 'rw_r_k': rw_r_k[l],
            'rw_gn_g': rw_gn_g[l], 'rw_gn_b': rw_gn_b[l], 'w_branch': w_branch[l], 'w_out': w_out[l],
        }
        mod = jax.nn.silu(c) @ ada_w[l] + ada_b[l]
        mod_c = jax.nn.silu(c_ctx) @ ada_w[l] + ada_b[l]
        sh1, sc1, g1, sh2, sc2, g2 = jnp.split(mod[:, None, :], 6, axis=-1)
        csh1, csc1, cg1, csh2, csc2, cg2 = jnp.split(mod_c, 6)

        h = rms_norm(x, norm1_g[l]) * (1.0 + sc1) + sh1
        hc = rms_norm(ctx, norm1_g[l]) * (1.0 + csc1) + csh1
        y, yc = token_mixers(h, hc, rows, cols, p, with_ctx)
        x = x + g1 * y
        h2 = rms_norm(x, norm2_g[l]) * (1.0 + sc2) + sh2
        x = x + g2 * moe_ffn(h2, router_w, router_bias, moe_w_gate[l], moe_w_up[l], moe_w_down[l])

        if with_ctx:
            ctx = ctx + cg1 * yc
            hc2 = rms_norm(ctx, norm2_g[l]) * (1.0 + csc2) + csh2
            ctx = ctx + cg2 * moe_ffn(hc2, router_w, router_bias, moe_w_gate[l], moe_w_up[l], moe_w_down[l])
    return x
```

```python
import functools

import jax
import jax.numpy as jnp
from jax import lax
import numpy as np
from jax.experimental import pallas as pl
from jax.experimental.pallas import tpu as pltpu

D_MODEL = 2048
DEPTH = 2
GRID_W = 64

NA_HEADS = 16
NA_HEAD_DIM = 64
NA_WIN_R_MAX = 8
NA_WIN_C = 16
NA_W = NA_HEADS * NA_HEAD_DIM

GLA_HEADS = 4
GLA_DK = 128
GLA_DV = 256
GLA_QK = GLA_HEADS * GLA_DK
GLA_V = GLA_HEADS * GLA_DV
GLA_GATE_RANK = 16
GLA_GATE_TAU = 16.0
GLA_CHUNK = 64

RW_HEADS = 16
RW_HEAD = 64
RW_W = RW_HEADS * RW_HEAD
RW_DECAY_RANK = 96
RW_A_RANK = 96
RW_GATE_RANK = 256
RW_SPLITS = (RW_W, RW_W, RW_W, 2 * RW_DECAY_RANK, 2 * RW_A_RANK, RW_GATE_RANK)
RW_IN = sum(RW_SPLITS)
RW_GN_EPS = 64e-5

N_BRANCH = 3
BRANCH_W = 1024
IN_SPLITS = (NA_W, NA_W, NA_W, GLA_QK, GLA_QK, GLA_V, GLA_V, 2 * GLA_GATE_RANK, RW_IN, N_BRANCH * D_MODEL)
D_IN = sum(IN_SPLITS)

N_EXPERTS = 16
N_GROUPS = 4
EXPERTS_PER_GROUP = N_EXPERTS // N_GROUPS
TOP_K = 2
D_EXPERT = 512

ROPE_BASE = 10000.0
EPS = 1e-6
F32 = jnp.float32
BF16 = jnp.bfloat16

V7X_VMEM_LIMIT_BYTES = 48 * 1024 * 1024
LANES = 128


def _mm_kernel(a_ref, b_ref, o_ref):
    o_ref[...] = jnp.dot(a_ref[...], b_ref[...], preferred_element_type=F32).astype(o_ref.dtype)


def matmul(a, b, *, tm=512, tn=512, out_dtype=F32):
    M, K = a.shape
    _, N = b.shape
    assert M % tm == 0 and N % tn == 0, (M, N, tm, tn)
    return pl.pallas_call(
        _mm_kernel,
        grid=(N // tn, M // tm),
        in_specs=[pl.BlockSpec((tm, K), lambda n, m: (m, 0)),
                  pl.BlockSpec((K, tn), lambda n, m: (0, n))],
        out_specs=pl.BlockSpec((tm, tn), lambda n, m: (m, n)),
        out_shape=jax.ShapeDtypeStruct((M, N), out_dtype),
        compiler_params=pltpu.CompilerParams(
            dimension_semantics=("arbitrary", "arbitrary"),
            vmem_limit_bytes=V7X_VMEM_LIMIT_BYTES),
        name="matmul",
    )(a.astype(BF16), b.astype(BF16))


def _moe_kernel(x_ref, comb_ref, wg_ref, wu_ref, wd_ref, o_ref, acc_ref):
    e = pl.program_id(1)

    @pl.when(e == 0)
    def _():
        acc_ref[...] = jnp.zeros_like(acc_ref)

    x = x_ref[...]
    g = jnp.dot(x, wg_ref[0], preferred_element_type=F32)
    u = jnp.dot(x, wu_ref[0], preferred_element_type=F32)
    comb = comb_ref[...]
    lane = lax.broadcasted_iota(jnp.int32, comb.shape, 1)
    ce = jnp.sum(jnp.where(lane == e, comb, 0.0), axis=1, keepdims=True)
    act = (g * jax.nn.sigmoid(g)) * u * ce
    acc_ref[...] += jnp.dot(act.astype(BF16), wd_ref[0], preferred_element_type=F32)

    @pl.when(e == N_EXPERTS - 1)
    def _():
        o_ref[...] = acc_ref[...]


def moe_experts(t, comb, w_gate, w_up, w_down, *, tm=512):
    M, D = t.shape
    assert M % tm == 0
    return pl.pallas_call(
        _moe_kernel,
        grid=(M // tm, N_EXPERTS),
        in_specs=[pl.BlockSpec((tm, D), lambda m, e: (m, 0)),
                  pl.BlockSpec((tm, N_EXPERTS), lambda m, e: (m, 0)),
                  pl.BlockSpec((1, D, D_EXPERT), lambda m, e: (e, 0, 0)),
                  pl.BlockSpec((1, D, D_EXPERT), lambda m, e: (e, 0, 0)),
                  pl.BlockSpec((1, D_EXPERT, D), lambda m, e: (e, 0, 0))],
        out_specs=pl.BlockSpec((tm, D), lambda m, e: (m, 0)),
        out_shape=jax.ShapeDtypeStruct((M, D), F32),
        scratch_shapes=[pltpu.VMEM((tm, D), F32)],
        compiler_params=pltpu.CompilerParams(
            dimension_semantics=("arbitrary", "arbitrary"),
            vmem_limit_bytes=V7X_VMEM_LIMIT_BYTES),
        name="moe_experts",
    )(t.astype(BF16), comb, w_gate.astype(BF16), w_up.astype(BF16), w_down.astype(BF16))


def split_at(z, sizes):
    return jnp.split(z, np.cumsum(sizes)[:-1].tolist(), axis=-1)


def rms_norm(x, g):
    xf = x.astype(F32)
    y = xf * lax.rsqrt(jnp.mean(xf * xf, axis=-1, keepdims=True) + EPS)
    return (y * g.astype(F32)).astype(x.dtype)


def axial_angles(pos, dim):
    inv = ROPE_BASE ** (-jnp.arange(0, dim, 2, dtype=F32) / dim)
    ang = pos.astype(F32)[:, None] * inv[None, :]
    return jnp.cos(ang), jnp.sin(ang)


def rotate_half(x, cos, sin):
    x1, x2 = jnp.split(x, 2, axis=-1)
    cos = cos[None, :, None, :]
    sin = sin[None, :, None, :]
    return jnp.concatenate([x1 * cos - x2 * sin, x1 * sin + x2 * cos], axis=-1)


def rope_2d(x, rows, cols):
    half = x.shape[-1] // 2
    xr, xc = jnp.split(x.astype(F32), 2, axis=-1)
    xr = rotate_half(xr, *axial_angles(rows, half))
    xc = rotate_half(xc, *axial_angles(cols, half))
    return jnp.concatenate([xr, xc], axis=-1).astype(x.dtype)


def flip(t):
    return jnp.flip(t, axis=1)


def na_mixer(zl, zc, q_norm, k_norm, rpb, with_ctx):
    B, S, _ = zl[0].shape
    scale = NA_HEAD_DIM ** -0.5

    def heads(t):
        return t.reshape(t.shape[0], t.shape[1], NA_HEADS, NA_HEAD_DIM)

    ql = rms_norm(heads(zl[0]), q_norm) * scale
    kl = rms_norm(heads(zl[1]), k_norm)
    vl = heads(zl[2])
    qc = rms_norm(heads(zc[0]), q_norm) * scale
    kc = rms_norm(heads(zc[1]), k_norm)
    vc = heads(zc[2])

    n_rows = S // GRID_W
    win_r = min(NA_WIN_R_MAX, n_rows)
    n_loc = win_r * NA_WIN_C

    def grid(t):
        return t.reshape(B, n_rows, GRID_W, NA_HEADS, NA_HEAD_DIM)

    qg, kg, vg = grid(ql), grid(kl), grid(vl)
    col = np.arange(GRID_W)
    col_start = np.clip(col - NA_WIN_C // 2, 0, GRID_W - NA_WIN_C)
    col_idx = col_start[:, None] + np.arange(NA_WIN_C)[None, :]
    dc_idx = col_idx - col[:, None] + NA_WIN_C - 1
    rpb_c = rpb[:, :, dc_idx]

    def row_block(r):
        rs = jnp.clip(r - win_r // 2, 0, n_rows - win_r)
        q_r = lax.dynamic_index_in_dim(qg, r, axis=1, keepdims=False)
        k_win = lax.dynamic_slice_in_dim(kg, rs, win_r, axis=1)[:, :, col_idx]
        v_win = lax.dynamic_slice_in_dim(vg, rs, win_r, axis=1)[:, :, col_idx]
        dr_idx = rs + jnp.arange(win_r) - r + NA_WIN_R_MAX - 1
        bias = jnp.take(rpb_c, dr_idx, axis=1).transpose(0, 2, 1, 3)
        s_loc = jnp.einsum('bqhd,bnqjhd->bhqnj', q_r, k_win).astype(F32) + bias.astype(F32)[None]
        s_ctx = jnp.einsum('bqhd,blhd->bhql', q_r, kc).astype(F32)
        scores = jnp.concatenate([s_loc.reshape(B, NA_HEADS, GRID_W, n_loc), s_ctx], axis=-1)
        p = jax.nn.softmax(scores, axis=-1).astype(vl.dtype)
        p_loc = p[..., :n_loc].reshape(B, NA_HEADS, GRID_W, win_r, NA_WIN_C)
        return (jnp.einsum('bhqnj,bnqjhd->bqhd', p_loc, v_win)
                + jnp.einsum('bhql,blhd->bqhd', p[..., n_loc:], vc))

    o = lax.map(row_block, jnp.arange(n_rows))
    y = o.transpose(1, 0, 2, 3, 4).reshape(B, S, NA_W)
    yc = None
    if with_ctx:
        pc = jax.nn.softmax(jnp.einsum('bqhd,blhd->bhql', qc, kc).astype(F32), axis=-1).astype(vc.dtype)
        yc = jnp.einsum('bhql,blhd->bqhd', pc, vc).reshape(B, -1, NA_W)
    return y, yc


def gla_scan(q, k, v, log_a, s0):
    q, k, v, log_a = (t.astype(F32) for t in (q, k, v, log_a))
    B, T, H, dk = q.shape
    dv = v.shape[-1]
    C = GLA_CHUNK
    N = T // C

    def chunks(t):
        return t.reshape(B, N, C, H, t.shape[-1])

    q, k, v, log_a = chunks(q), chunks(k), chunks(v), chunks(log_a)
    b = jnp.cumsum(log_a, axis=2)
    b_last = b[:, :, -1:]
    b_ref = b[:, :, C // 2:C // 2 + 1]
    qi = q * jnp.exp(b - b_ref)
    kj = k * jnp.exp(b_ref - b)
    att = jnp.einsum('bnihd,bnjhd->bnhij', qi, kj)
    att = jnp.where(jnp.tril(jnp.ones((C, C), dtype=bool)), att, 0.0)
    y = jnp.einsum('bnhij,bnjhv->bnihv', att, v)
    kv = jnp.einsum('bnjhd,bnjhv->bnhdv', k * jnp.exp(b_last - b), v)
    decay = jnp.exp(b_last[:, :, 0])

    def step(s, inp):
        dec, kv_n = inp
        return dec[..., None] * s + kv_n, s

    s_fin, s_prev = lax.scan(step, s0, (jnp.moveaxis(decay, 1, 0), jnp.moveaxis(kv, 1, 0)))
    s_prev = jnp.moveaxis(s_prev, 0, 1)
    y = y + jnp.einsum('bnihd,bnhdv->bnihv', q * jnp.exp(b), s_prev)
    return y.reshape(B, T, H, dv), s_fin


def gla_mixer(zl, zc, rows, cols, gate_w2, gate_b, norm_g, with_ctx):
    def prep(z, pos):
        q, k, v, r, gd = z
        B, T, _ = q.shape
        q = q.reshape(B, T, GLA_HEADS, GLA_DK) * GLA_DK ** -0.5
        k = k.reshape(B, T, GLA_HEADS, GLA_DK)
        if pos is not None:
            q = rope_2d(q, pos[0], pos[1])
            k = rope_2d(k, pos[0], pos[1])
        v = v.reshape(B, T, GLA_HEADS, GLA_DV)
        gds = jnp.split(gd, 2, axis=-1)
        log_a = []
        for d in range(2):
            pre = (gds[d] @ gate_w2[d] + gate_b[d]).astype(F32)
            log_a.append((jax.nn.log_sigmoid(pre) / GLA_GATE_TAU).reshape(B, T, GLA_HEADS, GLA_DK))
        return q, k, v, r, log_a

    ql, kl, vl, rl, lal = prep(zl, (rows, cols))
    qc, kc, vc, rc, lac = prep(zc, None)
    B = ql.shape[0]
    s0 = jnp.zeros((B, GLA_HEADS, GLA_DK, GLA_DV), F32)
    yc_f, sc_f = gla_scan(qc, kc, vc, lac[0], s0)
    yc_b, sc_b = gla_scan(flip(qc), flip(kc), flip(vc), flip(lac[1]), s0)
    yl_f, _ = gla_scan(ql, kl, vl, lal[0], sc_f)
    yl_b, _ = gla_scan(flip(ql), flip(kl), flip(vl), flip(lal[1]), sc_b)

    def finish(y_f, y_b_rev, r):
        y = rms_norm(y_f + flip(y_b_rev), norm_g)
        return (y.reshape(r.shape) * jax.nn.silu(r.astype(F32))).astype(r.dtype)

    y = finish(yl_f, yl_b, rl)
    yc = finish(yc_f, yc_b, rc) if with_ctx else None
    return y, yc


def token_shift(z, mu):
    prev = jnp.pad(z[:, :-1], ((0, 0), (1, 0), (0, 0)))
    nxt = jnp.pad(z[:, 1:], ((0, 0), (0, 1), (0, 0)))
    return z + mu * (0.5 * (prev + nxt) - z)


def rwkv_scan(r, w, k, v, kk, a, s0):
    def step(s, inp):
        r_t, w_t, k_t, v_t, kk_t, a_t = inp
        sa = jnp.einsum('bhvk,bhk->bhv', s, kk_t)
        s = (s * w_t[:, :, None, :] - sa[..., None] * (kk_t * a_t)[:, :, None, :]
             + v_t[..., None] * k_t[:, :, None, :])
        return s, jnp.einsum('bhvk,bhk->bhv', s, r_t)

    xs = tuple(jnp.moveaxis(t, 1, 0) for t in (r, w, k, v, kk, a))
    s_fin, ys = lax.scan(step, s0, xs)
    return jnp.moveaxis(ys, 0, 1), s_fin


def rwkv_prep(z, mu, w0, w2, a0, a2, g2, k_k, k_a):
    B, T, _ = z.shape
    z = token_shift(z, mu)
    r, k, v, wd, ad, gd = split_at(z, RW_SPLITS)

    def heads(t):
        return t.reshape(B, T, RW_HEADS, RW_HEAD).astype(F32)

    kk = heads(k * k_k)
    kk = kk * lax.rsqrt(jnp.sum(kk * kk, axis=-1, keepdims=True) + EPS)
    wds = jnp.split(wd, 2, axis=-1)
    ads = jnp.split(ad, 2, axis=-1)
    per_dir = []
    for d in range(2):
        w_log = (-jax.nn.softplus(-(w0[d] + jnp.tanh(wds[d]) @ w2[d])) - 0.5).astype(F32)
        decay = jnp.exp(-jnp.exp(w_log))
        a = jax.nn.sigmoid(a0[d] + ads[d] @ a2[d])
        k_d = k * (1.0 + (a - 1.0) * k_a)
        per_dir.append((heads(decay), heads(k_d), heads(a)))
    g = jax.nn.sigmoid(gd) @ g2
    return heads(r), heads(k), heads(v), kk, per_dir, g


def rwkv_mixer(zl, zc, mu, w0, w2, a0, a2, g2, k_k, k_a, r_k, gn_g, gn_b, with_ctx):
    rl, kl, vl, kkl, dl, gl = rwkv_prep(zl, mu, w0, w2, a0, a2, g2, k_k, k_a)
    rc, kc, vc, kkc, dcx, gc = rwkv_prep(zc, mu, w0, w2, a0, a2, g2, k_k, k_a)
    B = rl.shape[0]
    s0 = jnp.zeros((B, RW_HEADS, RW_HEAD, RW_HEAD), F32)
    yc_f, sc_f = rwkv_scan(rc, dcx[0][0], dcx[0][1], vc, kkc, dcx[0][2], s0)
    yc_b, sc_b = rwkv_scan(flip(rc), flip(dcx[1][0]), flip(dcx[1][1]), flip(vc), flip(kkc), flip(dcx[1][2]), s0)
    yl_f, _ = rwkv_scan(rl, dl[0][0], dl[0][1], vl, kkl, dl[0][2], sc_f)
    yl_b, _ = rwkv_scan(flip(rl), flip(dl[1][0]), flip(dl[1][1]), flip(vl), flip(kkl), flip(dl[1][2]), sc_b)
    gamma = gn_g.astype(F32).reshape(RW_HEADS, RW_HEAD)
    beta = gn_b.astype(F32).reshape(RW_HEADS, RW_HEAD)

    def finish(y_f, y_b_rev, r, k, v, g):
        y = y_f + flip(y_b_rev)
        mean = jnp.mean(y, axis=-1, keepdims=True)
        var = jnp.mean(jnp.square(y - mean), axis=-1, keepdims=True)
        y = (y - mean) * lax.rsqrt(var + RW_GN_EPS) * gamma + beta
        y = y + jnp.sum(r * k * r_k.astype(F32), axis=-1, keepdims=True) * v
        return (y.reshape(g.shape) * g.astype(F32)).astype(g.dtype)

    y = finish(yl_f, yl_b, rl, kl, vl, gl)
    yc = finish(yc_f, yc_b, rc, kc, vc, gc) if with_ctx else None
    return y, yc


def merge_branches(ys, gates, w_branch, w_out):
    shp = ys[0].shape[:-1]
    g = jax.nn.sigmoid(gates.reshape(gates.shape[:-1] + (N_BRANCH, D_MODEL)))
    acc = 0.0
    for i in range(N_BRANCH):
        proj = matmul(ys[i].reshape(-1, BRANCH_W), w_branch[i]).reshape(shp + (D_MODEL,))
        acc = acc + g[..., i, :] * proj
    return matmul(acc.reshape(-1, D_MODEL), w_out).reshape(shp + (D_MODEL,))


def pad_cols(w, mult):
    n = w.shape[-1]
    pad = (-n) % mult
    return jnp.pad(w, ((0, 0), (0, pad))) if pad else w


def token_mixers(h, hc, rows, cols, p, with_ctx):
    B, S, D = h.shape
    Lc = hc.shape[1]
    h_all = jnp.concatenate([h.reshape(-1, D), hc.reshape(-1, D)], axis=0)
    z_all = matmul(h_all, pad_cols(p['w_in'], 768), tn=768)[:, :D_IN]
    zl = split_at(z_all[:B * S].reshape(B, S, D_IN), IN_SPLITS)
    zc = split_at(z_all[B * S:].reshape(B, Lc, D_IN), IN_SPLITS)
    y_na, yc_na = na_mixer(zl[0:3], zc[0:3], p['na_q_norm'], p['na_k_norm'], p['na_rpb'], with_ctx)
    y_gla, yc_gla = gla_mixer(zl[3:8], zc[3:8], rows, cols, p['gla_gate_w2'], p['gla_gate_b'],
                              p['gla_norm_g'], with_ctx)
    y_rw, yc_rw = rwkv_mixer(zl[8], zc[8], p['rw_mu'], p['rw_w0'], p['rw_w2'], p['rw_a0'], p['rw_a2'],
                             p['rw_g2'], p['rw_k_k'], p['rw_k_a'], p['rw_r_k'], p['rw_gn_g'],
                             p['rw_gn_b'], with_ctx)
    y = merge_branches((y_na, y_gla, y_rw), zl[9], p['w_branch'], p['w_out'])
    yc = merge_branches((yc_na, yc_gla, yc_rw), zc[9], p['w_branch'], p['w_out']) if with_ctx else None
    return y, yc


def moe_ffn(h, router_w, router_bias, w_gate, w_up, w_down):
    shp = h.shape
    t = h.reshape(-1, shp[-1])
    s = jax.nn.sigmoid(jnp.dot(t.astype(F32), router_w.astype(F32), precision=lax.Precision.HIGHEST))
    biased = s + router_bias.astype(F32)
    grp = biased.reshape(-1, N_GROUPS, EXPERTS_PER_GROUP)
    grp_score = jnp.sum(lax.top_k(grp, TOP_K)[0], axis=-1)
    g_sel = jnp.argmax(grp_score, axis=-1)
    in_group = (jnp.arange(N_EXPERTS) // EXPERTS_PER_GROUP)[None, :] == g_sel[:, None]
    _, idx = lax.top_k(jnp.where(in_group, biased, -jnp.inf), TOP_K)
    w_sel = jnp.take_along_axis(s, idx, axis=-1)
    w_sel = w_sel / jnp.sum(w_sel, axis=-1, keepdims=True)
    comb = jnp.sum(jax.nn.one_hot(idx, N_EXPERTS, dtype=F32) * w_sel[..., None], axis=1)
    y = moe_experts(t, comb, w_gate, w_up, w_down)
    return y.reshape(shp)


def kernel(x, c, ctx, c_ctx, ada_w, ada_b, norm1_g, norm2_g, w_in, na_q_norm, na_k_norm, na_rpb,
           gla_gate_w2, gla_gate_b, gla_norm_g, rw_mu, rw_w0, rw_w2, rw_a0, rw_a2, rw_g2,
           rw_k_k, rw_k_a, rw_r_k, rw_gn_g, rw_gn_b, w_branch, w_out, router_w, router_bias,
           moe_w_gate, moe_w_up, moe_w_down):
    S = x.shape[1]
    t = jnp.arange(S, dtype=jnp.int32)
    rows = t // GRID_W
    cols = t % GRID_W
    for l in range(DEPTH):
        with_ctx = l < DEPTH - 1
        p = {
            'w_in': w_in[l], 'na_q_norm': na_q_norm[l], 'na_k_norm': na_k_norm[l], 'na_rpb': na_rpb[l],
            'gla_gate_w2': gla_gate_w2[l], 'gla_gate_b': gla_gate_b[l], 'gla_norm_g': gla_norm_g[l],
            'rw_mu': rw_mu[l], 'rw_w0': rw_w0[l], 'rw_w2': rw_w2[l], 'rw_a0': rw_a0[l], 'rw_a2': rw_a2[l],
            'rw_g2': rw_g2[l], 'rw_k_k': rw_k_k[l], 'rw_k_a': rw_k_a[l], 'rw_r_k': rw_r_k[l],
            'rw_gn_g': rw_gn_g[l], 'rw_gn_b': rw_gn_b[l], 'w_branch': w_branch[l], 'w_out': w_out[l],
        }
        mod = jax.nn.silu(c) @ ada_w[l] + ada_b[l]
        mod_c = jax.nn.silu(c_ctx) @ ada_w[l] + ada_b[l]
        sh1, sc1, g1, sh2, sc2, g2 = jnp.split(mod[:, None, :], 6, axis=-1)
        csh1, csc1, cg1, csh2, csc2, cg2 = jnp.split(mod_c, 6)

        h = rms_norm(x, norm1_g[l]) * (1.0 + sc1) + sh1
        hc = rms_norm(ctx, norm1_g[l]) * (1.0 + csc1) + csh1
        y, yc = token_mixers(h, hc, rows, cols, p, with_ctx)
        x = x + g1 * y
        h2 = rms_norm(x, norm2_g[l]) * (1.0 + sc2) + sh2
        x = x + g2 * moe_ffn(h2, router_w, router_bias, moe_w_gate[l], moe_w_up[l], moe_w_down[l])

        if with_ctx:
            ctx = ctx + cg1 * yc
            hc2 = rms_norm(ctx, norm2_g[l]) * (1.0 + csc2) + csh2
            ctx = ctx + cg2 * moe_ffn(hc2, router_w, router_bias, moe_w_gate[l], moe_w_up[l], moe_w_down[l])
    return x
```

```python
import functools

import jax
import jax.numpy as jnp
from jax import lax
import numpy as np
from jax.experimental import pallas as pl
from jax.experimental.pallas import tpu as pltpu

D_MODEL = 2048
DEPTH = 2
GRID_W = 64

NA_HEADS = 16
NA_HEAD_DIM = 64
NA_WIN_R_MAX = 8
NA_WIN_C = 16
NA_W = NA_HEADS * NA_HEAD_DIM

GLA_HEADS = 4
GLA_DK = 128
GLA_DV = 256
GLA_QK = GLA_HEADS * GLA_DK
GLA_V = GLA_HEADS * GLA_DV
GLA_GATE_RANK = 16
GLA_GATE_TAU = 16.0
GLA_CHUNK = 64

RW_HEADS = 16
RW_HEAD = 64
RW_W = RW_HEADS * RW_HEAD
RW_DECAY_RANK = 96
RW_A_RANK = 96
RW_GATE_RANK = 256
RW_SPLITS = (RW_W, RW_W, RW_W, 2 * RW_DECAY_RANK, 2 * RW_A_RANK, RW_GATE_RANK)
RW_IN = sum(RW_SPLITS)
RW_GN_EPS = 64e-5

N_BRANCH = 3
BRANCH_W = 1024
IN_SPLITS = (NA_W, NA_W, NA_W, GLA_QK, GLA_QK, GLA_V, GLA_V, 2 * GLA_GATE_RANK, RW_IN, N_BRANCH * D_MODEL)
D_IN = sum(IN_SPLITS)

N_EXPERTS = 16
N_GROUPS = 4
EXPERTS_PER_GROUP = N_EXPERTS // N_GROUPS
TOP_K = 2
D_EXPERT = 512

ROPE_BASE = 10000.0
EPS = 1e-6
F32 = jnp.float32
BF16 = jnp.bfloat16

V7X_VMEM_LIMIT_BYTES = 48 * 1024 * 1024
LANES = 128


def _mm_kernel(a_ref, b_ref, o_ref):
    o_ref[...] = jnp.dot(a_ref[...], b_ref[...], preferred_element_type=F32).astype(o_ref.dtype)


def matmul(a, b, *, tm=512, tn=512, out_dtype=F32):
    M, K = a.shape
    _, N = b.shape
    assert M % tm == 0 and N % tn == 0, (M, N, tm, tn)
    return pl.pallas_call(
        _mm_kernel,
        grid=(N // tn, M // tm),
        in_specs=[pl.BlockSpec((tm, K), lambda n, m: (m, 0)),
                  pl.BlockSpec((K, tn), lambda n, m: (0, n))],
        out_specs=pl.BlockSpec((tm, tn), lambda n, m: (m, n)),
        out_shape=jax.ShapeDtypeStruct((M, N), out_dtype),
        compiler_params=pltpu.CompilerParams(
            dimension_semantics=("arbitrary", "arbitrary"),
            vmem_limit_bytes=V7X_VMEM_LIMIT_BYTES),
        name="matmul",
    )(a.astype(BF16), b.astype(BF16))


def _moe_kernel(x_ref, comb_ref, wg_ref, wu_ref, wd_ref, o_ref, acc_ref):
    e = pl.program_id(1)

    @pl.when(e == 0)
    def _():
        acc_ref[...] = jnp.zeros_like(acc_ref)

    x = x_ref[...]
    g = jnp.dot(x, wg_ref[0], preferred_element_type=F32)
    u = jnp.dot(x, wu_ref[0], preferred_element_type=F32)
    comb = comb_ref[...]
    lane = lax.broadcasted_iota(jnp.int32, comb.shape, 1)
    ce = jnp.sum(jnp.where(lane == e, comb, 0.0), axis=1, keepdims=True)
    act = (g * jax.nn.sigmoid(g)) * u * ce
    acc_ref[...] += jnp.dot(act.astype(BF16), wd_ref[0], preferred_element_type=F32)

    @pl.when(e == N_EXPERTS - 1)
    def _():
        o_ref[...] = acc_ref[...]


def moe_experts(t, comb, w_gate, w_up, w_down, *, tm=512):
    M, D = t.shape
    assert M % tm == 0
    return pl.pallas_call(
        _moe_kernel,
        grid=(M // tm, N_EXPERTS),
        in_specs=[pl.BlockSpec((tm, D), lambda m, e: (m, 0)),
                  pl.BlockSpec((tm, N_EXPERTS), lambda m, e: (m, 0)),
                  pl.BlockSpec((1, D, D_EXPERT), lambda m, e: (e, 0, 0)),
                  pl.BlockSpec((1, D, D_EXPERT), lambda m, e: (e, 0, 0)),
                  pl.BlockSpec((1, D_EXPERT, D), lambda m, e: (e, 0, 0))],
        out_specs=pl.BlockSpec((tm, D), lambda m, e: (m, 0)),
        out_shape=jax.ShapeDtypeStruct((M, D), F32),
        scratch_shapes=[pltpu.VMEM((tm, D), F32)],
        compiler_params=pltpu.CompilerParams(
            dimension_semantics=("arbitrary", "arbitrary"),
            vmem_limit_bytes=V7X_VMEM_LIMIT_BYTES),
        name="moe_experts",
    )(t.astype(BF16), comb, w_gate.astype(BF16), w_up.astype(BF16), w_down.astype(BF16))


def split_at(z, sizes):
    return jnp.split(z, np.cumsum(sizes)[:-1].tolist(), axis=-1)


def rms_norm(x, g):
    xf = x.astype(F32)
    y = xf * lax.rsqrt(jnp.mean(xf * xf, axis=-1, keepdims=True) + EPS)
    return (y * g.astype(F32)).astype(x.dtype)


def axial_angles(pos, dim):
    inv = ROPE_BASE ** (-jnp.arange(0, dim, 2, dtype=F32) / dim)
    ang = pos.astype(F32)[:, None] * inv[None, :]
    return jnp.cos(ang), jnp.sin(ang)


def rotate_half(x, cos, sin):
    x1, x2 = jnp.split(x, 2, axis=-1)
    cos = cos[None, :, None, :]
    sin = sin[None, :, None, :]
    return jnp.concatenate([x1 * cos - x2 * sin, x1 * sin + x2 * cos], axis=-1)


def rope_2d(x, rows, cols):
    half = x.shape[-1] // 2
    xr, xc = jnp.split(x.astype(F32), 2, axis=-1)
    xr = rotate_half(xr, *axial_angles(rows, half))
    xc = rotate_half(xc, *axial_angles(cols, half))
    return jnp.concatenate([xr, xc], axis=-1).astype(x.dtype)


def flip(t):
    return jnp.flip(t, axis=1)


NEG_BIG = -1e30
NT_DIMS = (((1,), (1,)), ((), ()))
TN_DIMS = (((0,), (0,)), ((), ()))
NN_DIMS = (((1,), (0,)), ((), ()))


def _split_bf16(x):
    hi = x.astype(BF16)
    lo = (x - hi.astype(F32)).astype(BF16)
    return hi, lo


def _head_sum(x2):
    row = lax.broadcasted_iota(jnp.int32, (LANES, LANES), 0) // NA_HEAD_DIM
    col = lax.broadcasted_iota(jnp.int32, (LANES, LANES), 1) // NA_HEAD_DIM
    ones_bd = jnp.where(row == col, 1.0, 0.0).astype(BF16)
    hi, lo = _split_bf16(x2)
    return (jnp.dot(hi, ones_bd, preferred_element_type=F32)
            + jnp.dot(lo, ones_bd, preferred_element_type=F32))


def _headnorm_kernel(x_ref, g_ref, o_ref):
    x = x_ref[...]
    ms = _head_sum(x * x) * (1.0 / NA_HEAD_DIM)
    o_ref[...] = (x * lax.rsqrt(ms + EPS) * g_ref[...]).astype(o_ref.dtype)


def head_rms_norm(z, col_block0, n_col_blocks, gain128, *, tm=512):
    M = z.shape[0]
    return pl.pallas_call(
        _headnorm_kernel,
        grid=(M // tm, n_col_blocks),
        in_specs=[pl.BlockSpec((tm, LANES), lambda i, j: (i, col_block0 + j)),
                  pl.BlockSpec((1, LANES), lambda i, j: (0, 0))],
        out_specs=pl.BlockSpec((tm, LANES), lambda i, j: (i, j)),
        out_shape=jax.ShapeDtypeStruct((M, n_col_blocks * LANES), BF16),
        compiler_params=pltpu.CompilerParams(dimension_semantics=("arbitrary", "arbitrary")),
        name="head_rms_norm",
    )(z, gain128)


def _softmax_pv(s_list, v_list):
    m = s_list[0].max(axis=-1, keepdims=True)
    for s in s_list[1:]:
        m = jnp.maximum(m, s.max(axis=-1, keepdims=True))
    den = 0.0
    acc = 0.0
    for s, v in zip(s_list, v_list):
        p = jnp.exp(s - m)
        den = den + p.sum(axis=-1, keepdims=True)
        acc = acc + jnp.dot(p.astype(BF16), v, preferred_element_type=F32)
    return acc / den


def _na_kernel(q_ref, k_ref, v_ref, kc_ref, vc_ref, bias_ref, o_ref, *, n_rows, win_r):
    r = pl.program_id(2)
    rs = jnp.clip(r - win_r // 2, 0, n_rows - win_r)
    start = pl.multiple_of(rs * GRID_W, GRID_W)
    kw = k_ref[pl.ds(start, win_r * GRID_W), :]
    vw = v_ref[pl.ds(start, win_r * GRID_W), :]
    kc = kc_ref[...]
    vc = vc_ref[...]
    q = q_ref[...]
    lane = lax.broadcasted_iota(jnp.int32, q.shape, 1)
    outs = []
    for h in range(2):
        qh = jnp.where((lane // NA_HEAD_DIM) == h, q, jnp.zeros_like(q))
        s_loc = lax.dot_general(qh, kw, NT_DIMS, preferred_element_type=F32) + bias_ref[h, 0]
        s_ctx = lax.dot_general(qh, kc, NT_DIMS, preferred_element_type=F32)
        outs.append(_softmax_pv([s_loc, s_ctx], [vw, vc]))
    o_ref[...] = jnp.where(lane < NA_HEAD_DIM, outs[0], outs[1]).astype(o_ref.dtype)


def _ctx_attn_kernel(q_ref, k_ref, v_ref, o_ref):
    q = q_ref[...]
    k = k_ref[...]
    v = v_ref[...]
    lane = lax.broadcasted_iota(jnp.int32, q.shape, 1)
    outs = []
    for h in range(2):
        qh = jnp.where((lane // NA_HEAD_DIM) == h, q, jnp.zeros_like(q))
        s = lax.dot_general(qh, k, NT_DIMS, preferred_element_type=F32)
        outs.append(_softmax_pv([s], [v]))
    o_ref[...] = jnp.where(lane < NA_HEAD_DIM, outs[0], outs[1]).astype(o_ref.dtype)


def na_bias_table(rpb, n_rows, win_r):
    col = np.arange(GRID_W)
    col_start = np.clip(col - NA_WIN_C // 2, 0, GRID_W - NA_WIN_C)
    kcol = np.arange(GRID_W)
    valid = (kcol[None, :] >= col_start[:, None]) & (kcol[None, :] < col_start[:, None] + NA_WIN_C)
    dc = np.clip(kcol[None, :] - col[:, None] + NA_WIN_C - 1, 0, 2 * NA_WIN_C - 2)
    dr = np.arange(NA_WIN_R_MAX)[:, None] + np.arange(win_r)[None, :]
    dr = np.clip(dr, 0, 2 * NA_WIN_R_MAX - 2)
    t = rpb.astype(F32)[:, dr][:, :, :, dc]
    t = jnp.where(valid[None, None, None], t, NEG_BIG)
    t = t.transpose(0, 1, 3, 2, 4)
    return t.reshape(rpb.shape[0], NA_WIN_R_MAX, GRID_W, win_r * GRID_W)


def na_attention(qn, kn, vb, rpb, B, S, Lc, with_ctx):
    n_rows = S // GRID_W
    win_r = min(NA_WIN_R_MAX, n_rows)
    assert win_r == NA_WIN_R_MAX and (B * S) % Lc == 0
    HP = NA_HEADS // 2
    bias = na_bias_table(rpb, n_rows, win_r)
    ctx_blk0 = (B * S) // Lc

    def variant(r):
        return jnp.clip(r - win_r // 2, 0, n_rows - win_r) - r + NA_WIN_R_MAX - 1

    y = pl.pallas_call(
        functools.partial(_na_kernel, n_rows=n_rows, win_r=win_r),
        grid=(HP, B, n_rows),
        in_specs=[pl.BlockSpec((GRID_W, LANES), lambda hp, b, r: (b * n_rows + r, hp)),
                  pl.BlockSpec((S, LANES), lambda hp, b, r: (b, hp)),
                  pl.BlockSpec((S, LANES), lambda hp, b, r: (b, hp)),
                  pl.BlockSpec((Lc, LANES), lambda hp, b, r: (ctx_blk0 + b, hp)),
                  pl.BlockSpec((Lc, LANES), lambda hp, b, r: (ctx_blk0 + b, hp)),
                  pl.BlockSpec((2, 1, GRID_W, win_r * GRID_W), lambda hp, b, r: (hp, variant(r), 0, 0))],
        out_specs=pl.BlockSpec((GRID_W, LANES), lambda hp, b, r: (b * n_rows + r, hp)),
        out_shape=jax.ShapeDtypeStruct((B * S, NA_W), F32),
        compiler_params=pltpu.CompilerParams(
            dimension_semantics=("arbitrary", "arbitrary", "arbitrary"),
            vmem_limit_bytes=V7X_VMEM_LIMIT_BYTES),
        name="na_attention",
    )(qn, kn, vb, kn, vb, bias)
    yc = None
    if with_ctx:
        yc = pl.pallas_call(
            _ctx_attn_kernel,
            grid=(HP, B),
            in_specs=[pl.BlockSpec((Lc, LANES), lambda hp, b: (ctx_blk0 + b, hp))] * 3,
            out_specs=pl.BlockSpec((Lc, LANES), lambda hp, b: (b, hp)),
            out_shape=jax.ShapeDtypeStruct((B * Lc, NA_W), F32),
            compiler_params=pltpu.CompilerParams(dimension_semantics=("arbitrary", "arbitrary")),
            name="ctx_attention",
        )(qn, kn, vb)
    return y, yc


def na_mixer(z_all, B, S, Lc, q_norm, k_norm, rpb, with_ctx):
    scale = NA_HEAD_DIM ** -0.5
    nb = NA_W // LANES
    qn = head_rms_norm(z_all, 0, nb, jnp.tile(q_norm.astype(F32) * scale, 2)[None])
    kn = head_rms_norm(z_all, nb, nb, jnp.tile(k_norm.astype(F32), 2)[None])
    vb = z_all[:, 2 * NA_W:3 * NA_W].astype(BF16)
    y, yc = na_attention(qn, kn, vb, rpb, B, S, Lc, with_ctx)
    return y.reshape(B, S, NA_W), (yc.reshape(B, Lc, NA_W) if with_ctx else None)


def gla_scan(q, k, v, log_a, s0):
    q, k, v, log_a = (t.astype(F32) for t in (q, k, v, log_a))
    B, T, H, dk = q.shape
    dv = v.shape[-1]
    C = GLA_CHUNK
    N = T // C

    def chunks(t):
        return t.reshape(B, N, C, H, t.shape[-1])

    q, k, v, log_a = chunks(q), chunks(k), chunks(v), chunks(log_a)
    b = jnp.cumsum(log_a, axis=2)
    b_last = b[:, :, -1:]
    b_ref = b[:, :, C // 2:C // 2 + 1]
    qi = q * jnp.exp(b - b_ref)
    kj = k * jnp.exp(b_ref - b)
    att = jnp.einsum('bnihd,bnjhd->bnhij', qi, kj)
    att = jnp.where(jnp.tril(jnp.ones((C, C), dtype=bool)), att, 0.0)
    y = jnp.einsum('bnhij,bnjhv->bnihv', att, v)
    kv = jnp.einsum('bnjhd,bnjhv->bnhdv', k * jnp.exp(b_last - b), v)
    decay = jnp.exp(b_last[:, :, 0])

    def step(s, inp):
        dec, kv_n = inp
        return dec[..., None] * s + kv_n, s

    s_fin, s_prev = lax.scan(step, s0, (jnp.moveaxis(decay, 1, 0), jnp.moveaxis(kv, 1, 0)))
    s_prev = jnp.moveaxis(s_prev, 0, 1)
    y = y + jnp.einsum('bnihd,bnhdv->bnihv', q * jnp.exp(b), s_prev)
    return y.reshape(B, T, H, dv), s_fin


def gla_mixer(zl, zc, rows, cols, gate_w2, gate_b, norm_g, with_ctx):
    def prep(z, pos):
        q, k, v, r, gd = z
        B, T, _ = q.shape
        q = q.reshape(B, T, GLA_HEADS, GLA_DK) * GLA_DK ** -0.5
        k = k.reshape(B, T, GLA_HEADS, GLA_DK)
        if pos is not None:
            q = rope_2d(q, pos[0], pos[1])
            k = rope_2d(k, pos[0], pos[1])
        v = v.reshape(B, T, GLA_HEADS, GLA_DV)
        gds = jnp.split(gd, 2, axis=-1)
        log_a = []
        for d in range(2):
            pre = (gds[d] @ gate_w2[d] + gate_b[d]).astype(F32)
            log_a.append((jax.nn.log_sigmoid(pre) / GLA_GATE_TAU).reshape(B, T, GLA_HEADS, GLA_DK))
        return q, k, v, r, log_a

    ql, kl, vl, rl, lal = prep(zl, (rows, cols))
    qc, kc, vc, rc, lac = prep(zc, None)
    B = ql.shape[0]
    s0 = jnp.zeros((B, GLA_HEADS, GLA_DK, GLA_DV), F32)
    yc_f, sc_f = gla_scan(qc, kc, vc, lac[0], s0)
    yc_b, sc_b = gla_scan(flip(qc), flip(kc), flip(vc), flip(lac[1]), s0)
    yl_f, _ = gla_scan(ql, kl, vl, lal[0], sc_f)
    yl_b, _ = gla_scan(flip(ql), flip(kl), flip(vl), flip(lal[1]), sc_b)

    def finish(y_f, y_b_rev, r):
        y = rms_norm(y_f + flip(y_b_rev), norm_g)
        return (y.reshape(r.shape) * jax.nn.silu(r.astype(F32))).astype(r.dtype)

    y = finish(yl_f, yl_b, rl)
    yc = finish(yc_f, yc_b, rc) if with_ctx else None
    return y, yc


def token_shift(z, mu):
    prev = jnp.pad(z[:, :-1], ((0, 0), (1, 0), (0, 0)))
    nxt = jnp.pad(z[:, 1:], ((0, 0), (0, 1), (0, 0)))
    return z + mu * (0.5 * (prev + nxt) - z)


RW_CHUNK = 64
RW_PASSES = 3


def _mm(a, b, dims=NN_DIMS, passes=3):
    a_hi, a_lo = _split_bf16(a)
    b_hi, b_lo = _split_bf16(b)
    out = lax.dot_general(a_hi, b_hi, dims, preferred_element_type=F32)
    if passes >= 3:
        out = out + lax.dot_general(a_hi, b_lo, dims, preferred_element_type=F32)
        out = out + lax.dot_general(a_lo, b_hi, dims, preferred_element_type=F32)
    return out


def _rw_chunk_kernel(r_ref, v_ref, kap_ref, lw_ref, kd_ref, bet_ref, s0_ref, y_ref, sfin_ref, s_scr,
                     *, reverse, n_batch, n_chunks, passes):
    C = RW_CHUNK
    n = pl.program_id(1)

    @pl.when(n == 0)
    def _():
        s_scr[...] = s0_ref[0]

    row = lax.broadcasted_iota(jnp.int32, (C, 2 * RW_HEAD), 0)
    lane = lax.broadcasted_iota(jnp.int32, (C, 2 * RW_HEAD), 1)
    col = lane % RW_HEAD
    head0 = lane < RW_HEAD
    if reverse:
        strict = col > row
        incl = col >= row
    else:
        strict = col < row
        incl = col <= row
    ti = lax.broadcasted_iota(jnp.int32, (C, C), 0)
    tj = lax.broadcasted_iota(jnp.int32, (C, C), 1)
    tri = jnp.where((tj >= ti) if reverse else (tj <= ti), 1.0, 0.0).astype(BF16)
    level_masks = []
    s = 1
    while s < C:
        level_masks.append(strict & (row // (2 * s) == col // (2 * s)) & (row // s != col // s))
        s *= 2
    eye = jnp.where(row == col, 1.0, 0.0)
    brow = lax.broadcasted_iota(jnp.int32, (2 * RW_HEAD, 2 * RW_HEAD), 0) // RW_HEAD
    bcol = lax.broadcasted_iota(jnp.int32, (2 * RW_HEAD, 2 * RW_HEAD), 1) // RW_HEAD
    same_head = brow == bcol

    def bd(x):
        return jnp.concatenate([jnp.where(head0, x, 0.0), jnp.where(head0, 0.0, x)], axis=0)

    def pp(a_pair, b_pair):
        return _mm(a_pair, bd(b_pair), NN_DIMS, passes)

    for b in range(n_batch):
        r = r_ref[b]
        v = v_ref[b]
        kap = kap_ref[b]
        lw = lw_ref[b]
        kd = kd_ref[b]
        bet = bet_ref[b]
        S = s_scr[b]

        lw_hi, lw_lo = _split_bf16(lw)
        cs = (jnp.dot(tri, lw_hi, preferred_element_type=F32)
              + jnp.dot(tri, lw_lo, preferred_element_type=F32))
        tot = cs[0:1, :] if reverse else cs[C - 1:C, :]
        inv = jnp.exp(-cs)
        fin = jnp.exp(tot - cs)
        kap_t = kap * jnp.exp(cs - lw)
        r_t = r * jnp.exp(cs)
        k_t = kd * inv
        b_t = bet * inv
        k_h = kd * fin
        b_h = bet * fin

        lhs = jnp.concatenate([kap_t, r_t], axis=0)
        a_k = _mm(lhs, bd(k_t), NT_DIMS, passes)
        a_b = _mm(lhs, bd(b_t), NT_DIMS, passes)
        akv = jnp.where(strict, a_k[:C], 0.0)
        ark = jnp.where(incl, a_k[C:], 0.0)
        lmat = jnp.where(strict, a_b[:C], 0.0)
        arb = jnp.where(incl, a_b[C:], 0.0)

        t_inv = eye - jnp.where(level_masks[0], lmat, 0.0)
        for m in level_masks[1:]:
            off = jnp.where(m, lmat, 0.0)
            t_inv = t_inv - pp(t_inv, pp(off, t_inv))

        s0_both = _mm(lhs, S, NT_DIMS, passes)
        x = s0_both[:C] + pp(akv, v)
        u = pp(t_inv, x)
        y = s0_both[C:] + pp(ark, v) - pp(arb, u)
        upd = _mm(v, k_h, TN_DIMS, passes) - _mm(u, b_h, TN_DIMS, passes)
        s_scr[b] = S * jnp.exp(tot) + jnp.where(same_head, upd, 0.0)
        y_ref[b] = y

    @pl.when(n == n_chunks - 1)
    def _():
        sfin_ref[0] = s_scr[...]


def rwkv_chunk_scan(r, v, kap, lw, kd, bet, s0, *, reverse, passes):
    B, T, W = r.shape
    HP = W // (2 * RW_HEAD)
    N = T // RW_CHUNK
    if reverse:
        seq_map = lambda hp, n: (0, N - 1 - n, hp)
    else:
        seq_map = lambda hp, n: (0, n, hp)
    seq_spec = pl.BlockSpec((B, RW_CHUNK, 2 * RW_HEAD), seq_map)
    st_spec = pl.BlockSpec((1, B, 2 * RW_HEAD, 2 * RW_HEAD), lambda hp, n: (hp, 0, 0, 0))
    return pl.pallas_call(
        functools.partial(_rw_chunk_kernel, reverse=reverse, n_batch=B, n_chunks=N, passes=passes),
        grid=(HP, N),
        in_specs=[seq_spec] * 6 + [st_spec],
        out_specs=[seq_spec, st_spec],
        out_shape=[jax.ShapeDtypeStruct((B, T, W), F32),
                   jax.ShapeDtypeStruct((HP, B, 2 * RW_HEAD, 2 * RW_HEAD), F32)],
        scratch_shapes=[pltpu.VMEM((B, 2 * RW_HEAD, 2 * RW_HEAD), F32)],
        compiler_params=pltpu.CompilerParams(
            dimension_semantics=("arbitrary", "arbitrary"),
            vmem_limit_bytes=V7X_VMEM_LIMIT_BYTES),
        name="rwkv_chunk_scan_rev" if reverse else "rwkv_chunk_scan",
    )(r, v, kap, lw, kd, bet, s0)


def rwkv_prep(z, mu, w0, w2, a0, a2, g2, k_k, k_a):
    B, T, _ = z.shape
    z = token_shift(z, mu)
    r, k, v, wd, ad, gd = split_at(z, RW_SPLITS)

    def heads(t):
        return t.reshape(B, T, RW_HEADS, RW_HEAD).astype(F32)

    kk = heads(k * k_k)
    kk = (kk * lax.rsqrt(jnp.sum(kk * kk, axis=-1, keepdims=True) + EPS)).reshape(B, T, RW_W)
    wds = jnp.split(wd, 2, axis=-1)
    ads = jnp.split(ad, 2, axis=-1)
    per_dir = []
    for d in range(2):
        w_log = (-jax.nn.softplus(-(w0[d] + jnp.tanh(wds[d]) @ w2[d])) - 0.5).astype(F32)
        log_decay = -jnp.exp(w_log)
        a = jax.nn.sigmoid(a0[d] + ads[d] @ a2[d])
        k_d = k * (1.0 + (a - 1.0) * k_a)
        per_dir.append((log_decay, k_d, kk * a))
    g = jax.nn.sigmoid(gd) @ g2
    return r, k, v, kk, per_dir, g


def rwkv_mixer(zl, zc, mu, w0, w2, a0, a2, g2, k_k, k_a, r_k, gn_g, gn_b, with_ctx):
    rl, kl, vl, kkl, dl, gl = rwkv_prep(zl, mu, w0, w2, a0, a2, g2, k_k, k_a)
    rc, kc, vc, kkc, dcx, gc = rwkv_prep(zc, mu, w0, w2, a0, a2, g2, k_k, k_a)
    B = rl.shape[0]
    s0 = jnp.zeros((RW_HEADS // 2, B, 2 * RW_HEAD, 2 * RW_HEAD), F32)
    scan = functools.partial(rwkv_chunk_scan, passes=RW_PASSES)
    yc_f, sc_f = scan(rc, vc, kkc, *dcx[0], s0, reverse=False)
    yc_b, sc_b = scan(rc, vc, kkc, *dcx[1], s0, reverse=True)
    yl_f, _ = scan(rl, vl, kkl, *dl[0], sc_f, reverse=False)
    yl_b, _ = scan(rl, vl, kkl, *dl[1], sc_b, reverse=True)
    gamma = gn_g.astype(F32).reshape(RW_HEADS, RW_HEAD)
    beta = gn_b.astype(F32).reshape(RW_HEADS, RW_HEAD)

    def finish(y_f, y_b, r, k, v, g):
        Bq, T, _ = r.shape

        def hd(t):
            return t.reshape(Bq, T, RW_HEADS, RW_HEAD)

        y = hd(y_f + y_b)
        r, k, v = hd(r), hd(k), hd(v)
        mean = jnp.mean(y, axis=-1, keepdims=True)
        var = jnp.mean(jnp.square(y - mean), axis=-1, keepdims=True)
        y = (y - mean) * lax.rsqrt(var + RW_GN_EPS) * gamma + beta
        y = y + jnp.sum(r * k * r_k.astype(F32), axis=-1, keepdims=True) * v
        return (y.reshape(g.shape) * g.astype(F32)).astype(g.dtype)

    y = finish(yl_f, yl_b, rl, kl, vl, gl)
    yc = finish(yc_f, yc_b, rc, kc, vc, gc) if with_ctx else None
    return y, yc


def merge_branches(ys, gates, w_branch, w_out):
    shp = ys[0].shape[:-1]
    g = jax.nn.sigmoid(gates.reshape(gates.shape[:-1] + (N_BRANCH, D_MODEL)))
    acc = 0.0
    for i in range(N_BRANCH):
        proj = matmul(ys[i].reshape(-1, BRANCH_W), w_branch[i]).reshape(shp + (D_MODEL,))
        acc = acc + g[..., i, :] * proj
    return matmul(acc.reshape(-1, D_MODEL), w_out).reshape(shp + (D_MODEL,))


def pad_cols(w, mult):
    n = w.shape[-1]
    pad = (-n) % mult
    return jnp.pad(w, ((0, 0), (0, pad))) if pad else w


def token_mixers(h, hc, rows, cols, p, with_ctx):
    B, S, D = h.shape
    Lc = hc.shape[1]
    h_all = jnp.concatenate([h.reshape(-1, D), hc.reshape(-1, D)], axis=0)
    z_all = matmul(h_all, pad_cols(p['w_in'], 768), tn=768)
    zl = split_at(z_all[:B * S, :D_IN].reshape(B, S, D_IN), IN_SPLITS)
    zc = split_at(z_all[B * S:, :D_IN].reshape(B, Lc, D_IN), IN_SPLITS)
    y_na, yc_na = na_mixer(z_all, B, S, Lc, p['na_q_norm'], p['na_k_norm'], p['na_rpb'], with_ctx)
    y_gla, yc_gla = gla_mixer(zl[3:8], zc[3:8], rows, cols, p['gla_gate_w2'], p['gla_gate_b'],
                              p['gla_norm_g'], with_ctx)
    y_rw, yc_rw = rwkv_mixer(zl[8], zc[8], p['rw_mu'], p['rw_w0'], p['rw_w2'], p['rw_a0'], p['rw_a2'],
                             p['rw_g2'], p['rw_k_k'], p['rw_k_a'], p['rw_r_k'], p['rw_gn_g'],
                             p['rw_gn_b'], with_ctx)
    y = merge_branches((y_na, y_gla, y_rw), zl[9], p['w_branch'], p['w_out'])
    yc = merge_branches((yc_na, yc_gla, yc_rw), zc[9], p['w_branch'], p['w_out']) if with_ctx else None
    return y, yc


def moe_ffn(h, router_w, router_bias, w_gate, w_up, w_down):
    shp = h.shape
    t = h.reshape(-1, shp[-1])
    s = jax.nn.sigmoid(jnp.dot(t.astype(F32), router_w.astype(F32), precision=lax.Precision.HIGHEST))
    biased = s + router_bias.astype(F32)
    grp = biased.reshape(-1, N_GROUPS, EXPERTS_PER_GROUP)
    grp_score = jnp.sum(lax.top_k(grp, TOP_K)[0], axis=-1)
    g_sel = jnp.argmax(grp_score, axis=-1)
    in_group = (jnp.arange(N_EXPERTS) // EXPERTS_PER_GROUP)[None, :] == g_sel[:, None]
    _, idx = lax.top_k(jnp.where(in_group, biased, -jnp.inf), TOP_K)
    w_sel = jnp.take_along_axis(s, idx, axis=-1)
    w_sel = w_sel / jnp.sum(w_sel, axis=-1, keepdims=True)
    comb = jnp.sum(jax.nn.one_hot(idx, N_EXPERTS, dtype=F32) * w_sel[..., None], axis=1)
    y = moe_experts(t, comb, w_gate, w_up, w_down)
    return y.reshape(shp)


def kernel(x, c, ctx, c_ctx, ada_w, ada_b, norm1_g, norm2_g, w_in, na_q_norm, na_k_norm, na_rpb,
           gla_gate_w2, gla_gate_b, gla_norm_g, rw_mu, rw_w0, rw_w2, rw_a0, rw_a2, rw_g2,
           rw_k_k, rw_k_a, rw_r_k, rw_gn_g, rw_gn_b, w_branch, w_out, router_w, router_bias,
           moe_w_gate, moe_w_up, moe_w_down):
    S = x.shape[1]
    t = jnp.arange(S, dtype=jnp.int32)
    rows = t // GRID_W
    cols = t % GRID_W
    for l in range(DEPTH):
        with_ctx = l < DEPTH - 1
        p = {
            'w_in': w_in[l], 'na_q_norm': na_q_norm[l], 'na_k_norm': na_k_norm[l], 'na_rpb': na_rpb[l],
            'gla_gate_w2': gla_gate_w2[l], 'gla_gate_b': gla_gate_b[l], 'gla_norm_g': gla_norm_g[l],
            'rw_mu': rw_mu[l], 'rw_w0': rw_w0[l], 'rw_w2': rw_w2[l], 'rw_a0': rw_a0[l], 'rw_a2': rw_a2[l],
            'rw_g2': rw_g2[l], 'rw_k_k': rw_k_k[l], 'rw_k_a': rw_k_a[l], 'rw_r_k': rw_r_k[l],
            'rw_gn_g': rw_gn_g[l], 'rw_gn_b': rw_gn_b[l], 'w_branch': w_branch[l], 'w_out': w_out[l],
        }
        mod = jax.nn.silu(c) @ ada_w[l] + ada_b[l]
        mod_c = jax.nn.silu(c_ctx) @ ada_w[l] + ada_b[l]
        sh1, sc1, g1, sh2, sc2, g2 = jnp.split(mod[:, None, :], 6, axis=-1)
        csh1, csc1, cg1, csh2, csc2, cg2 = jnp.split(mod_c, 6)

        h = rms_norm(x, norm1_g[l]) * (1.0 + sc1) + sh1
        hc = rms_norm(ctx, norm1_g[l]) * (1.0 + csc1) + csh1
        y, yc = token_mixers(h, hc, rows, cols, p, with_ctx)
        x = x + g1 * y
        h2 = rms_norm(x, norm2_g[l]) * (1.0 + sc2) + sh2
        x = x + g2 * moe_ffn(h2, router_w, router_bias, moe_w_gate[l], moe_w_up[l], moe_w_down[l])

        if with_ctx:
            ctx = ctx + cg1 * yc
            hc2 = rms_norm(ctx, norm2_g[l]) * (1.0 + csc2) + csh2
            ctx = ctx + cg2 * moe_ffn(hc2, router_w, router_bias, moe_w_gate[l], moe_w_up[l], moe_w_down[l])
    return x
```

```python
import functools

import jax
import jax.numpy as jnp
from jax import lax
import numpy as np
from jax.experimental import pallas as pl
from jax.experimental.pallas import tpu as pltpu

D_MODEL = 2048
DEPTH = 2
GRID_W = 64

NA_HEADS = 16
NA_HEAD_DIM = 64
NA_WIN_R_MAX = 8
NA_WIN_C = 16
NA_W = NA_HEADS * NA_HEAD_DIM

GLA_HEADS = 4
GLA_DK = 128
GLA_DV = 256
GLA_QK = GLA_HEADS * GLA_DK
GLA_V = GLA_HEADS * GLA_DV
GLA_GATE_RANK = 16
GLA_GATE_TAU = 16.0
GLA_CHUNK = 64

RW_HEADS = 16
RW_HEAD = 64
RW_W = RW_HEADS * RW_HEAD
RW_DECAY_RANK = 96
RW_A_RANK = 96
RW_GATE_RANK = 256
RW_SPLITS = (RW_W, RW_W, RW_W, 2 * RW_DECAY_RANK, 2 * RW_A_RANK, RW_GATE_RANK)
RW_IN = sum(RW_SPLITS)
RW_GN_EPS = 64e-5
RW_CHUNK = 64
RW_PASSES = 1
RW_PAIRS_PER_STEP = 8

N_BRANCH = 3
BRANCH_W = 1024
IN_SPLITS = (NA_W, NA_W, NA_W, GLA_QK, GLA_QK, GLA_V, GLA_V, 2 * GLA_GATE_RANK, RW_IN, N_BRANCH * D_MODEL)
D_IN = sum(IN_SPLITS)
IN_OFFSETS = tuple(int(o) for o in np.cumsum((0,) + IN_SPLITS[:-1]))

N_EXPERTS = 16
N_GROUPS = 4
EXPERTS_PER_GROUP = N_EXPERTS // N_GROUPS
TOP_K = 2
D_EXPERT = 512

ROPE_BASE = 10000.0
EPS = 1e-6
F32 = jnp.float32
BF16 = jnp.bfloat16

V7X_VMEM_LIMIT_BYTES = 48 * 1024 * 1024
LANES = 128

NEG_BIG = -1e30
NT_DIMS = (((1,), (1,)), ((), ()))
TN_DIMS = (((0,), (0,)), ((), ()))
NN_DIMS = (((1,), (0,)), ((), ()))


def _split_bf16(x):
    hi = x.astype(BF16)
    lo = (x - hi.astype(F32)).astype(BF16)
    return hi, lo


def _mm(a, b, dims=NN_DIMS, passes=1):
    if passes == 1:
        return lax.dot_general(a.astype(BF16), b.astype(BF16), dims, preferred_element_type=F32)
    a_hi, a_lo = _split_bf16(a)
    b_hi, b_lo = _split_bf16(b)
    out = lax.dot_general(a_hi, b_hi, dims, preferred_element_type=F32)
    out = out + lax.dot_general(a_hi, b_lo, dims, preferred_element_type=F32)
    return out + lax.dot_general(a_lo, b_hi, dims, preferred_element_type=F32)


def _cumsum_rows(tri, x):
    hi, lo = _split_bf16(x)
    return jnp.dot(tri, hi, preferred_element_type=F32) + jnp.dot(tri, lo, preferred_element_type=F32)


def _mm_kernel(a_ref, b_ref, o_ref):
    o_ref[...] = jnp.dot(a_ref[...], b_ref[...], preferred_element_type=F32).astype(o_ref.dtype)


def matmul(a, b, *, tm=512, tn=512, out_dtype=F32):
    M, K = a.shape
    _, N = b.shape
    assert M % tm == 0 and N % tn == 0, (M, N, tm, tn)
    return pl.pallas_call(
        _mm_kernel,
        grid=(N // tn, M // tm),
        in_specs=[pl.BlockSpec((tm, K), lambda n, m: (m, 0)),
                  pl.BlockSpec((K, tn), lambda n, m: (0, n))],
        out_specs=pl.BlockSpec((tm, tn), lambda n, m: (m, n)),
        out_shape=jax.ShapeDtypeStruct((M, N), out_dtype),
        compiler_params=pltpu.CompilerParams(
            dimension_semantics=("arbitrary", "arbitrary"),
            vmem_limit_bytes=V7X_VMEM_LIMIT_BYTES),
        name="matmul",
    )(a.astype(BF16), b.astype(BF16))


def _moe_kernel(x_ref, comb_ref, wg_ref, wu_ref, wd_ref, o_ref, acc_ref):
    e = pl.program_id(1)

    @pl.when(e == 0)
    def _():
        acc_ref[...] = jnp.zeros_like(acc_ref)

    x = x_ref[...]
    g = jnp.dot(x, wg_ref[0], preferred_element_type=F32)
    u = jnp.dot(x, wu_ref[0], preferred_element_type=F32)
    comb = comb_ref[...]
    lane = lax.broadcasted_iota(jnp.int32, comb.shape, 1)
    ce = jnp.sum(jnp.where(lane == e, comb, 0.0), axis=1, keepdims=True)
    act = (g * jax.nn.sigmoid(g)) * u * ce
    acc_ref[...] += jnp.dot(act.astype(BF16), wd_ref[0], preferred_element_type=F32)

    @pl.when(e == N_EXPERTS - 1)
    def _():
        o_ref[...] = acc_ref[...]


def moe_experts(t, comb, w_gate, w_up, w_down, *, tm=512):
    M, D = t.shape
    assert M % tm == 0
    return pl.pallas_call(
        _moe_kernel,
        grid=(M // tm, N_EXPERTS),
        in_specs=[pl.BlockSpec((tm, D), lambda m, e: (m, 0)),
                  pl.BlockSpec((tm, N_EXPERTS), lambda m, e: (m, 0)),
                  pl.BlockSpec((1, D, D_EXPERT), lambda m, e: (e, 0, 0)),
                  pl.BlockSpec((1, D, D_EXPERT), lambda m, e: (e, 0, 0)),
                  pl.BlockSpec((1, D_EXPERT, D), lambda m, e: (e, 0, 0))],
        out_specs=pl.BlockSpec((tm, D), lambda m, e: (m, 0)),
        out_shape=jax.ShapeDtypeStruct((M, D), F32),
        scratch_shapes=[pltpu.VMEM((tm, D), F32)],
        compiler_params=pltpu.CompilerParams(
            dimension_semantics=("arbitrary", "arbitrary"),
            vmem_limit_bytes=V7X_VMEM_LIMIT_BYTES),
        name="moe_experts",
    )(t.astype(BF16), comb, w_gate.astype(BF16), w_up.astype(BF16), w_down.astype(BF16))


def split_at(z, sizes):
    return jnp.split(z, np.cumsum(sizes)[:-1].tolist(), axis=-1)


def rms_norm(x, g):
    xf = x.astype(F32)
    y = xf * lax.rsqrt(jnp.mean(xf * xf, axis=-1, keepdims=True) + EPS)
    return (y * g.astype(F32)).astype(x.dtype)


def _head_sum(x2):
    row = lax.broadcasted_iota(jnp.int32, (LANES, LANES), 0) // NA_HEAD_DIM
    col = lax.broadcasted_iota(jnp.int32, (LANES, LANES), 1) // NA_HEAD_DIM
    ones_bd = jnp.where(row == col, 1.0, 0.0).astype(BF16)
    return _cumsum_rows_right(x2, ones_bd)


def _cumsum_rows_right(x, m):
    hi, lo = _split_bf16(x)
    return jnp.dot(hi, m, preferred_element_type=F32) + jnp.dot(lo, m, preferred_element_type=F32)


def _headnorm_kernel(x_ref, g_ref, o_ref):
    x = x_ref[...]
    ms = _head_sum(x * x) * (1.0 / NA_HEAD_DIM)
    o_ref[...] = (x * lax.rsqrt(ms + EPS) * g_ref[...]).astype(o_ref.dtype)


def head_rms_norm(z, col_block0, n_col_blocks, gain128, *, tm=512):
    M = z.shape[0]
    return pl.pallas_call(
        _headnorm_kernel,
        grid=(M // tm, n_col_blocks),
        in_specs=[pl.BlockSpec((tm, LANES), lambda i, j: (i, col_block0 + j)),
                  pl.BlockSpec((1, LANES), lambda i, j: (0, 0))],
        out_specs=pl.BlockSpec((tm, LANES), lambda i, j: (i, j)),
        out_shape=jax.ShapeDtypeStruct((M, n_col_blocks * LANES), BF16),
        compiler_params=pltpu.CompilerParams(dimension_semantics=("arbitrary", "arbitrary")),
        name="head_rms_norm",
    )(z, gain128)


def _softmax_pv(s_list, v_list):
    m = s_list[0].max(axis=-1, keepdims=True)
    for s in s_list[1:]:
        m = jnp.maximum(m, s.max(axis=-1, keepdims=True))
    den = 0.0
    acc = 0.0
    for s, v in zip(s_list, v_list):
        p = jnp.exp(s - m)
        den = den + p.sum(axis=-1, keepdims=True)
        acc = acc + jnp.dot(p.astype(BF16), v, preferred_element_type=F32)
    return acc / den


def _na_kernel(q_ref, k_ref, v_ref, kc_ref, vc_ref, bias_ref, o_ref, *, n_rows, win_r):
    r = pl.program_id(2)
    rs = jnp.clip(r - win_r // 2, 0, n_rows - win_r)
    start = pl.multiple_of(rs * GRID_W, GRID_W)
    kw = k_ref[pl.ds(start, win_r * GRID_W), :]
    vw = v_ref[pl.ds(start, win_r * GRID_W), :]
    kc = kc_ref[...]
    vc = vc_ref[...]
    q = q_ref[...]
    lane = lax.broadcasted_iota(jnp.int32, q.shape, 1)
    outs = []
    for h in range(2):
        qh = jnp.where((lane // NA_HEAD_DIM) == h, q, jnp.zeros_like(q))
        s_loc = lax.dot_general(qh, kw, NT_DIMS, preferred_element_type=F32) + bias_ref[h, 0]
        s_ctx = lax.dot_general(qh, kc, NT_DIMS, preferred_element_type=F32)
        outs.append(_softmax_pv([s_loc, s_ctx], [vw, vc]))
    o_ref[...] = jnp.where(lane < NA_HEAD_DIM, outs[0], outs[1]).astype(o_ref.dtype)


def _ctx_attn_kernel(q_ref, k_ref, v_ref, o_ref):
    q = q_ref[...]
    k = k_ref[...]
    v = v_ref[...]
    lane = lax.broadcasted_iota(jnp.int32, q.shape, 1)
    outs = []
    for h in range(2):
        qh = jnp.where((lane // NA_HEAD_DIM) == h, q, jnp.zeros_like(q))
        s = lax.dot_general(qh, k, NT_DIMS, preferred_element_type=F32)
        outs.append(_softmax_pv([s], [v]))
    o_ref[...] = jnp.where(lane < NA_HEAD_DIM, outs[0], outs[1]).astype(o_ref.dtype)


def na_bias_table(rpb, n_rows, win_r):
    col = np.arange(GRID_W)
    col_start = np.clip(col - NA_WIN_C // 2, 0, GRID_W - NA_WIN_C)
    kcol = np.arange(GRID_W)
    valid = (kcol[None, :] >= col_start[:, None]) & (kcol[None, :] < col_start[:, None] + NA_WIN_C)
    dc = np.clip(kcol[None, :] - col[:, None] + NA_WIN_C - 1, 0, 2 * NA_WIN_C - 2)
    dr = np.arange(NA_WIN_R_MAX)[:, None] + np.arange(win_r)[None, :]
    dr = np.clip(dr, 0, 2 * NA_WIN_R_MAX - 2)
    t = rpb.astype(F32)[:, dr][:, :, :, dc]
    t = jnp.where(valid[None, None, None], t, NEG_BIG)
    t = t.transpose(0, 1, 3, 2, 4)
    return t.reshape(rpb.shape[0], NA_WIN_R_MAX, GRID_W, win_r * GRID_W)


def na_attention(qn, kn, vb, rpb, B, S, Lc, with_ctx):
    n_rows = S // GRID_W
    win_r = min(NA_WIN_R_MAX, n_rows)
    assert win_r == NA_WIN_R_MAX and (B * S) % Lc == 0
    HP = NA_HEADS // 2
    bias = na_bias_table(rpb, n_rows, win_r)
    ctx_blk0 = (B * S) // Lc

    def variant(r):
        return jnp.clip(r - win_r // 2, 0, n_rows - win_r) - r + NA_WIN_R_MAX - 1

    y = pl.pallas_call(
        functools.partial(_na_kernel, n_rows=n_rows, win_r=win_r),
        grid=(HP, B, n_rows),
        in_specs=[pl.BlockSpec((GRID_W, LANES), lambda hp, b, r: (b * n_rows + r, hp)),
                  pl.BlockSpec((S, LANES), lambda hp, b, r: (b, hp)),
                  pl.BlockSpec((S, LANES), lambda hp, b, r: (b, hp)),
                  pl.BlockSpec((Lc, LANES), lambda hp, b, r: (ctx_blk0 + b, hp)),
                  pl.BlockSpec((Lc, LANES), lambda hp, b, r: (ctx_blk0 + b, hp)),
                  pl.BlockSpec((2, 1, GRID_W, win_r * GRID_W), lambda hp, b, r: (hp, variant(r), 0, 0))],
        out_specs=pl.BlockSpec((GRID_W, LANES), lambda hp, b, r: (b * n_rows + r, hp)),
        out_shape=jax.ShapeDtypeStruct((B * S, NA_W), F32),
        compiler_params=pltpu.CompilerParams(
            dimension_semantics=("arbitrary", "arbitrary", "arbitrary"),
            vmem_limit_bytes=V7X_VMEM_LIMIT_BYTES),
        name="na_attention",
    )(qn, kn, vb, kn, vb, bias)
    yc = None
    if with_ctx:
        yc = pl.pallas_call(
            _ctx_attn_kernel,
            grid=(HP, B),
            in_specs=[pl.BlockSpec((Lc, LANES), lambda hp, b: (ctx_blk0 + b, hp))] * 3,
            out_specs=pl.BlockSpec((Lc, LANES), lambda hp, b: (b, hp)),
            out_shape=jax.ShapeDtypeStruct((B * Lc, NA_W), F32),
            compiler_params=pltpu.CompilerParams(dimension_semantics=("arbitrary", "arbitrary")),
            name="ctx_attention",
        )(qn, kn, vb)
    return y, yc


def na_mixer(z_all, B, S, Lc, q_norm, k_norm, rpb, with_ctx):
    scale = NA_HEAD_DIM ** -0.5
    nb = NA_W // LANES
    qn = head_rms_norm(z_all, 0, nb, jnp.tile(q_norm.astype(F32) * scale, 2)[None])
    kn = head_rms_norm(z_all, nb, nb, jnp.tile(k_norm.astype(F32), 2)[None])
    vb = z_all[:, 2 * NA_W:3 * NA_W].astype(BF16)
    y, yc = na_attention(qn, kn, vb, rpb, B, S, Lc, with_ctx)
    return y.reshape(B, S, NA_W), (yc.reshape(B, Lc, NA_W) if with_ctx else None)


def _log_sigmoid(x):
    return jnp.minimum(x, 0.0) - jnp.log(1.0 + jnp.exp(-jnp.abs(x)))


def _gla_kernel(*refs, reverse, rope, finish, n_chunks):
    it = iter(refs)
    q_ref, k_ref, v_ref, gd_ref, w2_ref, gb_ref, s0_ref = (next(it) for _ in range(7))
    if rope:
        cos_ref, sin_ref = next(it), next(it)
    if finish:
        yo_ref, r_ref, g_ref = next(it), next(it), next(it)
    y_ref, sfin_ref, s_scr = next(it), next(it), next(it)

    C = GLA_CHUNK
    n = pl.program_id(1)
    H = range(GLA_HEADS)

    @pl.when(n == 0)
    def _():
        s_scr[...] = s0_ref[0]

    ti = lax.broadcasted_iota(jnp.int32, (C, C), 0)
    tj = lax.broadcasted_iota(jnp.int32, (C, C), 1)
    incl = (tj >= ti) if reverse else (tj <= ti)
    tri = jnp.where(incl, 1.0, 0.0).astype(BF16)

    pre = _mm(gd_ref[...], w2_ref[...], NN_DIMS, 3) + gb_ref[...]
    la = _log_sigmoid(pre) * (1.0 / GLA_GATE_TAU)
    cs = _cumsum_rows(tri, la)
    ref_i = C // 2 - 1 if reverse else C // 2
    last_i = 0 if reverse else C - 1
    b_ref = cs[ref_i:ref_i + 1, :]
    b_last = cs[last_i:last_i + 1, :]

    q = q_ref[...] * (GLA_DK ** -0.5)
    k = k_ref[...]
    if rope:
        lane = lax.broadcasted_iota(jnp.int32, q.shape, 1)
        first = (lane % (GLA_DK // 2)) < GLA_DK // 4
        cos = jnp.concatenate([cos_ref[...]] * GLA_HEADS, axis=1)
        sin = jnp.concatenate([sin_ref[...]] * GLA_HEADS, axis=1)

        def rot(x):
            partner = jnp.where(first, pltpu.roll(x, GLA_QK - GLA_DK // 4, axis=1),
                                pltpu.roll(x, GLA_DK // 4, axis=1))
            return x * cos + partner * sin

        q = rot(q)
        k = rot(k)
    qi = q * jnp.exp(cs - b_ref)
    kj = k * jnp.exp(b_ref - cs)
    qe = q * jnp.exp(cs)
    ke = k * jnp.exp(b_last - cs)
    dec = jnp.exp(b_last)
    v_all = v_ref[...]

    def hk(x, h):
        return x[:, h * GLA_DK:(h + 1) * GLA_DK]

    def hv(x, h):
        return x[:, h * GLA_DV:(h + 1) * GLA_DV]

    St = [s_scr[h] for h in H]
    V = [hv(v_all, h) for h in H]
    att = [jnp.where(incl, _mm(hk(qi, h), hk(kj, h), NT_DIMS), 0.0) for h in H]
    y = [_mm(att[h], V[h], NN_DIMS) + _mm(hk(qe, h), St[h], NT_DIMS) for h in H]
    kvt = [_mm(V[h], hk(ke, h), TN_DIMS) for h in H]
    for h in H:
        s_scr[h] = St[h] * hk(dec, h) + kvt[h]
    if finish:
        yo = yo_ref[...]
        r = r_ref[...]
        g = g_ref[...]
        outs = []
        for h in H:
            ys = y[h] + hv(yo, h)
            ms = jnp.mean(ys * ys, axis=-1, keepdims=True)
            rh = hv(r, h)
            outs.append(ys * lax.rsqrt(ms + EPS) * g * (rh * jax.nn.sigmoid(rh)))
        y_ref[...] = jnp.concatenate(outs, axis=1)
    else:
        y_ref[...] = jnp.concatenate(y, axis=1)

    @pl.when(n == n_chunks - 1)
    def _():
        sfin_ref[0] = s_scr[...]


def gla_rope_tables(seq_len):
    t = np.arange(seq_len)
    quarter = GLA_DK // 4
    inv = ROPE_BASE ** (-np.arange(0, 2 * quarter, 2, dtype=np.float64) / (2 * quarter))
    ar = (t // GRID_W)[:, None] * inv[None, :]
    ac = (t % GRID_W)[:, None] * inv[None, :]
    cos = np.concatenate([np.cos(ar), np.cos(ar), np.cos(ac), np.cos(ac)], axis=1)
    sin = np.concatenate([-np.sin(ar), np.sin(ar), -np.sin(ac), np.sin(ac)], axis=1)
    return jnp.asarray(cos, F32), jnp.asarray(sin, F32)


def gla_scan(z, w2pad, gate_b, s0, *, batch, seq_len, row0, reverse, rope, finish_with=None):
    N = seq_len // GLA_CHUNK
    blk0 = row0 // GLA_CHUNK
    off_q, off_k, off_v, off_r, off_gd = IN_OFFSETS[3:8]
    assert off_q % GLA_QK == 0 and off_k % GLA_QK == 0 and off_v % GLA_V == 0 and off_r % GLA_V == 0
    assert off_gd % LANES == 0

    def tb(n):
        return N - 1 - n if reverse else n

    def rb(b, n):
        return blk0 + b * N + tb(n)

    def ob(b, n):
        return b * N + tb(n)

    in_specs = [pl.BlockSpec((GLA_CHUNK, GLA_QK), lambda b, n: (rb(b, n), off_q // GLA_QK)),
                pl.BlockSpec((GLA_CHUNK, GLA_QK), lambda b, n: (rb(b, n), off_k // GLA_QK)),
                pl.BlockSpec((GLA_CHUNK, GLA_V), lambda b, n: (rb(b, n), off_v // GLA_V)),
                pl.BlockSpec((GLA_CHUNK, LANES), lambda b, n: (rb(b, n), off_gd // LANES)),
                pl.BlockSpec((LANES, GLA_QK), lambda b, n: (0, 0)),
                pl.BlockSpec((1, GLA_QK), lambda b, n: (0, 0)),
                pl.BlockSpec((1, GLA_HEADS, GLA_DV, GLA_DK), lambda b, n: (b, 0, 0, 0))]
    args = [z, z, z, z, w2pad, gate_b, s0]
    if rope:
        cos, sin = gla_rope_tables(seq_len)
        in_specs += [pl.BlockSpec((GLA_CHUNK, GLA_DK), lambda b, n: (tb(n), 0))] * 2
        args += [cos, sin]
    if finish_with is not None:
        y_other, norm_g = finish_with
        in_specs += [pl.BlockSpec((GLA_CHUNK, GLA_V), lambda b, n: (ob(b, n), 0)),
                     pl.BlockSpec((GLA_CHUNK, GLA_V), lambda b, n: (rb(b, n), off_r // GLA_V)),
                     pl.BlockSpec((1, GLA_DV), lambda b, n: (0, 0))]
        args += [y_other, z, norm_g]
    return pl.pallas_call(
        functools.partial(_gla_kernel, reverse=reverse, rope=rope, finish=finish_with is not None, n_chunks=N),
        grid=(batch, N),
        in_specs=in_specs,
        out_specs=[pl.BlockSpec((GLA_CHUNK, GLA_V), lambda b, n: (ob(b, n), 0)),
                   pl.BlockSpec((1, GLA_HEADS, GLA_DV, GLA_DK), lambda b, n: (b, 0, 0, 0))],
        out_shape=[jax.ShapeDtypeStruct((batch * seq_len, GLA_V), F32),
                   jax.ShapeDtypeStruct((batch, GLA_HEADS, GLA_DV, GLA_DK), F32)],
        scratch_shapes=[pltpu.VMEM((GLA_HEADS, GLA_DV, GLA_DK), F32)],
        compiler_params=pltpu.CompilerParams(
            dimension_semantics=("arbitrary", "arbitrary"),
            vmem_limit_bytes=V7X_VMEM_LIMIT_BYTES),
        name="gla_scan_rev" if reverse else "gla_scan",
    )(*args)


def gla_mixer(z_all, B, S, Lc, gate_w2, gate_b, norm_g, with_ctx):
    s0 = jnp.zeros((B, GLA_HEADS, GLA_DV, GLA_DK), F32)
    g = norm_g.astype(F32)[None]
    w2 = [jnp.zeros((LANES, GLA_QK), F32).at[d * GLA_GATE_RANK:(d + 1) * GLA_GATE_RANK].set(gate_w2[d])
          for d in range(2)]
    gb = [gate_b[d].astype(F32)[None] for d in range(2)]
    ctx = dict(batch=B, seq_len=Lc, row0=B * S, rope=False)
    lat = dict(batch=B, seq_len=S, row0=0, rope=True)
    yc_f, sc_f = gla_scan(z_all, w2[0], gb[0], s0, reverse=False, **ctx)
    yc, sc_b = gla_scan(z_all, w2[1], gb[1], s0, reverse=True, finish_with=(yc_f, g), **ctx)
    yl_f, _ = gla_scan(z_all, w2[0], gb[0], sc_f, reverse=False, **lat)
    yl, _ = gla_scan(z_all, w2[1], gb[1], sc_b, reverse=True, finish_with=(yl_f, g), **lat)
    return yl.reshape(B, S, GLA_V), (yc.reshape(B, Lc, GLA_V) if with_ctx else None)


def token_shift(z, mu):
    prev = jnp.pad(z[:, :-1], ((0, 0), (1, 0), (0, 0)))
    nxt = jnp.pad(z[:, 1:], ((0, 0), (0, 1), (0, 0)))
    return z + mu * (0.5 * (prev + nxt) - z)


def _rw_chunk_kernel(r_ref, v_ref, kap_ref, lw_ref, kd_ref, bet_ref, s0_ref, y_ref, sfin_ref, s_scr,
                     *, reverse, n_par, n_chunks, passes):
    C = RW_CHUNK
    n = pl.program_id(2)
    P = range(n_par)

    @pl.when(n == 0)
    def _():
        s_scr[...] = s0_ref[0, 0]

    row = lax.broadcasted_iota(jnp.int32, (C, LANES), 0)
    lane = lax.broadcasted_iota(jnp.int32, (C, LANES), 1)
    col = lane % RW_HEAD
    head0 = lane < RW_HEAD
    strict = (col > row) if reverse else (col < row)
    incl = (col >= row) if reverse else (col <= row)
    ti = lax.broadcasted_iota(jnp.int32, (C, C), 0)
    tj = lax.broadcasted_iota(jnp.int32, (C, C), 1)
    tri = jnp.where((tj >= ti) if reverse else (tj <= ti), 1.0, 0.0).astype(BF16)
    level_masks = []
    s = 1
    while s < C:
        level_masks.append(strict & (row // (2 * s) == col // (2 * s)) & (row // s != col // s))
        s *= 2
    eye = jnp.where(row == col, 1.0, 0.0)
    brow = lax.broadcasted_iota(jnp.int32, (LANES, LANES), 0) // RW_HEAD
    bcol = lax.broadcasted_iota(jnp.int32, (LANES, LANES), 1) // RW_HEAD
    same_head = brow == bcol

    def sl(x, p):
        return x[:, p * LANES:(p + 1) * LANES]

    def bd(x):
        return jnp.concatenate([jnp.where(head0, x, 0.0), jnp.where(head0, 0.0, x)], axis=0)

    def pp(a_pair, b_pair):
        return _mm(a_pair, bd(b_pair), NN_DIMS, passes)

    lw_all = lw_ref[...]
    cs_all = _cumsum_rows(tri, lw_all)
    tot_all = cs_all[0:1, :] if reverse else cs_all[C - 1:C, :]
    inv_all = jnp.exp(-cs_all)
    fin_all = jnp.exp(tot_all - cs_all)
    kd_all = kd_ref[...]
    bet_all = bet_ref[...]
    kap_t = kap_ref[...] * jnp.exp(cs_all - lw_all)
    r_t = r_ref[...] * jnp.exp(cs_all)
    k_t = kd_all * inv_all
    b_t = bet_all * inv_all
    k_h = kd_all * fin_all
    b_h = bet_all * fin_all
    dec = jnp.exp(tot_all)
    v_all = v_ref[...]

    S = [s_scr[p] for p in P]
    V = [sl(v_all, p) for p in P]
    lhs = [jnp.concatenate([sl(kap_t, p), sl(r_t, p)], axis=0) for p in P]
    a_k = [_mm(lhs[p], bd(sl(k_t, p)), NT_DIMS, passes) for p in P]
    a_b = [_mm(lhs[p], bd(sl(b_t, p)), NT_DIMS, passes) for p in P]
    s0_both = [_mm(lhs[p], S[p], NT_DIMS, passes) for p in P]
    lmat = [jnp.where(strict, a_b[p][:C], 0.0) for p in P]
    t_inv = [eye - jnp.where(level_masks[0], lmat[p], 0.0) for p in P]
    for m in level_masks[1:]:
        tmp = [pp(jnp.where(m, lmat[p], 0.0), t_inv[p]) for p in P]
        t_inv = [t_inv[p] - pp(t_inv[p], tmp[p]) for p in P]
    x = [s0_both[p][:C] + pp(jnp.where(strict, a_k[p][:C], 0.0), V[p]) for p in P]
    u = [pp(t_inv[p], x[p]) for p in P]
    y = [s0_both[p][C:] + pp(jnp.where(incl, a_k[p][C:], 0.0), V[p])
         - pp(jnp.where(incl, a_b[p][C:], 0.0), u[p]) for p in P]
    upd = [_mm(V[p], sl(k_h, p), TN_DIMS, passes) - _mm(u[p], sl(b_h, p), TN_DIMS, passes) for p in P]
    for p in P:
        s_scr[p] = S[p] * sl(dec, p) + jnp.where(same_head, upd[p], 0.0)
    y_ref[...] = jnp.concatenate(y, axis=1)

    @pl.when(n == n_chunks - 1)
    def _():
        sfin_ref[0, 0] = s_scr[...]


def rwkv_chunk_scan(r, v, kap, lw, kd, bet, s0, *, batch, seq_len, row0, reverse):
    n_par = RW_PAIRS_PER_STEP
    W = r.shape[1]
    G = W // (n_par * LANES)
    N = seq_len // RW_CHUNK
    blk0 = row0 // RW_CHUNK

    def tb(n):
        return N - 1 - n if reverse else n

    seq_in = pl.BlockSpec((RW_CHUNK, n_par * LANES), lambda b, g, n: (blk0 + b * N + tb(n), g))
    seq_out = pl.BlockSpec((RW_CHUNK, n_par * LANES), lambda b, g, n: (b * N + tb(n), g))
    st_spec = pl.BlockSpec((1, 1, n_par, LANES, LANES), lambda b, g, n: (b, g, 0, 0, 0))
    return pl.pallas_call(
        functools.partial(_rw_chunk_kernel, reverse=reverse, n_par=n_par, n_chunks=N, passes=RW_PASSES),
        grid=(batch, G, N),
        in_specs=[seq_in] * 6 + [st_spec],
        out_specs=[seq_out, st_spec],
        out_shape=[jax.ShapeDtypeStruct((batch * seq_len, W), F32),
                   jax.ShapeDtypeStruct((batch, G, n_par, LANES, LANES), F32)],
        scratch_shapes=[pltpu.VMEM((n_par, LANES, LANES), F32)],
        compiler_params=pltpu.CompilerParams(
            dimension_semantics=("arbitrary", "arbitrary", "arbitrary"),
            vmem_limit_bytes=V7X_VMEM_LIMIT_BYTES),
        name="rwkv_chunk_scan_rev" if reverse else "rwkv_chunk_scan",
    )(r, v, kap, lw, kd, bet, s0)


def rwkv_prep(z, mu, w0, w2, a0, a2, g2, k_k, k_a):
    B, T, _ = z.shape
    z = token_shift(z, mu)
    r, k, v, wd, ad, gd = split_at(z, RW_SPLITS)

    def heads(t):
        return t.reshape(B, T, RW_HEADS, RW_HEAD).astype(F32)

    kk = heads(k * k_k)
    kk = (kk * lax.rsqrt(jnp.sum(kk * kk, axis=-1, keepdims=True) + EPS)).reshape(B, T, RW_W)
    wds = jnp.split(wd, 2, axis=-1)
    ads = jnp.split(ad, 2, axis=-1)
    per_dir = []
    for d in range(2):
        w_log = (-jax.nn.softplus(-(w0[d] + jnp.tanh(wds[d]) @ w2[d])) - 0.5).astype(F32)
        log_decay = -jnp.exp(w_log)
        a = jax.nn.sigmoid(a0[d] + ads[d] @ a2[d])
        k_d = k * (1.0 + (a - 1.0) * k_a)
        per_dir.append((log_decay, k_d, kk * a))
    g = jax.nn.sigmoid(gd) @ g2
    return r, k, v, kk, per_dir, g


def rwkv_mixer(zl, zc, mu, w0, w2, a0, a2, g2, k_k, k_a, r_k, gn_g, gn_b, with_ctx):
    rl, kl, vl, kkl, dl, gl = rwkv_prep(zl, mu, w0, w2, a0, a2, g2, k_k, k_a)
    rc, kc, vc, kkc, dcx, gc = rwkv_prep(zc, mu, w0, w2, a0, a2, g2, k_k, k_a)
    B, S, _ = rl.shape
    Lc = rc.shape[1]
    s0 = jnp.zeros((B, RW_W // (RW_PAIRS_PER_STEP * LANES), RW_PAIRS_PER_STEP, LANES, LANES), F32)

    def scan(r, v, kk, d, s_init, T, reverse):
        flat = [t.reshape(B * T, RW_W) for t in (r, v, kk) + tuple(d)]
        y, s_fin = rwkv_chunk_scan(*flat, s_init, batch=B, seq_len=T, row0=0, reverse=reverse)
        return y.reshape(B, T, RW_W), s_fin

    yc_f, sc_f = scan(rc, vc, kkc, dcx[0], s0, Lc, False)
    yc_b, sc_b = scan(rc, vc, kkc, dcx[1], s0, Lc, True)
    yl_f, _ = scan(rl, vl, kkl, dl[0], sc_f, S, False)
    yl_b, _ = scan(rl, vl, kkl, dl[1], sc_b, S, True)
    gamma = gn_g.astype(F32).reshape(RW_HEADS, RW_HEAD)
    beta = gn_b.astype(F32).reshape(RW_HEADS, RW_HEAD)

    def finish(y_f, y_b, r, k, v, g):
        Bq, T, _ = r.shape

        def hd(t):
            return t.reshape(Bq, T, RW_HEADS, RW_HEAD)

        y = hd(y_f + y_b)
        r, k, v = hd(r), hd(k), hd(v)
        mean = jnp.mean(y, axis=-1, keepdims=True)
        var = jnp.mean(jnp.square(y - mean), axis=-1, keepdims=True)
        y = (y - mean) * lax.rsqrt(var + RW_GN_EPS) * gamma + beta
        y = y + jnp.sum(r * k * r_k.astype(F32), axis=-1, keepdims=True) * v
        return (y.reshape(g.shape) * g.astype(F32)).astype(g.dtype)

    y = finish(yl_f, yl_b, rl, kl, vl, gl)
    yc = finish(yc_f, yc_b, rc, kc, vc, gc) if with_ctx else None
    return y, yc


def merge_branches(ys, gates, w_branch, w_out):
    shp = ys[0].shape[:-1]
    g = jax.nn.sigmoid(gates.reshape(gates.shape[:-1] + (N_BRANCH, D_MODEL)))
    acc = 0.0
    for i in range(N_BRANCH):
        proj = matmul(ys[i].reshape(-1, BRANCH_W), w_branch[i]).reshape(shp + (D_MODEL,))
        acc = acc + g[..., i, :] * proj
    return matmul(acc.reshape(-1, D_MODEL), w_out).reshape(shp + (D_MODEL,))


def pad_cols(w, mult):
    n = w.shape[-1]
    pad = (-n) % mult
    return jnp.pad(w, ((0, 0), (0, pad))) if pad else w


def token_mixers(h, hc, p, with_ctx):
    B, S, D = h.shape
    Lc = hc.shape[1]
    h_all = jnp.concatenate([h.reshape(-1, D), hc.reshape(-1, D)], axis=0)
    z_all = matmul(h_all, pad_cols(p['w_in'], 768), tn=768)
    o_rw, o_gate = IN_OFFSETS[8], IN_OFFSETS[9]

    def part(lo, hi):
        return (z_all[:B * S, lo:hi].reshape(B, S, hi - lo), z_all[B * S:, lo:hi].reshape(B, Lc, hi - lo))

    y_na, yc_na = na_mixer(z_all, B, S, Lc, p['na_q_norm'], p['na_k_norm'], p['na_rpb'], with_ctx)
    y_gla, yc_gla = gla_mixer(z_all, B, S, Lc, p['gla_gate_w2'], p['gla_gate_b'], p['gla_norm_g'], with_ctx)
    zl_rw, zc_rw = part(o_rw, o_gate)
    y_rw, yc_rw = rwkv_mixer(zl_rw, zc_rw, p['rw_mu'], p['rw_w0'], p['rw_w2'], p['rw_a0'], p['rw_a2'],
                             p['rw_g2'], p['rw_k_k'], p['rw_k_a'], p['rw_r_k'], p['rw_gn_g'],
                             p['rw_gn_b'], with_ctx)
    zl_g, zc_g = part(o_gate, D_IN)
    y = merge_branches((y_na, y_gla, y_rw), zl_g, p['w_branch'], p['w_out'])
    yc = merge_branches((yc_na, yc_gla, yc_rw), zc_g, p['w_branch'], p['w_out']) if with_ctx else None
    return y, yc


def moe_ffn(h, router_w, router_bias, w_gate, w_up, w_down):
    shp = h.shape
    t = h.reshape(-1, shp[-1])
    s = jax.nn.sigmoid(jnp.dot(t.astype(F32), router_w.astype(F32), precision=lax.Precision.HIGHEST))
    biased = s + router_bias.astype(F32)
    grp = biased.reshape(-1, N_GROUPS, EXPERTS_PER_GROUP)
    grp_score = jnp.sum(lax.top_k(grp, TOP_K)[0], axis=-1)
    g_sel = jnp.argmax(grp_score, axis=-1)
    in_group = (jnp.arange(N_EXPERTS) // EXPERTS_PER_GROUP)[None, :] == g_sel[:, None]
    _, idx = lax.top_k(jnp.where(in_group, biased, -jnp.inf), TOP_K)
    w_sel = jnp.take_along_axis(s, idx, axis=-1)
    w_sel = w_sel / jnp.sum(w_sel, axis=-1, keepdims=True)
    comb = jnp.sum(jax.nn.one_hot(idx, N_EXPERTS, dtype=F32) * w_sel[..., None], axis=1)
    y = moe_experts(t, comb, w_gate, w_up, w_down)
    return y.reshape(shp)


def kernel(x, c, ctx, c_ctx, ada_w, ada_b, norm1_g, norm2_g, w_in, na_q_norm, na_k_norm, na_rpb,
           gla_gate_w2, gla_gate_b, gla_norm_g, rw_mu, rw_w0, rw_w2, rw_a0, rw_a2, rw_g2,
           rw_k_k, rw_k_a, rw_r_k, rw_gn_g, rw_gn_b, w_branch, w_out, router_w, router_bias,
           moe_w_gate, moe_w_up, moe_w_down):
    for l in range(DEPTH):
        with_ctx = l < DEPTH - 1
        p = {
            'w_in': w_in[l], 'na_q_norm': na_q_norm[l], 'na_k_norm': na_k_norm[l], 'na_rpb': na_rpb[l],
            'gla_gate_w2': gla_gate_w2[l], 'gla_gate_b': gla_gate_b[l], 'gla_norm_g': gla_norm_g[l],
            'rw_mu': rw_mu[l], 'rw_w0': rw_w0[l], 'rw_w2': rw_w2[l], 'rw_a0': rw_a0[l], 'rw_a2': rw_a2[l],
            'rw_g2': rw_g2[l], 'rw_k_k': rw_k_k[l], 'rw_k_a': rw_k_a[l], 'rw_r_k': rw_r_k[l],
            'rw_gn_g': rw_gn_g[l], 'rw_gn_b': rw_gn_b[l], 'w_branch': w_branch[l], 'w_out': w_out[l],
        }
        mod = jax.nn.silu(c) @ ada_w[l] + ada_b[l]
        mod_c = jax.nn.silu(c_ctx) @ ada_w[l] + ada_b[l]
        sh1, sc1, g1, sh2, sc2, g2 = jnp.split(mod[:, None, :], 6, axis=-1)
        csh1, csc1, cg1, csh2, csc2, cg2 = jnp.split(mod_c, 6)

        h = rms_norm(x, norm1_g[l]) * (1.0 + sc1) + sh1
        hc = rms_norm(ctx, norm1_g[l]) * (1.0 + csc1) + csh1
        y, yc = token_mixers(h, hc, p, with_ctx)
        x = x + g1 * y
        h2 = rms_norm(x, norm2_g[l]) * (1.0 + sc2) + sh2
        x = x + g2 * moe_ffn(h2, router_w, router_bias, moe_w_gate[l], moe_w_up[l], moe_w_down[l])

        if with_ctx:
            ctx = ctx + cg1 * yc
            hc2 = rms_norm(ctx, norm2_g[l]) * (1.0 + csc2) + csh2
            ctx = ctx + cg2 * moe_ffn(hc2, router_w, router_bias, moe_w_gate[l], moe_w_up[l], moe_w_down[l])
    return x
```

```python
import functools

import jax
import jax.numpy as jnp
from jax import lax
import numpy as np
from jax.experimental import pallas as pl
from jax.experimental.pallas import tpu as pltpu

D_MODEL = 2048
DEPTH = 2
GRID_W = 64

NA_HEADS = 16
NA_HEAD_DIM = 64
NA_WIN_R_MAX = 8
NA_WIN_C = 16
NA_W = NA_HEADS * NA_HEAD_DIM

GLA_HEADS = 4
GLA_DK = 128
GLA_DV = 256
GLA_QK = GLA_HEADS * GLA_DK
GLA_V = GLA_HEADS * GLA_DV
GLA_GATE_RANK = 16
GLA_GATE_TAU = 16.0
GLA_CHUNK = 64

RW_HEADS = 16
RW_HEAD = 64
RW_W = RW_HEADS * RW_HEAD
RW_DECAY_RANK = 96
RW_A_RANK = 96
RW_GATE_RANK = 256
RW_SPLITS = (RW_W, RW_W, RW_W, 2 * RW_DECAY_RANK, 2 * RW_A_RANK, RW_GATE_RANK)
RW_IN = sum(RW_SPLITS)
RW_GN_EPS = 64e-5
RW_CHUNK = 64
RW_PASSES = 1
RW_PAIRS_PER_STEP = 8

N_BRANCH = 3
BRANCH_W = 1024
IN_SPLITS = (NA_W, NA_W, NA_W, GLA_QK, GLA_QK, GLA_V, GLA_V, 2 * GLA_GATE_RANK, RW_IN, N_BRANCH * D_MODEL)
D_IN = sum(IN_SPLITS)
IN_OFFSETS = tuple(int(o) for o in np.cumsum((0,) + IN_SPLITS[:-1]))

LANES = 128
RW_LR_W = 4 * LANES
C_NAQ, C_NAK, C_NAV = 0, 1024, 2048
C_GV, C_GR = 3072, 4096
C_RR, C_RK, C_RV = 5120, 6144, 7168
C_GATE = 8192
C_GQ, C_GK = 14336, 14848
C_RLR, C_RGD, C_GGD = 15360, 15872, 16128
Z_W = 16384

N_EXPERTS = 16
N_GROUPS = 4
EXPERTS_PER_GROUP = N_EXPERTS // N_GROUPS
TOP_K = 2
D_EXPERT = 512

ROPE_BASE = 10000.0
EPS = 1e-6
F32 = jnp.float32
BF16 = jnp.bfloat16

V7X_VMEM_LIMIT_BYTES = 48 * 1024 * 1024

NEG_BIG = -1e30
NT_DIMS = (((1,), (1,)), ((), ()))
TN_DIMS = (((0,), (0,)), ((), ()))
NN_DIMS = (((1,), (0,)), ((), ()))


def _split_bf16(x):
    hi = x.astype(BF16)
    lo = (x - hi.astype(F32)).astype(BF16)
    return hi, lo


def _mm(a, b, dims=NN_DIMS, passes=1):
    if passes == 1:
        return lax.dot_general(a.astype(BF16), b.astype(BF16), dims, preferred_element_type=F32)
    a_hi, a_lo = _split_bf16(a)
    b_hi, b_lo = _split_bf16(b)
    out = lax.dot_general(a_hi, b_hi, dims, preferred_element_type=F32)
    out = out + lax.dot_general(a_hi, b_lo, dims, preferred_element_type=F32)
    return out + lax.dot_general(a_lo, b_hi, dims, preferred_element_type=F32)


def _cumsum_rows(tri, x):
    hi, lo = _split_bf16(x)
    return jnp.dot(tri, hi, preferred_element_type=F32) + jnp.dot(tri, lo, preferred_element_type=F32)


def _mm_kernel(a_ref, b_ref, o_ref):
    o_ref[...] = jnp.dot(a_ref[...], b_ref[...], preferred_element_type=F32).astype(o_ref.dtype)


def matmul(a, b, *, tm=512, tn=512, out_dtype=F32):
    M, K = a.shape
    _, N = b.shape
    assert M % tm == 0 and N % tn == 0, (M, N, tm, tn)
    return pl.pallas_call(
        _mm_kernel,
        grid=(N // tn, M // tm),
        in_specs=[pl.BlockSpec((tm, K), lambda n, m: (m, 0)),
                  pl.BlockSpec((K, tn), lambda n, m: (0, n))],
        out_specs=pl.BlockSpec((tm, tn), lambda n, m: (m, n)),
        out_shape=jax.ShapeDtypeStruct((M, N), out_dtype),
        compiler_params=pltpu.CompilerParams(
            dimension_semantics=("arbitrary", "arbitrary"),
            vmem_limit_bytes=V7X_VMEM_LIMIT_BYTES),
        name="matmul",
    )(a.astype(BF16), b.astype(BF16))


def _moe_kernel(x_ref, comb_ref, wg_ref, wu_ref, wd_ref, o_ref, acc_ref):
    e = pl.program_id(1)

    @pl.when(e == 0)
    def _():
        acc_ref[...] = jnp.zeros_like(acc_ref)

    x = x_ref[...]
    g = jnp.dot(x, wg_ref[0], preferred_element_type=F32)
    u = jnp.dot(x, wu_ref[0], preferred_element_type=F32)
    comb = comb_ref[...]
    lane = lax.broadcasted_iota(jnp.int32, comb.shape, 1)
    ce = jnp.sum(jnp.where(lane == e, comb, 0.0), axis=1, keepdims=True)
    act = (g * jax.nn.sigmoid(g)) * u * ce
    acc_ref[...] += jnp.dot(act.astype(BF16), wd_ref[0], preferred_element_type=F32)

    @pl.when(e == N_EXPERTS - 1)
    def _():
        o_ref[...] = acc_ref[...]


def moe_experts(t, comb, w_gate, w_up, w_down, *, tm=512):
    M, D = t.shape
    assert M % tm == 0
    return pl.pallas_call(
        _moe_kernel,
        grid=(M // tm, N_EXPERTS),
        in_specs=[pl.BlockSpec((tm, D), lambda m, e: (m, 0)),
                  pl.BlockSpec((tm, N_EXPERTS), lambda m, e: (m, 0)),
                  pl.BlockSpec((1, D, D_EXPERT), lambda m, e: (e, 0, 0)),
                  pl.BlockSpec((1, D, D_EXPERT), lambda m, e: (e, 0, 0)),
                  pl.BlockSpec((1, D_EXPERT, D), lambda m, e: (e, 0, 0))],
        out_specs=pl.BlockSpec((tm, D), lambda m, e: (m, 0)),
        out_shape=jax.ShapeDtypeStruct((M, D), F32),
        scratch_shapes=[pltpu.VMEM((tm, D), F32)],
        compiler_params=pltpu.CompilerParams(
            dimension_semantics=("arbitrary", "arbitrary"),
            vmem_limit_bytes=V7X_VMEM_LIMIT_BYTES),
        name="moe_experts",
    )(t.astype(BF16), comb, w_gate.astype(BF16), w_up.astype(BF16), w_down.astype(BF16))


def split_at(z, sizes):
    return jnp.split(z, np.cumsum(sizes)[:-1].tolist(), axis=-1)


def rms_norm(x, g):
    xf = x.astype(F32)
    y = xf * lax.rsqrt(jnp.mean(xf * xf, axis=-1, keepdims=True) + EPS)
    return (y * g.astype(F32)).astype(x.dtype)


def _head_sum(x2):
    row = lax.broadcasted_iota(jnp.int32, (LANES, LANES), 0) // NA_HEAD_DIM
    col = lax.broadcasted_iota(jnp.int32, (LANES, LANES), 1) // NA_HEAD_DIM
    ones_bd = jnp.where(row == col, 1.0, 0.0).astype(BF16)
    return _cumsum_rows_right(x2, ones_bd)


def _cumsum_rows_right(x, m):
    hi, lo = _split_bf16(x)
    return jnp.dot(hi, m, preferred_element_type=F32) + jnp.dot(lo, m, preferred_element_type=F32)


def _na_qkv_kernel(x_ref, g_ref, o_ref, *, n_norm_blocks):
    x = x_ref[...]

    @pl.when(pl.program_id(1) < n_norm_blocks)
    def _():
        ms = _head_sum(x * x) * (1.0 / NA_HEAD_DIM)
        o_ref[...] = (x * lax.rsqrt(ms + EPS) * g_ref[...]).astype(o_ref.dtype)

    @pl.when(pl.program_id(1) >= n_norm_blocks)
    def _():
        o_ref[...] = x.astype(o_ref.dtype)


def na_qkv(z, gains, *, tm=512):
    M = z.shape[0]
    n_norm = 2 * NA_W // LANES
    return pl.pallas_call(
        functools.partial(_na_qkv_kernel, n_norm_blocks=n_norm),
        grid=(M // tm, 3 * NA_W // LANES),
        in_specs=[pl.BlockSpec((tm, LANES), lambda i, j: (i, j)),
                  pl.BlockSpec((1, LANES), lambda i, j: (0, jnp.minimum(j, n_norm - 1)))],
        out_specs=pl.BlockSpec((tm, LANES), lambda i, j: (i, j)),
        out_shape=jax.ShapeDtypeStruct((M, 3 * NA_W), BF16),
        compiler_params=pltpu.CompilerParams(dimension_semantics=("arbitrary", "arbitrary")),
        name="na_qkv",
    )(z, gains)


def _softmax_pv(s_list, v_list):
    m = s_list[0].max(axis=-1, keepdims=True)
    for s in s_list[1:]:
        m = jnp.maximum(m, s.max(axis=-1, keepdims=True))
    den = 0.0
    acc = 0.0
    for s, v in zip(s_list, v_list):
        p = jnp.exp(s - m)
        den = den + p.sum(axis=-1, keepdims=True)
        acc = acc + jnp.dot(p.astype(BF16), v, preferred_element_type=F32)
    return acc / den


def _na_kernel(q_ref, k_ref, v_ref, kc_ref, vc_ref, bias_ref, o_ref, *, n_rows, win_r):
    r = pl.program_id(2)
    rs = jnp.clip(r - win_r // 2, 0, n_rows - win_r)
    start = pl.multiple_of(rs * GRID_W, GRID_W)
    kw = k_ref[pl.ds(start, win_r * GRID_W), :]
    vw = v_ref[pl.ds(start, win_r * GRID_W), :]
    kc = kc_ref[...]
    vc = vc_ref[...]
    q = q_ref[...]
    lane = lax.broadcasted_iota(jnp.int32, q.shape, 1)
    outs = []
    for h in range(2):
        qh = jnp.where((lane // NA_HEAD_DIM) == h, q, jnp.zeros_like(q))
        s_loc = lax.dot_general(qh, kw, NT_DIMS, preferred_element_type=F32) + bias_ref[h, 0]
        s_ctx = lax.dot_general(qh, kc, NT_DIMS, preferred_element_type=F32)
        outs.append(_softmax_pv([s_loc, s_ctx], [vw, vc]))
    o_ref[...] = jnp.where(lane < NA_HEAD_DIM, outs[0], outs[1]).astype(o_ref.dtype)


def _ctx_attn_kernel(q_ref, k_ref, v_ref, o_ref):
    q = q_ref[...]
    k = k_ref[...]
    v = v_ref[...]
    lane = lax.broadcasted_iota(jnp.int32, q.shape, 1)
    outs = []
    for h in range(2):
        qh = jnp.where((lane // NA_HEAD_DIM) == h, q, jnp.zeros_like(q))
        s = lax.dot_general(qh, k, NT_DIMS, preferred_element_type=F32)
        outs.append(_softmax_pv([s], [v]))
    o_ref[...] = jnp.where(lane < NA_HEAD_DIM, outs[0], outs[1]).astype(o_ref.dtype)


def na_bias_table(rpb, n_rows, win_r):
    col = np.arange(GRID_W)
    col_start = np.clip(col - NA_WIN_C // 2, 0, GRID_W - NA_WIN_C)
    kcol = np.arange(GRID_W)
    valid = (kcol[None, :] >= col_start[:, None]) & (kcol[None, :] < col_start[:, None] + NA_WIN_C)
    dc = np.clip(kcol[None, :] - col[:, None] + NA_WIN_C - 1, 0, 2 * NA_WIN_C - 2)
    dr = np.arange(NA_WIN_R_MAX)[:, None] + np.arange(win_r)[None, :]
    dr = np.clip(dr, 0, 2 * NA_WIN_R_MAX - 2)
    t = rpb.astype(F32)[:, dr][:, :, :, dc]
    t = jnp.where(valid[None, None, None], t, NEG_BIG)
    t = t.transpose(0, 1, 3, 2, 4)
    return t.reshape(rpb.shape[0], NA_WIN_R_MAX, GRID_W, win_r * GRID_W)


def na_attention(qkv, rpb, B, S, Lc, with_ctx):
    n_rows = S // GRID_W
    win_r = min(NA_WIN_R_MAX, n_rows)
    assert win_r == NA_WIN_R_MAX and (B * S) % Lc == 0
    HP = NA_HEADS // 2
    bias = na_bias_table(rpb, n_rows, win_r)
    ctx_blk0 = (B * S) // Lc

    def variant(r):
        return jnp.clip(r - win_r // 2, 0, n_rows - win_r) - r + NA_WIN_R_MAX - 1

    y = pl.pallas_call(
        functools.partial(_na_kernel, n_rows=n_rows, win_r=win_r),
        grid=(HP, B, n_rows),
        in_specs=[pl.BlockSpec((GRID_W, LANES), lambda hp, b, r: (b * n_rows + r, hp)),
                  pl.BlockSpec((S, LANES), lambda hp, b, r: (b, HP + hp)),
                  pl.BlockSpec((S, LANES), lambda hp, b, r: (b, 2 * HP + hp)),
                  pl.BlockSpec((Lc, LANES), lambda hp, b, r: (ctx_blk0 + b, HP + hp)),
                  pl.BlockSpec((Lc, LANES), lambda hp, b, r: (ctx_blk0 + b, 2 * HP + hp)),
                  pl.BlockSpec((2, 1, GRID_W, win_r * GRID_W), lambda hp, b, r: (hp, variant(r), 0, 0))],
        out_specs=pl.BlockSpec((GRID_W, LANES), lambda hp, b, r: (b * n_rows + r, hp)),
        out_shape=jax.ShapeDtypeStruct((B * S, NA_W), F32),
        compiler_params=pltpu.CompilerParams(
            dimension_semantics=("arbitrary", "arbitrary", "arbitrary"),
            vmem_limit_bytes=V7X_VMEM_LIMIT_BYTES),
        name="na_attention",
    )(qkv, qkv, qkv, qkv, qkv, bias)
    yc = None
    if with_ctx:
        yc = pl.pallas_call(
            _ctx_attn_kernel,
            grid=(HP, B),
            in_specs=[pl.BlockSpec((Lc, LANES), lambda hp, b: (ctx_blk0 + b, hp)),
                      pl.BlockSpec((Lc, LANES), lambda hp, b: (ctx_blk0 + b, HP + hp)),
                      pl.BlockSpec((Lc, LANES), lambda hp, b: (ctx_blk0 + b, 2 * HP + hp))],
            out_specs=pl.BlockSpec((Lc, LANES), lambda hp, b: (b, hp)),
            out_shape=jax.ShapeDtypeStruct((B * Lc, NA_W), F32),
            compiler_params=pltpu.CompilerParams(dimension_semantics=("arbitrary", "arbitrary")),
            name="ctx_attention",
        )(qkv, qkv, qkv)
    return y, yc


def na_mixer(z_all, B, S, Lc, q_norm, k_norm, rpb, with_ctx):
    assert (C_NAQ, C_NAK, C_NAV) == (0, NA_W, 2 * NA_W)
    scale = NA_HEAD_DIM ** -0.5
    gains = jnp.concatenate([jnp.tile(q_norm.astype(F32) * scale, NA_HEADS),
                             jnp.tile(k_norm.astype(F32), NA_HEADS)])[None]
    y, yc = na_attention(na_qkv(z_all, gains), rpb, B, S, Lc, with_ctx)
    return y, yc


def _log_sigmoid(x):
    return jnp.minimum(x, 0.0) - jnp.log(1.0 + jnp.exp(-jnp.abs(x)))


def _gla_kernel(*refs, reverse, rope, finish, n_chunks):
    it = iter(refs)
    q_ref, k_ref, v_ref, gd_ref, w2_ref, gb_ref, s0_ref = (next(it) for _ in range(7))
    if rope:
        cos_ref, sin_ref = next(it), next(it)
    if finish:
        yo_ref, r_ref, g_ref = next(it), next(it), next(it)
    y_ref, sfin_ref, s_scr = next(it), next(it), next(it)

    C = GLA_CHUNK
    n = pl.program_id(1)
    H = range(GLA_HEADS)

    @pl.when(n == 0)
    def _():
        s_scr[...] = s0_ref[0]

    ti = lax.broadcasted_iota(jnp.int32, (C, C), 0)
    tj = lax.broadcasted_iota(jnp.int32, (C, C), 1)
    incl = (tj >= ti) if reverse else (tj <= ti)
    tri = jnp.where(incl, 1.0, 0.0).astype(BF16)

    pre = _mm(gd_ref[...], w2_ref[...], NN_DIMS, 3) + gb_ref[...]
    la = _log_sigmoid(pre) * (1.0 / GLA_GATE_TAU)
    cs = _cumsum_rows(tri, la)
    ref_i = C // 2 - 1 if reverse else C // 2
    last_i = 0 if reverse else C - 1
    b_ref = cs[ref_i:ref_i + 1, :]
    b_last = cs[last_i:last_i + 1, :]

    q = q_ref[...] * (GLA_DK ** -0.5)
    k = k_ref[...]
    if rope:
        lane = lax.broadcasted_iota(jnp.int32, q.shape, 1)
        first = (lane % (GLA_DK // 2)) < GLA_DK // 4
        cos = jnp.concatenate([cos_ref[...]] * GLA_HEADS, axis=1)
        sin = jnp.concatenate([sin_ref[...]] * GLA_HEADS, axis=1)

        def rot(x):
            partner = jnp.where(first, pltpu.roll(x, GLA_QK - GLA_DK // 4, axis=1),
                                pltpu.roll(x, GLA_DK // 4, axis=1))
            return x * cos + partner * sin

        q = rot(q)
        k = rot(k)
    qi = q * jnp.exp(cs - b_ref)
    kj = k * jnp.exp(b_ref - cs)
    qe = q * jnp.exp(cs)
    ke = k * jnp.exp(b_last - cs)
    dec = jnp.exp(b_last)
    v_all = v_ref[...]

    def hk(x, h):
        return x[:, h * GLA_DK:(h + 1) * GLA_DK]

    def hv(x, h):
        return x[:, h * GLA_DV:(h + 1) * GLA_DV]

    St = [s_scr[h] for h in H]
    V = [hv(v_all, h) for h in H]
    att = [jnp.where(incl, _mm(hk(qi, h), hk(kj, h), NT_DIMS), 0.0) for h in H]
    y = [_mm(att[h], V[h], NN_DIMS) + _mm(hk(qe, h), St[h], NT_DIMS) for h in H]
    kvt = [_mm(V[h], hk(ke, h), TN_DIMS) for h in H]
    for h in H:
        s_scr[h] = St[h] * hk(dec, h) + kvt[h]
    if finish:
        yo = yo_ref[...]
        r = r_ref[...]
        g = g_ref[...]
        outs = []
        for h in H:
            ys = y[h] + hv(yo, h)
            ms = jnp.mean(ys * ys, axis=-1, keepdims=True)
            rh = hv(r, h)
            outs.append(ys * lax.rsqrt(ms + EPS) * g * (rh * jax.nn.sigmoid(rh)))
        y_ref[...] = jnp.concatenate(outs, axis=1)
    else:
        y_ref[...] = jnp.concatenate(y, axis=1)

    @pl.when(n == n_chunks - 1)
    def _():
        sfin_ref[0] = s_scr[...]


def gla_rope_tables(seq_len):
    t = np.arange(seq_len)
    quarter = GLA_DK // 4
    inv = ROPE_BASE ** (-np.arange(0, 2 * quarter, 2, dtype=np.float64) / (2 * quarter))
    ar = (t // GRID_W)[:, None] * inv[None, :]
    ac = (t % GRID_W)[:, None] * inv[None, :]
    cos = np.concatenate([np.cos(ar), np.cos(ar), np.cos(ac), np.cos(ac)], axis=1)
    sin = np.concatenate([-np.sin(ar), np.sin(ar), -np.sin(ac), np.sin(ac)], axis=1)
    return jnp.asarray(cos, F32), jnp.asarray(sin, F32)


def gla_scan(z, w2pad, gate_b, s0, *, batch, seq_len, row0, reverse, rope, finish_with=None):
    N = seq_len // GLA_CHUNK
    blk0 = row0 // GLA_CHUNK
    off_q, off_k, off_v, off_r, off_gd = C_GQ, C_GK, C_GV, C_GR, C_GGD
    assert off_q % GLA_QK == 0 and off_k % GLA_QK == 0 and off_v % GLA_V == 0 and off_r % GLA_V == 0
    assert off_gd % LANES == 0

    def tb(n):
        return N - 1 - n if reverse else n

    def rb(b, n):
        return blk0 + b * N + tb(n)

    def ob(b, n):
        return b * N + tb(n)

    in_specs = [pl.BlockSpec((GLA_CHUNK, GLA_QK), lambda b, n: (rb(b, n), off_q // GLA_QK)),
                pl.BlockSpec((GLA_CHUNK, GLA_QK), lambda b, n: (rb(b, n), off_k // GLA_QK)),
                pl.BlockSpec((GLA_CHUNK, GLA_V), lambda b, n: (rb(b, n), off_v // GLA_V)),
                pl.BlockSpec((GLA_CHUNK, LANES), lambda b, n: (rb(b, n), off_gd // LANES)),
                pl.BlockSpec((LANES, GLA_QK), lambda b, n: (0, 0)),
                pl.BlockSpec((1, GLA_QK), lambda b, n: (0, 0)),
                pl.BlockSpec((1, GLA_HEADS, GLA_DV, GLA_DK), lambda b, n: (b, 0, 0, 0))]
    args = [z, z, z, z, w2pad, gate_b, s0]
    if rope:
        cos, sin = gla_rope_tables(seq_len)
        in_specs += [pl.BlockSpec((GLA_CHUNK, GLA_DK), lambda b, n: (tb(n), 0))] * 2
        args += [cos, sin]
    if finish_with is not None:
        y_other, norm_g = finish_with
        in_specs += [pl.BlockSpec((GLA_CHUNK, GLA_V), lambda b, n: (ob(b, n), 0)),
                     pl.BlockSpec((GLA_CHUNK, GLA_V), lambda b, n: (rb(b, n), off_r // GLA_V)),
                     pl.BlockSpec((1, GLA_DV), lambda b, n: (0, 0))]
        args += [y_other, z, norm_g]
    return pl.pallas_call(
        functools.partial(_gla_kernel, reverse=reverse, rope=rope, finish=finish_with is not None, n_chunks=N),
        grid=(batch, N),
        in_specs=in_specs,
        out_specs=[pl.BlockSpec((GLA_CHUNK, GLA_V), lambda b, n: (ob(b, n), 0)),
                   pl.BlockSpec((1, GLA_HEADS, GLA_DV, GLA_DK), lambda b, n: (b, 0, 0, 0))],
        out_shape=[jax.ShapeDtypeStruct((batch * seq_len, GLA_V), F32),
                   jax.ShapeDtypeStruct((batch, GLA_HEADS, GLA_DV, GLA_DK), F32)],
        scratch_shapes=[pltpu.VMEM((GLA_HEADS, GLA_DV, GLA_DK), F32)],
        compiler_params=pltpu.CompilerParams(
            dimension_semantics=("arbitrary", "arbitrary"),
            vmem_limit_bytes=V7X_VMEM_LIMIT_BYTES),
        name="gla_scan_rev" if reverse else "gla_scan",
    )(*args)


def gla_mixer(z_all, B, S, Lc, gate_w2, gate_b, norm_g, with_ctx):
    s0 = jnp.zeros((B, GLA_HEADS, GLA_DV, GLA_DK), F32)
    g = norm_g.astype(F32)[None]
    w2 = [jnp.zeros((LANES, GLA_QK), F32).at[d * GLA_GATE_RANK:(d + 1) * GLA_GATE_RANK].set(gate_w2[d])
          for d in range(2)]
    gb = [gate_b[d].astype(F32)[None] for d in range(2)]
    ctx = dict(batch=B, seq_len=Lc, row0=B * S, rope=False)
    lat = dict(batch=B, seq_len=S, row0=0, rope=True)
    yc_f, sc_f = gla_scan(z_all, w2[0], gb[0], s0, reverse=False, **ctx)
    yc, sc_b = gla_scan(z_all, w2[1], gb[1], s0, reverse=True, finish_with=(yc_f, g), **ctx)
    yl_f, _ = gla_scan(z_all, w2[0], gb[0], sc_f, reverse=False, **lat)
    yl, _ = gla_scan(z_all, w2[1], gb[1], sc_b, reverse=True, finish_with=(yl_f, g), **lat)
    return yl, (yc if with_ctx else None)


RW_PREP_TM = 256
RW_HALO = 8
RW_N_PIECES = 5


def _heads_sum(x):
    return jnp.concatenate([_head_sum(x[:, j * LANES:(j + 1) * LANES]) for j in range(x.shape[1] // LANES)],
                           axis=1)


def _rw_prep_kernel(*refs, tm, n_latent_tiles, tiles_per_seq):
    P = RW_N_PIECES
    z_refs, prev_refs, next_refs, mu_refs = refs[0:P], refs[P:2 * P], refs[2 * P:3 * P], refs[3 * P:4 * P]
    kk_ref, ka_ref, rk_ref, w0_ref, a0_ref, w2_ref, a2_ref, g2_ref = refs[4 * P:4 * P + 8]
    (r_o, v_o, kap_o, lw0_o, kd0_o, b0_o, lw1_o, kd1_o, b1_o, bonus_o, g_o) = refs[4 * P + 8:]
    i = pl.program_id(0)
    latent = i < n_latent_tiles
    first = jnp.logical_or(jnp.logical_not(latent), i % tiles_per_seq == 0)
    last = jnp.logical_or(jnp.logical_not(latent), i % tiles_per_seq == tiles_per_seq - 1)

    def shifted(j):
        z = z_refs[j][...]
        row = lax.broadcasted_iota(jnp.int32, z.shape, 0)
        prev_row = jnp.where(first, 0.0, prev_refs[j][RW_HALO - 1:RW_HALO, :])
        next_row = jnp.where(last, 0.0, next_refs[j][0:1, :])
        prev = jnp.where(row == 0, prev_row, pltpu.roll(z, 1, axis=0))
        nxt = jnp.where(row == tm - 1, next_row, pltpu.roll(z, tm - 1, axis=0))
        return z + mu_refs[j][...] * (0.5 * (prev + nxt) - z)

    r, k, v, lr, gd = (shifted(j) for j in range(P))
    kkr = k * kk_ref[...]
    kap = kkr * lax.rsqrt(_heads_sum(kkr * kkr) + EPS)
    r_o[...] = r
    v_o[...] = v
    kap_o[...] = kap
    bonus_o[...] = _heads_sum(r * k * rk_ref[...]) * v
    g_o[...] = _mm(jax.nn.sigmoid(gd), g2_ref[...], NN_DIMS, 3)
    ka = ka_ref[...]
    for d, (lw_o, kd_o, b_o) in enumerate(((lw0_o, kd0_o, b0_o), (lw1_o, kd1_o, b1_o))):
        wd = lr[:, d * LANES:(d + 1) * LANES]
        ad = lr[:, (2 + d) * LANES:(3 + d) * LANES]
        w_log = _log_sigmoid(w0_ref[d:d + 1, :] + _mm(jnp.tanh(wd), w2_ref[d], NN_DIMS, 3)) - 0.5
        lw_o[...] = -jnp.exp(w_log)
        a = jax.nn.sigmoid(a0_ref[d:d + 1, :] + _mm(ad, a2_ref[d], NN_DIMS, 3))
        kd_o[...] = k * (1.0 + (a - 1.0) * ka)
        b_o[...] = kap * a


def rwkv_prep(z_all, n_latent_rows, seq_len, mu_p, k_k, k_a, r_k, w0, a0, w2p, a2p, g2):
    M = z_all.shape[0]
    tm = RW_PREP_TM
    assert M % tm == 0 and n_latent_rows % tm == 0 and seq_len % tm == 0
    widths = (RW_W, RW_W, RW_W, RW_LR_W, RW_GATE_RANK)
    offs = (C_RR, C_RK, C_RV, C_RLR, C_RGD)
    n_halo_blocks = M // RW_HALO
    z_specs, prev_specs, next_specs, mu_specs = [], [], [], []
    for w, o in zip(widths, offs):
        assert o % w == 0
        cb = o // w
        z_specs.append(pl.BlockSpec((tm, w), lambda i, cb=cb: (i, cb)))
        prev_specs.append(pl.BlockSpec(
            (RW_HALO, w), lambda i, cb=cb: (jnp.maximum(i * (tm // RW_HALO) - 1, 0), cb)))
        next_specs.append(pl.BlockSpec(
            (RW_HALO, w), lambda i, cb=cb: (jnp.minimum((i + 1) * (tm // RW_HALO), n_halo_blocks - 1), cb)))
        mu_specs.append(pl.BlockSpec((1, w), lambda i: (0, 0)))

    def full(shape):
        return pl.BlockSpec(shape, lambda i: (0,) * len(shape))

    par_specs = [full((1, RW_W))] * 3 + [full((2, RW_W))] * 2 + [full((2, LANES, RW_W))] * 2 \
        + [full((RW_GATE_RANK, RW_W))]
    out_spec = pl.BlockSpec((tm, RW_W), lambda i: (i, 0))
    return pl.pallas_call(
        functools.partial(_rw_prep_kernel, tm=tm, n_latent_tiles=n_latent_rows // tm,
                          tiles_per_seq=seq_len // tm),
        grid=(M // tm,),
        in_specs=z_specs + prev_specs + next_specs + mu_specs + par_specs,
        out_specs=[out_spec] * 11,
        out_shape=[jax.ShapeDtypeStruct((M, RW_W), F32)] * 11,
        compiler_params=pltpu.CompilerParams(
            dimension_semantics=("arbitrary",), vmem_limit_bytes=V7X_VMEM_LIMIT_BYTES),
        name="rwkv_prep",
    )(*([z_all] * 15), *mu_p, k_k, k_a, r_k, w0, a0, w2p, a2p, g2)


def _rw_chunk_kernel(*refs, reverse, n_par, n_chunks, passes, finish):
    r_ref, v_ref, kap_ref, lw_ref, kd_ref, bet_ref, s0_ref = refs[:7]
    if finish:
        yo_ref, bonus_ref, g_ref, gam_ref, gnb_ref = refs[7:12]
    y_ref, sfin_ref, s_scr = refs[-3:]
    _rw_chunk_body(r_ref, v_ref, kap_ref, lw_ref, kd_ref, bet_ref, s0_ref, y_ref, sfin_ref, s_scr,
                   (yo_ref, bonus_ref, g_ref, gam_ref, gnb_ref) if finish else None,
                   reverse=reverse, n_par=n_par, n_chunks=n_chunks, passes=passes)


def _rw_chunk_body(r_ref, v_ref, kap_ref, lw_ref, kd_ref, bet_ref, s0_ref, y_ref, sfin_ref, s_scr, fin_refs,
                   *, reverse, n_par, n_chunks, passes):
    C = RW_CHUNK
    n = pl.program_id(2)
    P = range(n_par)

    @pl.when(n == 0)
    def _():
        s_scr[...] = s0_ref[0, 0]

    row = lax.broadcasted_iota(jnp.int32, (C, LANES), 0)
    lane = lax.broadcasted_iota(jnp.int32, (C, LANES), 1)
    col = lane % RW_HEAD
    head0 = lane < RW_HEAD
    strict = (col > row) if reverse else (col < row)
    incl = (col >= row) if reverse else (col <= row)
    ti = lax.broadcasted_iota(jnp.int32, (C, C), 0)
    tj = lax.broadcasted_iota(jnp.int32, (C, C), 1)
    tri = jnp.where((tj >= ti) if reverse else (tj <= ti), 1.0, 0.0).astype(BF16)
    level_masks = []
    s = 1
    while s < C:
        level_masks.append(strict & (row // (2 * s) == col // (2 * s)) & (row // s != col // s))
        s *= 2
    eye = jnp.where(row == col, 1.0, 0.0)
    brow = lax.broadcasted_iota(jnp.int32, (LANES, LANES), 0) // RW_HEAD
    bcol = lax.broadcasted_iota(jnp.int32, (LANES, LANES), 1) // RW_HEAD
    same_head = brow == bcol

    def sl(x, p):
        return x[:, p * LANES:(p + 1) * LANES]

    def bd(x):
        return jnp.concatenate([jnp.where(head0, x, 0.0), jnp.where(head0, 0.0, x)], axis=0)

    def pp(a_pair, b_pair):
        return _mm(a_pair, bd(b_pair), NN_DIMS, passes)

    lw_all = lw_ref[...]
    cs_all = _cumsum_rows(tri, lw_all)
    tot_all = cs_all[0:1, :] if reverse else cs_all[C - 1:C, :]
    inv_all = jnp.exp(-cs_all)
    fin_all = jnp.exp(tot_all - cs_all)
    kd_all = kd_ref[...]
    bet_all = bet_ref[...]
    kap_t = kap_ref[...] * jnp.exp(cs_all - lw_all)
    r_t = r_ref[...] * jnp.exp(cs_all)
    k_t = kd_all * inv_all
    b_t = bet_all * inv_all
    k_h = kd_all * fin_all
    b_h = bet_all * fin_all
    dec = jnp.exp(tot_all)
    v_all = v_ref[...]

    S = [s_scr[p] for p in P]
    V = [sl(v_all, p) for p in P]
    lhs = [jnp.concatenate([sl(kap_t, p), sl(r_t, p)], axis=0) for p in P]
    a_k = [_mm(lhs[p], bd(sl(k_t, p)), NT_DIMS, passes) for p in P]
    a_b = [_mm(lhs[p], bd(sl(b_t, p)), NT_DIMS, passes) for p in P]
    s0_both = [_mm(lhs[p], S[p], NT_DIMS, passes) for p in P]
    lmat = [jnp.where(strict, a_b[p][:C], 0.0) for p in P]
    t_inv = [eye - jnp.where(level_masks[0], lmat[p], 0.0) for p in P]
    for m in level_masks[1:]:
        tmp = [pp(jnp.where(m, lmat[p], 0.0), t_inv[p]) for p in P]
        t_inv = [t_inv[p] - pp(t_inv[p], tmp[p]) for p in P]
    x = [s0_both[p][:C] + pp(jnp.where(strict, a_k[p][:C], 0.0), V[p]) for p in P]
    u = [pp(t_inv[p], x[p]) for p in P]
    y = [s0_both[p][C:] + pp(jnp.where(incl, a_k[p][C:], 0.0), V[p])
         - pp(jnp.where(incl, a_b[p][C:], 0.0), u[p]) for p in P]
    upd = [_mm(V[p], sl(k_h, p), TN_DIMS, passes) - _mm(u[p], sl(b_h, p), TN_DIMS, passes) for p in P]
    for p in P:
        s_scr[p] = S[p] * sl(dec, p) + jnp.where(same_head, upd[p], 0.0)
    if fin_refs is None:
        y_ref[...] = jnp.concatenate(y, axis=1)
    else:
        yo_ref, bonus_ref, g_ref, gam_ref, gnb_ref = fin_refs
        ys = jnp.concatenate(y, axis=1) + yo_ref[...]
        cen = ys - _heads_sum(ys) * (1.0 / RW_HEAD)
        var = _heads_sum(cen * cen) * (1.0 / RW_HEAD)
        out = cen * lax.rsqrt(var + RW_GN_EPS) * gam_ref[...] + gnb_ref[...] + bonus_ref[...]
        y_ref[...] = out * g_ref[...]

    @pl.when(n == n_chunks - 1)
    def _():
        sfin_ref[0, 0] = s_scr[...]


def rwkv_chunk_scan(r, v, kap, lw, kd, bet, s0, *, batch, seq_len, row0, reverse, finish_with=None):
    n_par = RW_PAIRS_PER_STEP
    W = r.shape[1]
    G = W // (n_par * LANES)
    N = seq_len // RW_CHUNK
    blk0 = row0 // RW_CHUNK

    def tb(n):
        return N - 1 - n if reverse else n

    seq_in = pl.BlockSpec((RW_CHUNK, n_par * LANES), lambda b, g, n: (blk0 + b * N + tb(n), g))
    seq_out = pl.BlockSpec((RW_CHUNK, n_par * LANES), lambda b, g, n: (b * N + tb(n), g))
    st_spec = pl.BlockSpec((1, 1, n_par, LANES, LANES), lambda b, g, n: (b, g, 0, 0, 0))
    in_specs = [seq_in] * 6 + [st_spec]
    args = [r, v, kap, lw, kd, bet, s0]
    if finish_with is not None:
        y_other, bonus, gate, gamma, beta = finish_with
        vec_spec = pl.BlockSpec((1, n_par * LANES), lambda b, g, n: (0, g))
        in_specs += [seq_out, seq_in, seq_in, vec_spec, vec_spec]
        args += [y_other, bonus, gate, gamma, beta]
    return pl.pallas_call(
        functools.partial(_rw_chunk_kernel, reverse=reverse, n_par=n_par, n_chunks=N, passes=RW_PASSES,
                          finish=finish_with is not None),
        grid=(batch, G, N),
        in_specs=in_specs,
        out_specs=[seq_out, st_spec],
        out_shape=[jax.ShapeDtypeStruct((batch * seq_len, W), F32),
                   jax.ShapeDtypeStruct((batch, G, n_par, LANES, LANES), F32)],
        scratch_shapes=[pltpu.VMEM((n_par, LANES, LANES), F32)],
        compiler_params=pltpu.CompilerParams(
            dimension_semantics=("arbitrary", "arbitrary", "arbitrary"),
            vmem_limit_bytes=V7X_VMEM_LIMIT_BYTES),
        name="rwkv_chunk_scan_rev" if reverse else "rwkv_chunk_scan",
    )(*args)


def _pad_rows(w, rows):
    return jnp.pad(w.astype(F32), ((0, 0), (0, rows - w.shape[1]), (0, 0)))


def rwkv_mixer(z_all, B, S, Lc, mu, w0, w2, a0, a2, g2, k_k, k_a, r_k, gn_g, gn_b, with_ctx):
    assert Lc == RW_PREP_TM
    mu = mu.astype(F32)
    o = np.cumsum((0,) + RW_SPLITS)
    lr_pad = jnp.zeros((LANES - RW_DECAY_RANK,), F32)
    mu_lr = jnp.concatenate([mu[o[3]:o[3] + RW_DECAY_RANK], lr_pad, mu[o[3] + RW_DECAY_RANK:o[4]], lr_pad,
                             mu[o[4]:o[4] + RW_A_RANK], lr_pad, mu[o[4] + RW_A_RANK:o[5]], lr_pad])
    mu_p = [mu[o[0]:o[1]][None], mu[o[1]:o[2]][None], mu[o[2]:o[3]][None], mu_lr[None], mu[o[5]:o[6]][None]]
    row = lambda t: t.astype(F32).reshape(1, RW_W)
    prep = rwkv_prep(z_all, B * S, S, mu_p, row(k_k), row(k_a), row(r_k), w0.astype(F32), a0.astype(F32),
                     _pad_rows(w2, LANES), _pad_rows(a2, LANES), g2.astype(F32))
    r, v, kap, lw0, kd0, b0, lw1, kd1, b1, bonus, gate = prep
    fin = (bonus, gate, row(gn_g), row(gn_b))
    s0 = jnp.zeros((B, RW_W // (RW_PAIRS_PER_STEP * LANES), RW_PAIRS_PER_STEP, LANES, LANES), F32)
    ctx = dict(batch=B, seq_len=Lc, row0=B * S)
    lat = dict(batch=B, seq_len=S, row0=0)
    yc_f, sc_f = rwkv_chunk_scan(r, v, kap, lw0, kd0, b0, s0, reverse=False, **ctx)
    yc, sc_b = rwkv_chunk_scan(r, v, kap, lw1, kd1, b1, s0, reverse=True, finish_with=(yc_f,) + fin, **ctx)
    yl_f, _ = rwkv_chunk_scan(r, v, kap, lw0, kd0, b0, sc_f, reverse=False, **lat)
    yl, _ = rwkv_chunk_scan(r, v, kap, lw1, kd1, b1, sc_b, reverse=True, finish_with=(yl_f,) + fin, **lat)
    return yl, (yc if with_ctx else None)


def merge_branches(ys, gates, w_branch, w_out):
    g = jax.nn.sigmoid(gates)
    acc = 0.0
    for i in range(N_BRANCH):
        acc = acc + g[:, i * D_MODEL:(i + 1) * D_MODEL] * matmul(ys[i], w_branch[i])
    return matmul(acc, w_out)


def relayout_w_in(w):
    o = IN_OFFSETS
    rw = o[8] + np.cumsum((0,) + RW_SPLITS)
    zeros = lambda n: jnp.zeros((w.shape[0], n), w.dtype)
    seg = lambda lo, n: w[:, lo:lo + n]
    lr_pad = LANES - RW_DECAY_RANK
    pieces = [
        seg(o[0], 3 * NA_W),
        seg(o[5], GLA_V), seg(o[6], GLA_V),
        seg(rw[0], 3 * RW_W),
        seg(o[9], N_BRANCH * D_MODEL),
        seg(o[3], GLA_QK), seg(o[4], GLA_QK),
        seg(rw[3], RW_DECAY_RANK), zeros(lr_pad), seg(rw[3] + RW_DECAY_RANK, RW_DECAY_RANK), zeros(lr_pad),
        seg(rw[4], RW_A_RANK), zeros(lr_pad), seg(rw[4] + RW_A_RANK, RW_A_RANK), zeros(lr_pad),
        seg(rw[5], RW_GATE_RANK),
        seg(o[7], 2 * GLA_GATE_RANK),
    ]
    used = sum(x.shape[1] for x in pieces)
    assert used == C_GGD + 2 * GLA_GATE_RANK
    return jnp.concatenate(pieces + [zeros(Z_W - used)], axis=1).astype(BF16)


def token_mixers(h, hc, p, with_ctx):
    B, S, D = h.shape
    Lc = hc.shape[1]
    M_lat = B * S
    h_all = jnp.concatenate([h.reshape(-1, D), hc.reshape(-1, D)], axis=0)
    z_all = matmul(h_all, relayout_w_in(p['w_in']), tn=1024)
    y_na, yc_na = na_mixer(z_all, B, S, Lc, p['na_q_norm'], p['na_k_norm'], p['na_rpb'], with_ctx)
    y_gla, yc_gla = gla_mixer(z_all, B, S, Lc, p['gla_gate_w2'], p['gla_gate_b'], p['gla_norm_g'], with_ctx)
    y_rw, yc_rw = rwkv_mixer(z_all, B, S, Lc, p['rw_mu'], p['rw_w0'], p['rw_w2'], p['rw_a0'], p['rw_a2'],
                             p['rw_g2'], p['rw_k_k'], p['rw_k_a'], p['rw_r_k'], p['rw_gn_g'],
                             p['rw_gn_b'], with_ctx)
    gates = z_all[:, C_GATE:C_GATE + N_BRANCH * D_MODEL]
    y = merge_branches((y_na, y_gla, y_rw), gates[:M_lat], p['w_branch'], p['w_out']).reshape(B, S, D)
    yc = None
    if with_ctx:
        yc = merge_branches((yc_na, yc_gla, yc_rw), gates[M_lat:], p['w_branch'], p['w_out']).reshape(B, Lc, D)
    return y, yc


def moe_ffn(h, router_w, router_bias, w_gate, w_up, w_down):
    shp = h.shape
    t = h.reshape(-1, shp[-1])
    s = jax.nn.sigmoid(jnp.dot(t.astype(F32), router_w.astype(F32), precision=lax.Precision.HIGHEST))
    biased = s + router_bias.astype(F32)
    grp = biased.reshape(-1, N_GROUPS, EXPERTS_PER_GROUP)
    grp_score = jnp.sum(lax.top_k(grp, TOP_K)[0], axis=-1)
    g_sel = jnp.argmax(grp_score, axis=-1)
    in_group = (jnp.arange(N_EXPERTS) // EXPERTS_PER_GROUP)[None, :] == g_sel[:, None]
    _, idx = lax.top_k(jnp.where(in_group, biased, -jnp.inf), TOP_K)
    w_sel = jnp.take_along_axis(s, idx, axis=-1)
    w_sel = w_sel / jnp.sum(w_sel, axis=-1, keepdims=True)
    comb = jnp.sum(jax.nn.one_hot(idx, N_EXPERTS, dtype=F32) * w_sel[..., None], axis=1)
    y = moe_experts(t, comb, w_gate, w_up, w_down)
    return y.reshape(shp)


def kernel(x, c, ctx, c_ctx, ada_w, ada_b, norm1_g, norm2_g, w_in, na_q_norm, na_k_norm, na_rpb,
           gla_gate_w2, gla_gate_b, gla_norm_g, rw_mu, rw_w0, rw_w2, rw_a0, rw_a2, rw_g2,
           rw_k_k, rw_k_a, rw_r_k, rw_gn_g, rw_gn_b, w_branch, w_out, router_w, router_bias,
           moe_w_gate, moe_w_up, moe_w_down):
    for l in range(DEPTH):
        with_ctx = l < DEPTH - 1
        p = {
            'w_in': w_in[l], 'na_q_norm': na_q_norm[l], 'na_k_norm': na_k_norm[l], 'na_rpb': na_rpb[l],
            'gla_gate_w2': gla_gate_w2[l], 'gla_gate_b': gla_gate_b[l], 'gla_norm_g': gla_norm_g[l],
            'rw_mu': rw_mu[l], 'rw_w0': rw_w0[l], 'rw_w2': rw_w2[l], 'rw_a0': rw_a0[l], 'rw_a2': rw_a2[l],
            'rw_g2': rw_g2[l], 'rw_k_k': rw_k_k[l], 'rw_k_a': rw_k_a[l], 'rw_r_k': rw_r_k[l],
            'rw_gn_g': rw_gn_g[l], 'rw_gn_b': rw_gn_b[l], 'w_branch': w_branch[l], 'w_out': w_out[l],
        }
        mod = jax.nn.silu(c) @ ada_w[l] + ada_b[l]
        mod_c = jax.nn.silu(c_ctx) @ ada_w[l] + ada_b[l]
        sh1, sc1, g1, sh2, sc2, g2 = jnp.split(mod[:, None, :], 6, axis=-1)
        csh1, csc1, cg1, csh2, csc2, cg2 = jnp.split(mod_c, 6)

        h = rms_norm(x, norm1_g[l]) * (1.0 + sc1) + sh1
        hc = rms_norm(ctx, norm1_g[l]) * (1.0 + csc1) + csh1
        y, yc = token_mixers(h, hc, p, with_ctx)
        x = x + g1 * y
        h2 = rms_norm(x, norm2_g[l]) * (1.0 + sc2) + sh2
        x = x + g2 * moe_ffn(h2, router_w, router_bias, moe_w_gate[l], moe_w_up[l], moe_w_down[l])

        if with_ctx:
            ctx = ctx + cg1 * yc
            hc2 = rms_norm(ctx, norm2_g[l]) * (1.0 + csc2) + csh2
            ctx = ctx + cg2 * moe_ffn(hc2, router_w, router_bias, moe_w_gate[l], moe_w_up[l], moe_w_down[l])
    return x
```

```python
import functools

import jax
import jax.numpy as jnp
from jax import lax
import numpy as np
from jax.experimental import pallas as pl
from jax.experimental.pallas import tpu as pltpu

D_MODEL = 2048
DEPTH = 2
GRID_W = 64

NA_HEADS = 16
NA_HEAD_DIM = 64
NA_WIN_R_MAX = 8
NA_WIN_C = 16
NA_W = NA_HEADS * NA_HEAD_DIM

GLA_HEADS = 4
GLA_DK = 128
GLA_DV = 256
GLA_QK = GLA_HEADS * GLA_DK
GLA_V = GLA_HEADS * GLA_DV
GLA_GATE_RANK = 16
GLA_GATE_TAU = 16.0
GLA_CHUNK = 64

RW_HEADS = 16
RW_HEAD = 64
RW_W = RW_HEADS * RW_HEAD
RW_DECAY_RANK = 96
RW_A_RANK = 96
RW_GATE_RANK = 256
RW_SPLITS = (RW_W, RW_W, RW_W, 2 * RW_DECAY_RANK, 2 * RW_A_RANK, RW_GATE_RANK)
RW_IN = sum(RW_SPLITS)
RW_GN_EPS = 64e-5
RW_CHUNK = 64
RW_PASSES = 1
RW_PAIRS_PER_STEP = 8

N_BRANCH = 3
BRANCH_W = 1024
IN_SPLITS = (NA_W, NA_W, NA_W, GLA_QK, GLA_QK, GLA_V, GLA_V, 2 * GLA_GATE_RANK, RW_IN, N_BRANCH * D_MODEL)
D_IN = sum(IN_SPLITS)
IN_OFFSETS = tuple(int(o) for o in np.cumsum((0,) + IN_SPLITS[:-1]))

LANES = 128
RW_LR_W = 4 * LANES
C_NAQ, C_NAK, C_NAV = 0, 1024, 2048
C_GV, C_GR = 3072, 4096
C_RR, C_RK, C_RV = 5120, 6144, 7168
C_GATE = 8192
C_GQ, C_GK = 14336, 14848
C_RLR, C_RGD, C_GGD = 15360, 15872, 16128
Z_W = 16384

N_EXPERTS = 16
N_GROUPS = 4
EXPERTS_PER_GROUP = N_EXPERTS // N_GROUPS
TOP_K = 2
D_EXPERT = 512

ROPE_BASE = 10000.0
EPS = 1e-6
F32 = jnp.float32
BF16 = jnp.bfloat16

V7X_VMEM_LIMIT_BYTES = 48 * 1024 * 1024

NEG_BIG = -1e30
NT_DIMS = (((1,), (1,)), ((), ()))
TN_DIMS = (((0,), (0,)), ((), ()))
NN_DIMS = (((1,), (0,)), ((), ()))


def _split_bf16(x):
    hi = x.astype(BF16)
    lo = (x - hi.astype(F32)).astype(BF16)
    return hi, lo


def _mm(a, b, dims=NN_DIMS, passes=1):
    if passes == 1:
        return lax.dot_general(a.astype(BF16), b.astype(BF16), dims, preferred_element_type=F32)
    a_hi, a_lo = _split_bf16(a)
    b_hi, b_lo = _split_bf16(b)
    out = lax.dot_general(a_hi, b_hi, dims, preferred_element_type=F32)
    out = out + lax.dot_general(a_hi, b_lo, dims, preferred_element_type=F32)
    return out + lax.dot_general(a_lo, b_hi, dims, preferred_element_type=F32)


def _cumsum_rows(tri, x):
    hi, lo = _split_bf16(x)
    return jnp.dot(tri, hi, preferred_element_type=F32) + jnp.dot(tri, lo, preferred_element_type=F32)


def _mm_kernel(a_ref, b_ref, o_ref):
    o_ref[...] = jnp.dot(a_ref[...], b_ref[...], preferred_element_type=F32).astype(o_ref.dtype)


def matmul(a, b, *, tm=512, tn=512, out_dtype=F32):
    M, K = a.shape
    _, N = b.shape
    assert M % tm == 0 and N % tn == 0, (M, N, tm, tn)
    return pl.pallas_call(
        _mm_kernel,
        grid=(N // tn, M // tm),
        in_specs=[pl.BlockSpec((tm, K), lambda n, m: (m, 0)),
                  pl.BlockSpec((K, tn), lambda n, m: (0, n))],
        out_specs=pl.BlockSpec((tm, tn), lambda n, m: (m, n)),
        out_shape=jax.ShapeDtypeStruct((M, N), out_dtype),
        compiler_params=pltpu.CompilerParams(
            dimension_semantics=("arbitrary", "arbitrary"),
            vmem_limit_bytes=V7X_VMEM_LIMIT_BYTES),
        name="matmul",
    )(a.astype(BF16), b.astype(BF16))


def _moe_kernel(h_ref, comb_ref, wg_ref, wu_ref, wd_ref, x_ref, mod_ref, o_ref, acc_ref):
    e = pl.program_id(1)

    @pl.when(e == 0)
    def _():
        acc_ref[...] = jnp.zeros_like(acc_ref)

    h = h_ref[...]
    g = jnp.dot(h, wg_ref[0], preferred_element_type=F32)
    u = jnp.dot(h, wu_ref[0], preferred_element_type=F32)
    comb = comb_ref[...]
    lane = lax.broadcasted_iota(jnp.int32, comb.shape, 1)
    ce = jnp.sum(jnp.where(lane == e, comb, 0.0), axis=1, keepdims=True)
    act = (g * jax.nn.sigmoid(g)) * u * ce
    acc_ref[...] += jnp.dot(act.astype(BF16), wd_ref[0], preferred_element_type=F32)

    @pl.when(e == N_EXPERTS - 1)
    def _():
        o_ref[...] = x_ref[...] + mod_ref[0, N_MOD - 1:N_MOD, :] * acc_ref[...]


def moe_experts(h, comb, w_gate, w_up, w_down, x, mod, *, rows_per_cond=None, fixed_row=None, tm=512):
    M, D = h.shape
    assert M % tm == 0
    cond = _cond_of_tile(rows_per_cond, tm, fixed_row)
    return pl.pallas_call(
        _moe_kernel,
        grid=(M // tm, N_EXPERTS),
        in_specs=[pl.BlockSpec((tm, D), lambda m, e: (m, 0)),
                  pl.BlockSpec((tm, N_EXPERTS), lambda m, e: (m, 0)),
                  pl.BlockSpec((1, D, D_EXPERT), lambda m, e: (e, 0, 0)),
                  pl.BlockSpec((1, D, D_EXPERT), lambda m, e: (e, 0, 0)),
                  pl.BlockSpec((1, D_EXPERT, D), lambda m, e: (e, 0, 0)),
                  pl.BlockSpec((tm, D), lambda m, e: (m, 0)),
                  pl.BlockSpec((1, N_MOD, D), lambda m, e: (cond(m), 0, 0))],
        out_specs=pl.BlockSpec((tm, D), lambda m, e: (m, 0)),
        out_shape=jax.ShapeDtypeStruct((M, D), F32),
        scratch_shapes=[pltpu.VMEM((tm, D), F32)],
        compiler_params=pltpu.CompilerParams(
            dimension_semantics=("arbitrary", "arbitrary"),
            vmem_limit_bytes=V7X_VMEM_LIMIT_BYTES),
        name="moe_experts",
    )(h, comb, w_gate, w_up, w_down, x, mod)


N_COND = 8
N_MOD = 6
ROW_TM = 256


def _ada_kernel(c_ref, w_ref, b_ref, o_ref):
    c = c_ref[...]
    o_ref[0] = _mm(c * jax.nn.sigmoid(c), w_ref[0], NN_DIMS, 3) + b_ref[0]


def ada_modulation(c_rows, ada_w, ada_b, *, tn=1024):
    L, D, N = ada_w.shape
    return pl.pallas_call(
        _ada_kernel,
        grid=(L, N // tn),
        in_specs=[pl.BlockSpec((N_COND, D), lambda l, n: (0, 0)),
                  pl.BlockSpec((1, D, tn), lambda l, n: (l, 0, n)),
                  pl.BlockSpec((1, 1, tn), lambda l, n: (l, 0, n))],
        out_specs=pl.BlockSpec((1, N_COND, tn), lambda l, n: (l, 0, n)),
        out_shape=jax.ShapeDtypeStruct((L, N_COND, N), F32),
        compiler_params=pltpu.CompilerParams(
            dimension_semantics=("arbitrary", "arbitrary"), vmem_limit_bytes=V7X_VMEM_LIMIT_BYTES),
        name="ada_modulation",
    )(c_rows, ada_w, ada_b.reshape(L, 1, N))


def _norm_mod(x, g, shift, scale):
    y = x * lax.rsqrt(jnp.mean(x * x, axis=-1, keepdims=True) + EPS)
    return y * g * (1.0 + scale) + shift


def _prenorm_kernel(x_ref, c_ref, g_ref, mod_ref, o_ref, *, n_latent_tiles):
    def emit(src_ref):
        o_ref[...] = _norm_mod(src_ref[...], g_ref[...], mod_ref[0, 0:1, :], mod_ref[0, 1:2, :]).astype(o_ref.dtype)

    @pl.when(pl.program_id(0) < n_latent_tiles)
    def _():
        emit(x_ref)

    @pl.when(pl.program_id(0) >= n_latent_tiles)
    def _():
        emit(c_ref)


def _top2_sum(a, b, c, d):
    hi1, lo1 = jnp.maximum(a, b), jnp.minimum(a, b)
    hi2, lo2 = jnp.maximum(c, d), jnp.minimum(c, d)
    return jnp.maximum(hi1, hi2) + jnp.maximum(jnp.minimum(hi1, hi2), jnp.maximum(lo1, lo2))


def _prenorm_router_kernel(x_ref, g_ref, mod_ref, rw_ref, rb_ref, o_ref, comb_ref):
    h = _norm_mod(x_ref[...], g_ref[...], mod_ref[0, 3:4, :], mod_ref[0, 4:5, :])
    o_ref[...] = h.astype(o_ref.dtype)
    s = jax.nn.sigmoid(_mm(rw_ref[...], h, NT_DIMS, 3))
    biased = s + rb_ref[...]
    rows = [biased[e:e + 1, :] for e in range(N_EXPERTS)]
    G = EXPERTS_PER_GROUP
    score = [_top2_sum(*rows[g * G:(g + 1) * G]) for g in range(N_GROUPS)]
    picked = []
    for e in range(N_EXPERTS):
        g = e // G
        ok = None
        for g2 in range(N_GROUPS):
            if g2 != g:
                t = (score[g] > score[g2]) if g2 < g else (score[g] >= score[g2])
                ok = t if ok is None else jnp.logical_and(ok, t)
        rank = 0.0
        for e2 in range(g * G, (g + 1) * G):
            if e2 != e:
                ahead = (rows[e2] >= rows[e]) if e2 < e else (rows[e2] > rows[e])
                rank = rank + jnp.where(ahead, 1.0, 0.0)
        picked.append(jnp.where(jnp.logical_and(ok, rank < TOP_K), s[e:e + 1, :], 0.0))
    w = jnp.concatenate(picked, axis=0)
    comb_ref[...] = w / jnp.sum(w, axis=0, keepdims=True)


def _cond_of_tile(rows_per_cond, tile_rows, fixed_row):
    if fixed_row is not None:
        return lambda m: fixed_row
    assert rows_per_cond % tile_rows == 0
    return lambda m: m // (rows_per_cond // tile_rows)


def prenorm(x, ctx, gain, mod, *, seq_len):
    n_lat, n_ctx = x.shape[0] // ROW_TM, ctx.shape[0] // ROW_TM
    n_batch = x.shape[0] // seq_len
    cond = _cond_of_tile(seq_len, ROW_TM, None)
    return pl.pallas_call(
        functools.partial(_prenorm_kernel, n_latent_tiles=n_lat),
        grid=(n_lat + n_ctx,),
        in_specs=[pl.BlockSpec((ROW_TM, D_MODEL), lambda i: (jnp.minimum(i, n_lat - 1), 0)),
                  pl.BlockSpec((ROW_TM, D_MODEL), lambda i: (jnp.maximum(i - n_lat, 0), 0)),
                  pl.BlockSpec((1, D_MODEL), lambda i: (0, 0)),
                  pl.BlockSpec((1, N_MOD, D_MODEL), lambda i: (jnp.where(i < n_lat, cond(i), n_batch), 0, 0))],
        out_specs=pl.BlockSpec((ROW_TM, D_MODEL), lambda i: (i, 0)),
        out_shape=jax.ShapeDtypeStruct((x.shape[0] + ctx.shape[0], D_MODEL), BF16),
        compiler_params=pltpu.CompilerParams(dimension_semantics=("arbitrary",)),
        name="prenorm",
    )(x, ctx, gain, mod)


def prenorm_router(x, gain, mod, router_w_t, router_bias, *, rows_per_cond=None, fixed_row=None):
    M = x.shape[0]
    cond = _cond_of_tile(rows_per_cond, ROW_TM, fixed_row)
    x_spec = pl.BlockSpec((ROW_TM, D_MODEL), lambda i: (i, 0))
    return pl.pallas_call(
        _prenorm_router_kernel,
        grid=(M // ROW_TM,),
        in_specs=[x_spec,
                  pl.BlockSpec((1, D_MODEL), lambda i: (0, 0)),
                  pl.BlockSpec((1, N_MOD, D_MODEL), lambda i: (cond(i), 0, 0)),
                  pl.BlockSpec((N_EXPERTS, D_MODEL), lambda i: (0, 0)),
                  pl.BlockSpec((N_EXPERTS, 1), lambda i: (0, 0))],
        out_specs=[x_spec, pl.BlockSpec((N_EXPERTS, ROW_TM), lambda i: (0, i))],
        out_shape=[jax.ShapeDtypeStruct((M, D_MODEL), BF16),
                   jax.ShapeDtypeStruct((N_EXPERTS, M), F32)],
        compiler_params=pltpu.CompilerParams(dimension_semantics=("arbitrary",)),
        name="prenorm_router",
    )(x, gain, mod, router_w_t, router_bias)


def _merge_kernel(y0_ref, y1_ref, y2_ref, g0_ref, g1_ref, g2_ref, wb_ref, o_ref):
    acc = 0.0
    for i, (y_ref, g_ref) in enumerate(((y0_ref, g0_ref), (y1_ref, g1_ref), (y2_ref, g2_ref))):
        acc = acc + jax.nn.sigmoid(g_ref[...]) * jnp.dot(y_ref[...].astype(BF16), wb_ref[i],
                                                         preferred_element_type=F32)
    o_ref[...] = acc.astype(o_ref.dtype)


def merge_branches(ys, z_all, row0, w_branch, *, tm=512, tn=512):
    M = ys[0].shape[0]
    assert M % tm == 0 and row0 % tm == 0 and C_GATE % tn == 0 and D_MODEL % tn == 0
    row_blk0 = row0 // tm
    y_spec = pl.BlockSpec((tm, BRANCH_W), lambda m, n: (m, 0))

    def gate_spec(i):
        return pl.BlockSpec((tm, tn), lambda m, n: (row_blk0 + m, (C_GATE + i * D_MODEL) // tn + n))

    return pl.pallas_call(
        _merge_kernel,
        grid=(M // tm, D_MODEL // tn),
        in_specs=[y_spec] * 3 + [gate_spec(i) for i in range(N_BRANCH)]
        + [pl.BlockSpec((N_BRANCH, BRANCH_W, tn), lambda m, n: (0, 0, n))],
        out_specs=pl.BlockSpec((tm, tn), lambda m, n: (m, n)),
        out_shape=jax.ShapeDtypeStruct((M, D_MODEL), BF16),
        compiler_params=pltpu.CompilerParams(
            dimension_semantics=("arbitrary", "arbitrary"), vmem_limit_bytes=V7X_VMEM_LIMIT_BYTES),
        name="merge_branches",
    )(*ys, z_all, z_all, z_all, w_branch.astype(BF16))


def _residual_mm_kernel(a_ref, w_ref, x_ref, mod_ref, o_ref, *, gate_row):
    y = jnp.dot(a_ref[...], w_ref[...], preferred_element_type=F32)
    o_ref[...] = x_ref[...] + mod_ref[0, gate_row:gate_row + 1, :] * y


def residual_matmul(a, w, x, mod, *, gate_row, rows_per_cond=None, fixed_row=None, tm=512, tn=512):
    M, K = a.shape
    N = w.shape[1]
    assert M % tm == 0 and N % tn == 0
    cond = _cond_of_tile(rows_per_cond, tm, fixed_row)
    return pl.pallas_call(
        functools.partial(_residual_mm_kernel, gate_row=gate_row),
        grid=(N // tn, M // tm),
        in_specs=[pl.BlockSpec((tm, K), lambda n, m: (m, 0)),
                  pl.BlockSpec((K, tn), lambda n, m: (0, n)),
                  pl.BlockSpec((tm, tn), lambda n, m: (m, n)),
                  pl.BlockSpec((1, N_MOD, tn), lambda n, m: (cond(m), 0, n))],
        out_specs=pl.BlockSpec((tm, tn), lambda n, m: (m, n)),
        out_shape=jax.ShapeDtypeStruct((M, N), F32),
        compiler_params=pltpu.CompilerParams(
            dimension_semantics=("arbitrary", "arbitrary"), vmem_limit_bytes=V7X_VMEM_LIMIT_BYTES),
        name="residual_matmul",
    )(a, w.astype(BF16), x, mod)


def _head_sum(x2):
    row = lax.broadcasted_iota(jnp.int32, (LANES, LANES), 0) // NA_HEAD_DIM
    col = lax.broadcasted_iota(jnp.int32, (LANES, LANES), 1) // NA_HEAD_DIM
    ones_bd = jnp.where(row == col, 1.0, 0.0).astype(BF16)
    return _cumsum_rows_right(x2, ones_bd)


def _cumsum_rows_right(x, m):
    hi, lo = _split_bf16(x)
    return jnp.dot(hi, m, preferred_element_type=F32) + jnp.dot(lo, m, preferred_element_type=F32)


def _na_qkv_kernel(x_ref, g_ref, o_ref):
    x = x_ref[...]

    @pl.when(pl.program_id(1) < 2)
    def _():
        ms = _heads_sum(x * x) * (1.0 / NA_HEAD_DIM)
        o_ref[...] = (x * lax.rsqrt(ms + EPS) * g_ref[...]).astype(o_ref.dtype)

    @pl.when(pl.program_id(1) == 2)
    def _():
        o_ref[...] = x.astype(o_ref.dtype)


def na_qkv(z, gains, *, tm=256):
    M = z.shape[0]
    return pl.pallas_call(
        _na_qkv_kernel,
        grid=(M // tm, 3),
        in_specs=[pl.BlockSpec((tm, NA_W), lambda i, j: (i, j)),
                  pl.BlockSpec((1, NA_W), lambda i, j: (0, jnp.minimum(j, 1)))],
        out_specs=pl.BlockSpec((tm, NA_W), lambda i, j: (i, j)),
        out_shape=jax.ShapeDtypeStruct((M, 3 * NA_W), BF16),
        compiler_params=pltpu.CompilerParams(dimension_semantics=("arbitrary", "arbitrary")),
        name="na_qkv",
    )(z, gains)


def _softmax_pv(s_list, v_list):
    m = s_list[0].max(axis=-1, keepdims=True)
    for s in s_list[1:]:
        m = jnp.maximum(m, s.max(axis=-1, keepdims=True))
    den = 0.0
    acc = 0.0
    for s, v in zip(s_list, v_list):
        p = jnp.exp(s - m)
        den = den + p.sum(axis=-1, keepdims=True)
        acc = acc + jnp.dot(p.astype(BF16), v, preferred_element_type=F32)
    return acc / den


def _na_kernel(q_ref, k_ref, v_ref, kc_ref, vc_ref, bias_ref, o_ref, *, n_rows, win_r):
    r = pl.program_id(2)
    rs = jnp.clip(r - win_r // 2, 0, n_rows - win_r)
    start = pl.multiple_of(rs * GRID_W, GRID_W)
    kw = k_ref[pl.ds(start, win_r * GRID_W), :]
    vw = v_ref[pl.ds(start, win_r * GRID_W), :]
    kc = kc_ref[...]
    vc = vc_ref[...]
    q = q_ref[...]
    lane = lax.broadcasted_iota(jnp.int32, q.shape, 1)
    outs = []
    for h in range(2):
        qh = jnp.where((lane // NA_HEAD_DIM) == h, q, jnp.zeros_like(q))
        s_loc = lax.dot_general(qh, kw, NT_DIMS, preferred_element_type=F32) + bias_ref[h, 0]
        s_ctx = lax.dot_general(qh, kc, NT_DIMS, preferred_element_type=F32)
        outs.append(_softmax_pv([s_loc, s_ctx], [vw, vc]))
    o_ref[...] = jnp.where(lane < NA_HEAD_DIM, outs[0], outs[1]).astype(o_ref.dtype)


def _ctx_attn_kernel(q_ref, k_ref, v_ref, o_ref):
    q = q_ref[...]
    k = k_ref[...]
    v = v_ref[...]
    lane = lax.broadcasted_iota(jnp.int32, q.shape, 1)
    outs = []
    for h in range(2):
        qh = jnp.where((lane // NA_HEAD_DIM) == h, q, jnp.zeros_like(q))
        s = lax.dot_general(qh, k, NT_DIMS, preferred_element_type=F32)
        outs.append(_softmax_pv([s], [v]))
    o_ref[...] = jnp.where(lane < NA_HEAD_DIM, outs[0], outs[1]).astype(o_ref.dtype)


def na_bias_table(rpb, n_rows, win_r):
    col = np.arange(GRID_W)
    col_start = np.clip(col - NA_WIN_C // 2, 0, GRID_W - NA_WIN_C)
    kcol = np.arange(GRID_W)
    valid = (kcol[None, :] >= col_start[:, None]) & (kcol[None, :] < col_start[:, None] + NA_WIN_C)
    dc = np.clip(kcol[None, :] - col[:, None] + NA_WIN_C - 1, 0, 2 * NA_WIN_C - 2)
    dr = np.arange(NA_WIN_R_MAX)[:, None] + np.arange(win_r)[None, :]
    dr = np.clip(dr, 0, 2 * NA_WIN_R_MAX - 2)
    t = rpb.astype(F32)[:, dr][:, :, :, dc]
    t = jnp.where(valid[None, None, None], t, NEG_BIG)
    t = t.transpose(0, 1, 3, 2, 4)
    return t.reshape(rpb.shape[0], NA_WIN_R_MAX, GRID_W, win_r * GRID_W)


def na_attention(qkv, rpb, B, S, Lc, with_ctx):
    n_rows = S // GRID_W
    win_r = min(NA_WIN_R_MAX, n_rows)
    assert win_r == NA_WIN_R_MAX and (B * S) % Lc == 0
    HP = NA_HEADS // 2
    bias = na_bias_table(rpb, n_rows, win_r)
    ctx_blk0 = (B * S) // Lc

    def variant(r):
        return jnp.clip(r - win_r // 2, 0, n_rows - win_r) - r + NA_WIN_R_MAX - 1

    y = pl.pallas_call(
        functools.partial(_na_kernel, n_rows=n_rows, win_r=win_r),
        grid=(HP, B, n_rows),
        in_specs=[pl.BlockSpec((GRID_W, LANES), lambda hp, b, r: (b * n_rows + r, hp)),
                  pl.BlockSpec((S, LANES), lambda hp, b, r: (b, HP + hp)),
                  pl.BlockSpec((S, LANES), lambda hp, b, r: (b, 2 * HP + hp)),
                  pl.BlockSpec((Lc, LANES), lambda hp, b, r: (ctx_blk0 + b, HP + hp)),
                  pl.BlockSpec((Lc, LANES), lambda hp, b, r: (ctx_blk0 + b, 2 * HP + hp)),
                  pl.BlockSpec((2, 1, GRID_W, win_r * GRID_W), lambda hp, b, r: (hp, variant(r), 0, 0))],
        out_specs=pl.BlockSpec((GRID_W, LANES), lambda hp, b, r: (b * n_rows + r, hp)),
        out_shape=jax.ShapeDtypeStruct((B * S, NA_W), F32),
        compiler_params=pltpu.CompilerParams(
            dimension_semantics=("arbitrary", "arbitrary", "arbitrary"),
            vmem_limit_bytes=V7X_VMEM_LIMIT_BYTES),
        name="na_attention",
    )(qkv, qkv, qkv, qkv, qkv, bias)
    yc = None
    if with_ctx:
        yc = pl.pallas_call(
            _ctx_attn_kernel,
            grid=(HP, B),
            in_specs=[pl.BlockSpec((Lc, LANES), lambda hp, b: (ctx_blk0 + b, hp)),
                      pl.BlockSpec((Lc, LANES), lambda hp, b: (ctx_blk0 + b, HP + hp)),
                      pl.BlockSpec((Lc, LANES), lambda hp, b: (ctx_blk0 + b, 2 * HP + hp))],
            out_specs=pl.BlockSpec((Lc, LANES), lambda hp, b: (b, hp)),
            out_shape=jax.ShapeDtypeStruct((B * Lc, NA_W), F32),
            compiler_params=pltpu.CompilerParams(dimension_semantics=("arbitrary", "arbitrary")),
            name="ctx_attention",
        )(qkv, qkv, qkv)
    return y, yc


def na_mixer(z_all, B, S, Lc, q_norm, k_norm, rpb, with_ctx):
    assert (C_NAQ, C_NAK, C_NAV) == (0, NA_W, 2 * NA_W)
    scale = NA_HEAD_DIM ** -0.5
    gains = jnp.concatenate([jnp.tile(q_norm.astype(F32) * scale, NA_HEADS),
                             jnp.tile(k_norm.astype(F32), NA_HEADS)])[None]
    y, yc = na_attention(na_qkv(z_all, gains), rpb, B, S, Lc, with_ctx)
    return y, yc


def _log_sigmoid(x):
    return jnp.minimum(x, 0.0) - jnp.log(1.0 + jnp.exp(-jnp.abs(x)))


def _gla_kernel(*refs, reverse, rope, finish, n_chunks):
    it = iter(refs)
    q_ref, k_ref, v_ref, gd_ref, w2_ref, gb_ref, s0_ref = (next(it) for _ in range(7))
    if rope:
        cos_ref, sin_ref = next(it), next(it)
    if finish:
        yo_ref, r_ref, g_ref = next(it), next(it), next(it)
    y_ref, sfin_ref, s_scr = next(it), next(it), next(it)

    C = GLA_CHUNK
    n = pl.program_id(1)
    H = range(GLA_HEADS)

    @pl.when(n == 0)
    def _():
        s_scr[...] = s0_ref[0]

    ti = lax.broadcasted_iota(jnp.int32, (C, C), 0)
    tj = lax.broadcasted_iota(jnp.int32, (C, C), 1)
    incl = (tj >= ti) if reverse else (tj <= ti)
    tri = jnp.where(incl, 1.0, 0.0).astype(BF16)

    pre = _mm(gd_ref[...], w2_ref[...], NN_DIMS, 3) + gb_ref[...]
    la = _log_sigmoid(pre) * (1.0 / GLA_GATE_TAU)
    cs = _cumsum_rows(tri, la)
    ref_i = C // 2 - 1 if reverse else C // 2
    last_i = 0 if reverse else C - 1
    b_ref = cs[ref_i:ref_i + 1, :]
    b_last = cs[last_i:last_i + 1, :]

    q = q_ref[...] * (GLA_DK ** -0.5)
    k = k_ref[...]
    if rope:
        lane = lax.broadcasted_iota(jnp.int32, q.shape, 1)
        first = (lane % (GLA_DK // 2)) < GLA_DK // 4
        cos = jnp.concatenate([cos_ref[...]] * GLA_HEADS, axis=1)
        sin = jnp.concatenate([sin_ref[...]] * GLA_HEADS, axis=1)

        def rot(x):
            partner = jnp.where(first, pltpu.roll(x, GLA_QK - GLA_DK // 4, axis=1),
                                pltpu.roll(x, GLA_DK // 4, axis=1))
            return x * cos + partner * sin

        q = rot(q)
        k = rot(k)
    qi = q * jnp.exp(cs - b_ref)
    kj = k * jnp.exp(b_ref - cs)
    qe = q * jnp.exp(cs)
    ke = k * jnp.exp(b_last - cs)
    dec = jnp.exp(b_last)
    v_all = v_ref[...]

    def hk(x, h):
        return x[:, h * GLA_DK:(h + 1) * GLA_DK]

    def hv(x, h):
        return x[:, h * GLA_DV:(h + 1) * GLA_DV]

    St = [s_scr[h] for h in H]
    V = [hv(v_all, h) for h in H]
    att = [jnp.where(incl, _mm(hk(qi, h), hk(kj, h), NT_DIMS), 0.0) for h in H]
    y = [_mm(att[h], V[h], NN_DIMS) + _mm(hk(qe, h), St[h], NT_DIMS) for h in H]
    kvt = [_mm(V[h], hk(ke, h), TN_DIMS) for h in H]
    for h in H:
        s_scr[h] = St[h] * hk(dec, h) + kvt[h]
    if finish:
        yo = yo_ref[...]
        r = r_ref[...]
        g = g_ref[...]
        outs = []
        for h in H:
            ys = y[h] + hv(yo, h)
            ms = jnp.mean(ys * ys, axis=-1, keepdims=True)
            rh = hv(r, h)
            outs.append(ys * lax.rsqrt(ms + EPS) * g * (rh * jax.nn.sigmoid(rh)))
        y_ref[...] = jnp.concatenate(outs, axis=1)
    else:
        y_ref[...] = jnp.concatenate(y, axis=1)

    @pl.when(n == n_chunks - 1)
    def _():
        sfin_ref[0] = s_scr[...]


def gla_rope_tables(seq_len):
    t = np.arange(seq_len)
    quarter = GLA_DK // 4
    inv = ROPE_BASE ** (-np.arange(0, 2 * quarter, 2, dtype=np.float64) / (2 * quarter))
    ar = (t // GRID_W)[:, None] * inv[None, :]
    ac = (t % GRID_W)[:, None] * inv[None, :]
    cos = np.concatenate([np.cos(ar), np.cos(ar), np.cos(ac), np.cos(ac)], axis=1)
    sin = np.concatenate([-np.sin(ar), np.sin(ar), -np.sin(ac), np.sin(ac)], axis=1)
    return jnp.asarray(cos, F32), jnp.asarray(sin, F32)


def gla_scan(z, w2pad, gate_b, s0, *, batch, seq_len, row0, reverse, rope, finish_with=None):
    N = seq_len // GLA_CHUNK
    blk0 = row0 // GLA_CHUNK
    off_q, off_k, off_v, off_r, off_gd = C_GQ, C_GK, C_GV, C_GR, C_GGD
    assert off_q % GLA_QK == 0 and off_k % GLA_QK == 0 and off_v % GLA_V == 0 and off_r % GLA_V == 0
    assert off_gd % LANES == 0

    def tb(n):
        return N - 1 - n if reverse else n

    def rb(b, n):
        return blk0 + b * N + tb(n)

    def ob(b, n):
        return b * N + tb(n)

    in_specs = [pl.BlockSpec((GLA_CHUNK, GLA_QK), lambda b, n: (rb(b, n), off_q // GLA_QK)),
                pl.BlockSpec((GLA_CHUNK, GLA_QK), lambda b, n: (rb(b, n), off_k // GLA_QK)),
                pl.BlockSpec((GLA_CHUNK, GLA_V), lambda b, n: (rb(b, n), off_v // GLA_V)),
                pl.BlockSpec((GLA_CHUNK, LANES), lambda b, n: (rb(b, n), off_gd // LANES)),
                pl.BlockSpec((LANES, GLA_QK), lambda b, n: (0, 0)),
                pl.BlockSpec((1, GLA_QK), lambda b, n: (0, 0)),
                pl.BlockSpec((1, GLA_HEADS, GLA_DV, GLA_DK), lambda b, n: (b, 0, 0, 0))]
    args = [z, z, z, z, w2pad, gate_b, s0]
    if rope:
        cos, sin = gla_rope_tables(seq_len)
        in_specs += [pl.BlockSpec((GLA_CHUNK, GLA_DK), lambda b, n: (tb(n), 0))] * 2
        args += [cos, sin]
    if finish_with is not None:
        y_other, norm_g = finish_with
        in_specs += [pl.BlockSpec((GLA_CHUNK, GLA_V), lambda b, n: (ob(b, n), 0)),
                     pl.BlockSpec((GLA_CHUNK, GLA_V), lambda b, n: (rb(b, n), off_r // GLA_V)),
                     pl.BlockSpec((1, GLA_DV), lambda b, n: (0, 0))]
        args += [y_other, z, norm_g]
    return pl.pallas_call(
        functools.partial(_gla_kernel, reverse=reverse, rope=rope, finish=finish_with is not None, n_chunks=N),
        grid=(batch, N),
        in_specs=in_specs,
        out_specs=[pl.BlockSpec((GLA_CHUNK, GLA_V), lambda b, n: (ob(b, n), 0)),
                   pl.BlockSpec((1, GLA_HEADS, GLA_DV, GLA_DK), lambda b, n: (b, 0, 0, 0))],
        out_shape=[jax.ShapeDtypeStruct((batch * seq_len, GLA_V), F32),
                   jax.ShapeDtypeStruct((batch, GLA_HEADS, GLA_DV, GLA_DK), F32)],
        scratch_shapes=[pltpu.VMEM((GLA_HEADS, GLA_DV, GLA_DK), F32)],
        compiler_params=pltpu.CompilerParams(
            dimension_semantics=("arbitrary", "arbitrary"),
            vmem_limit_bytes=V7X_VMEM_LIMIT_BYTES),
        name="gla_scan_rev" if reverse else "gla_scan",
    )(*args)


def gla_mixer(z_all, B, S, Lc, gate_w2, gate_b, norm_g, with_ctx):
    s0 = jnp.zeros((B, GLA_HEADS, GLA_DV, GLA_DK), F32)
    g = norm_g.astype(F32)[None]
    w2 = [jnp.zeros((LANES, GLA_QK), F32).at[d * GLA_GATE_RANK:(d + 1) * GLA_GATE_RANK].set(gate_w2[d])
          for d in range(2)]
    gb = [gate_b[d].astype(F32)[None] for d in range(2)]
    ctx = dict(batch=B, seq_len=Lc, row0=B * S, rope=False)
    lat = dict(batch=B, seq_len=S, row0=0, rope=True)
    yc_f, sc_f = gla_scan(z_all, w2[0], gb[0], s0, reverse=False, **ctx)
    yc, sc_b = gla_scan(z_all, w2[1], gb[1], s0, reverse=True, finish_with=(yc_f, g), **ctx)
    yl_f, _ = gla_scan(z_all, w2[0], gb[0], sc_f, reverse=False, **lat)
    yl, _ = gla_scan(z_all, w2[1], gb[1], sc_b, reverse=True, finish_with=(yl_f, g), **lat)
    return yl, (yc if with_ctx else None)


RW_PREP_TM = 256
RW_HALO = 8
RW_N_PIECES = 5


def _heads_sum(x):
    return jnp.concatenate([_head_sum(x[:, j * LANES:(j + 1) * LANES]) for j in range(x.shape[1] // LANES)],
                           axis=1)


def _rw_prep_kernel(*refs, tm, n_latent_tiles, tiles_per_seq):
    P = RW_N_PIECES
    z_refs, prev_refs, next_refs, mu_refs = refs[0:P], refs[P:2 * P], refs[2 * P:3 * P], refs[3 * P:4 * P]
    kk_ref, ka_ref, rk_ref, w0_ref, a0_ref, w2_ref, a2_ref, g2_ref = refs[4 * P:4 * P + 8]
    (r_o, v_o, kap_o, lw0_o, kd0_o, b0_o, lw1_o, kd1_o, b1_o, bonus_o, g_o) = refs[4 * P + 8:]
    i = pl.program_id(0)
    latent = i < n_latent_tiles
    first = jnp.logical_or(jnp.logical_not(latent), i % tiles_per_seq == 0)
    last = jnp.logical_or(jnp.logical_not(latent), i % tiles_per_seq == tiles_per_seq - 1)

    def shifted(j):
        z = z_refs[j][...]
        row = lax.broadcasted_iota(jnp.int32, z.shape, 0)
        prev_row = jnp.where(first, 0.0, prev_refs[j][RW_HALO - 1:RW_HALO, :])
        next_row = jnp.where(last, 0.0, next_refs[j][0:1, :])
        prev = jnp.where(row == 0, prev_row, pltpu.roll(z, 1, axis=0))
        nxt = jnp.where(row == tm - 1, next_row, pltpu.roll(z, tm - 1, axis=0))
        return z + mu_refs[j][...] * (0.5 * (prev + nxt) - z)

    r, k, v, lr, gd = (shifted(j) for j in range(P))
    kkr = k * kk_ref[...]
    kap = kkr * lax.rsqrt(_heads_sum(kkr * kkr) + EPS)
    r_o[...] = r
    v_o[...] = v
    kap_o[...] = kap
    bonus_o[...] = _heads_sum(r * k * rk_ref[...]) * v
    g_o[...] = _mm(jax.nn.sigmoid(gd), g2_ref[...], NN_DIMS, 3)
    ka = ka_ref[...]
    for d, (lw_o, kd_o, b_o) in enumerate(((lw0_o, kd0_o, b0_o), (lw1_o, kd1_o, b1_o))):
        wd = lr[:, d * LANES:(d + 1) * LANES]
        ad = lr[:, (2 + d) * LANES:(3 + d) * LANES]
        w_log = _log_sigmoid(w0_ref[d:d + 1, :] + _mm(jnp.tanh(wd), w2_ref[d], NN_DIMS, 3)) - 0.5
        lw_o[...] = -jnp.exp(w_log)
        a = jax.nn.sigmoid(a0_ref[d:d + 1, :] + _mm(ad, a2_ref[d], NN_DIMS, 3))
        kd_o[...] = k * (1.0 + (a - 1.0) * ka)
        b_o[...] = kap * a


def rwkv_prep(z_all, n_latent_rows, seq_len, mu_p, k_k, k_a, r_k, w0, a0, w2p, a2p, g2):
    M = z_all.shape[0]
    tm = RW_PREP_TM
    assert M % tm == 0 and n_latent_rows % tm == 0 and seq_len % tm == 0
    widths = (RW_W, RW_W, RW_W, RW_LR_W, RW_GATE_RANK)
    offs = (C_RR, C_RK, C_RV, C_RLR, C_RGD)
    n_halo_blocks = M // RW_HALO
    z_specs, prev_specs, next_specs, mu_specs = [], [], [], []
    for w, o in zip(widths, offs):
        assert o % w == 0
        cb = o // w
        z_specs.append(pl.BlockSpec((tm, w), lambda i, cb=cb: (i, cb)))
        prev_specs.append(pl.BlockSpec(
            (RW_HALO, w), lambda i, cb=cb: (jnp.maximum(i * (tm // RW_HALO) - 1, 0), cb)))
        next_specs.append(pl.BlockSpec(
            (RW_HALO, w), lambda i, cb=cb: (jnp.minimum((i + 1) * (tm // RW_HALO), n_halo_blocks - 1), cb)))
        mu_specs.append(pl.BlockSpec((1, w), lambda i: (0, 0)))

    def full(shape):
        return pl.BlockSpec(shape, lambda i: (0,) * len(shape))

    par_specs = [full((1, RW_W))] * 3 + [full((2, RW_W))] * 2 + [full((2, LANES, RW_W))] * 2 \
        + [full((RW_GATE_RANK, RW_W))]
    out_spec = pl.BlockSpec((tm, RW_W), lambda i: (i, 0))
    return pl.pallas_call(
        functools.partial(_rw_prep_kernel, tm=tm, n_latent_tiles=n_latent_rows // tm,
                          tiles_per_seq=seq_len // tm),
        grid=(M // tm,),
        in_specs=z_specs + prev_specs + next_specs + mu_specs + par_specs,
        out_specs=[out_spec] * 11,
        out_shape=[jax.ShapeDtypeStruct((M, RW_W), F32)] * 11,
        compiler_params=pltpu.CompilerParams(
            dimension_semantics=("arbitrary",), vmem_limit_bytes=V7X_VMEM_LIMIT_BYTES),
        name="rwkv_prep",
    )(*([z_all] * 15), *mu_p, k_k, k_a, r_k, w0, a0, w2p, a2p, g2)


def _rw_chunk_kernel(*refs, reverse, n_par, n_chunks, passes, finish):
    r_ref, v_ref, kap_ref, lw_ref, kd_ref, bet_ref, s0_ref = refs[:7]
    if finish:
        yo_ref, bonus_ref, g_ref, gam_ref, gnb_ref = refs[7:12]
    y_ref, sfin_ref, s_scr = refs[-3:]
    _rw_chunk_body(r_ref, v_ref, kap_ref, lw_ref, kd_ref, bet_ref, s0_ref, y_ref, sfin_ref, s_scr,
                   (yo_ref, bonus_ref, g_ref, gam_ref, gnb_ref) if finish else None,
                   reverse=reverse, n_par=n_par, n_chunks=n_chunks, passes=passes)


def _rw_chunk_body(r_ref, v_ref, kap_ref, lw_ref, kd_ref, bet_ref, s0_ref, y_ref, sfin_ref, s_scr, fin_refs,
                   *, reverse, n_par, n_chunks, passes):
    C = RW_CHUNK
    n = pl.program_id(2)
    P = range(n_par)

    @pl.when(n == 0)
    def _():
        s_scr[...] = s0_ref[0, 0]

    row = lax.broadcasted_iota(jnp.int32, (C, LANES), 0)
    lane = lax.broadcasted_iota(jnp.int32, (C, LANES), 1)
    col = lane % RW_HEAD
    head0 = lane < RW_HEAD
    strict = (col > row) if reverse else (col < row)
    incl = (col >= row) if reverse else (col <= row)
    ti = lax.broadcasted_iota(jnp.int32, (C, C), 0)
    tj = lax.broadcasted_iota(jnp.int32, (C, C), 1)
    tri = jnp.where((tj >= ti) if reverse else (tj <= ti), 1.0, 0.0).astype(BF16)
    level_masks = []
    s = 1
    while s < C:
        level_masks.append(strict & (row // (2 * s) == col // (2 * s)) & (row // s != col // s))
        s *= 2
    eye = jnp.where(row == col, 1.0, 0.0)
    brow = lax.broadcasted_iota(jnp.int32, (LANES, LANES), 0) // RW_HEAD
    bcol = lax.broadcasted_iota(jnp.int32, (LANES, LANES), 1) // RW_HEAD
    same_head = brow == bcol

    def sl(x, p):
        return x[:, p * LANES:(p + 1) * LANES]

    def bd(x):
        return jnp.concatenate([jnp.where(head0, x, 0.0), jnp.where(head0, 0.0, x)], axis=0)

    def pp(a_pair, b_pair):
        return _mm(a_pair, bd(b_pair), NN_DIMS, passes)

    lw_all = lw_ref[...]
    cs_all = _cumsum_rows(tri, lw_all)
    tot_all = cs_all[0:1, :] if reverse else cs_all[C - 1:C, :]
    inv_all = jnp.exp(-cs_all)
    fin_all = jnp.exp(tot_all - cs_all)
    kd_all = kd_ref[...]
    bet_all = bet_ref[...]
    kap_t = kap_ref[...] * jnp.exp(cs_all - lw_all)
    r_t = r_ref[...] * jnp.exp(cs_all)
    k_t = kd_all * inv_all
    b_t = bet_all * inv_all
    k_h = kd_all * fin_all
    b_h = bet_all * fin_all
    dec = jnp.exp(tot_all)
    v_all = v_ref[...]

    S = [s_scr[p] for p in P]
    V = [sl(v_all, p) for p in P]
    lhs = [jnp.concatenate([sl(kap_t, p), sl(r_t, p)], axis=0) for p in P]
    a_k = [_mm(lhs[p], bd(sl(k_t, p)), NT_DIMS, passes) for p in P]
    a_b = [_mm(lhs[p], bd(sl(b_t, p)), NT_DIMS, passes) for p in P]
    s0_both = [_mm(lhs[p], S[p], NT_DIMS, passes) for p in P]
    lmat = [jnp.where(strict, a_b[p][:C], 0.0) for p in P]
    t_inv = [eye - jnp.where(level_masks[0], lmat[p], 0.0) for p in P]
    for m in level_masks[1:]:
        tmp = [pp(jnp.where(m, lmat[p], 0.0), t_inv[p]) for p in P]
        t_inv = [t_inv[p] - pp(t_inv[p], tmp[p]) for p in P]
    x = [s0_both[p][:C] + pp(jnp.where(strict, a_k[p][:C], 0.0), V[p]) for p in P]
    u = [pp(t_inv[p], x[p]) for p in P]
    y = [s0_both[p][C:] + pp(jnp.where(incl, a_k[p][C:], 0.0), V[p])
         - pp(jnp.where(incl, a_b[p][C:], 0.0), u[p]) for p in P]
    upd = [_mm(V[p], sl(k_h, p), TN_DIMS, passes) - _mm(u[p], sl(b_h, p), TN_DIMS, passes) for p in P]
    for p in P:
        s_scr[p] = S[p] * sl(dec, p) + jnp.where(same_head, upd[p], 0.0)
    if fin_refs is None:
        y_ref[...] = jnp.concatenate(y, axis=1)
    else:
        yo_ref, bonus_ref, g_ref, gam_ref, gnb_ref = fin_refs
        ys = jnp.concatenate(y, axis=1) + yo_ref[...]
        cen = ys - _heads_sum(ys) * (1.0 / RW_HEAD)
        var = _heads_sum(cen * cen) * (1.0 / RW_HEAD)
        out = cen * lax.rsqrt(var + RW_GN_EPS) * gam_ref[...] + gnb_ref[...] + bonus_ref[...]
        y_ref[...] = out * g_ref[...]

    @pl.when(n == n_chunks - 1)
    def _():
        sfin_ref[0, 0] = s_scr[...]


def rwkv_chunk_scan(r, v, kap, lw, kd, bet, s0, *, batch, seq_len, row0, reverse, finish_with=None):
    n_par = RW_PAIRS_PER_STEP
    W = r.shape[1]
    G = W // (n_par * LANES)
    N = seq_len // RW_CHUNK
    blk0 = row0 // RW_CHUNK

    def tb(n):
        return N - 1 - n if reverse else n

    seq_in = pl.BlockSpec((RW_CHUNK, n_par * LANES), lambda b, g, n: (blk0 + b * N + tb(n), g))
    seq_out = pl.BlockSpec((RW_CHUNK, n_par * LANES), lambda b, g, n: (b * N + tb(n), g))
    st_spec = pl.BlockSpec((1, 1, n_par, LANES, LANES), lambda b, g, n: (b, g, 0, 0, 0))
    in_specs = [seq_in] * 6 + [st_spec]
    args = [r, v, kap, lw, kd, bet, s0]
    if finish_with is not None:
        y_other, bonus, gate, gamma, beta = finish_with
        vec_spec = pl.BlockSpec((1, n_par * LANES), lambda b, g, n: (0, g))
        in_specs += [seq_out, seq_in, seq_in, vec_spec, vec_spec]
        args += [y_other, bonus, gate, gamma, beta]
    return pl.pallas_call(
        functools.partial(_rw_chunk_kernel, reverse=reverse, n_par=n_par, n_chunks=N, passes=RW_PASSES,
                          finish=finish_with is not None),
        grid=(batch, G, N),
        in_specs=in_specs,
        out_specs=[seq_out, st_spec],
        out_shape=[jax.ShapeDtypeStruct((batch * seq_len, W), F32),
                   jax.ShapeDtypeStruct((batch, G, n_par, LANES, LANES), F32)],
        scratch_shapes=[pltpu.VMEM((n_par, LANES, LANES), F32)],
        compiler_params=pltpu.CompilerParams(
            dimension_semantics=("arbitrary", "arbitrary", "arbitrary"),
            vmem_limit_bytes=V7X_VMEM_LIMIT_BYTES),
        name="rwkv_chunk_scan_rev" if reverse else "rwkv_chunk_scan",
    )(*args)


def _pad_rows(w, rows):
    return jnp.pad(w.astype(F32), ((0, 0), (0, rows - w.shape[1]), (0, 0)))


def rwkv_mixer(z_all, B, S, Lc, mu, w0, w2, a0, a2, g2, k_k, k_a, r_k, gn_g, gn_b, with_ctx):
    assert Lc == RW_PREP_TM
    mu = mu.astype(F32)
    o = np.cumsum((0,) + RW_SPLITS)
    lr_pad = jnp.zeros((LANES - RW_DECAY_RANK,), F32)
    mu_lr = jnp.concatenate([mu[o[3]:o[3] + RW_DECAY_RANK], lr_pad, mu[o[3] + RW_DECAY_RANK:o[4]], lr_pad,
                             mu[o[4]:o[4] + RW_A_RANK], lr_pad, mu[o[4] + RW_A_RANK:o[5]], lr_pad])
    mu_p = [mu[o[0]:o[1]][None], mu[o[1]:o[2]][None], mu[o[2]:o[3]][None], mu_lr[None], mu[o[5]:o[6]][None]]
    row = lambda t: t.astype(F32).reshape(1, RW_W)
    prep = rwkv_prep(z_all, B * S, S, mu_p, row(k_k), row(k_a), row(r_k), w0.astype(F32), a0.astype(F32),
                     _pad_rows(w2, LANES), _pad_rows(a2, LANES), g2.astype(F32))
    r, v, kap, lw0, kd0, b0, lw1, kd1, b1, bonus, gate = prep
    fin = (bonus, gate, row(gn_g), row(gn_b))
    s0 = jnp.zeros((B, RW_W // (RW_PAIRS_PER_STEP * LANES), RW_PAIRS_PER_STEP, LANES, LANES), F32)
    ctx = dict(batch=B, seq_len=Lc, row0=B * S)
    lat = dict(batch=B, seq_len=S, row0=0)
    yc_f, sc_f = rwkv_chunk_scan(r, v, kap, lw0, kd0, b0, s0, reverse=False, **ctx)
    yc, sc_b = rwkv_chunk_scan(r, v, kap, lw1, kd1, b1, s0, reverse=True, finish_with=(yc_f,) + fin, **ctx)
    yl_f, _ = rwkv_chunk_scan(r, v, kap, lw0, kd0, b0, sc_f, reverse=False, **lat)
    yl, _ = rwkv_chunk_scan(r, v, kap, lw1, kd1, b1, sc_b, reverse=True, finish_with=(yl_f,) + fin, **lat)
    return yl, (yc if with_ctx else None)


def relayout_w_in(w):
    o = IN_OFFSETS
    rw = o[8] + np.cumsum((0,) + RW_SPLITS)
    zeros = lambda n: jnp.zeros((w.shape[0], n), w.dtype)
    seg = lambda lo, n: w[:, lo:lo + n]
    lr_pad = LANES - RW_DECAY_RANK
    pieces = [
        seg(o[0], 3 * NA_W),
        seg(o[5], GLA_V), seg(o[6], GLA_V),
        seg(rw[0], 3 * RW_W),
        seg(o[9], N_BRANCH * D_MODEL),
        seg(o[3], GLA_QK), seg(o[4], GLA_QK),
        seg(rw[3], RW_DECAY_RANK), zeros(lr_pad), seg(rw[3] + RW_DECAY_RANK, RW_DECAY_RANK), zeros(lr_pad),
        seg(rw[4], RW_A_RANK), zeros(lr_pad), seg(rw[4] + RW_A_RANK, RW_A_RANK), zeros(lr_pad),
        seg(rw[5], RW_GATE_RANK),
        seg(o[7], 2 * GLA_GATE_RANK),
    ]
    used = sum(x.shape[1] for x in pieces)
    assert used == C_GGD + 2 * GLA_GATE_RANK
    return jnp.concatenate(pieces + [zeros(Z_W - used)], axis=1).astype(BF16)


def token_mixers(x, ctx, mod, p, B, S, Lc, with_ctx):
    h_all = prenorm(x, ctx, p['norm1_g'], mod, seq_len=S)
    z_all = matmul(h_all, relayout_w_in(p['w_in']), tn=1024)
    y_na, yc_na = na_mixer(z_all, B, S, Lc, p['na_q_norm'], p['na_k_norm'], p['na_rpb'], with_ctx)
    y_gla, yc_gla = gla_mixer(z_all, B, S, Lc, p['gla_gate_w2'], p['gla_gate_b'], p['gla_norm_g'], with_ctx)
    y_rw, yc_rw = rwkv_mixer(z_all, B, S, Lc, p['rw_mu'], p['rw_w0'], p['rw_w2'], p['rw_a0'], p['rw_a2'],
                             p['rw_g2'], p['rw_k_k'], p['rw_k_a'], p['rw_r_k'], p['rw_gn_g'],
                             p['rw_gn_b'], with_ctx)
    merged = merge_branches((y_na, y_gla, y_rw), z_all, 0, p['w_branch'])
    x = residual_matmul(merged, p['w_out'], x, mod, gate_row=2, rows_per_cond=S)
    if with_ctx:
        merged_c = merge_branches((yc_na, yc_gla, yc_rw), z_all, B * S, p['w_branch'])
        ctx = residual_matmul(merged_c, p['w_out'], ctx, mod, gate_row=2, fixed_row=B)
    return x, ctx


def moe_sublayer(x, mod, p, *, rows_per_cond=None, fixed_row=None):
    h2, comb_t = prenorm_router(x, p['norm2_g'], mod, p['router_w_t'], p['router_bias'],
                                rows_per_cond=rows_per_cond, fixed_row=fixed_row)
    return moe_experts(h2, comb_t.T, p['moe_w_gate'], p['moe_w_up'], p['moe_w_down'], x, mod,
                       rows_per_cond=rows_per_cond, fixed_row=fixed_row)


def kernel(x, c, ctx, c_ctx, ada_w, ada_b, norm1_g, norm2_g, w_in, na_q_norm, na_k_norm, na_rpb,
           gla_gate_w2, gla_gate_b, gla_norm_g, rw_mu, rw_w0, rw_w2, rw_a0, rw_a2, rw_g2,
           rw_k_k, rw_k_a, rw_r_k, rw_gn_g, rw_gn_b, w_branch, w_out, router_w, router_bias,
           moe_w_gate, moe_w_up, moe_w_down):
    B, S, D = x.shape
    Lc = ctx.shape[1]
    assert B + 1 <= N_COND and S % 512 == 0 and (B * Lc) % 512 == 0 and Lc % ROW_TM == 0
    c_rows = jnp.concatenate([c, c_ctx[None], jnp.zeros((N_COND - B - 1, D), c.dtype)], axis=0).astype(F32)
    mods = ada_modulation(c_rows, ada_w, ada_b).reshape(DEPTH, N_COND, N_MOD, D)
    x = x.reshape(B * S, D)
    ctx = ctx.reshape(B * Lc, D)
    router_w_t = router_w.astype(F32).T
    router_b = router_bias.astype(F32).reshape(N_EXPERTS, 1)
    for l in range(DEPTH):
        with_ctx = l < DEPTH - 1
        p = {
            'w_in': w_in[l], 'na_q_norm': na_q_norm[l], 'na_k_norm': na_k_norm[l], 'na_rpb': na_rpb[l],
            'gla_gate_w2': gla_gate_w2[l], 'gla_gate_b': gla_gate_b[l], 'gla_norm_g': gla_norm_g[l],
            'rw_mu': rw_mu[l], 'rw_w0': rw_w0[l], 'rw_w2': rw_w2[l], 'rw_a0': rw_a0[l], 'rw_a2': rw_a2[l],
            'rw_g2': rw_g2[l], 'rw_k_k': rw_k_k[l], 'rw_k_a': rw_k_a[l], 'rw_r_k': rw_r_k[l],
            'rw_gn_g': rw_gn_g[l], 'rw_gn_b': rw_gn_b[l], 'w_branch': w_branch[l], 'w_out': w_out[l],
            'norm1_g': norm1_g[l].astype(F32)[None], 'norm2_g': norm2_g[l].astype(F32)[None],
            'router_w_t': router_w_t, 'router_bias': router_b,
            'moe_w_gate': moe_w_gate[l].astype(BF16), 'moe_w_up': moe_w_up[l].astype(BF16),
            'moe_w_down': moe_w_down[l].astype(BF16),
        }
        x, ctx = token_mixers(x, ctx, mods[l], p, B, S, Lc, with_ctx)
        x = moe_sublayer(x, mods[l], p, rows_per_cond=S)
        if with_ctx:
            ctx = moe_sublayer(ctx, mods[l], p, fixed_row=B)
    return x.reshape(B, S, D)
```

```python
import functools

import jax
import jax.numpy as jnp
from jax import lax
import numpy as np
from jax.experimental import pallas as pl
from jax.experimental.pallas import tpu as pltpu

D_MODEL = 2048
DEPTH = 2
GRID_W = 64

NA_HEADS = 16
NA_HEAD_DIM = 64
NA_WIN_R_MAX = 8
NA_WIN_C = 16
NA_W = NA_HEADS * NA_HEAD_DIM
NA_ROWS_PER_STEP = 8

GLA_HEADS = 4
GLA_DK = 128
GLA_DV = 256
GLA_QK = GLA_HEADS * GLA_DK
GLA_V = GLA_HEADS * GLA_DV
GLA_GATE_RANK = 16
GLA_GATE_TAU = 16.0
GLA_CHUNK = 64

RW_HEADS = 16
RW_HEAD = 64
RW_W = RW_HEADS * RW_HEAD
RW_DECAY_RANK = 96
RW_A_RANK = 96
RW_GATE_RANK = 256
RW_SPLITS = (RW_W, RW_W, RW_W, 2 * RW_DECAY_RANK, 2 * RW_A_RANK, RW_GATE_RANK)
RW_IN = sum(RW_SPLITS)
RW_GN_EPS = 64e-5
RW_CHUNK = 64
RW_PASSES = 1
RW_PAIRS_PER_STEP = 8

N_BRANCH = 3
BRANCH_W = 1024
IN_SPLITS = (NA_W, NA_W, NA_W, GLA_QK, GLA_QK, GLA_V, GLA_V, 2 * GLA_GATE_RANK, RW_IN, N_BRANCH * D_MODEL)
D_IN = sum(IN_SPLITS)
IN_OFFSETS = tuple(int(o) for o in np.cumsum((0,) + IN_SPLITS[:-1]))

LANES = 128
RW_LR_W = 4 * LANES
C_NAQ, C_NAK, C_NAV = 0, 1024, 2048
C_GV, C_GR = 3072, 4096
C_RR, C_RK, C_RV = 5120, 6144, 7168
C_GATE = 8192
C_GQ, C_GK = 14336, 14848
C_RLR, C_RGD, C_GGD = 15360, 15872, 16128
Z_W = 16384

N_EXPERTS = 16
N_GROUPS = 4
EXPERTS_PER_GROUP = N_EXPERTS // N_GROUPS
TOP_K = 2
D_EXPERT = 512

ROPE_BASE = 10000.0
EPS = 1e-6
F32 = jnp.float32
BF16 = jnp.bfloat16

V7X_VMEM_LIMIT_BYTES = 48 * 1024 * 1024

NEG_BIG = -1e30
NT_DIMS = (((1,), (1,)), ((), ()))
TN_DIMS = (((0,), (0,)), ((), ()))
NN_DIMS = (((1,), (0,)), ((), ()))


def _split_bf16(x):
    hi = x.astype(BF16)
    lo = (x - hi.astype(F32)).astype(BF16)
    return hi, lo


def _mm(a, b, dims=NN_DIMS, passes=1):
    if passes == 1:
        return lax.dot_general(a.astype(BF16), b.astype(BF16), dims, preferred_element_type=F32)
    a_hi, a_lo = _split_bf16(a)
    b_hi, b_lo = _split_bf16(b)
    out = lax.dot_general(a_hi, b_hi, dims, preferred_element_type=F32)
    out = out + lax.dot_general(a_hi, b_lo, dims, preferred_element_type=F32)
    return out + lax.dot_general(a_lo, b_hi, dims, preferred_element_type=F32)


def _cumsum_rows(tri, x):
    hi, lo = _split_bf16(x)
    return jnp.dot(tri, hi, preferred_element_type=F32) + jnp.dot(tri, lo, preferred_element_type=F32)


def _mm_kernel(a_ref, b_ref, o_ref):
    o_ref[...] = jnp.dot(a_ref[...], b_ref[...], preferred_element_type=F32).astype(o_ref.dtype)


def matmul(a, b, *, tm=512, tn=512, out_dtype=F32):
    M, K = a.shape
    _, N = b.shape
    assert M % tm == 0 and N % tn == 0, (M, N, tm, tn)
    return pl.pallas_call(
        _mm_kernel,
        grid=(N // tn, M // tm),
        in_specs=[pl.BlockSpec((tm, K), lambda n, m: (m, 0)),
                  pl.BlockSpec((K, tn), lambda n, m: (0, n))],
        out_specs=pl.BlockSpec((tm, tn), lambda n, m: (m, n)),
        out_shape=jax.ShapeDtypeStruct((M, N), out_dtype),
        compiler_params=pltpu.CompilerParams(
            dimension_semantics=("arbitrary", "arbitrary"),
            vmem_limit_bytes=V7X_VMEM_LIMIT_BYTES),
        name="matmul",
    )(a.astype(BF16), b.astype(BF16))


def _moe_kernel(h_ref, comb_ref, wg_ref, wu_ref, wd_ref, x_ref, mod_ref, o_ref, acc_ref):
    e = pl.program_id(1)

    @pl.when(e == 0)
    def _():
        acc_ref[...] = jnp.zeros_like(acc_ref)

    h = h_ref[...]
    g = jnp.dot(h, wg_ref[0], preferred_element_type=F32)
    u = jnp.dot(h, wu_ref[0], preferred_element_type=F32)
    comb = comb_ref[...]
    lane = lax.broadcasted_iota(jnp.int32, comb.shape, 1)
    ce = jnp.sum(jnp.where(lane == e, comb, 0.0), axis=1, keepdims=True)
    act = (g * jax.nn.sigmoid(g)) * u * ce
    acc_ref[...] += jnp.dot(act.astype(BF16), wd_ref[0], preferred_element_type=F32)

    @pl.when(e == N_EXPERTS - 1)
    def _():
        o_ref[...] = x_ref[...] + mod_ref[0, N_MOD - 1:N_MOD, :] * acc_ref[...]


def moe_experts(h, comb, w_gate, w_up, w_down, x, mod, *, rows_per_cond=None, fixed_row=None, tm=512):
    M, D = h.shape
    assert M % tm == 0
    cond = _cond_of_tile(rows_per_cond, tm, fixed_row)
    return pl.pallas_call(
        _moe_kernel,
        grid=(M // tm, N_EXPERTS),
        in_specs=[pl.BlockSpec((tm, D), lambda m, e: (m, 0)),
                  pl.BlockSpec((tm, N_EXPERTS), lambda m, e: (m, 0)),
                  pl.BlockSpec((1, D, D_EXPERT), lambda m, e: (e, 0, 0)),
                  pl.BlockSpec((1, D, D_EXPERT), lambda m, e: (e, 0, 0)),
                  pl.BlockSpec((1, D_EXPERT, D), lambda m, e: (e, 0, 0)),
                  pl.BlockSpec((tm, D), lambda m, e: (m, 0)),
                  pl.BlockSpec((1, N_MOD, D), lambda m, e: (cond(m), 0, 0))],
        out_specs=pl.BlockSpec((tm, D), lambda m, e: (m, 0)),
        out_shape=jax.ShapeDtypeStruct((M, D), F32),
        scratch_shapes=[pltpu.VMEM((tm, D), F32)],
        compiler_params=pltpu.CompilerParams(
            dimension_semantics=("arbitrary", "arbitrary"),
            vmem_limit_bytes=V7X_VMEM_LIMIT_BYTES),
        name="moe_experts",
    )(h, comb, w_gate, w_up, w_down, x, mod)


N_COND = 8
N_MOD = 6
ROW_TM = 256


def _ada_kernel(c_ref, w_ref, b_ref, o_ref):
    c = c_ref[...]
    o_ref[0] = _mm(c * jax.nn.sigmoid(c), w_ref[0], NN_DIMS, 3) + b_ref[0]


def ada_modulation(c_rows, ada_w, ada_b, *, tn=1024):
    L, D, N = ada_w.shape
    return pl.pallas_call(
        _ada_kernel,
        grid=(L, N // tn),
        in_specs=[pl.BlockSpec((N_COND, D), lambda l, n: (0, 0)),
                  pl.BlockSpec((1, D, tn), lambda l, n: (l, 0, n)),
                  pl.BlockSpec((1, 1, tn), lambda l, n: (l, 0, n))],
        out_specs=pl.BlockSpec((1, N_COND, tn), lambda l, n: (l, 0, n)),
        out_shape=jax.ShapeDtypeStruct((L, N_COND, N), F32),
        compiler_params=pltpu.CompilerParams(
            dimension_semantics=("arbitrary", "arbitrary"), vmem_limit_bytes=V7X_VMEM_LIMIT_BYTES),
        name="ada_modulation",
    )(c_rows, ada_w, ada_b.reshape(L, 1, N))


def _norm_mod(x, g, shift, scale):
    y = x * lax.rsqrt(jnp.mean(x * x, axis=-1, keepdims=True) + EPS)
    return y * g * (1.0 + scale) + shift


def _prenorm_kernel(x_ref, c_ref, g_ref, mod_ref, o_ref, *, n_latent_tiles):
    def emit(src_ref):
        o_ref[...] = _norm_mod(src_ref[...], g_ref[...], mod_ref[0, 0:1, :], mod_ref[0, 1:2, :]).astype(o_ref.dtype)

    @pl.when(pl.program_id(0) < n_latent_tiles)
    def _():
        emit(x_ref)

    @pl.when(pl.program_id(0) >= n_latent_tiles)
    def _():
        emit(c_ref)


def _top2_sum(a, b, c, d):
    hi1, lo1 = jnp.maximum(a, b), jnp.minimum(a, b)
    hi2, lo2 = jnp.maximum(c, d), jnp.minimum(c, d)
    return jnp.maximum(hi1, hi2) + jnp.maximum(jnp.minimum(hi1, hi2), jnp.maximum(lo1, lo2))


def _prenorm_router_kernel(x_ref, g_ref, mod_ref, rw_ref, rb_ref, o_ref, comb_ref):
    h = _norm_mod(x_ref[...], g_ref[...], mod_ref[0, 3:4, :], mod_ref[0, 4:5, :])
    o_ref[...] = h.astype(o_ref.dtype)
    s = jax.nn.sigmoid(_mm(rw_ref[...], h, NT_DIMS, 3))
    biased = s + rb_ref[...]
    rows = [biased[e:e + 1, :] for e in range(N_EXPERTS)]
    G = EXPERTS_PER_GROUP
    score = [_top2_sum(*rows[g * G:(g + 1) * G]) for g in range(N_GROUPS)]
    picked = []
    for e in range(N_EXPERTS):
        g = e // G
        ok = None
        for g2 in range(N_GROUPS):
            if g2 != g:
                t = (score[g] > score[g2]) if g2 < g else (score[g] >= score[g2])
                ok = t if ok is None else jnp.logical_and(ok, t)
        rank = 0.0
        for e2 in range(g * G, (g + 1) * G):
            if e2 != e:
                ahead = (rows[e2] >= rows[e]) if e2 < e else (rows[e2] > rows[e])
                rank = rank + jnp.where(ahead, 1.0, 0.0)
        picked.append(jnp.where(jnp.logical_and(ok, rank < TOP_K), s[e:e + 1, :], 0.0))
    w = jnp.concatenate(picked, axis=0)
    comb_ref[...] = w / jnp.sum(w, axis=0, keepdims=True)


def _cond_of_tile(rows_per_cond, tile_rows, fixed_row):
    if fixed_row is not None:
        return lambda m: fixed_row
    assert rows_per_cond % tile_rows == 0
    return lambda m: m // (rows_per_cond // tile_rows)


def prenorm(x, ctx, gain, mod, *, seq_len):
    n_lat, n_ctx = x.shape[0] // ROW_TM, ctx.shape[0] // ROW_TM
    n_batch = x.shape[0] // seq_len
    cond = _cond_of_tile(seq_len, ROW_TM, None)
    return pl.pallas_call(
        functools.partial(_prenorm_kernel, n_latent_tiles=n_lat),
        grid=(n_lat + n_ctx,),
        in_specs=[pl.BlockSpec((ROW_TM, D_MODEL), lambda i: (jnp.minimum(i, n_lat - 1), 0)),
                  pl.BlockSpec((ROW_TM, D_MODEL), lambda i: (jnp.maximum(i - n_lat, 0), 0)),
                  pl.BlockSpec((1, D_MODEL), lambda i: (0, 0)),
                  pl.BlockSpec((1, N_MOD, D_MODEL), lambda i: (jnp.where(i < n_lat, cond(i), n_batch), 0, 0))],
        out_specs=pl.BlockSpec((ROW_TM, D_MODEL), lambda i: (i, 0)),
        out_shape=jax.ShapeDtypeStruct((x.shape[0] + ctx.shape[0], D_MODEL), BF16),
        compiler_params=pltpu.CompilerParams(dimension_semantics=("arbitrary",)),
        name="prenorm",
    )(x, ctx, gain, mod)


def prenorm_router(x, gain, mod, router_w_t, router_bias, *, rows_per_cond=None, fixed_row=None):
    M = x.shape[0]
    cond = _cond_of_tile(rows_per_cond, ROW_TM, fixed_row)
    x_spec = pl.BlockSpec((ROW_TM, D_MODEL), lambda i: (i, 0))
    return pl.pallas_call(
        _prenorm_router_kernel,
        grid=(M // ROW_TM,),
        in_specs=[x_spec,
                  pl.BlockSpec((1, D_MODEL), lambda i: (0, 0)),
                  pl.BlockSpec((1, N_MOD, D_MODEL), lambda i: (cond(i), 0, 0)),
                  pl.BlockSpec((N_EXPERTS, D_MODEL), lambda i: (0, 0)),
                  pl.BlockSpec((N_EXPERTS, 1), lambda i: (0, 0))],
        out_specs=[x_spec, pl.BlockSpec((N_EXPERTS, ROW_TM), lambda i: (0, i))],
        out_shape=[jax.ShapeDtypeStruct((M, D_MODEL), BF16),
                   jax.ShapeDtypeStruct((N_EXPERTS, M), F32)],
        compiler_params=pltpu.CompilerParams(dimension_semantics=("arbitrary",)),
        name="prenorm_router",
    )(x, gain, mod, router_w_t, router_bias)


def _merge_kernel(y0_ref, y1_ref, y2_ref, g0_ref, g1_ref, g2_ref, wb_ref, o_ref):
    acc = 0.0
    for i, (y_ref, g_ref) in enumerate(((y0_ref, g0_ref), (y1_ref, g1_ref), (y2_ref, g2_ref))):
        acc = acc + jax.nn.sigmoid(g_ref[...]) * jnp.dot(y_ref[...].astype(BF16), wb_ref[i],
                                                         preferred_element_type=F32)
    o_ref[...] = acc.astype(o_ref.dtype)


def merge_branches(ys, z_all, row0, w_branch, *, tm=512, tn=512):
    M = ys[0].shape[0]
    assert M % tm == 0 and row0 % tm == 0 and C_GATE % tn == 0 and D_MODEL % tn == 0
    row_blk0 = row0 // tm
    y_spec = pl.BlockSpec((tm, BRANCH_W), lambda m, n: (m, 0))

    def gate_spec(i):
        return pl.BlockSpec((tm, tn), lambda m, n: (row_blk0 + m, (C_GATE + i * D_MODEL) // tn + n))

    return pl.pallas_call(
        _merge_kernel,
        grid=(M // tm, D_MODEL // tn),
        in_specs=[y_spec] * 3 + [gate_spec(i) for i in range(N_BRANCH)]
        + [pl.BlockSpec((N_BRANCH, BRANCH_W, tn), lambda m, n: (0, 0, n))],
        out_specs=pl.BlockSpec((tm, tn), lambda m, n: (m, n)),
        out_shape=jax.ShapeDtypeStruct((M, D_MODEL), BF16),
        compiler_params=pltpu.CompilerParams(
            dimension_semantics=("arbitrary", "arbitrary"), vmem_limit_bytes=V7X_VMEM_LIMIT_BYTES),
        name="merge_branches",
    )(*ys, z_all, z_all, z_all, w_branch.astype(BF16))


def _residual_mm_kernel(a_ref, w_ref, x_ref, mod_ref, o_ref, *, gate_row):
    y = jnp.dot(a_ref[...], w_ref[...], preferred_element_type=F32)
    o_ref[...] = x_ref[...] + mod_ref[0, gate_row:gate_row + 1, :] * y


def residual_matmul(a, w, x, mod, *, gate_row, rows_per_cond=None, fixed_row=None, tm=512, tn=512):
    M, K = a.shape
    N = w.shape[1]
    assert M % tm == 0 and N % tn == 0
    cond = _cond_of_tile(rows_per_cond, tm, fixed_row)
    return pl.pallas_call(
        functools.partial(_residual_mm_kernel, gate_row=gate_row),
        grid=(N // tn, M // tm),
        in_specs=[pl.BlockSpec((tm, K), lambda n, m: (m, 0)),
                  pl.BlockSpec((K, tn), lambda n, m: (0, n)),
                  pl.BlockSpec((tm, tn), lambda n, m: (m, n)),
                  pl.BlockSpec((1, N_MOD, tn), lambda n, m: (cond(m), 0, n))],
        out_specs=pl.BlockSpec((tm, tn), lambda n, m: (m, n)),
        out_shape=jax.ShapeDtypeStruct((M, N), F32),
        compiler_params=pltpu.CompilerParams(
            dimension_semantics=("arbitrary", "arbitrary"), vmem_limit_bytes=V7X_VMEM_LIMIT_BYTES),
        name="residual_matmul",
    )(a, w.astype(BF16), x, mod)


def _head_sum(x2):
    row = lax.broadcasted_iota(jnp.int32, (LANES, LANES), 0) // NA_HEAD_DIM
    col = lax.broadcasted_iota(jnp.int32, (LANES, LANES), 1) // NA_HEAD_DIM
    ones_bd = jnp.where(row == col, 1.0, 0.0).astype(BF16)
    return _cumsum_rows_right(x2, ones_bd)


def _cumsum_rows_right(x, m):
    hi, lo = _split_bf16(x)
    return jnp.dot(hi, m, preferred_element_type=F32) + jnp.dot(lo, m, preferred_element_type=F32)


def _na_qkv_kernel(x_ref, g_ref, o_ref):
    x = x_ref[...]

    @pl.when(pl.program_id(1) < 2)
    def _():
        ms = _heads_sum(x * x) * (1.0 / NA_HEAD_DIM)
        o_ref[...] = (x * lax.rsqrt(ms + EPS) * g_ref[...]).astype(o_ref.dtype)

    @pl.when(pl.program_id(1) == 2)
    def _():
        o_ref[...] = x.astype(o_ref.dtype)


def na_qkv(z, gains, *, tm=256):
    M = z.shape[0]
    return pl.pallas_call(
        _na_qkv_kernel,
        grid=(M // tm, 3),
        in_specs=[pl.BlockSpec((tm, NA_W), lambda i, j: (i, j)),
                  pl.BlockSpec((1, NA_W), lambda i, j: (0, jnp.minimum(j, 1)))],
        out_specs=pl.BlockSpec((tm, NA_W), lambda i, j: (i, j)),
        out_shape=jax.ShapeDtypeStruct((M, 3 * NA_W), BF16),
        compiler_params=pltpu.CompilerParams(dimension_semantics=("arbitrary", "arbitrary")),
        name="na_qkv",
    )(z, gains)


def _softmax_pv(s_list, v_list):
    m = s_list[0].max(axis=-1, keepdims=True)
    for s in s_list[1:]:
        m = jnp.maximum(m, s.max(axis=-1, keepdims=True))
    den = 0.0
    acc = 0.0
    for s, v in zip(s_list, v_list):
        p = jnp.exp(s - m)
        den = den + p.sum(axis=-1, keepdims=True)
        acc = acc + jnp.dot(p.astype(BF16), v, preferred_element_type=F32)
    return acc / den


def _na_kernel(*refs, n_rows, win_r, rows_per_step):
    q_ref, k_ref, v_ref, kc_ref, vc_ref = refs[:5]
    bias_refs = refs[5:5 + rows_per_step]
    o_ref = refs[-1]
    kc = kc_ref[...]
    vc = vc_ref[...]
    lane = lax.broadcasted_iota(jnp.int32, (GRID_W, LANES), 1)
    chains = [(i, h) for i in range(rows_per_step) for h in range(2)]
    kw, vw = [], []
    for i in range(rows_per_step):
        r = pl.program_id(2) * rows_per_step + i
        rs = jnp.clip(r - win_r // 2, 0, n_rows - win_r)
        start = pl.multiple_of(rs * GRID_W, GRID_W)
        kw.append(k_ref[pl.ds(start, win_r * GRID_W), :])
        vw.append(v_ref[pl.ds(start, win_r * GRID_W), :])
    qh = []
    for i, h in chains:
        q = q_ref[i * GRID_W:(i + 1) * GRID_W, :]
        qh.append(jnp.where((lane // NA_HEAD_DIM) == h, q, jnp.zeros_like(q)))
    s_loc = [lax.dot_general(qh[c], kw[i], NT_DIMS, preferred_element_type=F32) + bias_refs[i][h, 0]
             for c, (i, h) in enumerate(chains)]
    s_ctx = [lax.dot_general(qh[c], kc, NT_DIMS, preferred_element_type=F32) for c in range(len(chains))]
    m = [jnp.maximum(s_loc[c].max(axis=-1, keepdims=True), s_ctx[c].max(axis=-1, keepdims=True))
         for c in range(len(chains))]
    p_loc = [jnp.exp(s_loc[c] - m[c]) for c in range(len(chains))]
    p_ctx = [jnp.exp(s_ctx[c] - m[c]) for c in range(len(chains))]
    den = [p_loc[c].sum(axis=-1, keepdims=True) + p_ctx[c].sum(axis=-1, keepdims=True)
           for c in range(len(chains))]
    acc = [jnp.dot(p_loc[c].astype(BF16), vw[i], preferred_element_type=F32)
           + jnp.dot(p_ctx[c].astype(BF16), vc, preferred_element_type=F32)
           for c, (i, h) in enumerate(chains)]
    out = [acc[c] / den[c] for c in range(len(chains))]
    for i in range(rows_per_step):
        o_ref[i * GRID_W:(i + 1) * GRID_W, :] = jnp.where(lane < NA_HEAD_DIM, out[2 * i], out[2 * i + 1])


def _ctx_attn_kernel(q_ref, k_ref, v_ref, o_ref):
    q = q_ref[...]
    k = k_ref[...]
    v = v_ref[...]
    lane = lax.broadcasted_iota(jnp.int32, q.shape, 1)
    outs = []
    for h in range(2):
        qh = jnp.where((lane // NA_HEAD_DIM) == h, q, jnp.zeros_like(q))
        s = lax.dot_general(qh, k, NT_DIMS, preferred_element_type=F32)
        outs.append(_softmax_pv([s], [v]))
    o_ref[...] = jnp.where(lane < NA_HEAD_DIM, outs[0], outs[1]).astype(o_ref.dtype)


def na_bias_table(rpb, n_rows, win_r):
    col = np.arange(GRID_W)
    col_start = np.clip(col - NA_WIN_C // 2, 0, GRID_W - NA_WIN_C)
    kcol = np.arange(GRID_W)
    valid = (kcol[None, :] >= col_start[:, None]) & (kcol[None, :] < col_start[:, None] + NA_WIN_C)
    dc = np.clip(kcol[None, :] - col[:, None] + NA_WIN_C - 1, 0, 2 * NA_WIN_C - 2)
    dr = np.arange(NA_WIN_R_MAX)[:, None] + np.arange(win_r)[None, :]
    dr = np.clip(dr, 0, 2 * NA_WIN_R_MAX - 2)
    t = rpb.astype(F32)[:, dr][:, :, :, dc]
    t = jnp.where(valid[None, None, None], t, NEG_BIG)
    t = t.transpose(0, 1, 3, 2, 4)
    return t.reshape(rpb.shape[0], NA_WIN_R_MAX, GRID_W, win_r * GRID_W)


def na_attention(qkv, rpb, B, S, Lc, with_ctx):
    n_rows = S // GRID_W
    win_r = min(NA_WIN_R_MAX, n_rows)
    assert win_r == NA_WIN_R_MAX and (B * S) % Lc == 0
    HP = NA_HEADS // 2
    bias = na_bias_table(rpb, n_rows, win_r)
    ctx_blk0 = (B * S) // Lc

    def variant(r):
        return jnp.clip(r - win_r // 2, 0, n_rows - win_r) - r + NA_WIN_R_MAX - 1

    R = NA_ROWS_PER_STEP
    assert n_rows % R == 0
    steps = n_rows // R
    bias_specs = [pl.BlockSpec((2, 1, GRID_W, win_r * GRID_W),
                               lambda hp, b, r, i=i: (hp, variant(r * R + i), 0, 0)) for i in range(R)]
    y = pl.pallas_call(
        functools.partial(_na_kernel, n_rows=n_rows, win_r=win_r, rows_per_step=R),
        grid=(HP, B, steps),
        in_specs=[pl.BlockSpec((R * GRID_W, LANES), lambda hp, b, r: (b * steps + r, hp)),
                  pl.BlockSpec((S, LANES), lambda hp, b, r: (b, HP + hp)),
                  pl.BlockSpec((S, LANES), lambda hp, b, r: (b, 2 * HP + hp)),
                  pl.BlockSpec((Lc, LANES), lambda hp, b, r: (ctx_blk0 + b, HP + hp)),
                  pl.BlockSpec((Lc, LANES), lambda hp, b, r: (ctx_blk0 + b, 2 * HP + hp))] + bias_specs,
        out_specs=pl.BlockSpec((R * GRID_W, LANES), lambda hp, b, r: (b * steps + r, hp)),
        out_shape=jax.ShapeDtypeStruct((B * S, NA_W), F32),
        compiler_params=pltpu.CompilerParams(
            dimension_semantics=("arbitrary", "arbitrary", "arbitrary"),
            vmem_limit_bytes=V7X_VMEM_LIMIT_BYTES),
        name="na_attention",
    )(qkv, qkv, qkv, qkv, qkv, *([bias] * R))
    yc = None
    if with_ctx:
        yc = pl.pallas_call(
            _ctx_attn_kernel,
            grid=(HP, B),
            in_specs=[pl.BlockSpec((Lc, LANES), lambda hp, b: (ctx_blk0 + b, hp)),
                      pl.BlockSpec((Lc, LANES), lambda hp, b: (ctx_blk0 + b, HP + hp)),
                      pl.BlockSpec((Lc, LANES), lambda hp, b: (ctx_blk0 + b, 2 * HP + hp))],
            out_specs=pl.BlockSpec((Lc, LANES), lambda hp, b: (b, hp)),
            out_shape=jax.ShapeDtypeStruct((B * Lc, NA_W), F32),
            compiler_params=pltpu.CompilerParams(dimension_semantics=("arbitrary", "arbitrary")),
            name="ctx_attention",
        )(qkv, qkv, qkv)
    return y, yc


def na_mixer(z_all, B, S, Lc, q_norm, k_norm, rpb, with_ctx):
    assert (C_NAQ, C_NAK, C_NAV) == (0, NA_W, 2 * NA_W)
    scale = NA_HEAD_DIM ** -0.5
    gains = jnp.concatenate([jnp.tile(q_norm.astype(F32) * scale, NA_HEADS),
                             jnp.tile(k_norm.astype(F32), NA_HEADS)])[None]
    y, yc = na_attention(na_qkv(z_all, gains), rpb, B, S, Lc, with_ctx)
    return y, yc


def _log_sigmoid(x):
    return jnp.minimum(x, 0.0) - jnp.log(1.0 + jnp.exp(-jnp.abs(x)))


def _gla_kernel(*refs, reverse, rope, finish, n_chunks):
    it = iter(refs)
    q_ref, k_ref, v_ref, gd_ref, w2_ref, gb_ref, s0_ref = (next(it) for _ in range(7))
    if rope:
        cos_ref, sin_ref = next(it), next(it)
    if finish:
        yo_ref, r_ref, g_ref = next(it), next(it), next(it)
    y_ref, sfin_ref, s_scr = next(it), next(it), next(it)

    C = GLA_CHUNK
    n = pl.program_id(1)
    H = range(GLA_HEADS)

    @pl.when(n == 0)
    def _():
        s_scr[...] = s0_ref[0]

    ti = lax.broadcasted_iota(jnp.int32, (C, C), 0)
    tj = lax.broadcasted_iota(jnp.int32, (C, C), 1)
    incl = (tj >= ti) if reverse else (tj <= ti)
    tri = jnp.where(incl, 1.0, 0.0).astype(BF16)

    pre = _mm(gd_ref[...], w2_ref[...], NN_DIMS, 3) + gb_ref[...]
    la = _log_sigmoid(pre) * (1.0 / GLA_GATE_TAU)
    cs = _cumsum_rows(tri, la)
    ref_i = C // 2 - 1 if reverse else C // 2
    last_i = 0 if reverse else C - 1
    b_ref = cs[ref_i:ref_i + 1, :]
    b_last = cs[last_i:last_i + 1, :]

    q = q_ref[...] * (GLA_DK ** -0.5)
    k = k_ref[...]
    if rope:
        lane = lax.broadcasted_iota(jnp.int32, q.shape, 1)
        first = (lane % (GLA_DK // 2)) < GLA_DK // 4
        cos = jnp.concatenate([cos_ref[...]] * GLA_HEADS, axis=1)
        sin = jnp.concatenate([sin_ref[...]] * GLA_HEADS, axis=1)

        def rot(x):
            partner = jnp.where(first, pltpu.roll(x, GLA_QK - GLA_DK // 4, axis=1),
                                pltpu.roll(x, GLA_DK // 4, axis=1))
            return x * cos + partner * sin

        q = rot(q)
        k = rot(k)
    qi = q * jnp.exp(cs - b_ref)
    kj = k * jnp.exp(b_ref - cs)
    qe = q * jnp.exp(cs)
    ke = k * jnp.exp(b_last - cs)
    dec = jnp.exp(b_last)
    v_all = v_ref[...]

    def hk(x, h):
        return x[:, h * GLA_DK:(h + 1) * GLA_DK]

    def hv(x, h):
        return x[:, h * GLA_DV:(h + 1) * GLA_DV]

    St = [s_scr[h] for h in H]
    V = [hv(v_all, h) for h in H]
    att = [jnp.where(incl, _mm(hk(qi, h), hk(kj, h), NT_DIMS), 0.0) for h in H]
    y = [_mm(att[h], V[h], NN_DIMS) + _mm(hk(qe, h), St[h], NT_DIMS) for h in H]
    kvt = [_mm(V[h], hk(ke, h), TN_DIMS) for h in H]
    for h in H:
        s_scr[h] = St[h] * hk(dec, h) + kvt[h]
    if finish:
        yo = yo_ref[...]
        r = r_ref[...]
        g = g_ref[...]
        outs = []
        for h in H:
            ys = y[h] + hv(yo, h)
            ms = jnp.mean(ys * ys, axis=-1, keepdims=True)
            rh = hv(r, h)
            outs.append(ys * lax.rsqrt(ms + EPS) * g * (rh * jax.nn.sigmoid(rh)))
        y_ref[...] = jnp.concatenate(outs, axis=1)
    else:
        y_ref[...] = jnp.concatenate(y, axis=1)

    @pl.when(n == n_chunks - 1)
    def _():
        sfin_ref[0] = s_scr[...]


def gla_rope_tables(seq_len):
    t = np.arange(seq_len)
    quarter = GLA_DK // 4
    inv = ROPE_BASE ** (-np.arange(0, 2 * quarter, 2, dtype=np.float64) / (2 * quarter))
    ar = (t // GRID_W)[:, None] * inv[None, :]
    ac = (t % GRID_W)[:, None] * inv[None, :]
    cos = np.concatenate([np.cos(ar), np.cos(ar), np.cos(ac), np.cos(ac)], axis=1)
    sin = np.concatenate([-np.sin(ar), np.sin(ar), -np.sin(ac), np.sin(ac)], axis=1)
    return jnp.asarray(cos, F32), jnp.asarray(sin, F32)


def gla_scan(z, w2pad, gate_b, s0, *, batch, seq_len, row0, reverse, rope, finish_with=None):
    N = seq_len // GLA_CHUNK
    blk0 = row0 // GLA_CHUNK
    off_q, off_k, off_v, off_r, off_gd = C_GQ, C_GK, C_GV, C_GR, C_GGD
    assert off_q % GLA_QK == 0 and off_k % GLA_QK == 0 and off_v % GLA_V == 0 and off_r % GLA_V == 0
    assert off_gd % LANES == 0

    def tb(n):
        return N - 1 - n if reverse else n

    def rb(b, n):
        return blk0 + b * N + tb(n)

    def ob(b, n):
        return b * N + tb(n)

    in_specs = [pl.BlockSpec((GLA_CHUNK, GLA_QK), lambda b, n: (rb(b, n), off_q // GLA_QK)),
                pl.BlockSpec((GLA_CHUNK, GLA_QK), lambda b, n: (rb(b, n), off_k // GLA_QK)),
                pl.BlockSpec((GLA_CHUNK, GLA_V), lambda b, n: (rb(b, n), off_v // GLA_V)),
                pl.BlockSpec((GLA_CHUNK, LANES), lambda b, n: (rb(b, n), off_gd // LANES)),
                pl.BlockSpec((LANES, GLA_QK), lambda b, n: (0, 0)),
                pl.BlockSpec((1, GLA_QK), lambda b, n: (0, 0)),
                pl.BlockSpec((1, GLA_HEADS, GLA_DV, GLA_DK), lambda b, n: (b, 0, 0, 0))]
    args = [z, z, z, z, w2pad, gate_b, s0]
    if rope:
        cos, sin = gla_rope_tables(seq_len)
        in_specs += [pl.BlockSpec((GLA_CHUNK, GLA_DK), lambda b, n: (tb(n), 0))] * 2
        args += [cos, sin]
    if finish_with is not None:
        y_other, norm_g = finish_with
        in_specs += [pl.BlockSpec((GLA_CHUNK, GLA_V), lambda b, n: (ob(b, n), 0)),
                     pl.BlockSpec((GLA_CHUNK, GLA_V), lambda b, n: (rb(b, n), off_r // GLA_V)),
                     pl.BlockSpec((1, GLA_DV), lambda b, n: (0, 0))]
        args += [y_other, z, norm_g]
    return pl.pallas_call(
        functools.partial(_gla_kernel, reverse=reverse, rope=rope, finish=finish_with is not None, n_chunks=N),
        grid=(batch, N),
        in_specs=in_specs,
        out_specs=[pl.BlockSpec((GLA_CHUNK, GLA_V), lambda b, n: (ob(b, n), 0)),
                   pl.BlockSpec((1, GLA_HEADS, GLA_DV, GLA_DK), lambda b, n: (b, 0, 0, 0))],
        out_shape=[jax.ShapeDtypeStruct((batch * seq_len, GLA_V), F32),
                   jax.ShapeDtypeStruct((batch, GLA_HEADS, GLA_DV, GLA_DK), F32)],
        scratch_shapes=[pltpu.VMEM((GLA_HEADS, GLA_DV, GLA_DK), F32)],
        compiler_params=pltpu.CompilerParams(
            dimension_semantics=("arbitrary", "arbitrary"),
            vmem_limit_bytes=V7X_VMEM_LIMIT_BYTES),
        name="gla_scan_rev" if reverse else "gla_scan",
    )(*args)


def gla_mixer(z_all, B, S, Lc, gate_w2, gate_b, norm_g, with_ctx):
    s0 = jnp.zeros((B, GLA_HEADS, GLA_DV, GLA_DK), F32)
    g = norm_g.astype(F32)[None]
    w2 = [jnp.zeros((LANES, GLA_QK), F32).at[d * GLA_GATE_RANK:(d + 1) * GLA_GATE_RANK].set(gate_w2[d])
          for d in range(2)]
    gb = [gate_b[d].astype(F32)[None] for d in range(2)]
    ctx = dict(batch=B, seq_len=Lc, row0=B * S, rope=False)
    lat = dict(batch=B, seq_len=S, row0=0, rope=True)
    yc_f, sc_f = gla_scan(z_all, w2[0], gb[0], s0, reverse=False, **ctx)
    yc, sc_b = gla_scan(z_all, w2[1], gb[1], s0, reverse=True, finish_with=(yc_f, g), **ctx)
    yl_f, _ = gla_scan(z_all, w2[0], gb[0], sc_f, reverse=False, **lat)
    yl, _ = gla_scan(z_all, w2[1], gb[1], sc_b, reverse=True, finish_with=(yl_f, g), **lat)
    return yl, (yc if with_ctx else None)


RW_PREP_TM = 256
RW_HALO = 8
RW_N_PIECES = 5


def _heads_sum(x):
    return jnp.concatenate([_head_sum(x[:, j * LANES:(j + 1) * LANES]) for j in range(x.shape[1] // LANES)],
                           axis=1)


def _rw_prep_kernel(*refs, tm, n_latent_tiles, tiles_per_seq):
    P = RW_N_PIECES
    z_refs, prev_refs, next_refs, mu_refs = refs[0:P], refs[P:2 * P], refs[2 * P:3 * P], refs[3 * P:4 * P]
    kk_ref, ka_ref, rk_ref, w0_ref, a0_ref, w2_ref, a2_ref, g2_ref = refs[4 * P:4 * P + 8]
    (r_o, v_o, kap_o, lw0_o, kd0_o, b0_o, lw1_o, kd1_o, b1_o, bonus_o, g_o) = refs[4 * P + 8:]
    i = pl.program_id(0)
    latent = i < n_latent_tiles
    first = jnp.logical_or(jnp.logical_not(latent), i % tiles_per_seq == 0)
    last = jnp.logical_or(jnp.logical_not(latent), i % tiles_per_seq == tiles_per_seq - 1)

    def shifted(j):
        z = z_refs[j][...]
        row = lax.broadcasted_iota(jnp.int32, z.shape, 0)
        prev_row = jnp.where(first, 0.0, prev_refs[j][RW_HALO - 1:RW_HALO, :])
        next_row = jnp.where(last, 0.0, next_refs[j][0:1, :])
        prev = jnp.where(row == 0, prev_row, pltpu.roll(z, 1, axis=0))
        nxt = jnp.where(row == tm - 1, next_row, pltpu.roll(z, tm - 1, axis=0))
        return z + mu_refs[j][...] * (0.5 * (prev + nxt) - z)

    r, k, v, lr, gd = (shifted(j) for j in range(P))
    kkr = k * kk_ref[...]
    kap = kkr * lax.rsqrt(_heads_sum(kkr * kkr) + EPS)
    r_o[...] = r
    v_o[...] = v
    kap_o[...] = kap
    bonus_o[...] = _heads_sum(r * k * rk_ref[...]) * v
    g_o[...] = _mm(jax.nn.sigmoid(gd), g2_ref[...], NN_DIMS, 3)
    ka = ka_ref[...]
    for d, (lw_o, kd_o, b_o) in enumerate(((lw0_o, kd0_o, b0_o), (lw1_o, kd1_o, b1_o))):
        wd = lr[:, d * LANES:(d + 1) * LANES]
        ad = lr[:, (2 + d) * LANES:(3 + d) * LANES]
        w_log = _log_sigmoid(w0_ref[d:d + 1, :] + _mm(jnp.tanh(wd), w2_ref[d], NN_DIMS, 3)) - 0.5
        lw_o[...] = -jnp.exp(w_log)
        a = jax.nn.sigmoid(a0_ref[d:d + 1, :] + _mm(ad, a2_ref[d], NN_DIMS, 3))
        kd_o[...] = k * (1.0 + (a - 1.0) * ka)
        b_o[...] = kap * a


def rwkv_prep(z_all, n_latent_rows, seq_len, mu_p, k_k, k_a, r_k, w0, a0, w2p, a2p, g2):
    M = z_all.shape[0]
    tm = RW_PREP_TM
    assert M % tm == 0 and n_latent_rows % tm == 0 and seq_len % tm == 0
    widths = (RW_W, RW_W, RW_W, RW_LR_W, RW_GATE_RANK)
    offs = (C_RR, C_RK, C_RV, C_RLR, C_RGD)
    n_halo_blocks = M // RW_HALO
    z_specs, prev_specs, next_specs, mu_specs = [], [], [], []
    for w, o in zip(widths, offs):
        assert o % w == 0
        cb = o // w
        z_specs.append(pl.BlockSpec((tm, w), lambda i, cb=cb: (i, cb)))
        prev_specs.append(pl.BlockSpec(
            (RW_HALO, w), lambda i, cb=cb: (jnp.maximum(i * (tm // RW_HALO) - 1, 0), cb)))
        next_specs.append(pl.BlockSpec(
            (RW_HALO, w), lambda i, cb=cb: (jnp.minimum((i + 1) * (tm // RW_HALO), n_halo_blocks - 1), cb)))
        mu_specs.append(pl.BlockSpec((1, w), lambda i: (0, 0)))

    def full(shape):
        return pl.BlockSpec(shape, lambda i: (0,) * len(shape))

    par_specs = [full((1, RW_W))] * 3 + [full((2, RW_W))] * 2 + [full((2, LANES, RW_W))] * 2 \
        + [full((RW_GATE_RANK, RW_W))]
    out_spec = pl.BlockSpec((tm, RW_W), lambda i: (i, 0))
    return pl.pallas_call(
        functools.partial(_rw_prep_kernel, tm=tm, n_latent_tiles=n_latent_rows // tm,
                          tiles_per_seq=seq_len // tm),
        grid=(M // tm,),
        in_specs=z_specs + prev_specs + next_specs + mu_specs + par_specs,
        out_specs=[out_spec] * 11,
        out_shape=[jax.ShapeDtypeStruct((M, RW_W), F32)] * 11,
        compiler_params=pltpu.CompilerParams(
            dimension_semantics=("arbitrary",), vmem_limit_bytes=V7X_VMEM_LIMIT_BYTES),
        name="rwkv_prep",
    )(*([z_all] * 15), *mu_p, k_k, k_a, r_k, w0, a0, w2p, a2p, g2)


def _rw_chunk_kernel(*refs, reverse, n_par, n_chunks, passes, finish):
    r_ref, v_ref, kap_ref, lw_ref, kd_ref, bet_ref, s0_ref = refs[:7]
    if finish:
        yo_ref, bonus_ref, g_ref, gam_ref, gnb_ref = refs[7:12]
    y_ref, sfin_ref, s_scr = refs[-3:]
    _rw_chunk_body(r_ref, v_ref, kap_ref, lw_ref, kd_ref, bet_ref, s0_ref, y_ref, sfin_ref, s_scr,
                   (yo_ref, bonus_ref, g_ref, gam_ref, gnb_ref) if finish else None,
                   reverse=reverse, n_par=n_par, n_chunks=n_chunks, passes=passes)


def _rw_chunk_body(r_ref, v_ref, kap_ref, lw_ref, kd_ref, bet_ref, s0_ref, y_ref, sfin_ref, s_scr, fin_refs,
                   *, reverse, n_par, n_chunks, passes):
    C = RW_CHUNK
    n = pl.program_id(2)
    P = range(n_par)

    @pl.when(n == 0)
    def _():
        s_scr[...] = s0_ref[0, 0]

    row = lax.broadcasted_iota(jnp.int32, (C, LANES), 0)
    lane = lax.broadcasted_iota(jnp.int32, (C, LANES), 1)
    col = lane % RW_HEAD
    head0 = lane < RW_HEAD
    strict = (col > row) if reverse else (col < row)
    incl = (col >= row) if reverse else (col <= row)
    ti = lax.broadcasted_iota(jnp.int32, (C, C), 0)
    tj = lax.broadcasted_iota(jnp.int32, (C, C), 1)
    tri = jnp.where((tj >= ti) if reverse else (tj <= ti), 1.0, 0.0).astype(BF16)
    level_masks = []
    s = 1
    while s < C:
        level_masks.append(strict & (row // (2 * s) == col // (2 * s)) & (row // s != col // s))
        s *= 2
    eye = jnp.where(row == col, 1.0, 0.0)
    brow = lax.broadcasted_iota(jnp.int32, (LANES, LANES), 0) // RW_HEAD
    bcol = lax.broadcasted_iota(jnp.int32, (LANES, LANES), 1) // RW_HEAD
    same_head = brow == bcol

    def sl(x, p):
        return x[:, p * LANES:(p + 1) * LANES]

    def bd(x):
        return jnp.concatenate([jnp.where(head0, x, 0.0), jnp.where(head0, 0.0, x)], axis=0)

    def pp(a_pair, b_pair):
        return _mm(a_pair, bd(b_pair), NN_DIMS, passes)

    lw_all = lw_ref[...]
    cs_all = _cumsum_rows(tri, lw_all)
    tot_all = cs_all[0:1, :] if reverse else cs_all[C - 1:C, :]
    inv_all = jnp.exp(-cs_all)
    fin_all = jnp.exp(tot_all - cs_all)
    kd_all = kd_ref[...]
    bet_all = bet_ref[...]
    kap_t = kap_ref[...] * jnp.exp(cs_all - lw_all)
    r_t = r_ref[...] * jnp.exp(cs_all)
    k_t = kd_all * inv_all
    b_t = bet_all * inv_all
    k_h = kd_all * fin_all
    b_h = bet_all * fin_all
    dec = jnp.exp(tot_all)
    v_all = v_ref[...]

    S = [s_scr[p] for p in P]
    V = [sl(v_all, p) for p in P]
    lhs = [jnp.concatenate([sl(kap_t, p), sl(r_t, p)], axis=0) for p in P]
    a_k = [_mm(lhs[p], bd(sl(k_t, p)), NT_DIMS, passes) for p in P]
    a_b = [_mm(lhs[p], bd(sl(b_t, p)), NT_DIMS, passes) for p in P]
    s0_both = [_mm(lhs[p], S[p], NT_DIMS, passes) for p in P]
    lmat = [jnp.where(strict, a_b[p][:C], 0.0) for p in P]
    t_inv = [eye - jnp.where(level_masks[0], lmat[p], 0.0) for p in P]
    for m in level_masks[1:]:
        tmp = [pp(jnp.where(m, lmat[p], 0.0), t_inv[p]) for p in P]
        t_inv = [t_inv[p] - pp(t_inv[p], tmp[p]) for p in P]
    x = [s0_both[p][:C] + pp(jnp.where(strict, a_k[p][:C], 0.0), V[p]) for p in P]
    u = [pp(t_inv[p], x[p]) for p in P]
    y = [s0_both[p][C:] + pp(jnp.where(incl, a_k[p][C:], 0.0), V[p])
         - pp(jnp.where(incl, a_b[p][C:], 0.0), u[p]) for p in P]
    upd = [_mm(V[p], sl(k_h, p), TN_DIMS, passes) - _mm(u[p], sl(b_h, p), TN_DIMS, passes) for p in P]
    for p in P:
        s_scr[p] = S[p] * sl(dec, p) + jnp.where(same_head, upd[p], 0.0)
    if fin_refs is None:
        y_ref[...] = jnp.concatenate(y, axis=1)
    else:
        yo_ref, bonus_ref, g_ref, gam_ref, gnb_ref = fin_refs
        ys = jnp.concatenate(y, axis=1) + yo_ref[...]
        cen = ys - _heads_sum(ys) * (1.0 / RW_HEAD)
        var = _heads_sum(cen * cen) * (1.0 / RW_HEAD)
        out = cen * lax.rsqrt(var + RW_GN_EPS) * gam_ref[...] + gnb_ref[...] + bonus_ref[...]
        y_ref[...] = out * g_ref[...]

    @pl.when(n == n_chunks - 1)
    def _():
        sfin_ref[0, 0] = s_scr[...]


def rwkv_chunk_scan(r, v, kap, lw, kd, bet, s0, *, batch, seq_len, row0, reverse, finish_with=None):
    n_par = RW_PAIRS_PER_STEP
    W = r.shape[1]
    G = W // (n_par * LANES)
    N = seq_len // RW_CHUNK
    blk0 = row0 // RW_CHUNK

    def tb(n):
        return N - 1 - n if reverse else n

    seq_in = pl.BlockSpec((RW_CHUNK, n_par * LANES), lambda b, g, n: (blk0 + b * N + tb(n), g))
    seq_out = pl.BlockSpec((RW_CHUNK, n_par * LANES), lambda b, g, n: (b * N + tb(n), g))
    st_spec = pl.BlockSpec((1, 1, n_par, LANES, LANES), lambda b, g, n: (b, g, 0, 0, 0))
    in_specs = [seq_in] * 6 + [st_spec]
    args = [r, v, kap, lw, kd, bet, s0]
    if finish_with is not None:
        y_other, bonus, gate, gamma, beta = finish_with
        vec_spec = pl.BlockSpec((1, n_par * LANES), lambda b, g, n: (0, g))
        in_specs += [seq_out, seq_in, seq_in, vec_spec, vec_spec]
        args += [y_other, bonus, gate, gamma, beta]
    return pl.pallas_call(
        functools.partial(_rw_chunk_kernel, reverse=reverse, n_par=n_par, n_chunks=N, passes=RW_PASSES,
                          finish=finish_with is not None),
        grid=(batch, G, N),
        in_specs=in_specs,
        out_specs=[seq_out, st_spec],
        out_shape=[jax.ShapeDtypeStruct((batch * seq_len, W), F32),
                   jax.ShapeDtypeStruct((batch, G, n_par, LANES, LANES), F32)],
        scratch_shapes=[pltpu.VMEM((n_par, LANES, LANES), F32)],
        compiler_params=pltpu.CompilerParams(
            dimension_semantics=("arbitrary", "arbitrary", "arbitrary"),
            vmem_limit_bytes=V7X_VMEM_LIMIT_BYTES),
        name="rwkv_chunk_scan_rev" if reverse else "rwkv_chunk_scan",
    )(*args)


def _pad_rows(w, rows):
    return jnp.pad(w.astype(F32), ((0, 0), (0, rows - w.shape[1]), (0, 0)))


def rwkv_mixer(z_all, B, S, Lc, mu, w0, w2, a0, a2, g2, k_k, k_a, r_k, gn_g, gn_b, with_ctx):
    assert Lc == RW_PREP_TM
    mu = mu.astype(F32)
    o = np.cumsum((0,) + RW_SPLITS)
    lr_pad = jnp.zeros((LANES - RW_DECAY_RANK,), F32)
    mu_lr = jnp.concatenate([mu[o[3]:o[3] + RW_DECAY_RANK], lr_pad, mu[o[3] + RW_DECAY_RANK:o[4]], lr_pad,
                             mu[o[4]:o[4] + RW_A_RANK], lr_pad, mu[o[4] + RW_A_RANK:o[5]], lr_pad])
    mu_p = [mu[o[0]:o[1]][None], mu[o[1]:o[2]][None], mu[o[2]:o[3]][None], mu_lr[None], mu[o[5]:o[6]][None]]
    row = lambda t: t.astype(F32).reshape(1, RW_W)
    prep = rwkv_prep(z_all, B * S, S, mu_p, row(k_k), row(k_a), row(r_k), w0.astype(F32), a0.astype(F32),
                     _pad_rows(w2, LANES), _pad_rows(a2, LANES), g2.astype(F32))
    r, v, kap, lw0, kd0, b0, lw1, kd1, b1, bonus, gate = prep
    fin = (bonus, gate, row(gn_g), row(gn_b))
    s0 = jnp.zeros((B, RW_W // (RW_PAIRS_PER_STEP * LANES), RW_PAIRS_PER_STEP, LANES, LANES), F32)
    ctx = dict(batch=B, seq_len=Lc, row0=B * S)
    lat = dict(batch=B, seq_len=S, row0=0)
    yc_f, sc_f = rwkv_chunk_scan(r, v, kap, lw0, kd0, b0, s0, reverse=False, **ctx)
    yc, sc_b = rwkv_chunk_scan(r, v, kap, lw1, kd1, b1, s0, reverse=True, finish_with=(yc_f,) + fin, **ctx)
    yl_f, _ = rwkv_chunk_scan(r, v, kap, lw0, kd0, b0, sc_f, reverse=False, **lat)
    yl, _ = rwkv_chunk_scan(r, v, kap, lw1, kd1, b1, sc_b, reverse=True, finish_with=(yl_f,) + fin, **lat)
    return yl, (yc if with_ctx else None)


def relayout_w_in(w):
    o = IN_OFFSETS
    rw = o[8] + np.cumsum((0,) + RW_SPLITS)
    zeros = lambda n: jnp.zeros((w.shape[0], n), w.dtype)
    seg = lambda lo, n: w[:, lo:lo + n]
    lr_pad = LANES - RW_DECAY_RANK
    pieces = [
        seg(o[0], 3 * NA_W),
        seg(o[5], GLA_V), seg(o[6], GLA_V),
        seg(rw[0], 3 * RW_W),
        seg(o[9], N_BRANCH * D_MODEL),
        seg(o[3], GLA_QK), seg(o[4], GLA_QK),
        seg(rw[3], RW_DECAY_RANK), zeros(lr_pad), seg(rw[3] + RW_DECAY_RANK, RW_DECAY_RANK), zeros(lr_pad),
        seg(rw[4], RW_A_RANK), zeros(lr_pad), seg(rw[4] + RW_A_RANK, RW_A_RANK), zeros(lr_pad),
        seg(rw[5], RW_GATE_RANK),
        seg(o[7], 2 * GLA_GATE_RANK),
    ]
    used = sum(x.shape[1] for x in pieces)
    assert used == C_GGD + 2 * GLA_GATE_RANK
    return jnp.concatenate(pieces + [zeros(Z_W - used)], axis=1).astype(BF16)


def token_mixers(x, ctx, mod, p, B, S, Lc, with_ctx):
    h_all = prenorm(x, ctx, p['norm1_g'], mod, seq_len=S)
    z_all = matmul(h_all, relayout_w_in(p['w_in']), tn=1024)
    y_na, yc_na = na_mixer(z_all, B, S, Lc, p['na_q_norm'], p['na_k_norm'], p['na_rpb'], with_ctx)
    y_gla, yc_gla = gla_mixer(z_all, B, S, Lc, p['gla_gate_w2'], p['gla_gate_b'], p['gla_norm_g'], with_ctx)
    y_rw, yc_rw = rwkv_mixer(z_all, B, S, Lc, p['rw_mu'], p['rw_w0'], p['rw_w2'], p['rw_a0'], p['rw_a2'],
                             p['rw_g2'], p['rw_k_k'], p['rw_k_a'], p['rw_r_k'], p['rw_gn_g'],
                             p['rw_gn_b'], with_ctx)
    merged = merge_branches((y_na, y_gla, y_rw), z_all, 0, p['w_branch'])
    x = residual_matmul(merged, p['w_out'], x, mod, gate_row=2, rows_per_cond=S)
    if with_ctx:
        merged_c = merge_branches((yc_na, yc_gla, yc_rw), z_all, B * S, p['w_branch'])
        ctx = residual_matmul(merged_c, p['w_out'], ctx, mod, gate_row=2, fixed_row=B)
    return x, ctx


def moe_sublayer(x, mod, p, *, rows_per_cond=None, fixed_row=None):
    h2, comb_t = prenorm_router(x, p['norm2_g'], mod, p['router_w_t'], p['router_bias'],
                                rows_per_cond=rows_per_cond, fixed_row=fixed_row)
    return moe_experts(h2, comb_t.T, p['moe_w_gate'], p['moe_w_up'], p['moe_w_down'], x, mod,
                       rows_per_cond=rows_per_cond, fixed_row=fixed_row)


def kernel(x, c, ctx, c_ctx, ada_w, ada_b, norm1_g, norm2_g, w_in, na_q_norm, na_k_norm, na_rpb,
           gla_gate_w2, gla_gate_b, gla_norm_g, rw_mu, rw_w0, rw_w2, rw_a0, rw_a2, rw_g2,
           rw_k_k, rw_k_a, rw_r_k, rw_gn_g, rw_gn_b, w_branch, w_out, router_w, router_bias,
           moe_w_gate, moe_w_up, moe_w_down):
    B, S, D = x.shape
    Lc = ctx.shape[1]
    assert B + 1 <= N_COND and S % 512 == 0 and (B * Lc) % 512 == 0 and Lc % ROW_TM == 0
    c_rows = jnp.concatenate([c, c_ctx[None], jnp.zeros((N_COND - B - 1, D), c.dtype)], axis=0).astype(F32)
    mods = ada_modulation(c_rows, ada_w, ada_b).reshape(DEPTH, N_COND, N_MOD, D)
    x = x.reshape(B * S, D)
    ctx = ctx.reshape(B * Lc, D)
    router_w_t = router_w.astype(F32).T
    router_b = router_bias.astype(F32).reshape(N_EXPERTS, 1)
    for l in range(DEPTH):
        with_ctx = l < DEPTH - 1
        p = {
            'w_in': w_in[l], 'na_q_norm': na_q_norm[l], 'na_k_norm': na_k_norm[l], 'na_rpb': na_rpb[l],
            'gla_gate_w2': gla_gate_w2[l], 'gla_gate_b': gla_gate_b[l], 'gla_norm_g': gla_norm_g[l],
            'rw_mu': rw_mu[l], 'rw_w0': rw_w0[l], 'rw_w2': rw_w2[l], 'rw_a0': rw_a0[l], 'rw_a2': rw_a2[l],
            'rw_g2': rw_g2[l], 'rw_k_k': rw_k_k[l], 'rw_k_a': rw_k_a[l], 'rw_r_k': rw_r_k[l],
            'rw_gn_g': rw_gn_g[l], 'rw_gn_b': rw_gn_b[l], 'w_branch': w_branch[l], 'w_out': w_out[l],
            'norm1_g': norm1_g[l].astype(F32)[None], 'norm2_g': norm2_g[l].astype(F32)[None],
            'router_w_t': router_w_t, 'router_bias': router_b,
            'moe_w_gate': moe_w_gate[l].astype(BF16), 'moe_w_up': moe_w_up[l].astype(BF16),
            'moe_w_down': moe_w_down[l].astype(BF16),
        }
        x, ctx = token_mixers(x, ctx, mods[l], p, B, S, Lc, with_ctx)
        x = moe_sublayer(x, mods[l], p, rows_per_cond=S)
        if with_ctx:
            ctx = moe_sublayer(ctx, mods[l], p, fixed_row=B)
    return x.reshape(B, S, D)
```

```python
import functools

import jax
import jax.numpy as jnp
from jax import lax
import numpy as np
from jax.experimental import pallas as pl
from jax.experimental.pallas import tpu as pltpu
from jax.experimental.pallas import tpu_sc as plsc

D_MODEL = 2048
DEPTH = 2
GRID_W = 64

NA_HEADS = 16
NA_HEAD_DIM = 64
NA_WIN_R_MAX = 8
NA_WIN_C = 16
NA_W = NA_HEADS * NA_HEAD_DIM
NA_ROWS_PER_STEP = 8

GLA_HEADS = 4
GLA_DK = 128
GLA_DV = 256
GLA_QK = GLA_HEADS * GLA_DK
GLA_V = GLA_HEADS * GLA_DV
GLA_GATE_RANK = 16
GLA_GATE_TAU = 16.0
GLA_CHUNK = 64

RW_HEADS = 16
RW_HEAD = 64
RW_W = RW_HEADS * RW_HEAD
RW_DECAY_RANK = 96
RW_A_RANK = 96
RW_GATE_RANK = 256
RW_SPLITS = (RW_W, RW_W, RW_W, 2 * RW_DECAY_RANK, 2 * RW_A_RANK, RW_GATE_RANK)
RW_IN = sum(RW_SPLITS)
RW_GN_EPS = 64e-5
RW_CHUNK = 64
RW_PASSES = 1
RW_PAIRS_PER_STEP = 8

N_BRANCH = 3
BRANCH_W = 1024
IN_SPLITS = (NA_W, NA_W, NA_W, GLA_QK, GLA_QK, GLA_V, GLA_V, 2 * GLA_GATE_RANK, RW_IN, N_BRANCH * D_MODEL)
D_IN = sum(IN_SPLITS)
IN_OFFSETS = tuple(int(o) for o in np.cumsum((0,) + IN_SPLITS[:-1]))

LANES = 128
RW_LR_W = 4 * LANES
C_NAQ, C_NAK, C_NAV = 0, 1024, 2048
C_GV, C_GR = 3072, 4096
C_RR, C_RK, C_RV = 5120, 6144, 7168
C_GATE = 8192
C_GQ, C_GK = 14336, 14848
C_RLR, C_RGD, C_GGD = 15360, 15872, 16128
Z_W = 16384

N_EXPERTS = 16
N_GROUPS = 4
EXPERTS_PER_GROUP = N_EXPERTS // N_GROUPS
TOP_K = 2
D_EXPERT = 512

ROPE_BASE = 10000.0
EPS = 1e-6
F32 = jnp.float32
BF16 = jnp.bfloat16

V7X_VMEM_LIMIT_BYTES = 48 * 1024 * 1024

NEG_BIG = -1e30
NT_DIMS = (((1,), (1,)), ((), ()))
TN_DIMS = (((0,), (0,)), ((), ()))
NN_DIMS = (((1,), (0,)), ((), ()))


def _split_bf16(x):
    hi = x.astype(BF16)
    lo = (x - hi.astype(F32)).astype(BF16)
    return hi, lo


def _mm(a, b, dims=NN_DIMS, passes=1):
    if passes == 1:
        return lax.dot_general(a.astype(BF16), b.astype(BF16), dims, preferred_element_type=F32)
    a_hi, a_lo = _split_bf16(a)
    b_hi, b_lo = _split_bf16(b)
    out = lax.dot_general(a_hi, b_hi, dims, preferred_element_type=F32)
    out = out + lax.dot_general(a_hi, b_lo, dims, preferred_element_type=F32)
    return out + lax.dot_general(a_lo, b_hi, dims, preferred_element_type=F32)


def _cumsum_rows(tri, x):
    hi, lo = _split_bf16(x)
    return jnp.dot(tri, hi, preferred_element_type=F32) + jnp.dot(tri, lo, preferred_element_type=F32)


def _mm_kernel(a_ref, b_ref, o_ref):
    o_ref[...] = jnp.dot(a_ref[...], b_ref[...], preferred_element_type=F32).astype(o_ref.dtype)


def matmul(a, b, *, tm=512, tn=512, out_dtype=F32):
    M, K = a.shape
    _, N = b.shape
    assert M % tm == 0 and N % tn == 0, (M, N, tm, tn)
    return pl.pallas_call(
        _mm_kernel,
        grid=(N // tn, M // tm),
        in_specs=[pl.BlockSpec((tm, K), lambda n, m: (m, 0)),
                  pl.BlockSpec((K, tn), lambda n, m: (0, n))],
        out_specs=pl.BlockSpec((tm, tn), lambda n, m: (m, n)),
        out_shape=jax.ShapeDtypeStruct((M, N), out_dtype),
        compiler_params=pltpu.CompilerParams(
            dimension_semantics=("arbitrary", "arbitrary"),
            vmem_limit_bytes=V7X_VMEM_LIMIT_BYTES),
        name="matmul",
    )(a.astype(BF16), b.astype(BF16))


def _moe_kernel(h_ref, comb_ref, wg_ref, wu_ref, wd_ref, x_ref, mod_ref, o_ref, acc_ref):
    e = pl.program_id(1)

    @pl.when(e == 0)
    def _():
        acc_ref[...] = jnp.zeros_like(acc_ref)

    h = h_ref[...]
    g = jnp.dot(h, wg_ref[0], preferred_element_type=F32)
    u = jnp.dot(h, wu_ref[0], preferred_element_type=F32)
    comb = comb_ref[...]
    lane = lax.broadcasted_iota(jnp.int32, comb.shape, 1)
    ce = jnp.sum(jnp.where(lane == e, comb, 0.0), axis=1, keepdims=True)
    act = (g * jax.nn.sigmoid(g)) * u * ce
    acc_ref[...] += jnp.dot(act.astype(BF16), wd_ref[0], preferred_element_type=F32)

    @pl.when(e == N_EXPERTS - 1)
    def _():
        o_ref[...] = x_ref[...] + mod_ref[0, N_MOD - 1:N_MOD, :] * acc_ref[...]


def moe_experts(h, comb, w_gate, w_up, w_down, x, mod, *, rows_per_cond=None, fixed_row=None, tm=512):
    M, D = h.shape
    assert M % tm == 0
    cond = _cond_of_tile(rows_per_cond, tm, fixed_row)
    return pl.pallas_call(
        _moe_kernel,
        grid=(M // tm, N_EXPERTS),
        in_specs=[pl.BlockSpec((tm, D), lambda m, e: (m, 0)),
                  pl.BlockSpec((tm, N_EXPERTS), lambda m, e: (m, 0)),
                  pl.BlockSpec((1, D, D_EXPERT), lambda m, e: (e, 0, 0)),
                  pl.BlockSpec((1, D, D_EXPERT), lambda m, e: (e, 0, 0)),
                  pl.BlockSpec((1, D_EXPERT, D), lambda m, e: (e, 0, 0)),
                  pl.BlockSpec((tm, D), lambda m, e: (m, 0)),
                  pl.BlockSpec((1, N_MOD, D), lambda m, e: (cond(m), 0, 0))],
        out_specs=pl.BlockSpec((tm, D), lambda m, e: (m, 0)),
        out_shape=jax.ShapeDtypeStruct((M, D), F32),
        scratch_shapes=[pltpu.VMEM((tm, D), F32)],
        compiler_params=pltpu.CompilerParams(
            dimension_semantics=("arbitrary", "arbitrary"),
            vmem_limit_bytes=V7X_VMEM_LIMIT_BYTES),
        name="moe_experts",
    )(h, comb, w_gate, w_up, w_down, x, mod)


N_COND = 8
N_MOD = 6
ROW_TM = 256


def _ada_kernel(c_ref, w_ref, b_ref, o_ref):
    c = c_ref[...]
    o_ref[0] = _mm(c * jax.nn.sigmoid(c), w_ref[0], NN_DIMS, 3) + b_ref[0]


def ada_modulation(c_rows, ada_w, ada_b, *, tn=1024):
    L, D, N = ada_w.shape
    return pl.pallas_call(
        _ada_kernel,
        grid=(L, N // tn),
        in_specs=[pl.BlockSpec((N_COND, D), lambda l, n: (0, 0)),
                  pl.BlockSpec((1, D, tn), lambda l, n: (l, 0, n)),
                  pl.BlockSpec((1, 1, tn), lambda l, n: (l, 0, n))],
        out_specs=pl.BlockSpec((1, N_COND, tn), lambda l, n: (l, 0, n)),
        out_shape=jax.ShapeDtypeStruct((L, N_COND, N), F32),
        compiler_params=pltpu.CompilerParams(
            dimension_semantics=("arbitrary", "arbitrary"), vmem_limit_bytes=V7X_VMEM_LIMIT_BYTES),
        name="ada_modulation",
    )(c_rows, ada_w, ada_b.reshape(L, 1, N))


def _norm_mod(x, g, shift, scale):
    y = x * lax.rsqrt(jnp.mean(x * x, axis=-1, keepdims=True) + EPS)
    return y * g * (1.0 + scale) + shift


def _prenorm_kernel(x_ref, c_ref, g_ref, mod_ref, o_ref, *, n_latent_tiles):
    def emit(src_ref):
        o_ref[...] = _norm_mod(src_ref[...], g_ref[...], mod_ref[0, 0:1, :], mod_ref[0, 1:2, :]).astype(o_ref.dtype)

    @pl.when(pl.program_id(0) < n_latent_tiles)
    def _():
        emit(x_ref)

    @pl.when(pl.program_id(0) >= n_latent_tiles)
    def _():
        emit(c_ref)


def _top2_sum(a, b, c, d):
    hi1, lo1 = jnp.maximum(a, b), jnp.minimum(a, b)
    hi2, lo2 = jnp.maximum(c, d), jnp.minimum(c, d)
    return jnp.maximum(hi1, hi2) + jnp.maximum(jnp.minimum(hi1, hi2), jnp.maximum(lo1, lo2))


def _prenorm_router_kernel(x_ref, g_ref, mod_ref, rw_ref, rb_ref, o_ref, comb_ref):
    h = _norm_mod(x_ref[...], g_ref[...], mod_ref[0, 3:4, :], mod_ref[0, 4:5, :])
    o_ref[...] = h.astype(o_ref.dtype)
    s = jax.nn.sigmoid(_mm(rw_ref[...], h, NT_DIMS, 3))
    biased = s + rb_ref[...]
    rows = [biased[e:e + 1, :] for e in range(N_EXPERTS)]
    G = EXPERTS_PER_GROUP
    score = [_top2_sum(*rows[g * G:(g + 1) * G]) for g in range(N_GROUPS)]
    picked = []
    for e in range(N_EXPERTS):
        g = e // G
        ok = None
        for g2 in range(N_GROUPS):
            if g2 != g:
                t = (score[g] > score[g2]) if g2 < g else (score[g] >= score[g2])
                ok = t if ok is None else jnp.logical_and(ok, t)
        rank = 0.0
        for e2 in range(g * G, (g + 1) * G):
            if e2 != e:
                ahead = (rows[e2] >= rows[e]) if e2 < e else (rows[e2] > rows[e])
                rank = rank + jnp.where(ahead, 1.0, 0.0)
        picked.append(jnp.where(jnp.logical_and(ok, rank < TOP_K), s[e:e + 1, :], 0.0))
    w = jnp.concatenate(picked, axis=0)
    comb_ref[...] = w / jnp.sum(w, axis=0, keepdims=True)


def _cond_of_tile(rows_per_cond, tile_rows, fixed_row):
    if fixed_row is not None:
        return lambda m: fixed_row
    assert rows_per_cond % tile_rows == 0
    return lambda m: m // (rows_per_cond // tile_rows)


def prenorm(x, ctx, gain, mod, *, seq_len):
    n_lat, n_ctx = x.shape[0] // ROW_TM, ctx.shape[0] // ROW_TM
    n_batch = x.shape[0] // seq_len
    cond = _cond_of_tile(seq_len, ROW_TM, None)
    return pl.pallas_call(
        functools.partial(_prenorm_kernel, n_latent_tiles=n_lat),
        grid=(n_lat + n_ctx,),
        in_specs=[pl.BlockSpec((ROW_TM, D_MODEL), lambda i: (jnp.minimum(i, n_lat - 1), 0)),
                  pl.BlockSpec((ROW_TM, D_MODEL), lambda i: (jnp.maximum(i - n_lat, 0), 0)),
                  pl.BlockSpec((1, D_MODEL), lambda i: (0, 0)),
                  pl.BlockSpec((1, N_MOD, D_MODEL), lambda i: (jnp.where(i < n_lat, cond(i), n_batch), 0, 0))],
        out_specs=pl.BlockSpec((ROW_TM, D_MODEL), lambda i: (i, 0)),
        out_shape=jax.ShapeDtypeStruct((x.shape[0] + ctx.shape[0], D_MODEL), BF16),
        compiler_params=pltpu.CompilerParams(dimension_semantics=("arbitrary",)),
        name="prenorm",
    )(x, ctx, gain, mod)


def prenorm_router(x, gain, mod, router_w_t, router_bias, *, rows_per_cond=None, fixed_row=None):
    M = x.shape[0]
    cond = _cond_of_tile(rows_per_cond, ROW_TM, fixed_row)
    x_spec = pl.BlockSpec((ROW_TM, D_MODEL), lambda i: (i, 0))
    return pl.pallas_call(
        _prenorm_router_kernel,
        grid=(M // ROW_TM,),
        in_specs=[x_spec,
                  pl.BlockSpec((1, D_MODEL), lambda i: (0, 0)),
                  pl.BlockSpec((1, N_MOD, D_MODEL), lambda i: (cond(i), 0, 0)),
                  pl.BlockSpec((N_EXPERTS, D_MODEL), lambda i: (0, 0)),
                  pl.BlockSpec((N_EXPERTS, 1), lambda i: (0, 0))],
        out_specs=[x_spec, pl.BlockSpec((N_EXPERTS, ROW_TM), lambda i: (0, i))],
        out_shape=[jax.ShapeDtypeStruct((M, D_MODEL), F32),
                   jax.ShapeDtypeStruct((N_EXPERTS, M), F32)],
        compiler_params=pltpu.CompilerParams(dimension_semantics=("arbitrary",)),
        name="prenorm_router",
    )(x, gain, mod, router_w_t, router_bias)


def _merge_kernel(y0_ref, y1_ref, y2_ref, g0_ref, g1_ref, g2_ref, wb_ref, o_ref):
    acc = 0.0
    for i, (y_ref, g_ref) in enumerate(((y0_ref, g0_ref), (y1_ref, g1_ref), (y2_ref, g2_ref))):
        acc = acc + jax.nn.sigmoid(g_ref[...]) * jnp.dot(y_ref[...].astype(BF16), wb_ref[i],
                                                         preferred_element_type=F32)
    o_ref[...] = acc.astype(o_ref.dtype)


def merge_branches(ys, z_all, row0, w_branch, *, tm=512, tn=512):
    M = ys[0].shape[0]
    assert M % tm == 0 and row0 % tm == 0 and C_GATE % tn == 0 and D_MODEL % tn == 0
    row_blk0 = row0 // tm
    y_spec = pl.BlockSpec((tm, BRANCH_W), lambda m, n: (m, 0))

    def gate_spec(i):
        return pl.BlockSpec((tm, tn), lambda m, n: (row_blk0 + m, (C_GATE + i * D_MODEL) // tn + n))

    return pl.pallas_call(
        _merge_kernel,
        grid=(M // tm, D_MODEL // tn),
        in_specs=[y_spec] * 3 + [gate_spec(i) for i in range(N_BRANCH)]
        + [pl.BlockSpec((N_BRANCH, BRANCH_W, tn), lambda m, n: (0, 0, n))],
        out_specs=pl.BlockSpec((tm, tn), lambda m, n: (m, n)),
        out_shape=jax.ShapeDtypeStruct((M, D_MODEL), BF16),
        compiler_params=pltpu.CompilerParams(
            dimension_semantics=("arbitrary", "arbitrary"), vmem_limit_bytes=V7X_VMEM_LIMIT_BYTES),
        name="merge_branches",
    )(*ys, z_all, z_all, z_all, w_branch.astype(BF16))


def _residual_mm_kernel(a_ref, w_ref, x_ref, mod_ref, o_ref, *, gate_row):
    y = jnp.dot(a_ref[...], w_ref[...], preferred_element_type=F32)
    o_ref[...] = x_ref[...] + mod_ref[0, gate_row:gate_row + 1, :] * y


def residual_matmul(a, w, x, mod, *, gate_row, rows_per_cond=None, fixed_row=None, tm=512, tn=512):
    M, K = a.shape
    N = w.shape[1]
    assert M % tm == 0 and N % tn == 0
    cond = _cond_of_tile(rows_per_cond, tm, fixed_row)
    return pl.pallas_call(
        functools.partial(_residual_mm_kernel, gate_row=gate_row),
        grid=(N // tn, M // tm),
        in_specs=[pl.BlockSpec((tm, K), lambda n, m: (m, 0)),
                  pl.BlockSpec((K, tn), lambda n, m: (0, n)),
                  pl.BlockSpec((tm, tn), lambda n, m: (m, n)),
                  pl.BlockSpec((1, N_MOD, tn), lambda n, m: (cond(m), 0, n))],
        out_specs=pl.BlockSpec((tm, tn), lambda n, m: (m, n)),
        out_shape=jax.ShapeDtypeStruct((M, N), F32),
        compiler_params=pltpu.CompilerParams(
            dimension_semantics=("arbitrary", "arbitrary"), vmem_limit_bytes=V7X_VMEM_LIMIT_BYTES),
        name="residual_matmul",
    )(a, w.astype(BF16), x, mod)


def _head_sum(x2):
    row = lax.broadcasted_iota(jnp.int32, (LANES, LANES), 0) // NA_HEAD_DIM
    col = lax.broadcasted_iota(jnp.int32, (LANES, LANES), 1) // NA_HEAD_DIM
    ones_bd = jnp.where(row == col, 1.0, 0.0).astype(BF16)
    return _cumsum_rows_right(x2, ones_bd)


def _cumsum_rows_right(x, m):
    hi, lo = _split_bf16(x)
    return jnp.dot(hi, m, preferred_element_type=F32) + jnp.dot(lo, m, preferred_element_type=F32)


def _na_qkv_kernel(x_ref, g_ref, o_ref):
    x = x_ref[...]

    @pl.when(pl.program_id(1) < 2)
    def _():
        ms = _heads_sum(x * x) * (1.0 / NA_HEAD_DIM)
        o_ref[...] = (x * lax.rsqrt(ms + EPS) * g_ref[...]).astype(o_ref.dtype)

    @pl.when(pl.program_id(1) == 2)
    def _():
        o_ref[...] = x.astype(o_ref.dtype)


def na_qkv(z, gains, *, tm=256):
    M = z.shape[0]
    return pl.pallas_call(
        _na_qkv_kernel,
        grid=(M // tm, 3),
        in_specs=[pl.BlockSpec((tm, NA_W), lambda i, j: (i, j)),
                  pl.BlockSpec((1, NA_W), lambda i, j: (0, jnp.minimum(j, 1)))],
        out_specs=pl.BlockSpec((tm, NA_W), lambda i, j: (i, j)),
        out_shape=jax.ShapeDtypeStruct((M, 3 * NA_W), BF16),
        compiler_params=pltpu.CompilerParams(dimension_semantics=("arbitrary", "arbitrary")),
        name="na_qkv",
    )(z, gains)


def _softmax_pv(s_list, v_list):
    m = s_list[0].max(axis=-1, keepdims=True)
    for s in s_list[1:]:
        m = jnp.maximum(m, s.max(axis=-1, keepdims=True))
    den = 0.0
    acc = 0.0
    for s, v in zip(s_list, v_list):
        p = jnp.exp(s - m)
        den = den + p.sum(axis=-1, keepdims=True)
        acc = acc + jnp.dot(p.astype(BF16), v, preferred_element_type=F32)
    return acc / den


def _na_kernel(*refs, n_rows, win_r, rows_per_step):
    q_ref, k_ref, v_ref, kc_ref, vc_ref = refs[:5]
    bias_refs = refs[5:5 + rows_per_step]
    o_ref = refs[-1]
    kc = kc_ref[...]
    vc = vc_ref[...]
    lane = lax.broadcasted_iota(jnp.int32, (GRID_W, LANES), 1)
    chains = [(i, h) for i in range(rows_per_step) for h in range(2)]
    kw, vw = [], []
    for i in range(rows_per_step):
        r = pl.program_id(2) * rows_per_step + i
        rs = jnp.clip(r - win_r // 2, 0, n_rows - win_r)
        start = pl.multiple_of(rs * GRID_W, GRID_W)
        kw.append(k_ref[pl.ds(start, win_r * GRID_W), :])
        vw.append(v_ref[pl.ds(start, win_r * GRID_W), :])
    qh = []
    for i, h in chains:
        q = q_ref[i * GRID_W:(i + 1) * GRID_W, :]
        qh.append(jnp.where((lane // NA_HEAD_DIM) == h, q, jnp.zeros_like(q)))
    s_loc = [lax.dot_general(qh[c], kw[i], NT_DIMS, preferred_element_type=F32) + bias_refs[i][h, 0]
             for c, (i, h) in enumerate(chains)]
    s_ctx = [lax.dot_general(qh[c], kc, NT_DIMS, preferred_element_type=F32) for c in range(len(chains))]
    m = [jnp.maximum(s_loc[c].max(axis=-1, keepdims=True), s_ctx[c].max(axis=-1, keepdims=True))
         for c in range(len(chains))]
    p_loc = [jnp.exp(s_loc[c] - m[c]) for c in range(len(chains))]
    p_ctx = [jnp.exp(s_ctx[c] - m[c]) for c in range(len(chains))]
    den = [p_loc[c].sum(axis=-1, keepdims=True) + p_ctx[c].sum(axis=-1, keepdims=True)
           for c in range(len(chains))]
    acc = [jnp.dot(p_loc[c].astype(BF16), vw[i], preferred_element_type=F32)
           + jnp.dot(p_ctx[c].astype(BF16), vc, preferred_element_type=F32)
           for c, (i, h) in enumerate(chains)]
    out = [acc[c] / den[c] for c in range(len(chains))]
    for i in range(rows_per_step):
        o_ref[i * GRID_W:(i + 1) * GRID_W, :] = jnp.where(lane < NA_HEAD_DIM, out[2 * i], out[2 * i + 1])


def _ctx_attn_kernel(q_ref, k_ref, v_ref, o_ref):
    q = q_ref[...]
    k = k_ref[...]
    v = v_ref[...]
    lane = lax.broadcasted_iota(jnp.int32, q.shape, 1)
    outs = []
    for h in range(2):
        qh = jnp.where((lane // NA_HEAD_DIM) == h, q, jnp.zeros_like(q))
        s = lax.dot_general(qh, k, NT_DIMS, preferred_element_type=F32)
        outs.append(_softmax_pv([s], [v]))
    o_ref[...] = jnp.where(lane < NA_HEAD_DIM, outs[0], outs[1]).astype(o_ref.dtype)


def na_bias_table(rpb, n_rows, win_r):
    col = np.arange(GRID_W)
    col_start = np.clip(col - NA_WIN_C // 2, 0, GRID_W - NA_WIN_C)
    kcol = np.arange(GRID_W)
    valid = (kcol[None, :] >= col_start[:, None]) & (kcol[None, :] < col_start[:, None] + NA_WIN_C)
    dc = np.clip(kcol[None, :] - col[:, None] + NA_WIN_C - 1, 0, 2 * NA_WIN_C - 2)
    dr = np.arange(NA_WIN_R_MAX)[:, None] + np.arange(win_r)[None, :]
    dr = np.clip(dr, 0, 2 * NA_WIN_R_MAX - 2)
    t = rpb.astype(F32)[:, dr][:, :, :, dc]
    t = jnp.where(valid[None, None, None], t, NEG_BIG)
    t = t.transpose(0, 1, 3, 2, 4)
    return t.reshape(rpb.shape[0], NA_WIN_R_MAX, GRID_W, win_r * GRID_W)


def na_attention(qkv, rpb, B, S, Lc, with_ctx):
    n_rows = S // GRID_W
    win_r = min(NA_WIN_R_MAX, n_rows)
    assert win_r == NA_WIN_R_MAX and (B * S) % Lc == 0
    HP = NA_HEADS // 2
    bias = na_bias_table(rpb, n_rows, win_r)
    ctx_blk0 = (B * S) // Lc

    def variant(r):
        return jnp.clip(r - win_r // 2, 0, n_rows - win_r) - r + NA_WIN_R_MAX - 1

    R = NA_ROWS_PER_STEP
    assert n_rows % R == 0
    steps = n_rows // R
    bias_specs = [pl.BlockSpec((2, 1, GRID_W, win_r * GRID_W),
                               lambda hp, b, r, i=i: (hp, variant(r * R + i), 0, 0)) for i in range(R)]
    y = pl.pallas_call(
        functools.partial(_na_kernel, n_rows=n_rows, win_r=win_r, rows_per_step=R),
        grid=(HP, B, steps),
        in_specs=[pl.BlockSpec((R * GRID_W, LANES), lambda hp, b, r: (b * steps + r, hp)),
                  pl.BlockSpec((S, LANES), lambda hp, b, r: (b, HP + hp)),
                  pl.BlockSpec((S, LANES), lambda hp, b, r: (b, 2 * HP + hp)),
                  pl.BlockSpec((Lc, LANES), lambda hp, b, r: (ctx_blk0 + b, HP + hp)),
                  pl.BlockSpec((Lc, LANES), lambda hp, b, r: (ctx_blk0 + b, 2 * HP + hp))] + bias_specs,
        out_specs=pl.BlockSpec((R * GRID_W, LANES), lambda hp, b, r: (b * steps + r, hp)),
        out_shape=jax.ShapeDtypeStruct((B * S, NA_W), F32),
        compiler_params=pltpu.CompilerParams(
            dimension_semantics=("arbitrary", "arbitrary", "arbitrary"),
            vmem_limit_bytes=V7X_VMEM_LIMIT_BYTES),
        name="na_attention",
    )(qkv, qkv, qkv, qkv, qkv, *([bias] * R))
    yc = None
    if with_ctx:
        yc = pl.pallas_call(
            _ctx_attn_kernel,
            grid=(HP, B),
            in_specs=[pl.BlockSpec((Lc, LANES), lambda hp, b: (ctx_blk0 + b, hp)),
                      pl.BlockSpec((Lc, LANES), lambda hp, b: (ctx_blk0 + b, HP + hp)),
                      pl.BlockSpec((Lc, LANES), lambda hp, b: (ctx_blk0 + b, 2 * HP + hp))],
            out_specs=pl.BlockSpec((Lc, LANES), lambda hp, b: (b, hp)),
            out_shape=jax.ShapeDtypeStruct((B * Lc, NA_W), F32),
            compiler_params=pltpu.CompilerParams(dimension_semantics=("arbitrary", "arbitrary")),
            name="ctx_attention",
        )(qkv, qkv, qkv)
    return y, yc


def na_mixer(z_all, B, S, Lc, q_norm, k_norm, rpb, with_ctx):
    assert (C_NAQ, C_NAK, C_NAV) == (0, NA_W, 2 * NA_W)
    scale = NA_HEAD_DIM ** -0.5
    gains = jnp.concatenate([jnp.tile(q_norm.astype(F32) * scale, NA_HEADS),
                             jnp.tile(k_norm.astype(F32), NA_HEADS)])[None]
    y, yc = na_attention(na_qkv(z_all, gains), rpb, B, S, Lc, with_ctx)
    return y, yc


def _log_sigmoid(x):
    return jnp.minimum(x, 0.0) - jnp.log(1.0 + jnp.exp(-jnp.abs(x)))


def _gla_kernel(*refs, reverse, rope, finish, n_chunks):
    it = iter(refs)
    q_ref, k_ref, v_ref, gd_ref, w2_ref, gb_ref, s0_ref = (next(it) for _ in range(7))
    if rope:
        cos_ref, sin_ref = next(it), next(it)
    if finish:
        yo_ref, r_ref, g_ref = next(it), next(it), next(it)
    y_ref, sfin_ref, s_scr = next(it), next(it), next(it)

    C = GLA_CHUNK
    n = pl.program_id(1)
    H = range(GLA_HEADS)

    @pl.when(n == 0)
    def _():
        s_scr[...] = s0_ref[0]

    ti = lax.broadcasted_iota(jnp.int32, (C, C), 0)
    tj = lax.broadcasted_iota(jnp.int32, (C, C), 1)
    incl = (tj >= ti) if reverse else (tj <= ti)
    tri = jnp.where(incl, 1.0, 0.0).astype(BF16)

    pre = _mm(gd_ref[...], w2_ref[...], NN_DIMS, 3) + gb_ref[...]
    la = _log_sigmoid(pre) * (1.0 / GLA_GATE_TAU)
    cs = _cumsum_rows(tri, la)
    ref_i = C // 2 - 1 if reverse else C // 2
    last_i = 0 if reverse else C - 1
    b_ref = cs[ref_i:ref_i + 1, :]
    b_last = cs[last_i:last_i + 1, :]

    q = q_ref[...] * (GLA_DK ** -0.5)
    k = k_ref[...]
    if rope:
        lane = lax.broadcasted_iota(jnp.int32, q.shape, 1)
        first = (lane % (GLA_DK // 2)) < GLA_DK // 4
        cos = jnp.concatenate([cos_ref[...]] * GLA_HEADS, axis=1)
        sin = jnp.concatenate([sin_ref[...]] * GLA_HEADS, axis=1)

        def rot(x):
            partner = jnp.where(first, pltpu.roll(x, GLA_QK - GLA_DK // 4, axis=1),
                                pltpu.roll(x, GLA_DK // 4, axis=1))
            return x * cos + partner * sin

        q = rot(q)
        k = rot(k)
    qi = q * jnp.exp(cs - b_ref)
    kj = k * jnp.exp(b_ref - cs)
    qe = q * jnp.exp(cs)
    ke = k * jnp.exp(b_last - cs)
    dec = jnp.exp(b_last)
    v_all = v_ref[...]

    def hk(x, h):
        return x[:, h * GLA_DK:(h + 1) * GLA_DK]

    def hv(x, h):
        return x[:, h * GLA_DV:(h + 1) * GLA_DV]

    St = [s_scr[h] for h in H]
    V = [hv(v_all, h) for h in H]
    att = [jnp.where(incl, _mm(hk(qi, h), hk(kj, h), NT_DIMS), 0.0) for h in H]
    y = [_mm(att[h], V[h], NN_DIMS) + _mm(hk(qe, h), St[h], NT_DIMS) for h in H]
    kvt = [_mm(V[h], hk(ke, h), TN_DIMS) for h in H]
    for h in H:
        s_scr[h] = St[h] * hk(dec, h) + kvt[h]
    if finish:
        yo = yo_ref[...]
        r = r_ref[...]
        g = g_ref[...]
        outs = []
        for h in H:
            ys = y[h] + hv(yo, h)
            ms = jnp.mean(ys * ys, axis=-1, keepdims=True)
            rh = hv(r, h)
            outs.append(ys * lax.rsqrt(ms + EPS) * g * (rh * jax.nn.sigmoid(rh)))
        y_ref[...] = jnp.concatenate(outs, axis=1)
    else:
        y_ref[...] = jnp.concatenate(y, axis=1)

    @pl.when(n == n_chunks - 1)
    def _():
        sfin_ref[0] = s_scr[...]


def gla_rope_tables(seq_len):
    t = np.arange(seq_len)
    quarter = GLA_DK // 4
    inv = ROPE_BASE ** (-np.arange(0, 2 * quarter, 2, dtype=np.float64) / (2 * quarter))
    ar = (t // GRID_W)[:, None] * inv[None, :]
    ac = (t % GRID_W)[:, None] * inv[None, :]
    cos = np.concatenate([np.cos(ar), np.cos(ar), np.cos(ac), np.cos(ac)], axis=1)
    sin = np.concatenate([-np.sin(ar), np.sin(ar), -np.sin(ac), np.sin(ac)], axis=1)
    return jnp.asarray(cos, F32), jnp.asarray(sin, F32)


def gla_scan(z, w2pad, gate_b, s0, *, batch, seq_len, row0, reverse, rope, finish_with=None):
    N = seq_len // GLA_CHUNK
    blk0 = row0 // GLA_CHUNK
    off_q, off_k, off_v, off_r, off_gd = C_GQ, C_GK, C_GV, C_GR, C_GGD
    assert off_q % GLA_QK == 0 and off_k % GLA_QK == 0 and off_v % GLA_V == 0 and off_r % GLA_V == 0
    assert off_gd % LANES == 0

    def tb(n):
        return N - 1 - n if reverse else n

    def rb(b, n):
        return blk0 + b * N + tb(n)

    def ob(b, n):
        return b * N + tb(n)

    in_specs = [pl.BlockSpec((GLA_CHUNK, GLA_QK), lambda b, n: (rb(b, n), off_q // GLA_QK)),
                pl.BlockSpec((GLA_CHUNK, GLA_QK), lambda b, n: (rb(b, n), off_k // GLA_QK)),
                pl.BlockSpec((GLA_CHUNK, GLA_V), lambda b, n: (rb(b, n), off_v // GLA_V)),
                pl.BlockSpec((GLA_CHUNK, LANES), lambda b, n: (rb(b, n), off_gd // LANES)),
                pl.BlockSpec((LANES, GLA_QK), lambda b, n: (0, 0)),
                pl.BlockSpec((1, GLA_QK), lambda b, n: (0, 0)),
                pl.BlockSpec((1, GLA_HEADS, GLA_DV, GLA_DK), lambda b, n: (b, 0, 0, 0))]
    args = [z, z, z, z, w2pad, gate_b, s0]
    if rope:
        cos, sin = gla_rope_tables(seq_len)
        in_specs += [pl.BlockSpec((GLA_CHUNK, GLA_DK), lambda b, n: (tb(n), 0))] * 2
        args += [cos, sin]
    if finish_with is not None:
        y_other, norm_g = finish_with
        in_specs += [pl.BlockSpec((GLA_CHUNK, GLA_V), lambda b, n: (ob(b, n), 0)),
                     pl.BlockSpec((GLA_CHUNK, GLA_V), lambda b, n: (rb(b, n), off_r // GLA_V)),
                     pl.BlockSpec((1, GLA_DV), lambda b, n: (0, 0))]
        args += [y_other, z, norm_g]
    return pl.pallas_call(
        functools.partial(_gla_kernel, reverse=reverse, rope=rope, finish=finish_with is not None, n_chunks=N),
        grid=(batch, N),
        in_specs=in_specs,
        out_specs=[pl.BlockSpec((GLA_CHUNK, GLA_V), lambda b, n: (ob(b, n), 0)),
                   pl.BlockSpec((1, GLA_HEADS, GLA_DV, GLA_DK), lambda b, n: (b, 0, 0, 0))],
        out_shape=[jax.ShapeDtypeStruct((batch * seq_len, GLA_V), F32),
                   jax.ShapeDtypeStruct((batch, GLA_HEADS, GLA_DV, GLA_DK), F32)],
        scratch_shapes=[pltpu.VMEM((GLA_HEADS, GLA_DV, GLA_DK), F32)],
        compiler_params=pltpu.CompilerParams(
            dimension_semantics=("arbitrary", "arbitrary"),
            vmem_limit_bytes=V7X_VMEM_LIMIT_BYTES),
        name="gla_scan_rev" if reverse else "gla_scan",
    )(*args)


def gla_mixer(z_all, B, S, Lc, gate_w2, gate_b, norm_g, with_ctx):
    s0 = jnp.zeros((B, GLA_HEADS, GLA_DV, GLA_DK), F32)
    g = norm_g.astype(F32)[None]
    w2 = [jnp.zeros((LANES, GLA_QK), F32).at[d * GLA_GATE_RANK:(d + 1) * GLA_GATE_RANK].set(gate_w2[d])
          for d in range(2)]
    gb = [gate_b[d].astype(F32)[None] for d in range(2)]
    ctx = dict(batch=B, seq_len=Lc, row0=B * S, rope=False)
    lat = dict(batch=B, seq_len=S, row0=0, rope=True)
    yc_f, sc_f = gla_scan(z_all, w2[0], gb[0], s0, reverse=False, **ctx)
    yc, sc_b = gla_scan(z_all, w2[1], gb[1], s0, reverse=True, finish_with=(yc_f, g), **ctx)
    yl_f, _ = gla_scan(z_all, w2[0], gb[0], sc_f, reverse=False, **lat)
    yl, _ = gla_scan(z_all, w2[1], gb[1], sc_b, reverse=True, finish_with=(yl_f, g), **lat)
    return yl, (yc if with_ctx else None)


RW_PREP_TM = 256
RW_HALO = 8
RW_N_PIECES = 5


def _heads_sum(x):
    return jnp.concatenate([_head_sum(x[:, j * LANES:(j + 1) * LANES]) for j in range(x.shape[1] // LANES)],
                           axis=1)


def _rw_prep_kernel(*refs, tm, n_latent_tiles, tiles_per_seq):
    P = RW_N_PIECES
    z_refs, prev_refs, next_refs, mu_refs = refs[0:P], refs[P:2 * P], refs[2 * P:3 * P], refs[3 * P:4 * P]
    kk_ref, ka_ref, rk_ref, w0_ref, a0_ref, w2_ref, a2_ref, g2_ref = refs[4 * P:4 * P + 8]
    (r_o, v_o, kap_o, lw0_o, kd0_o, b0_o, lw1_o, kd1_o, b1_o, bonus_o, g_o) = refs[4 * P + 8:]
    i = pl.program_id(0)
    latent = i < n_latent_tiles
    first = jnp.logical_or(jnp.logical_not(latent), i % tiles_per_seq == 0)
    last = jnp.logical_or(jnp.logical_not(latent), i % tiles_per_seq == tiles_per_seq - 1)

    def shifted(j):
        z = z_refs[j][...]
        row = lax.broadcasted_iota(jnp.int32, z.shape, 0)
        prev_row = jnp.where(first, 0.0, prev_refs[j][RW_HALO - 1:RW_HALO, :])
        next_row = jnp.where(last, 0.0, next_refs[j][0:1, :])
        prev = jnp.where(row == 0, prev_row, pltpu.roll(z, 1, axis=0))
        nxt = jnp.where(row == tm - 1, next_row, pltpu.roll(z, tm - 1, axis=0))
        return z + mu_refs[j][...] * (0.5 * (prev + nxt) - z)

    r, k, v, lr, gd = (shifted(j) for j in range(P))
    kkr = k * kk_ref[...]
    kap = kkr * lax.rsqrt(_heads_sum(kkr * kkr) + EPS)
    r_o[...] = r
    v_o[...] = v
    kap_o[...] = kap
    bonus_o[...] = _heads_sum(r * k * rk_ref[...]) * v
    g_o[...] = _mm(jax.nn.sigmoid(gd), g2_ref[...], NN_DIMS, 3)
    ka = ka_ref[...]
    for d, (lw_o, kd_o, b_o) in enumerate(((lw0_o, kd0_o, b0_o), (lw1_o, kd1_o, b1_o))):
        wd = lr[:, d * LANES:(d + 1) * LANES]
        ad = lr[:, (2 + d) * LANES:(3 + d) * LANES]
        w_log = _log_sigmoid(w0_ref[d:d + 1, :] + _mm(jnp.tanh(wd), w2_ref[d], NN_DIMS, 3)) - 0.5
        lw_o[...] = -jnp.exp(w_log)
        a = jax.nn.sigmoid(a0_ref[d:d + 1, :] + _mm(ad, a2_ref[d], NN_DIMS, 3))
        kd_o[...] = k * (1.0 + (a - 1.0) * ka)
        b_o[...] = kap * a


def rwkv_prep(z_all, n_latent_rows, seq_len, mu_p, k_k, k_a, r_k, w0, a0, w2p, a2p, g2):
    M = z_all.shape[0]
    tm = RW_PREP_TM
    assert M % tm == 0 and n_latent_rows % tm == 0 and seq_len % tm == 0
    widths = (RW_W, RW_W, RW_W, RW_LR_W, RW_GATE_RANK)
    offs = (C_RR, C_RK, C_RV, C_RLR, C_RGD)
    n_halo_blocks = M // RW_HALO
    z_specs, prev_specs, next_specs, mu_specs = [], [], [], []
    for w, o in zip(widths, offs):
        assert o % w == 0
        cb = o // w
        z_specs.append(pl.BlockSpec((tm, w), lambda i, cb=cb: (i, cb)))
        prev_specs.append(pl.BlockSpec(
            (RW_HALO, w), lambda i, cb=cb: (jnp.maximum(i * (tm // RW_HALO) - 1, 0), cb)))
        next_specs.append(pl.BlockSpec(
            (RW_HALO, w), lambda i, cb=cb: (jnp.minimum((i + 1) * (tm // RW_HALO), n_halo_blocks - 1), cb)))
        mu_specs.append(pl.BlockSpec((1, w), lambda i: (0, 0)))

    def full(shape):
        return pl.BlockSpec(shape, lambda i: (0,) * len(shape))

    par_specs = [full((1, RW_W))] * 3 + [full((2, RW_W))] * 2 + [full((2, LANES, RW_W))] * 2 \
        + [full((RW_GATE_RANK, RW_W))]
    out_spec = pl.BlockSpec((tm, RW_W), lambda i: (i, 0))
    return pl.pallas_call(
        functools.partial(_rw_prep_kernel, tm=tm, n_latent_tiles=n_latent_rows // tm,
                          tiles_per_seq=seq_len // tm),
        grid=(M // tm,),
        in_specs=z_specs + prev_specs + next_specs + mu_specs + par_specs,
        out_specs=[out_spec] * 11,
        out_shape=[jax.ShapeDtypeStruct((M, RW_W), F32)] * 11,
        compiler_params=pltpu.CompilerParams(
            dimension_semantics=("arbitrary",), vmem_limit_bytes=V7X_VMEM_LIMIT_BYTES),
        name="rwkv_prep",
    )(*([z_all] * 15), *mu_p, k_k, k_a, r_k, w0, a0, w2p, a2p, g2)


def _rw_chunk_kernel(*refs, reverse, n_par, n_chunks, passes, finish):
    r_ref, v_ref, kap_ref, lw_ref, kd_ref, bet_ref, s0_ref = refs[:7]
    if finish:
        yo_ref, bonus_ref, g_ref, gam_ref, gnb_ref = refs[7:12]
    y_ref, sfin_ref, s_scr = refs[-3:]
    _rw_chunk_body(r_ref, v_ref, kap_ref, lw_ref, kd_ref, bet_ref, s0_ref, y_ref, sfin_ref, s_scr,
                   (yo_ref, bonus_ref, g_ref, gam_ref, gnb_ref) if finish else None,
                   reverse=reverse, n_par=n_par, n_chunks=n_chunks, passes=passes)


def _rw_chunk_body(r_ref, v_ref, kap_ref, lw_ref, kd_ref, bet_ref, s0_ref, y_ref, sfin_ref, s_scr, fin_refs,
                   *, reverse, n_par, n_chunks, passes):
    C = RW_CHUNK
    n = pl.program_id(2)
    P = range(n_par)

    @pl.when(n == 0)
    def _():
        s_scr[...] = s0_ref[0, 0]

    row = lax.broadcasted_iota(jnp.int32, (C, LANES), 0)
    lane = lax.broadcasted_iota(jnp.int32, (C, LANES), 1)
    col = lane % RW_HEAD
    head0 = lane < RW_HEAD
    strict = (col > row) if reverse else (col < row)
    incl = (col >= row) if reverse else (col <= row)
    ti = lax.broadcasted_iota(jnp.int32, (C, C), 0)
    tj = lax.broadcasted_iota(jnp.int32, (C, C), 1)
    tri = jnp.where((tj >= ti) if reverse else (tj <= ti), 1.0, 0.0).astype(BF16)
    level_masks = []
    s = 1
    while s < C:
        level_masks.append(strict & (row // (2 * s) == col // (2 * s)) & (row // s != col // s))
        s *= 2
    eye = jnp.where(row == col, 1.0, 0.0)
    brow = lax.broadcasted_iota(jnp.int32, (LANES, LANES), 0) // RW_HEAD
    bcol = lax.broadcasted_iota(jnp.int32, (LANES, LANES), 1) // RW_HEAD
    same_head = brow == bcol

    def sl(x, p):
        return x[:, p * LANES:(p + 1) * LANES]

    def bd(x):
        return jnp.concatenate([jnp.where(head0, x, 0.0), jnp.where(head0, 0.0, x)], axis=0)

    def pp(a_pair, b_pair):
        return _mm(a_pair, bd(b_pair), NN_DIMS, passes)

    lw_all = lw_ref[...]
    cs_all = _cumsum_rows(tri, lw_all)
    tot_all = cs_all[0:1, :] if reverse else cs_all[C - 1:C, :]
    inv_all = jnp.exp(-cs_all)
    fin_all = jnp.exp(tot_all - cs_all)
    kd_all = kd_ref[...]
    bet_all = bet_ref[...]
    kap_t = kap_ref[...] * jnp.exp(cs_all - lw_all)
    r_t = r_ref[...] * jnp.exp(cs_all)
    k_t = kd_all * inv_all
    b_t = bet_all * inv_all
    k_h = kd_all * fin_all
    b_h = bet_all * fin_all
    dec = jnp.exp(tot_all)
    v_all = v_ref[...]

    S = [s_scr[p] for p in P]
    V = [sl(v_all, p) for p in P]
    lhs = [jnp.concatenate([sl(kap_t, p), sl(r_t, p)], axis=0) for p in P]
    a_k = [_mm(lhs[p], bd(sl(k_t, p)), NT_DIMS, passes) for p in P]
    a_b = [_mm(lhs[p], bd(sl(b_t, p)), NT_DIMS, passes) for p in P]
    s0_both = [_mm(lhs[p], S[p], NT_DIMS, passes) for p in P]
    lmat = [jnp.where(strict, a_b[p][:C], 0.0) for p in P]
    t_inv = [eye - jnp.where(level_masks[0], lmat[p], 0.0) for p in P]
    for m in level_masks[1:]:
        tmp = [pp(jnp.where(m, lmat[p], 0.0), t_inv[p]) for p in P]
        t_inv = [t_inv[p] - pp(t_inv[p], tmp[p]) for p in P]
    x = [s0_both[p][:C] + pp(jnp.where(strict, a_k[p][:C], 0.0), V[p]) for p in P]
    u = [pp(t_inv[p], x[p]) for p in P]
    y = [s0_both[p][C:] + pp(jnp.where(incl, a_k[p][C:], 0.0), V[p])
         - pp(jnp.where(incl, a_b[p][C:], 0.0), u[p]) for p in P]
    upd = [_mm(V[p], sl(k_h, p), TN_DIMS, passes) - _mm(u[p], sl(b_h, p), TN_DIMS, passes) for p in P]
    for p in P:
        s_scr[p] = S[p] * sl(dec, p) + jnp.where(same_head, upd[p], 0.0)
    if fin_refs is None:
        y_ref[...] = jnp.concatenate(y, axis=1)
    else:
        yo_ref, bonus_ref, g_ref, gam_ref, gnb_ref = fin_refs
        ys = jnp.concatenate(y, axis=1) + yo_ref[...]
        cen = ys - _heads_sum(ys) * (1.0 / RW_HEAD)
        var = _heads_sum(cen * cen) * (1.0 / RW_HEAD)
        out = cen * lax.rsqrt(var + RW_GN_EPS) * gam_ref[...] + gnb_ref[...] + bonus_ref[...]
        y_ref[...] = out * g_ref[...]

    @pl.when(n == n_chunks - 1)
    def _():
        sfin_ref[0, 0] = s_scr[...]


def rwkv_chunk_scan(r, v, kap, lw, kd, bet, s0, *, batch, seq_len, row0, reverse, finish_with=None):
    n_par = RW_PAIRS_PER_STEP
    W = r.shape[1]
    G = W // (n_par * LANES)
    N = seq_len // RW_CHUNK
    blk0 = row0 // RW_CHUNK

    def tb(n):
        return N - 1 - n if reverse else n

    seq_in = pl.BlockSpec((RW_CHUNK, n_par * LANES), lambda b, g, n: (blk0 + b * N + tb(n), g))
    seq_out = pl.BlockSpec((RW_CHUNK, n_par * LANES), lambda b, g, n: (b * N + tb(n), g))
    st_spec = pl.BlockSpec((1, 1, n_par, LANES, LANES), lambda b, g, n: (b, g, 0, 0, 0))
    in_specs = [seq_in] * 6 + [st_spec]
    args = [r, v, kap, lw, kd, bet, s0]
    if finish_with is not None:
        y_other, bonus, gate, gamma, beta = finish_with
        vec_spec = pl.BlockSpec((1, n_par * LANES), lambda b, g, n: (0, g))
        in_specs += [seq_out, seq_in, seq_in, vec_spec, vec_spec]
        args += [y_other, bonus, gate, gamma, beta]
    return pl.pallas_call(
        functools.partial(_rw_chunk_kernel, reverse=reverse, n_par=n_par, n_chunks=N, passes=RW_PASSES,
                          finish=finish_with is not None),
        grid=(batch, G, N),
        in_specs=in_specs,
        out_specs=[seq_out, st_spec],
        out_shape=[jax.ShapeDtypeStruct((batch * seq_len, W), F32),
                   jax.ShapeDtypeStruct((batch, G, n_par, LANES, LANES), F32)],
        scratch_shapes=[pltpu.VMEM((n_par, LANES, LANES), F32)],
        compiler_params=pltpu.CompilerParams(
            dimension_semantics=("arbitrary", "arbitrary", "arbitrary"),
            vmem_limit_bytes=V7X_VMEM_LIMIT_BYTES),
        name="rwkv_chunk_scan_rev" if reverse else "rwkv_chunk_scan",
    )(*args)


def _pad_rows(w, rows):
    return jnp.pad(w.astype(F32), ((0, 0), (0, rows - w.shape[1]), (0, 0)))


def rwkv_mixer(z_all, B, S, Lc, mu, w0, w2, a0, a2, g2, k_k, k_a, r_k, gn_g, gn_b, with_ctx):
    assert Lc == RW_PREP_TM
    mu = mu.astype(F32)
    o = np.cumsum((0,) + RW_SPLITS)
    lr_pad = jnp.zeros((LANES - RW_DECAY_RANK,), F32)
    mu_lr = jnp.concatenate([mu[o[3]:o[3] + RW_DECAY_RANK], lr_pad, mu[o[3] + RW_DECAY_RANK:o[4]], lr_pad,
                             mu[o[4]:o[4] + RW_A_RANK], lr_pad, mu[o[4] + RW_A_RANK:o[5]], lr_pad])
    mu_p = [mu[o[0]:o[1]][None], mu[o[1]:o[2]][None], mu[o[2]:o[3]][None], mu_lr[None], mu[o[5]:o[6]][None]]
    row = lambda t: t.astype(F32).reshape(1, RW_W)
    prep = rwkv_prep(z_all, B * S, S, mu_p, row(k_k), row(k_a), row(r_k), w0.astype(F32), a0.astype(F32),
                     _pad_rows(w2, LANES), _pad_rows(a2, LANES), g2.astype(F32))
    r, v, kap, lw0, kd0, b0, lw1, kd1, b1, bonus, gate = prep
    fin = (bonus, gate, row(gn_g), row(gn_b))
    s0 = jnp.zeros((B, RW_W // (RW_PAIRS_PER_STEP * LANES), RW_PAIRS_PER_STEP, LANES, LANES), F32)
    ctx = dict(batch=B, seq_len=Lc, row0=B * S)
    lat = dict(batch=B, seq_len=S, row0=0)
    yc_f, sc_f = rwkv_chunk_scan(r, v, kap, lw0, kd0, b0, s0, reverse=False, **ctx)
    yc, sc_b = rwkv_chunk_scan(r, v, kap, lw1, kd1, b1, s0, reverse=True, finish_with=(yc_f,) + fin, **ctx)
    yl_f, _ = rwkv_chunk_scan(r, v, kap, lw0, kd0, b0, sc_f, reverse=False, **lat)
    yl, _ = rwkv_chunk_scan(r, v, kap, lw1, kd1, b1, sc_b, reverse=True, finish_with=(yl_f,) + fin, **lat)
    return yl, (yc if with_ctx else None)


def relayout_w_in(w):
    o = IN_OFFSETS
    rw = o[8] + np.cumsum((0,) + RW_SPLITS)
    zeros = lambda n: jnp.zeros((w.shape[0], n), w.dtype)
    seg = lambda lo, n: w[:, lo:lo + n]
    lr_pad = LANES - RW_DECAY_RANK
    pieces = [
        seg(o[0], 3 * NA_W),
        seg(o[5], GLA_V), seg(o[6], GLA_V),
        seg(rw[0], 3 * RW_W),
        seg(o[9], N_BRANCH * D_MODEL),
        seg(o[3], GLA_QK), seg(o[4], GLA_QK),
        seg(rw[3], RW_DECAY_RANK), zeros(lr_pad), seg(rw[3] + RW_DECAY_RANK, RW_DECAY_RANK), zeros(lr_pad),
        seg(rw[4], RW_A_RANK), zeros(lr_pad), seg(rw[4] + RW_A_RANK, RW_A_RANK), zeros(lr_pad),
        seg(rw[5], RW_GATE_RANK),
        seg(o[7], 2 * GLA_GATE_RANK),
    ]
    used = sum(x.shape[1] for x in pieces)
    assert used == C_GGD + 2 * GLA_GATE_RANK
    return jnp.concatenate(pieces + [zeros(Z_W - used)], axis=1).astype(BF16)


def token_mixers(x, ctx, mod, p, B, S, Lc, with_ctx):
    h_all = prenorm(x, ctx, p['norm1_g'], mod, seq_len=S)
    z_all = matmul(h_all, relayout_w_in(p['w_in']), tn=1024)
    y_na, yc_na = na_mixer(z_all, B, S, Lc, p['na_q_norm'], p['na_k_norm'], p['na_rpb'], with_ctx)
    y_gla, yc_gla = gla_mixer(z_all, B, S, Lc, p['gla_gate_w2'], p['gla_gate_b'], p['gla_norm_g'], with_ctx)
    y_rw, yc_rw = rwkv_mixer(z_all, B, S, Lc, p['rw_mu'], p['rw_w0'], p['rw_w2'], p['rw_a0'], p['rw_a2'],
                             p['rw_g2'], p['rw_k_k'], p['rw_k_a'], p['rw_r_k'], p['rw_gn_g'],
                             p['rw_gn_b'], with_ctx)
    merged = merge_branches((y_na, y_gla, y_rw), z_all, 0, p['w_branch'])
    x = residual_matmul(merged, p['w_out'], x, mod, gate_row=2, rows_per_cond=S)
    if with_ctx:
        merged_c = merge_branches((yc_na, yc_gla, yc_rw), z_all, B * S, p['w_branch'])
        ctx = residual_matmul(merged_c, p['w_out'], ctx, mod, gate_row=2, fixed_row=B)
    return x, ctx


SC_GATHER_WINDOW = 128
SC_PIECE = 256
MOE_TM = 512


def sc_gather_rows(x, indices):
    m, d = x.shape
    pieces = d // SC_PIECE
    idx = (indices.astype(jnp.int32)[:, None] * pieces + jnp.arange(pieces, dtype=jnp.int32)[None, :])
    n = idx.size
    assert d % SC_PIECE == 0 and n % SC_GATHER_WINDOW == 0
    mesh = plsc.VectorSubcoreMesh(core_axis_name="c", subcore_axis_name="s")

    @pl.kernel(out_type=jax.ShapeDtypeStruct((n, SC_PIECE), x.dtype), mesh=mesh)
    def gather_kernel(x_hbm, i_hbm, o_hbm):
        def body(i_vmem, o_vmem):
            pltpu.sync_copy(x_hbm.at[i_vmem.at[0]], o_vmem)

        pltpu.emit_pipeline(
            body,
            grid=(n // SC_GATHER_WINDOW,),
            in_specs=[pl.BlockSpec((1, SC_GATHER_WINDOW), index_map=lambda i: (0, i))],
            out_specs=[pl.BlockSpec((SC_GATHER_WINDOW, SC_PIECE), index_map=lambda i: (i, 0))],
            core_axis_name="s",
            dimension_semantics=(pltpu.PARALLEL,),
        )(i_hbm, o_hbm)

    out = gather_kernel(x.reshape(m * pieces, SC_PIECE), idx.reshape(1, n))
    return out.reshape(indices.shape[0], d)


def _moe_group_kernel(tg_ref, tu_ref, h_ref, comb_ref, wg_ref, wu_ref, wd_ref, o_ref, acc_ref, hb_ref):
    t = pl.program_id(0)
    j = pl.program_id(1)
    e = tg_ref[t] * EXPERTS_PER_GROUP + j

    @pl.when(j == 0)
    def _():
        acc_ref[...] = jnp.zeros_like(acc_ref)
        hb_ref[...] = h_ref[...].astype(BF16)

    @pl.when(tu_ref[t] > 0)
    def _():
        h = hb_ref[...]
        g = jnp.dot(h, wg_ref[0], preferred_element_type=F32)
        u = jnp.dot(h, wu_ref[0], preferred_element_type=F32)
        comb = comb_ref[...]
        lane = lax.broadcasted_iota(jnp.int32, comb.shape, 1)
        ce = jnp.sum(jnp.where(lane == e, comb, 0.0), axis=1, keepdims=True)
        act = (g * jax.nn.sigmoid(g)) * u * ce
        acc_ref[...] += jnp.dot(act.astype(BF16), wd_ref[0], preferred_element_type=F32)

    @pl.when(j == EXPERTS_PER_GROUP - 1)
    def _():
        o_ref[...] = acc_ref[...]


def moe_group_experts(h_sorted, comb_sorted, tile_group, tile_used, w_gate, w_up, w_down):
    P, D = h_sorted.shape
    tm = MOE_TM

    def w_map(t, j, tg, tu):
        return (tg[t] * EXPERTS_PER_GROUP + j, 0, 0)

    grid_spec = pltpu.PrefetchScalarGridSpec(
        num_scalar_prefetch=2,
        grid=(P // tm, EXPERTS_PER_GROUP),
        in_specs=[pl.BlockSpec((tm, D), lambda t, j, tg, tu: (t, 0)),
                  pl.BlockSpec((tm, N_EXPERTS), lambda t, j, tg, tu: (t, 0)),
                  pl.BlockSpec((1, D, D_EXPERT), w_map),
                  pl.BlockSpec((1, D, D_EXPERT), w_map),
                  pl.BlockSpec((1, D_EXPERT, D), w_map)],
        out_specs=pl.BlockSpec((tm, D), lambda t, j, tg, tu: (t, 0)),
        scratch_shapes=[pltpu.VMEM((tm, D), F32), pltpu.VMEM((tm, D), BF16)])
    return pl.pallas_call(
        _moe_group_kernel,
        grid_spec=grid_spec,
        out_shape=jax.ShapeDtypeStruct((P, D), F32),
        compiler_params=pltpu.CompilerParams(
            dimension_semantics=("arbitrary", "arbitrary"),
            vmem_limit_bytes=V7X_VMEM_LIMIT_BYTES),
        name="moe_group_experts",
    )(tile_group, tile_used, h_sorted, comb_sorted, w_gate, w_up, w_down)


def _residual_add_kernel(x_ref, y_ref, mod_ref, o_ref, *, gate_row):
    o_ref[...] = x_ref[...] + mod_ref[0, gate_row:gate_row + 1, :] * y_ref[...]


def residual_add(x, y, mod, *, gate_row, rows_per_cond=None, fixed_row=None):
    cond = _cond_of_tile(rows_per_cond, ROW_TM, fixed_row)
    spec = pl.BlockSpec((ROW_TM, D_MODEL), lambda i: (i, 0))
    return pl.pallas_call(
        functools.partial(_residual_add_kernel, gate_row=gate_row),
        grid=(x.shape[0] // ROW_TM,),
        in_specs=[spec, spec, pl.BlockSpec((1, N_MOD, D_MODEL), lambda i: (cond(i), 0, 0))],
        out_specs=spec,
        out_shape=jax.ShapeDtypeStruct(x.shape, F32),
        compiler_params=pltpu.CompilerParams(dimension_semantics=("arbitrary",)),
        name="residual_add",
    )(x, y, mod)


def group_dispatch_plan(comb_t):
    M = comb_t.shape[1]
    n_slots = M + N_GROUPS * MOE_TM
    grp_mass = comb_t.reshape(N_GROUPS, EXPERTS_PER_GROUP, M).sum(axis=1)
    grp = jnp.argmax(grp_mass > 0, axis=0).astype(jnp.int32)
    onehot = (grp[:, None] == jnp.arange(N_GROUPS, dtype=jnp.int32)[None, :]).astype(jnp.int32)
    pos = jnp.sum((jnp.cumsum(onehot, axis=0) - onehot) * onehot, axis=1)
    counts = jnp.sum(onehot, axis=0)
    padded = ((counts + MOE_TM - 1) // MOE_TM) * MOE_TM
    ends = jnp.cumsum(padded)
    starts = ends - padded
    slot = jnp.sum(onehot * starts[None, :], axis=1) + pos
    token = jnp.arange(M, dtype=jnp.int32)
    src = jnp.zeros((n_slots,), jnp.int32).at[slot].set(token)
    valid = jnp.zeros((n_slots,), F32).at[slot].set(1.0)
    tile_row = jnp.arange(n_slots // MOE_TM, dtype=jnp.int32) * MOE_TM
    tile_group = jnp.minimum(jnp.sum(tile_row[:, None] >= ends[None, :], axis=1), N_GROUPS - 1).astype(jnp.int32)
    tile_used = (tile_row < ends[-1]).astype(jnp.int32)
    return src, slot, valid, tile_group, tile_used


def moe_sublayer(x, mod, p, *, rows_per_cond=None, fixed_row=None):
    h2, comb_t = prenorm_router(x, p['norm2_g'], mod, p['router_w_t'], p['router_bias'],
                                rows_per_cond=rows_per_cond, fixed_row=fixed_row)
    src, slot, valid, tile_group, tile_used = group_dispatch_plan(comb_t)
    h_sorted = sc_gather_rows(h2, src)
    comb_sorted = jnp.take(comb_t.T, src, axis=0) * valid[:, None]
    y_sorted = moe_group_experts(h_sorted, comb_sorted, tile_group, tile_used,
                                 p['moe_w_gate'], p['moe_w_up'], p['moe_w_down'])
    y = sc_gather_rows(y_sorted, slot)
    return residual_add(x, y, mod, gate_row=N_MOD - 1, rows_per_cond=rows_per_cond, fixed_row=fixed_row)


def kernel(x, c, ctx, c_ctx, ada_w, ada_b, norm1_g, norm2_g, w_in, na_q_norm, na_k_norm, na_rpb,
           gla_gate_w2, gla_gate_b, gla_norm_g, rw_mu, rw_w0, rw_w2, rw_a0, rw_a2, rw_g2,
           rw_k_k, rw_k_a, rw_r_k, rw_gn_g, rw_gn_b, w_branch, w_out, router_w, router_bias,
           moe_w_gate, moe_w_up, moe_w_down):
    B, S, D = x.shape
    Lc = ctx.shape[1]
    assert B + 1 <= N_COND and S % 512 == 0 and (B * Lc) % 512 == 0 and Lc % ROW_TM == 0
    c_rows = jnp.concatenate([c, c_ctx[None], jnp.zeros((N_COND - B - 1, D), c.dtype)], axis=0).astype(F32)
    mods = ada_modulation(c_rows, ada_w, ada_b).reshape(DEPTH, N_COND, N_MOD, D)
    x = x.reshape(B * S, D)
    ctx = ctx.reshape(B * Lc, D)
    router_w_t = router_w.astype(F32).T
    router_b = router_bias.astype(F32).reshape(N_EXPERTS, 1)
    for l in range(DEPTH):
        with_ctx = l < DEPTH - 1
        p = {
            'w_in': w_in[l], 'na_q_norm': na_q_norm[l], 'na_k_norm': na_k_norm[l], 'na_rpb': na_rpb[l],
            'gla_gate_w2': gla_gate_w2[l], 'gla_gate_b': gla_gate_b[l], 'gla_norm_g': gla_norm_g[l],
            'rw_mu': rw_mu[l], 'rw_w0': rw_w0[l], 'rw_w2': rw_w2[l], 'rw_a0': rw_a0[l], 'rw_a2': rw_a2[l],
            'rw_g2': rw_g2[l], 'rw_k_k': rw_k_k[l], 'rw_k_a': rw_k_a[l], 'rw_r_k': rw_r_k[l],
            'rw_gn_g': rw_gn_g[l], 'rw_gn_b': rw_gn_b[l], 'w_branch': w_branch[l], 'w_out': w_out[l],
            'norm1_g': norm1_g[l].astype(F32)[None], 'norm2_g': norm2_g[l].astype(F32)[None],
            'router_w_t': router_w_t, 'router_bias': router_b,
            'moe_w_gate': moe_w_gate[l].astype(BF16), 'moe_w_up': moe_w_up[l].astype(BF16),
            'moe_w_down': moe_w_down[l].astype(BF16),
        }
        x, ctx = token_mixers(x, ctx, mods[l], p, B, S, Lc, with_ctx)
        x = moe_sublayer(x, mods[l], p, rows_per_cond=S)
        if with_ctx:
            ctx = moe_sublayer(ctx, mods[l], p, fixed_row=B)
    return x.reshape(B, S, D)
```

```python
import functools

import jax
import jax.numpy as jnp
from jax import lax
import numpy as np
from jax.experimental import pallas as pl
from jax.experimental.pallas import tpu as pltpu
from jax.experimental.pallas import tpu_sc as plsc

D_MODEL = 2048
DEPTH = 2
GRID_W = 64

NA_HEADS = 16
NA_HEAD_DIM = 64
NA_WIN_R_MAX = 8
NA_WIN_C = 16
NA_W = NA_HEADS * NA_HEAD_DIM
NA_ROWS_PER_STEP = 8

GLA_HEADS = 4
GLA_DK = 128
GLA_DV = 256
GLA_QK = GLA_HEADS * GLA_DK
GLA_V = GLA_HEADS * GLA_DV
GLA_GATE_RANK = 16
GLA_GATE_TAU = 16.0
GLA_CHUNK = 64
GLA_BATCH_PER_STEP = 4

RW_HEADS = 16
RW_HEAD = 64
RW_W = RW_HEADS * RW_HEAD
RW_DECAY_RANK = 96
RW_A_RANK = 96
RW_GATE_RANK = 256
RW_SPLITS = (RW_W, RW_W, RW_W, 2 * RW_DECAY_RANK, 2 * RW_A_RANK, RW_GATE_RANK)
RW_IN = sum(RW_SPLITS)
RW_GN_EPS = 64e-5
RW_CHUNK = 64
RW_PASSES = 1
RW_PAIRS_PER_STEP = 8

N_BRANCH = 3
BRANCH_W = 1024
IN_SPLITS = (NA_W, NA_W, NA_W, GLA_QK, GLA_QK, GLA_V, GLA_V, 2 * GLA_GATE_RANK, RW_IN, N_BRANCH * D_MODEL)
D_IN = sum(IN_SPLITS)
IN_OFFSETS = tuple(int(o) for o in np.cumsum((0,) + IN_SPLITS[:-1]))

LANES = 128
RW_LR_W = 4 * LANES
C_NAQ, C_NAK, C_NAV = 0, 1024, 2048
C_GV, C_GR = 3072, 4096
C_RR, C_RK, C_RV = 5120, 6144, 7168
C_GATE = 8192
C_GQ, C_GK = 14336, 14848
C_RLR, C_RGD, C_GGD = 15360, 15872, 16128
Z_W = 16384

N_EXPERTS = 16
N_GROUPS = 4
EXPERTS_PER_GROUP = N_EXPERTS // N_GROUPS
TOP_K = 2
D_EXPERT = 512

ROPE_BASE = 10000.0
EPS = 1e-6
F32 = jnp.float32
BF16 = jnp.bfloat16

V7X_VMEM_LIMIT_BYTES = 48 * 1024 * 1024

NEG_BIG = -1e30
NT_DIMS = (((1,), (1,)), ((), ()))
TN_DIMS = (((0,), (0,)), ((), ()))
NN_DIMS = (((1,), (0,)), ((), ()))


def _split_bf16(x):
    hi = x.astype(BF16)
    lo = (x - hi.astype(F32)).astype(BF16)
    return hi, lo


def _mm(a, b, dims=NN_DIMS, passes=1):
    if passes == 1:
        return lax.dot_general(a.astype(BF16), b.astype(BF16), dims, preferred_element_type=F32)
    a_hi, a_lo = _split_bf16(a)
    b_hi, b_lo = _split_bf16(b)
    out = lax.dot_general(a_hi, b_hi, dims, preferred_element_type=F32)
    out = out + lax.dot_general(a_hi, b_lo, dims, preferred_element_type=F32)
    return out + lax.dot_general(a_lo, b_hi, dims, preferred_element_type=F32)


def _cumsum_rows(tri, x):
    hi, lo = _split_bf16(x)
    return jnp.dot(tri, hi, preferred_element_type=F32) + jnp.dot(tri, lo, preferred_element_type=F32)


def _mm_kernel(a_ref, b_ref, o_ref):
    o_ref[...] = jnp.dot(a_ref[...], b_ref[...], preferred_element_type=F32).astype(o_ref.dtype)


def matmul(a, b, *, tm=512, tn=512, out_dtype=F32):
    M, K = a.shape
    _, N = b.shape
    assert M % tm == 0 and N % tn == 0, (M, N, tm, tn)
    return pl.pallas_call(
        _mm_kernel,
        grid=(N // tn, M // tm),
        in_specs=[pl.BlockSpec((tm, K), lambda n, m: (m, 0)),
                  pl.BlockSpec((K, tn), lambda n, m: (0, n))],
        out_specs=pl.BlockSpec((tm, tn), lambda n, m: (m, n)),
        out_shape=jax.ShapeDtypeStruct((M, N), out_dtype),
        compiler_params=pltpu.CompilerParams(
            dimension_semantics=("arbitrary", "arbitrary"),
            vmem_limit_bytes=V7X_VMEM_LIMIT_BYTES),
        name="matmul",
    )(a.astype(BF16), b.astype(BF16))


def _moe_kernel(h_ref, comb_ref, wg_ref, wu_ref, wd_ref, x_ref, mod_ref, o_ref, acc_ref):
    e = pl.program_id(1)

    @pl.when(e == 0)
    def _():
        acc_ref[...] = jnp.zeros_like(acc_ref)

    h = h_ref[...]
    g = jnp.dot(h, wg_ref[0], preferred_element_type=F32)
    u = jnp.dot(h, wu_ref[0], preferred_element_type=F32)
    comb = comb_ref[...]
    lane = lax.broadcasted_iota(jnp.int32, comb.shape, 1)
    ce = jnp.sum(jnp.where(lane == e, comb, 0.0), axis=1, keepdims=True)
    act = (g * jax.nn.sigmoid(g)) * u * ce
    acc_ref[...] += jnp.dot(act.astype(BF16), wd_ref[0], preferred_element_type=F32)

    @pl.when(e == N_EXPERTS - 1)
    def _():
        o_ref[...] = x_ref[...] + mod_ref[0, N_MOD - 1:N_MOD, :] * acc_ref[...]


def moe_experts(h, comb, w_gate, w_up, w_down, x, mod, *, rows_per_cond=None, fixed_row=None, tm=512):
    M, D = h.shape
    assert M % tm == 0
    cond = _cond_of_tile(rows_per_cond, tm, fixed_row)
    return pl.pallas_call(
        _moe_kernel,
        grid=(M // tm, N_EXPERTS),
        in_specs=[pl.BlockSpec((tm, D), lambda m, e: (m, 0)),
                  pl.BlockSpec((tm, N_EXPERTS), lambda m, e: (m, 0)),
                  pl.BlockSpec((1, D, D_EXPERT), lambda m, e: (e, 0, 0)),
                  pl.BlockSpec((1, D, D_EXPERT), lambda m, e: (e, 0, 0)),
                  pl.BlockSpec((1, D_EXPERT, D), lambda m, e: (e, 0, 0)),
                  pl.BlockSpec((tm, D), lambda m, e: (m, 0)),
                  pl.BlockSpec((1, N_MOD, D), lambda m, e: (cond(m), 0, 0))],
        out_specs=pl.BlockSpec((tm, D), lambda m, e: (m, 0)),
        out_shape=jax.ShapeDtypeStruct((M, D), F32),
        scratch_shapes=[pltpu.VMEM((tm, D), F32)],
        compiler_params=pltpu.CompilerParams(
            dimension_semantics=("arbitrary", "arbitrary"),
            vmem_limit_bytes=V7X_VMEM_LIMIT_BYTES),
        name="moe_experts",
    )(h, comb, w_gate, w_up, w_down, x, mod)


N_COND = 8
N_MOD = 6
ROW_TM = 256


def _ada_kernel(c_ref, w_ref, b_ref, o_ref):
    c = c_ref[...]
    o_ref[0] = _mm(c * jax.nn.sigmoid(c), w_ref[0], NN_DIMS, 3) + b_ref[0]


def ada_modulation(c_rows, ada_w, ada_b, *, tn=1024):
    L, D, N = ada_w.shape
    return pl.pallas_call(
        _ada_kernel,
        grid=(L, N // tn),
        in_specs=[pl.BlockSpec((N_COND, D), lambda l, n: (0, 0)),
                  pl.BlockSpec((1, D, tn), lambda l, n: (l, 0, n)),
                  pl.BlockSpec((1, 1, tn), lambda l, n: (l, 0, n))],
        out_specs=pl.BlockSpec((1, N_COND, tn), lambda l, n: (l, 0, n)),
        out_shape=jax.ShapeDtypeStruct((L, N_COND, N), F32),
        compiler_params=pltpu.CompilerParams(
            dimension_semantics=("arbitrary", "arbitrary"), vmem_limit_bytes=V7X_VMEM_LIMIT_BYTES),
        name="ada_modulation",
    )(c_rows, ada_w, ada_b.reshape(L, 1, N))


def _norm_mod(x, g, shift, scale):
    y = x * lax.rsqrt(jnp.mean(x * x, axis=-1, keepdims=True) + EPS)
    return y * g * (1.0 + scale) + shift


def _prenorm_kernel(x_ref, c_ref, g_ref, mod_ref, o_ref, *, n_latent_tiles):
    def emit(src_ref):
        o_ref[...] = _norm_mod(src_ref[...], g_ref[...], mod_ref[0, 0:1, :], mod_ref[0, 1:2, :]).astype(o_ref.dtype)

    @pl.when(pl.program_id(0) < n_latent_tiles)
    def _():
        emit(x_ref)

    @pl.when(pl.program_id(0) >= n_latent_tiles)
    def _():
        emit(c_ref)


def _top2_sum(a, b, c, d):
    hi1, lo1 = jnp.maximum(a, b), jnp.minimum(a, b)
    hi2, lo2 = jnp.maximum(c, d), jnp.minimum(c, d)
    return jnp.maximum(hi1, hi2) + jnp.maximum(jnp.minimum(hi1, hi2), jnp.maximum(lo1, lo2))


def _prenorm_router_kernel(x_ref, g_ref, mod_ref, rw_ref, rb_ref, o_ref, comb_ref):
    h = _norm_mod(x_ref[...], g_ref[...], mod_ref[0, 3:4, :], mod_ref[0, 4:5, :])
    o_ref[...] = h.astype(o_ref.dtype)
    s = jax.nn.sigmoid(_mm(rw_ref[...], h, NT_DIMS, 3))
    biased = s + rb_ref[...]
    rows = [biased[e:e + 1, :] for e in range(N_EXPERTS)]
    G = EXPERTS_PER_GROUP
    score = [_top2_sum(*rows[g * G:(g + 1) * G]) for g in range(N_GROUPS)]
    picked = []
    for e in range(N_EXPERTS):
        g = e // G
        ok = None
        for g2 in range(N_GROUPS):
            if g2 != g:
                t = (score[g] > score[g2]) if g2 < g else (score[g] >= score[g2])
                ok = t if ok is None else jnp.logical_and(ok, t)
        rank = 0.0
        for e2 in range(g * G, (g + 1) * G):
            if e2 != e:
                ahead = (rows[e2] >= rows[e]) if e2 < e else (rows[e2] > rows[e])
                rank = rank + jnp.where(ahead, 1.0, 0.0)
        picked.append(jnp.where(jnp.logical_and(ok, rank < TOP_K), s[e:e + 1, :], 0.0))
    w = jnp.concatenate(picked, axis=0)
    comb_ref[...] = w / jnp.sum(w, axis=0, keepdims=True)


def _cond_of_tile(rows_per_cond, tile_rows, fixed_row):
    if fixed_row is not None:
        return lambda m: fixed_row
    assert rows_per_cond % tile_rows == 0
    return lambda m: m // (rows_per_cond // tile_rows)


def prenorm(x, ctx, gain, mod, *, seq_len):
    n_lat, n_ctx = x.shape[0] // ROW_TM, ctx.shape[0] // ROW_TM
    n_batch = x.shape[0] // seq_len
    cond = _cond_of_tile(seq_len, ROW_TM, None)
    return pl.pallas_call(
        functools.partial(_prenorm_kernel, n_latent_tiles=n_lat),
        grid=(n_lat + n_ctx,),
        in_specs=[pl.BlockSpec((ROW_TM, D_MODEL), lambda i: (jnp.minimum(i, n_lat - 1), 0)),
                  pl.BlockSpec((ROW_TM, D_MODEL), lambda i: (jnp.maximum(i - n_lat, 0), 0)),
                  pl.BlockSpec((1, D_MODEL), lambda i: (0, 0)),
                  pl.BlockSpec((1, N_MOD, D_MODEL), lambda i: (jnp.where(i < n_lat, cond(i), n_batch), 0, 0))],
        out_specs=pl.BlockSpec((ROW_TM, D_MODEL), lambda i: (i, 0)),
        out_shape=jax.ShapeDtypeStruct((x.shape[0] + ctx.shape[0], D_MODEL), BF16),
        compiler_params=pltpu.CompilerParams(dimension_semantics=("arbitrary",)),
        name="prenorm",
    )(x, ctx, gain, mod)


def prenorm_router(x, gain, mod, router_w_t, router_bias, *, rows_per_cond=None, fixed_row=None):
    M = x.shape[0]
    cond = _cond_of_tile(rows_per_cond, ROW_TM, fixed_row)
    x_spec = pl.BlockSpec((ROW_TM, D_MODEL), lambda i: (i, 0))
    return pl.pallas_call(
        _prenorm_router_kernel,
        grid=(M // ROW_TM,),
        in_specs=[x_spec,
                  pl.BlockSpec((1, D_MODEL), lambda i: (0, 0)),
                  pl.BlockSpec((1, N_MOD, D_MODEL), lambda i: (cond(i), 0, 0)),
                  pl.BlockSpec((N_EXPERTS, D_MODEL), lambda i: (0, 0)),
                  pl.BlockSpec((N_EXPERTS, 1), lambda i: (0, 0))],
        out_specs=[x_spec, pl.BlockSpec((N_EXPERTS, ROW_TM), lambda i: (0, i))],
        out_shape=[jax.ShapeDtypeStruct((M, D_MODEL), BF16),
                   jax.ShapeDtypeStruct((N_EXPERTS, M), F32)],
        compiler_params=pltpu.CompilerParams(dimension_semantics=("arbitrary",)),
        name="prenorm_router",
    )(x, gain, mod, router_w_t, router_bias)


def _merge_kernel(y0_ref, y1_ref, y2_ref, g0_ref, g1_ref, g2_ref, wb_ref, o_ref):
    acc = 0.0
    for i, (y_ref, g_ref) in enumerate(((y0_ref, g0_ref), (y1_ref, g1_ref), (y2_ref, g2_ref))):
        acc = acc + jax.nn.sigmoid(g_ref[...]) * jnp.dot(y_ref[...].astype(BF16), wb_ref[i],
                                                         preferred_element_type=F32)
    o_ref[...] = acc.astype(o_ref.dtype)


def merge_branches(ys, z_all, row0, w_branch, *, tm=512, tn=512):
    M = ys[0].shape[0]
    assert M % tm == 0 and row0 % tm == 0 and C_GATE % tn == 0 and D_MODEL % tn == 0
    row_blk0 = row0 // tm
    y_spec = pl.BlockSpec((tm, BRANCH_W), lambda m, n: (m, 0))

    def gate_spec(i):
        return pl.BlockSpec((tm, tn), lambda m, n: (row_blk0 + m, (C_GATE + i * D_MODEL) // tn + n))

    return pl.pallas_call(
        _merge_kernel,
        grid=(M // tm, D_MODEL // tn),
        in_specs=[y_spec] * 3 + [gate_spec(i) for i in range(N_BRANCH)]
        + [pl.BlockSpec((N_BRANCH, BRANCH_W, tn), lambda m, n: (0, 0, n))],
        out_specs=pl.BlockSpec((tm, tn), lambda m, n: (m, n)),
        out_shape=jax.ShapeDtypeStruct((M, D_MODEL), BF16),
        compiler_params=pltpu.CompilerParams(
            dimension_semantics=("arbitrary", "arbitrary"), vmem_limit_bytes=V7X_VMEM_LIMIT_BYTES),
        name="merge_branches",
    )(*ys, z_all, z_all, z_all, w_branch.astype(BF16))


def _residual_mm_kernel(a_ref, w_ref, x_ref, mod_ref, o_ref, *, gate_row):
    y = jnp.dot(a_ref[...], w_ref[...], preferred_element_type=F32)
    o_ref[...] = x_ref[...] + mod_ref[0, gate_row:gate_row + 1, :] * y


def residual_matmul(a, w, x, mod, *, gate_row, rows_per_cond=None, fixed_row=None, tm=512, tn=512):
    M, K = a.shape
    N = w.shape[1]
    assert M % tm == 0 and N % tn == 0
    cond = _cond_of_tile(rows_per_cond, tm, fixed_row)
    return pl.pallas_call(
        functools.partial(_residual_mm_kernel, gate_row=gate_row),
        grid=(N // tn, M // tm),
        in_specs=[pl.BlockSpec((tm, K), lambda n, m: (m, 0)),
                  pl.BlockSpec((K, tn), lambda n, m: (0, n)),
                  pl.BlockSpec((tm, tn), lambda n, m: (m, n)),
                  pl.BlockSpec((1, N_MOD, tn), lambda n, m: (cond(m), 0, n))],
        out_specs=pl.BlockSpec((tm, tn), lambda n, m: (m, n)),
        out_shape=jax.ShapeDtypeStruct((M, N), F32),
        compiler_params=pltpu.CompilerParams(
            dimension_semantics=("arbitrary", "arbitrary"), vmem_limit_bytes=V7X_VMEM_LIMIT_BYTES),
        name="residual_matmul",
    )(a, w.astype(BF16), x, mod)


def _head_sum(x2):
    row = lax.broadcasted_iota(jnp.int32, (LANES, LANES), 0) // NA_HEAD_DIM
    col = lax.broadcasted_iota(jnp.int32, (LANES, LANES), 1) // NA_HEAD_DIM
    ones_bd = jnp.where(row == col, 1.0, 0.0).astype(BF16)
    return _cumsum_rows_right(x2, ones_bd)


def _cumsum_rows_right(x, m):
    hi, lo = _split_bf16(x)
    return jnp.dot(hi, m, preferred_element_type=F32) + jnp.dot(lo, m, preferred_element_type=F32)


def _na_qkv_kernel(x_ref, g_ref, o_ref):
    x = x_ref[...]

    @pl.when(pl.program_id(1) < 2)
    def _():
        ms = _heads_sum(x * x) * (1.0 / NA_HEAD_DIM)
        o_ref[...] = (x * lax.rsqrt(ms + EPS) * g_ref[...]).astype(o_ref.dtype)

    @pl.when(pl.program_id(1) == 2)
    def _():
        o_ref[...] = x.astype(o_ref.dtype)


def na_qkv(z, gains, *, tm=256):
    M = z.shape[0]
    return pl.pallas_call(
        _na_qkv_kernel,
        grid=(M // tm, 3),
        in_specs=[pl.BlockSpec((tm, NA_W), lambda i, j: (i, j)),
                  pl.BlockSpec((1, NA_W), lambda i, j: (0, jnp.minimum(j, 1)))],
        out_specs=pl.BlockSpec((tm, NA_W), lambda i, j: (i, j)),
        out_shape=jax.ShapeDtypeStruct((M, 3 * NA_W), BF16),
        compiler_params=pltpu.CompilerParams(dimension_semantics=("arbitrary", "arbitrary")),
        name="na_qkv",
    )(z, gains)


def _softmax_pv(s_list, v_list):
    m = s_list[0].max(axis=-1, keepdims=True)
    for s in s_list[1:]:
        m = jnp.maximum(m, s.max(axis=-1, keepdims=True))
    den = 0.0
    acc = 0.0
    for s, v in zip(s_list, v_list):
        p = jnp.exp(s - m)
        den = den + p.sum(axis=-1, keepdims=True)
        acc = acc + jnp.dot(p.astype(BF16), v, preferred_element_type=F32)
    return acc / den


def _na_kernel(*refs, n_rows, win_r, rows_per_step):
    q_ref, k_ref, v_ref, kc_ref, vc_ref = refs[:5]
    bias_refs = refs[5:5 + rows_per_step]
    o_ref = refs[-1]
    kc = kc_ref[...]
    vc = vc_ref[...]
    lane = lax.broadcasted_iota(jnp.int32, (GRID_W, LANES), 1)
    chains = [(i, h) for i in range(rows_per_step) for h in range(2)]
    kw, vw = [], []
    for i in range(rows_per_step):
        r = pl.program_id(2) * rows_per_step + i
        rs = jnp.clip(r - win_r // 2, 0, n_rows - win_r)
        start = pl.multiple_of(rs * GRID_W, GRID_W)
        kw.append(k_ref[pl.ds(start, win_r * GRID_W), :])
        vw.append(v_ref[pl.ds(start, win_r * GRID_W), :])
    qh = []
    for i, h in chains:
        q = q_ref[i * GRID_W:(i + 1) * GRID_W, :]
        qh.append(jnp.where((lane // NA_HEAD_DIM) == h, q, jnp.zeros_like(q)))
    s_loc = [lax.dot_general(qh[c], kw[i], NT_DIMS, preferred_element_type=F32) + bias_refs[i][h, 0]
             for c, (i, h) in enumerate(chains)]
    s_ctx = [lax.dot_general(qh[c], kc, NT_DIMS, preferred_element_type=F32) for c in range(len(chains))]
    m = [jnp.maximum(s_loc[c].max(axis=-1, keepdims=True), s_ctx[c].max(axis=-1, keepdims=True))
         for c in range(len(chains))]
    p_loc = [jnp.exp(s_loc[c] - m[c]) for c in range(len(chains))]
    p_ctx = [jnp.exp(s_ctx[c] - m[c]) for c in range(len(chains))]
    den = [p_loc[c].sum(axis=-1, keepdims=True) + p_ctx[c].sum(axis=-1, keepdims=True)
           for c in range(len(chains))]
    acc = [jnp.dot(p_loc[c].astype(BF16), vw[i], preferred_element_type=F32)
           + jnp.dot(p_ctx[c].astype(BF16), vc, preferred_element_type=F32)
           for c, (i, h) in enumerate(chains)]
    out = [acc[c] / den[c] for c in range(len(chains))]
    for i in range(rows_per_step):
        o_ref[i * GRID_W:(i + 1) * GRID_W, :] = jnp.where(lane < NA_HEAD_DIM, out[2 * i], out[2 * i + 1])


def _ctx_attn_kernel(q_ref, k_ref, v_ref, o_ref):
    q = q_ref[...]
    k = k_ref[...]
    v = v_ref[...]
    lane = lax.broadcasted_iota(jnp.int32, q.shape, 1)
    outs = []
    for h in range(2):
        qh = jnp.where((lane // NA_HEAD_DIM) == h, q, jnp.zeros_like(q))
        s = lax.dot_general(qh, k, NT_DIMS, preferred_element_type=F32)
        outs.append(_softmax_pv([s], [v]))
    o_ref[...] = jnp.where(lane < NA_HEAD_DIM, outs[0], outs[1]).astype(o_ref.dtype)


def na_bias_table(rpb, n_rows, win_r):
    col = np.arange(GRID_W)
    col_start = np.clip(col - NA_WIN_C // 2, 0, GRID_W - NA_WIN_C)
    kcol = np.arange(GRID_W)
    valid = (kcol[None, :] >= col_start[:, None]) & (kcol[None, :] < col_start[:, None] + NA_WIN_C)
    dc = np.clip(kcol[None, :] - col[:, None] + NA_WIN_C - 1, 0, 2 * NA_WIN_C - 2)
    dr = np.arange(NA_WIN_R_MAX)[:, None] + np.arange(win_r)[None, :]
    dr = np.clip(dr, 0, 2 * NA_WIN_R_MAX - 2)
    t = rpb.astype(F32)[:, dr][:, :, :, dc]
    t = jnp.where(valid[None, None, None], t, NEG_BIG)
    t = t.transpose(0, 1, 3, 2, 4)
    return t.reshape(rpb.shape[0], NA_WIN_R_MAX, GRID_W, win_r * GRID_W)


def na_attention(qkv, rpb, B, S, Lc, with_ctx):
    n_rows = S // GRID_W
    win_r = min(NA_WIN_R_MAX, n_rows)
    assert win_r == NA_WIN_R_MAX and (B * S) % Lc == 0
    HP = NA_HEADS // 2
    bias = na_bias_table(rpb, n_rows, win_r)
    ctx_blk0 = (B * S) // Lc

    def variant(r):
        return jnp.clip(r - win_r // 2, 0, n_rows - win_r) - r + NA_WIN_R_MAX - 1

    R = NA_ROWS_PER_STEP
    assert n_rows % R == 0
    steps = n_rows // R
    bias_specs = [pl.BlockSpec((2, 1, GRID_W, win_r * GRID_W),
                               lambda hp, b, r, i=i: (hp, variant(r * R + i), 0, 0)) for i in range(R)]
    y = pl.pallas_call(
        functools.partial(_na_kernel, n_rows=n_rows, win_r=win_r, rows_per_step=R),
        grid=(HP, B, steps),
        in_specs=[pl.BlockSpec((R * GRID_W, LANES), lambda hp, b, r: (b * steps + r, hp)),
                  pl.BlockSpec((S, LANES), lambda hp, b, r: (b, HP + hp)),
                  pl.BlockSpec((S, LANES), lambda hp, b, r: (b, 2 * HP + hp)),
                  pl.BlockSpec((Lc, LANES), lambda hp, b, r: (ctx_blk0 + b, HP + hp)),
                  pl.BlockSpec((Lc, LANES), lambda hp, b, r: (ctx_blk0 + b, 2 * HP + hp))] + bias_specs,
        out_specs=pl.BlockSpec((R * GRID_W, LANES), lambda hp, b, r: (b * steps + r, hp)),
        out_shape=jax.ShapeDtypeStruct((B * S, NA_W), F32),
        compiler_params=pltpu.CompilerParams(
            dimension_semantics=("arbitrary", "arbitrary", "arbitrary"),
            vmem_limit_bytes=V7X_VMEM_LIMIT_BYTES),
        name="na_attention",
    )(qkv, qkv, qkv, qkv, qkv, *([bias] * R))
    yc = None
    if with_ctx:
        yc = pl.pallas_call(
            _ctx_attn_kernel,
            grid=(HP, B),
            in_specs=[pl.BlockSpec((Lc, LANES), lambda hp, b: (ctx_blk0 + b, hp)),
                      pl.BlockSpec((Lc, LANES), lambda hp, b: (ctx_blk0 + b, HP + hp)),
                      pl.BlockSpec((Lc, LANES), lambda hp, b: (ctx_blk0 + b, 2 * HP + hp))],
            out_specs=pl.BlockSpec((Lc, LANES), lambda hp, b: (b, hp)),
            out_shape=jax.ShapeDtypeStruct((B * Lc, NA_W), F32),
            compiler_params=pltpu.CompilerParams(dimension_semantics=("arbitrary", "arbitrary")),
            name="ctx_attention",
        )(qkv, qkv, qkv)
    return y, yc


def na_mixer(z_all, B, S, Lc, q_norm, k_norm, rpb, with_ctx):
    assert (C_NAQ, C_NAK, C_NAV) == (0, NA_W, 2 * NA_W)
    scale = NA_HEAD_DIM ** -0.5
    gains = jnp.concatenate([jnp.tile(q_norm.astype(F32) * scale, NA_HEADS),
                             jnp.tile(k_norm.astype(F32), NA_HEADS)])[None]
    y, yc = na_attention(na_qkv(z_all, gains), rpb, B, S, Lc, with_ctx)
    return y, yc


def _log_sigmoid(x):
    return jnp.minimum(x, 0.0) - jnp.log(1.0 + jnp.exp(-jnp.abs(x)))


def _gla_kernel(*refs, reverse, rope, finish, n_chunks, nb):
    it = iter(refs)

    def take(count):
        return [next(it) for _ in range(count)]

    q_refs, k_refs, v_refs, gd_refs = take(nb), take(nb), take(nb), take(nb)
    w2_ref, gb_ref, s0_ref = take(3)
    if rope:
        cos_ref, sin_ref = take(2)
    if finish:
        yo_ref = next(it)
        r_refs = take(nb)
        g_ref = next(it)
    y_ref, sfin_ref, s_scr = take(3)

    C = GLA_CHUNK
    n = pl.program_id(1)
    chains = [(i, h) for i in range(nb) for h in range(GLA_HEADS)]

    @pl.when(n == 0)
    def _():
        s_scr[...] = s0_ref[...]

    def rows(refs_):
        return jnp.concatenate([r[...] for r in refs_], axis=0)

    ti = lax.broadcasted_iota(jnp.int32, (nb * C, nb * C), 0)
    tj = lax.broadcasted_iota(jnp.int32, (nb * C, nb * C), 1)
    same_seq = (ti // C) == (tj // C)
    tri = jnp.where(same_seq & ((tj >= ti) if reverse else (tj <= ti)), 1.0, 0.0).astype(BF16)
    ci = lax.broadcasted_iota(jnp.int32, (C, C), 0)
    cj = lax.broadcasted_iota(jnp.int32, (C, C), 1)
    incl = (cj >= ci) if reverse else (cj <= ci)

    pre = _mm(rows(gd_refs), w2_ref[...], NN_DIMS, 3) + gb_ref[...]
    la = _log_sigmoid(pre) * (1.0 / GLA_GATE_TAU)
    cs = _cumsum_rows(tri, la)
    ref_i = C // 2 - 1 if reverse else C // 2
    last_i = 0 if reverse else C - 1

    def per_seq_row(idx):
        return jnp.concatenate([jnp.broadcast_to(cs[i * C + idx:i * C + idx + 1, :], (C, GLA_QK))
                                for i in range(nb)], axis=0)

    b_ref = per_seq_row(ref_i)
    b_last = per_seq_row(last_i)

    q = rows(q_refs) * (GLA_DK ** -0.5)
    k = rows(k_refs)
    if rope:
        lane = lax.broadcasted_iota(jnp.int32, q.shape, 1)
        first = (lane % (GLA_DK // 2)) < GLA_DK // 4
        cos = jnp.concatenate([jnp.concatenate([cos_ref[...]] * GLA_HEADS, axis=1)] * nb, axis=0)
        sin = jnp.concatenate([jnp.concatenate([sin_ref[...]] * GLA_HEADS, axis=1)] * nb, axis=0)

        def rot(x):
            partner = jnp.where(first, pltpu.roll(x, GLA_QK - GLA_DK // 4, axis=1),
                                pltpu.roll(x, GLA_DK // 4, axis=1))
            return x * cos + partner * sin

        q = rot(q)
        k = rot(k)
    qi = q * jnp.exp(cs - b_ref)
    kj = k * jnp.exp(b_ref - cs)
    qe = q * jnp.exp(cs)
    ke = k * jnp.exp(b_last - cs)
    dec = jnp.exp(b_last)
    v_all = rows(v_refs)

    def hk(x, i, h):
        return x[i * C:(i + 1) * C, h * GLA_DK:(h + 1) * GLA_DK]

    def hv(x, i, h):
        return x[i * C:(i + 1) * C, h * GLA_DV:(h + 1) * GLA_DV]

    St = [s_scr[i, h] for i, h in chains]
    V = [hv(v_all, i, h) for i, h in chains]
    att = [jnp.where(incl, _mm(hk(qi, i, h), hk(kj, i, h), NT_DIMS), 0.0) for i, h in chains]
    y = [_mm(att[c], V[c], NN_DIMS) + _mm(hk(qe, i, h), St[c], NT_DIMS) for c, (i, h) in enumerate(chains)]
    kvt = [_mm(V[c], hk(ke, i, h), TN_DIMS) for c, (i, h) in enumerate(chains)]
    for c, (i, h) in enumerate(chains):
        s_scr[i, h] = St[c] * hk(dec, i, h)[0:1, :] + kvt[c]
    if finish:
        r = rows(r_refs)
        g = g_ref[...]
        for c, (i, h) in enumerate(chains):
            ys = y[c] + yo_ref[i, :, h * GLA_DV:(h + 1) * GLA_DV]
            ms = jnp.mean(ys * ys, axis=-1, keepdims=True)
            rh = hv(r, i, h)
            y[c] = ys * lax.rsqrt(ms + EPS) * g * (rh * jax.nn.sigmoid(rh))
    for i in range(nb):
        y_ref[i] = jnp.concatenate(y[i * GLA_HEADS:(i + 1) * GLA_HEADS], axis=1)

    @pl.when(n == n_chunks - 1)
    def _():
        sfin_ref[...] = s_scr[...]


def gla_rope_tables(seq_len):
    t = np.arange(seq_len)
    quarter = GLA_DK // 4
    inv = ROPE_BASE ** (-np.arange(0, 2 * quarter, 2, dtype=np.float64) / (2 * quarter))
    ar = (t // GRID_W)[:, None] * inv[None, :]
    ac = (t % GRID_W)[:, None] * inv[None, :]
    cos = np.concatenate([np.cos(ar), np.cos(ar), np.cos(ac), np.cos(ac)], axis=1)
    sin = np.concatenate([-np.sin(ar), np.sin(ar), -np.sin(ac), np.sin(ac)], axis=1)
    return jnp.asarray(cos, F32), jnp.asarray(sin, F32)


def gla_scan(z, w2pad, gate_b, s0, *, batch, seq_len, row0, reverse, rope, finish_with=None):
    N = seq_len // GLA_CHUNK
    blk0 = row0 // GLA_CHUNK
    off_q, off_k, off_v, off_r, off_gd = C_GQ, C_GK, C_GV, C_GR, C_GGD
    assert off_q % GLA_QK == 0 and off_k % GLA_QK == 0 and off_v % GLA_V == 0 and off_r % GLA_V == 0
    assert off_gd % LANES == 0

    def tb(n):
        return N - 1 - n if reverse else n

    def rb(b, n):
        return blk0 + b * N + tb(n)

    def ob(b, n):
        return b * N + tb(n)

    nb = GLA_BATCH_PER_STEP if batch % GLA_BATCH_PER_STEP == 0 else 1

    def tok_specs(width, col_off):
        return [pl.BlockSpec((GLA_CHUNK, width), lambda bb, n, i=i: (rb(bb * nb + i, n), col_off // width))
                for i in range(nb)]

    st_spec = pl.BlockSpec((nb, GLA_HEADS, GLA_DV, GLA_DK), lambda bb, n: (bb, 0, 0, 0))
    y_spec = pl.BlockSpec((nb, GLA_CHUNK, GLA_V), lambda bb, n: (bb, tb(n), 0))
    in_specs = (tok_specs(GLA_QK, off_q) + tok_specs(GLA_QK, off_k) + tok_specs(GLA_V, off_v)
                + tok_specs(LANES, off_gd)
                + [pl.BlockSpec((LANES, GLA_QK), lambda bb, n: (0, 0)),
                   pl.BlockSpec((1, GLA_QK), lambda bb, n: (0, 0)),
                   st_spec])
    args = [z] * (4 * nb) + [w2pad, gate_b, s0]
    if rope:
        cos, sin = gla_rope_tables(seq_len)
        in_specs += [pl.BlockSpec((GLA_CHUNK, GLA_DK), lambda bb, n: (tb(n), 0))] * 2
        args += [cos, sin]
    if finish_with is not None:
        y_other, norm_g = finish_with
        in_specs += [y_spec] + tok_specs(GLA_V, off_r) + [pl.BlockSpec((1, GLA_DV), lambda bb, n: (0, 0))]
        args += [y_other.reshape(batch, seq_len, GLA_V)] + [z] * nb + [norm_g]
    y, s_fin = pl.pallas_call(
        functools.partial(_gla_kernel, reverse=reverse, rope=rope, finish=finish_with is not None,
                          n_chunks=N, nb=nb),
        grid=(batch // nb, N),
        in_specs=in_specs,
        out_specs=[y_spec, st_spec],
        out_shape=[jax.ShapeDtypeStruct((batch, seq_len, GLA_V), F32),
                   jax.ShapeDtypeStruct((batch, GLA_HEADS, GLA_DV, GLA_DK), F32)],
        scratch_shapes=[pltpu.VMEM((nb, GLA_HEADS, GLA_DV, GLA_DK), F32)],
        compiler_params=pltpu.CompilerParams(
            dimension_semantics=("arbitrary", "arbitrary"),
            vmem_limit_bytes=V7X_VMEM_LIMIT_BYTES),
        name="gla_scan_rev" if reverse else "gla_scan",
    )(*args)
    return y.reshape(batch * seq_len, GLA_V), s_fin


def gla_mixer(z_all, B, S, Lc, gate_w2, gate_b, norm_g, with_ctx):
    s0 = jnp.zeros((B, GLA_HEADS, GLA_DV, GLA_DK), F32)
    g = norm_g.astype(F32)[None]
    w2 = [jnp.zeros((LANES, GLA_QK), F32).at[d * GLA_GATE_RANK:(d + 1) * GLA_GATE_RANK].set(gate_w2[d])
          for d in range(2)]
    gb = [gate_b[d].astype(F32)[None] for d in range(2)]
    ctx = dict(batch=B, seq_len=Lc, row0=B * S, rope=False)
    lat = dict(batch=B, seq_len=S, row0=0, rope=True)
    yc_f, sc_f = gla_scan(z_all, w2[0], gb[0], s0, reverse=False, **ctx)
    yc, sc_b = gla_scan(z_all, w2[1], gb[1], s0, reverse=True, finish_with=(yc_f, g), **ctx)
    yl_f, _ = gla_scan(z_all, w2[0], gb[0], sc_f, reverse=False, **lat)
    yl, _ = gla_scan(z_all, w2[1], gb[1], sc_b, reverse=True, finish_with=(yl_f, g), **lat)
    return yl, (yc if with_ctx else None)


RW_PREP_TM = 256
RW_HALO = 8
RW_N_PIECES = 5


def _heads_sum(x):
    return jnp.concatenate([_head_sum(x[:, j * LANES:(j + 1) * LANES]) for j in range(x.shape[1] // LANES)],
                           axis=1)


def _rw_prep_kernel(*refs, tm, n_latent_tiles, tiles_per_seq):
    P = RW_N_PIECES
    z_refs, prev_refs, next_refs, mu_refs = refs[0:P], refs[P:2 * P], refs[2 * P:3 * P], refs[3 * P:4 * P]
    kk_ref, ka_ref, rk_ref, w0_ref, a0_ref, w2_ref, a2_ref, g2_ref = refs[4 * P:4 * P + 8]
    (r_o, v_o, kap_o, lw0_o, kd0_o, b0_o, lw1_o, kd1_o, b1_o, bonus_o, g_o) = refs[4 * P + 8:]
    i = pl.program_id(0)
    latent = i < n_latent_tiles
    first = jnp.logical_or(jnp.logical_not(latent), i % tiles_per_seq == 0)
    last = jnp.logical_or(jnp.logical_not(latent), i % tiles_per_seq == tiles_per_seq - 1)

    def shifted(j):
        z = z_refs[j][...]
        row = lax.broadcasted_iota(jnp.int32, z.shape, 0)
        prev_row = jnp.where(first, 0.0, prev_refs[j][RW_HALO - 1:RW_HALO, :])
        next_row = jnp.where(last, 0.0, next_refs[j][0:1, :])
        prev = jnp.where(row == 0, prev_row, pltpu.roll(z, 1, axis=0))
        nxt = jnp.where(row == tm - 1, next_row, pltpu.roll(z, tm - 1, axis=0))
        return z + mu_refs[j][...] * (0.5 * (prev + nxt) - z)

    r, k, v, lr, gd = (shifted(j) for j in range(P))
    kkr = k * kk_ref[...]
    kap = kkr * lax.rsqrt(_heads_sum(kkr * kkr) + EPS)
    r_o[...] = r
    v_o[...] = v
    kap_o[...] = kap
    bonus_o[...] = _heads_sum(r * k * rk_ref[...]) * v
    g_o[...] = _mm(jax.nn.sigmoid(gd), g2_ref[...], NN_DIMS, 3)
    ka = ka_ref[...]
    for d, (lw_o, kd_o, b_o) in enumerate(((lw0_o, kd0_o, b0_o), (lw1_o, kd1_o, b1_o))):
        wd = lr[:, d * LANES:(d + 1) * LANES]
        ad = lr[:, (2 + d) * LANES:(3 + d) * LANES]
        w_log = _log_sigmoid(w0_ref[d:d + 1, :] + _mm(jnp.tanh(wd), w2_ref[d], NN_DIMS, 3)) - 0.5
        lw_o[...] = -jnp.exp(w_log)
        a = jax.nn.sigmoid(a0_ref[d:d + 1, :] + _mm(ad, a2_ref[d], NN_DIMS, 3))
        kd_o[...] = k * (1.0 + (a - 1.0) * ka)
        b_o[...] = kap * a


def rwkv_prep(z_all, n_latent_rows, seq_len, mu_p, k_k, k_a, r_k, w0, a0, w2p, a2p, g2):
    M = z_all.shape[0]
    tm = RW_PREP_TM
    assert M % tm == 0 and n_latent_rows % tm == 0 and seq_len % tm == 0
    widths = (RW_W, RW_W, RW_W, RW_LR_W, RW_GATE_RANK)
    offs = (C_RR, C_RK, C_RV, C_RLR, C_RGD)
    n_halo_blocks = M // RW_HALO
    z_specs, prev_specs, next_specs, mu_specs = [], [], [], []
    for w, o in zip(widths, offs):
        assert o % w == 0
        cb = o // w
        z_specs.append(pl.BlockSpec((tm, w), lambda i, cb=cb: (i, cb)))
        prev_specs.append(pl.BlockSpec(
            (RW_HALO, w), lambda i, cb=cb: (jnp.maximum(i * (tm // RW_HALO) - 1, 0), cb)))
        next_specs.append(pl.BlockSpec(
            (RW_HALO, w), lambda i, cb=cb: (jnp.minimum((i + 1) * (tm // RW_HALO), n_halo_blocks - 1), cb)))
        mu_specs.append(pl.BlockSpec((1, w), lambda i: (0, 0)))

    def full(shape):
        return pl.BlockSpec(shape, lambda i: (0,) * len(shape))

    par_specs = [full((1, RW_W))] * 3 + [full((2, RW_W))] * 2 + [full((2, LANES, RW_W))] * 2 \
        + [full((RW_GATE_RANK, RW_W))]
    out_spec = pl.BlockSpec((tm, RW_W), lambda i: (i, 0))
    return pl.pallas_call(
        functools.partial(_rw_prep_kernel, tm=tm, n_latent_tiles=n_latent_rows // tm,
                          tiles_per_seq=seq_len // tm),
        grid=(M // tm,),
        in_specs=z_specs + prev_specs + next_specs + mu_specs + par_specs,
        out_specs=[out_spec] * 11,
        out_shape=[jax.ShapeDtypeStruct((M, RW_W), F32)] * 11,
        compiler_params=pltpu.CompilerParams(
            dimension_semantics=("arbitrary",), vmem_limit_bytes=V7X_VMEM_LIMIT_BYTES),
        name="rwkv_prep",
    )(*([z_all] * 15), *mu_p, k_k, k_a, r_k, w0, a0, w2p, a2p, g2)


def _rw_chunk_kernel(*refs, reverse, n_par, n_chunks, passes, finish):
    r_ref, v_ref, kap_ref, lw_ref, kd_ref, bet_ref, s0_ref = refs[:7]
    if finish:
        yo_ref, bonus_ref, g_ref, gam_ref, gnb_ref = refs[7:12]
    y_ref, sfin_ref, s_scr = refs[-3:]
    _rw_chunk_body(r_ref, v_ref, kap_ref, lw_ref, kd_ref, bet_ref, s0_ref, y_ref, sfin_ref, s_scr,
                   (yo_ref, bonus_ref, g_ref, gam_ref, gnb_ref) if finish else None,
                   reverse=reverse, n_par=n_par, n_chunks=n_chunks, passes=passes)


def _rw_chunk_body(r_ref, v_ref, kap_ref, lw_ref, kd_ref, bet_ref, s0_ref, y_ref, sfin_ref, s_scr, fin_refs,
                   *, reverse, n_par, n_chunks, passes):
    C = RW_CHUNK
    n = pl.program_id(2)
    P = range(n_par)

    @pl.when(n == 0)
    def _():
        s_scr[...] = s0_ref[0, 0]

    row = lax.broadcasted_iota(jnp.int32, (C, LANES), 0)
    lane = lax.broadcasted_iota(jnp.int32, (C, LANES), 1)
    col = lane % RW_HEAD
    head0 = lane < RW_HEAD
    strict = (col > row) if reverse else (col < row)
    incl = (col >= row) if reverse else (col <= row)
    ti = lax.broadcasted_iota(jnp.int32, (C, C), 0)
    tj = lax.broadcasted_iota(jnp.int32, (C, C), 1)
    tri = jnp.where((tj >= ti) if reverse else (tj <= ti), 1.0, 0.0).astype(BF16)
    eye = jnp.where(row == col, 1.0, 0.0)
    brow = lax.broadcasted_iota(jnp.int32, (LANES, LANES), 0) // RW_HEAD
    bcol = lax.broadcasted_iota(jnp.int32, (LANES, LANES), 1) // RW_HEAD
    same_head = brow == bcol

    def sl(x, p):
        return x[:, p * LANES:(p + 1) * LANES]

    def bd(x):
        return jnp.concatenate([jnp.where(head0, x, 0.0), jnp.where(head0, 0.0, x)], axis=0)

    def pp(a_pair, b_pair):
        return _mm(a_pair, bd(b_pair), NN_DIMS, passes)

    lw_all = lw_ref[...]
    cs_all = _cumsum_rows(tri, lw_all)
    tot_all = cs_all[0:1, :] if reverse else cs_all[C - 1:C, :]
    inv_all = jnp.exp(-cs_all)
    fin_all = jnp.exp(tot_all - cs_all)
    kd_all = kd_ref[...]
    bet_all = bet_ref[...]
    kap_t = kap_ref[...] * jnp.exp(cs_all - lw_all)
    r_t = r_ref[...] * jnp.exp(cs_all)
    k_t = kd_all * inv_all
    b_t = bet_all * inv_all
    k_h = kd_all * fin_all
    b_h = bet_all * fin_all
    dec = jnp.exp(tot_all)
    v_all = v_ref[...]

    S = [s_scr[p] for p in P]
    V = [sl(v_all, p) for p in P]
    lhs = [jnp.concatenate([sl(kap_t, p), sl(r_t, p)], axis=0) for p in P]
    a_k = [_mm(lhs[p], bd(sl(k_t, p)), NT_DIMS, passes) for p in P]
    a_b = [_mm(lhs[p], bd(sl(b_t, p)), NT_DIMS, passes) for p in P]
    s0_both = [_mm(lhs[p], S[p], NT_DIMS, passes) for p in P]
    lmat = [jnp.where(strict, a_b[p][:C], 0.0) for p in P]
    power = [-lmat[p] for p in P]
    t_inv = [eye + power[p] for p in P]
    s = 2
    while s < C:
        power = [pp(power[p], power[p]) for p in P]
        t_inv = [t_inv[p] + pp(t_inv[p], power[p]) for p in P]
        s *= 2
    x = [s0_both[p][:C] + pp(jnp.where(strict, a_k[p][:C], 0.0), V[p]) for p in P]
    u = [pp(t_inv[p], x[p]) for p in P]
    y = [s0_both[p][C:] + pp(jnp.where(incl, a_k[p][C:], 0.0), V[p])
         - pp(jnp.where(incl, a_b[p][C:], 0.0), u[p]) for p in P]
    upd = [_mm(V[p], sl(k_h, p), TN_DIMS, passes) - _mm(u[p], sl(b_h, p), TN_DIMS, passes) for p in P]
    for p in P:
        s_scr[p] = S[p] * sl(dec, p) + jnp.where(same_head, upd[p], 0.0)
    if fin_refs is None:
        y_ref[...] = jnp.concatenate(y, axis=1)
    else:
        yo_ref, bonus_ref, g_ref, gam_ref, gnb_ref = fin_refs
        ys = jnp.concatenate(y, axis=1) + yo_ref[...]
        cen = ys - _heads_sum(ys) * (1.0 / RW_HEAD)
        var = _heads_sum(cen * cen) * (1.0 / RW_HEAD)
        out = cen * lax.rsqrt(var + RW_GN_EPS) * gam_ref[...] + gnb_ref[...] + bonus_ref[...]
        y_ref[...] = out * g_ref[...]

    @pl.when(n == n_chunks - 1)
    def _():
        sfin_ref[0, 0] = s_scr[...]


def rwkv_chunk_scan(r, v, kap, lw, kd, bet, s0, *, batch, seq_len, row0, reverse, finish_with=None):
    n_par = RW_PAIRS_PER_STEP
    W = r.shape[1]
    G = W // (n_par * LANES)
    N = seq_len // RW_CHUNK
    blk0 = row0 // RW_CHUNK

    def tb(n):
        return N - 1 - n if reverse else n

    seq_in = pl.BlockSpec((RW_CHUNK, n_par * LANES), lambda b, g, n: (blk0 + b * N + tb(n), g))
    seq_out = pl.BlockSpec((RW_CHUNK, n_par * LANES), lambda b, g, n: (b * N + tb(n), g))
    st_spec = pl.BlockSpec((1, 1, n_par, LANES, LANES), lambda b, g, n: (b, g, 0, 0, 0))
    in_specs = [seq_in] * 6 + [st_spec]
    args = [r, v, kap, lw, kd, bet, s0]
    if finish_with is not None:
        y_other, bonus, gate, gamma, beta = finish_with
        vec_spec = pl.BlockSpec((1, n_par * LANES), lambda b, g, n: (0, g))
        in_specs += [seq_out, seq_in, seq_in, vec_spec, vec_spec]
        args += [y_other, bonus, gate, gamma, beta]
    return pl.pallas_call(
        functools.partial(_rw_chunk_kernel, reverse=reverse, n_par=n_par, n_chunks=N, passes=RW_PASSES,
                          finish=finish_with is not None),
        grid=(batch, G, N),
        in_specs=in_specs,
        out_specs=[seq_out, st_spec],
        out_shape=[jax.ShapeDtypeStruct((batch * seq_len, W), F32),
                   jax.ShapeDtypeStruct((batch, G, n_par, LANES, LANES), F32)],
        scratch_shapes=[pltpu.VMEM((n_par, LANES, LANES), F32)],
        compiler_params=pltpu.CompilerParams(
            dimension_semantics=("arbitrary", "arbitrary", "arbitrary"),
            vmem_limit_bytes=V7X_VMEM_LIMIT_BYTES),
        name="rwkv_chunk_scan_rev" if reverse else "rwkv_chunk_scan",
    )(*args)


def _pad_rows(w, rows):
    return jnp.pad(w.astype(F32), ((0, 0), (0, rows - w.shape[1]), (0, 0)))


def rwkv_mixer(z_all, B, S, Lc, mu, w0, w2, a0, a2, g2, k_k, k_a, r_k, gn_g, gn_b, with_ctx):
    assert Lc == RW_PREP_TM
    mu = mu.astype(F32)
    o = np.cumsum((0,) + RW_SPLITS)
    lr_pad = jnp.zeros((LANES - RW_DECAY_RANK,), F32)
    mu_lr = jnp.concatenate([mu[o[3]:o[3] + RW_DECAY_RANK], lr_pad, mu[o[3] + RW_DECAY_RANK:o[4]], lr_pad,
                             mu[o[4]:o[4] + RW_A_RANK], lr_pad, mu[o[4] + RW_A_RANK:o[5]], lr_pad])
    mu_p = [mu[o[0]:o[1]][None], mu[o[1]:o[2]][None], mu[o[2]:o[3]][None], mu_lr[None], mu[o[5]:o[6]][None]]
    row = lambda t: t.astype(F32).reshape(1, RW_W)
    prep = rwkv_prep(z_all, B * S, S, mu_p, row(k_k), row(k_a), row(r_k), w0.astype(F32), a0.astype(F32),
                     _pad_rows(w2, LANES), _pad_rows(a2, LANES), g2.astype(F32))
    r, v, kap, lw0, kd0, b0, lw1, kd1, b1, bonus, gate = prep
    fin = (bonus, gate, row(gn_g), row(gn_b))
    s0 = jnp.zeros((B, RW_W // (RW_PAIRS_PER_STEP * LANES), RW_PAIRS_PER_STEP, LANES, LANES), F32)
    ctx = dict(batch=B, seq_len=Lc, row0=B * S)
    lat = dict(batch=B, seq_len=S, row0=0)
    yc_f, sc_f = rwkv_chunk_scan(r, v, kap, lw0, kd0, b0, s0, reverse=False, **ctx)
    yc, sc_b = rwkv_chunk_scan(r, v, kap, lw1, kd1, b1, s0, reverse=True, finish_with=(yc_f,) + fin, **ctx)
    yl_f, _ = rwkv_chunk_scan(r, v, kap, lw0, kd0, b0, sc_f, reverse=False, **lat)
    yl, _ = rwkv_chunk_scan(r, v, kap, lw1, kd1, b1, sc_b, reverse=True, finish_with=(yl_f,) + fin, **lat)
    return yl, (yc if with_ctx else None)


def relayout_w_in(w):
    o = IN_OFFSETS
    rw = o[8] + np.cumsum((0,) + RW_SPLITS)
    zeros = lambda n: jnp.zeros((w.shape[0], n), w.dtype)
    seg = lambda lo, n: w[:, lo:lo + n]
    lr_pad = LANES - RW_DECAY_RANK
    pieces = [
        seg(o[0], 3 * NA_W),
        seg(o[5], GLA_V), seg(o[6], GLA_V),
        seg(rw[0], 3 * RW_W),
        seg(o[9], N_BRANCH * D_MODEL),
        seg(o[3], GLA_QK), seg(o[4], GLA_QK),
        seg(rw[3], RW_DECAY_RANK), zeros(lr_pad), seg(rw[3] + RW_DECAY_RANK, RW_DECAY_RANK), zeros(lr_pad),
        seg(rw[4], RW_A_RANK), zeros(lr_pad), seg(rw[4] + RW_A_RANK, RW_A_RANK), zeros(lr_pad),
        seg(rw[5], RW_GATE_RANK),
        seg(o[7], 2 * GLA_GATE_RANK),
    ]
    used = sum(x.shape[1] for x in pieces)
    assert used == C_GGD + 2 * GLA_GATE_RANK
    return jnp.concatenate(pieces + [zeros(Z_W - used)], axis=1).astype(BF16)


def token_mixers(x, ctx, mod, p, B, S, Lc, with_ctx):
    h_all = prenorm(x, ctx, p['norm1_g'], mod, seq_len=S)
    z_all = matmul(h_all, relayout_w_in(p['w_in']), tn=1024)
    y_na, yc_na = na_mixer(z_all, B, S, Lc, p['na_q_norm'], p['na_k_norm'], p['na_rpb'], with_ctx)
    y_gla, yc_gla = gla_mixer(z_all, B, S, Lc, p['gla_gate_w2'], p['gla_gate_b'], p['gla_norm_g'], with_ctx)
    y_rw, yc_rw = rwkv_mixer(z_all, B, S, Lc, p['rw_mu'], p['rw_w0'], p['rw_w2'], p['rw_a0'], p['rw_a2'],
                             p['rw_g2'], p['rw_k_k'], p['rw_k_a'], p['rw_r_k'], p['rw_gn_g'],
                             p['rw_gn_b'], with_ctx)
    merged = merge_branches((y_na, y_gla, y_rw), z_all, 0, p['w_branch'])
    x = residual_matmul(merged, p['w_out'], x, mod, gate_row=2, rows_per_cond=S)
    if with_ctx:
        merged_c = merge_branches((yc_na, yc_gla, yc_rw), z_all, B * S, p['w_branch'])
        ctx = residual_matmul(merged_c, p['w_out'], ctx, mod, gate_row=2, fixed_row=B)
    return x, ctx


SC_GATHER_WINDOW = 128
SC_PIECE = 256
MOE_TM = 512


def sc_gather_rows(x, indices):
    m, d = x.shape
    pieces = d // SC_PIECE
    idx = (indices.astype(jnp.int32)[:, None] * pieces + jnp.arange(pieces, dtype=jnp.int32)[None, :])
    n = idx.size
    assert d % SC_PIECE == 0 and n % SC_GATHER_WINDOW == 0
    mesh = plsc.VectorSubcoreMesh(core_axis_name="c", subcore_axis_name="s")

    @pl.kernel(out_type=jax.ShapeDtypeStruct((n, SC_PIECE), x.dtype), mesh=mesh)
    def gather_kernel(x_hbm, i_hbm, o_hbm):
        def body(i_vmem, o_vmem):
            pltpu.sync_copy(x_hbm.at[i_vmem.at[0]], o_vmem)

        pltpu.emit_pipeline(
            body,
            grid=(n // SC_GATHER_WINDOW,),
            in_specs=[pl.BlockSpec((1, SC_GATHER_WINDOW), index_map=lambda i: (0, i))],
            out_specs=[pl.BlockSpec((SC_GATHER_WINDOW, SC_PIECE), index_map=lambda i: (i, 0))],
            core_axis_name="s",
            dimension_semantics=(pltpu.PARALLEL,),
        )(i_hbm, o_hbm)

    out = gather_kernel(x.reshape(m * pieces, SC_PIECE), idx.reshape(1, n))
    return out.reshape(indices.shape[0], d)


def _moe_group_kernel(tg_ref, tu_ref, h_ref, comb_ref, wg_ref, wu_ref, wd_ref, o_ref, acc_ref, hb_ref):
    t = pl.program_id(0)
    j = pl.program_id(1)
    e = tg_ref[t] * EXPERTS_PER_GROUP + j

    @pl.when(j == 0)
    def _():
        acc_ref[...] = jnp.zeros_like(acc_ref)
        hb_ref[...] = h_ref[...].astype(BF16)

    @pl.when(tu_ref[t] > 0)
    def _():
        h = hb_ref[...]
        g = jnp.dot(h, wg_ref[0], preferred_element_type=F32)
        u = jnp.dot(h, wu_ref[0], preferred_element_type=F32)
        comb = comb_ref[...]
        lane = lax.broadcasted_iota(jnp.int32, comb.shape, 1)
        ce = jnp.sum(jnp.where(lane == e, comb, 0.0), axis=1, keepdims=True)
        act = (g * jax.nn.sigmoid(g)) * u * ce
        acc_ref[...] += jnp.dot(act.astype(BF16), wd_ref[0], preferred_element_type=F32)

    @pl.when(j == EXPERTS_PER_GROUP - 1)
    def _():
        o_ref[...] = acc_ref[...]


def moe_group_experts(h_sorted, comb_sorted, tile_group, tile_used, w_gate, w_up, w_down):
    P, D = h_sorted.shape
    tm = MOE_TM

    def w_map(t, j, tg, tu):
        return (tg[t] * EXPERTS_PER_GROUP + j, 0, 0)

    grid_spec = pltpu.PrefetchScalarGridSpec(
        num_scalar_prefetch=2,
        grid=(P // tm, EXPERTS_PER_GROUP),
        in_specs=[pl.BlockSpec((tm, D), lambda t, j, tg, tu: (t, 0)),
                  pl.BlockSpec((tm, N_EXPERTS), lambda t, j, tg, tu: (t, 0)),
                  pl.BlockSpec((1, D, D_EXPERT), w_map),
                  pl.BlockSpec((1, D, D_EXPERT), w_map),
                  pl.BlockSpec((1, D_EXPERT, D), w_map)],
        out_specs=pl.BlockSpec((tm, D), lambda t, j, tg, tu: (t, 0)),
        scratch_shapes=[pltpu.VMEM((tm, D), F32), pltpu.VMEM((tm, D), BF16)])
    return pl.pallas_call(
        _moe_group_kernel,
        grid_spec=grid_spec,
        out_shape=jax.ShapeDtypeStruct((P, D), F32),
        compiler_params=pltpu.CompilerParams(
            dimension_semantics=("arbitrary", "arbitrary"),
            vmem_limit_bytes=V7X_VMEM_LIMIT_BYTES),
        name="moe_group_experts",
    )(tile_group, tile_used, h_sorted, comb_sorted, w_gate, w_up, w_down)


def _residual_add_kernel(x_ref, y_ref, mod_ref, o_ref, *, gate_row):
    o_ref[...] = x_ref[...] + mod_ref[0, gate_row:gate_row + 1, :] * y_ref[...]


def residual_add(x, y, mod, *, gate_row, rows_per_cond=None, fixed_row=None):
    cond = _cond_of_tile(rows_per_cond, ROW_TM, fixed_row)
    spec = pl.BlockSpec((ROW_TM, D_MODEL), lambda i: (i, 0))
    return pl.pallas_call(
        functools.partial(_residual_add_kernel, gate_row=gate_row),
        grid=(x.shape[0] // ROW_TM,),
        in_specs=[spec, spec, pl.BlockSpec((1, N_MOD, D_MODEL), lambda i: (cond(i), 0, 0))],
        out_specs=spec,
        out_shape=jax.ShapeDtypeStruct(x.shape, F32),
        compiler_params=pltpu.CompilerParams(dimension_semantics=("arbitrary",)),
        name="residual_add",
    )(x, y, mod)


def group_dispatch_plan(comb_t):
    M = comb_t.shape[1]
    n_slots = M + N_GROUPS * MOE_TM
    grp_mass = comb_t.reshape(N_GROUPS, EXPERTS_PER_GROUP, M).sum(axis=1)
    grp = jnp.argmax(grp_mass > 0, axis=0).astype(jnp.int32)
    onehot = (grp[:, None] == jnp.arange(N_GROUPS, dtype=jnp.int32)[None, :]).astype(jnp.int32)
    pos = jnp.sum((jnp.cumsum(onehot, axis=0) - onehot) * onehot, axis=1)
    counts = jnp.sum(onehot, axis=0)
    padded = ((counts + MOE_TM - 1) // MOE_TM) * MOE_TM
    ends = jnp.cumsum(padded)
    starts = ends - padded
    slot = jnp.sum(onehot * starts[None, :], axis=1) + pos
    token = jnp.arange(M, dtype=jnp.int32)
    src = jnp.zeros((n_slots,), jnp.int32).at[slot].set(token)
    valid = jnp.zeros((n_slots,), F32).at[slot].set(1.0)
    tile_row = jnp.arange(n_slots // MOE_TM, dtype=jnp.int32) * MOE_TM
    tile_group = jnp.minimum(jnp.sum(tile_row[:, None] >= ends[None, :], axis=1), N_GROUPS - 1).astype(jnp.int32)
    tile_used = (tile_row < ends[-1]).astype(jnp.int32)
    return src, slot, valid, tile_group, tile_used


def moe_sublayer(x, mod, p, *, rows_per_cond=None, fixed_row=None):
    h2, comb_t = prenorm_router(x, p['norm2_g'], mod, p['router_w_t'], p['router_bias'],
                                rows_per_cond=rows_per_cond, fixed_row=fixed_row)
    return moe_experts(h2, comb_t.T, p['moe_w_gate'], p['moe_w_up'], p['moe_w_down'], x, mod,
                       rows_per_cond=rows_per_cond, fixed_row=fixed_row)


def kernel(x, c, ctx, c_ctx, ada_w, ada_b, norm1_g, norm2_g, w_in, na_q_norm, na_k_norm, na_rpb,
           gla_gate_w2, gla_gate_b, gla_norm_g, rw_mu, rw_w0, rw_w2, rw_a0, rw_a2, rw_g2,
           rw_k_k, rw_k_a, rw_r_k, rw_gn_g, rw_gn_b, w_branch, w_out, router_w, router_bias,
           moe_w_gate, moe_w_up, moe_w_down):
    B, S, D = x.shape
    Lc = ctx.shape[1]
    assert B + 1 <= N_COND and S % 512 == 0 and (B * Lc) % 512 == 0 and Lc % ROW_TM == 0
    c_rows = jnp.concatenate([c, c_ctx[None], jnp.zeros((N_COND - B - 1, D), c.dtype)], axis=0).astype(F32)
    mods = ada_modulation(c_rows, ada_w, ada_b).reshape(DEPTH, N_COND, N_MOD, D)
    x = x.reshape(B * S, D)
    ctx = ctx.reshape(B * Lc, D)
    router_w_t = router_w.astype(F32).T
    router_b = router_bias.astype(F32).reshape(N_EXPERTS, 1)
    for l in range(DEPTH):
        with_ctx = l < DEPTH - 1
        p = {
            'w_in': w_in[l], 'na_q_norm': na_q_norm[l], 'na_k_norm': na_k_norm[l], 'na_rpb': na_rpb[l],
            'gla_gate_w2': gla_gate_w2[l], 'gla_gate_b': gla_gate_b[l], 'gla_norm_g': gla_norm_g[l],
            'rw_mu': rw_mu[l], 'rw_w0': rw_w0[l], 'rw_w2': rw_w2[l], 'rw_a0': rw_a0[l], 'rw_a2': rw_a2[l],
            'rw_g2': rw_g2[l], 'rw_k_k': rw_k_k[l], 'rw_k_a': rw_k_a[l], 'rw_r_k': rw_r_k[l],
            'rw_gn_g': rw_gn_g[l], 'rw_gn_b': rw_gn_b[l], 'w_branch': w_branch[l], 'w_out': w_out[l],
            'norm1_g': norm1_g[l].astype(F32)[None], 'norm2_g': norm2_g[l].astype(F32)[None],
            'router_w_t': router_w_t, 'router_bias': router_b,
            'moe_w_gate': moe_w_gate[l].astype(BF16), 'moe_w_up': moe_w_up[l].astype(BF16),
            'moe_w_down': moe_w_down[l].astype(BF16),
        }
        x, ctx = token_mixers(x, ctx, mods[l], p, B, S, Lc, with_ctx)
        x = moe_sublayer(x, mods[l], p, rows_per_cond=S)
        if with_ctx:
            ctx = moe_sublayer(ctx, mods[l], p, fixed_row=B)
    return x.reshape(B, S, D)
```

```python
import functools

import jax
import jax.numpy as jnp
from jax import lax
import numpy as np
from jax.experimental import pallas as pl
from jax.experimental.pallas import tpu as pltpu

D_MODEL = 2048
DEPTH = 2
GRID_W = 64

NA_HEADS = 16
NA_HEAD_DIM = 64
NA_WIN_R_MAX = 8
NA_WIN_C = 16
NA_W = NA_HEADS * NA_HEAD_DIM
NA_ROWS_PER_STEP = 8

GLA_HEADS = 4
GLA_DK = 128
GLA_DV = 256
GLA_QK = GLA_HEADS * GLA_DK
GLA_V = GLA_HEADS * GLA_DV
GLA_GATE_RANK = 16
GLA_GATE_TAU = 16.0
GLA_CHUNK = 64
GLA_BATCH_PER_STEP = 4

RW_HEADS = 16
RW_HEAD = 64
RW_W = RW_HEADS * RW_HEAD
RW_DECAY_RANK = 96
RW_A_RANK = 96
RW_GATE_RANK = 256
RW_SPLITS = (RW_W, RW_W, RW_W, 2 * RW_DECAY_RANK, 2 * RW_A_RANK, RW_GATE_RANK)
RW_IN = sum(RW_SPLITS)
RW_GN_EPS = 64e-5
RW_CHUNK = 64
RW_PASSES = 1
RW_PAIRS_PER_STEP = 8
RW_BATCH_PER_STEP = 4

N_BRANCH = 3
BRANCH_W = 1024
IN_SPLITS = (NA_W, NA_W, NA_W, GLA_QK, GLA_QK, GLA_V, GLA_V, 2 * GLA_GATE_RANK, RW_IN, N_BRANCH * D_MODEL)
D_IN = sum(IN_SPLITS)
IN_OFFSETS = tuple(int(o) for o in np.cumsum((0,) + IN_SPLITS[:-1]))

LANES = 128
RW_LR_W = 4 * LANES
C_NAQ, C_NAK, C_NAV = 0, 1024, 2048
C_GV, C_GR = 3072, 4096
C_RR, C_RK, C_RV = 5120, 6144, 7168
C_GATE = 8192
C_GQ, C_GK = 14336, 14848
C_RLR, C_RGD, C_GGD = 15360, 15872, 16128
Z_W = 16384

N_EXPERTS = 16
N_GROUPS = 4
EXPERTS_PER_GROUP = N_EXPERTS // N_GROUPS
TOP_K = 2
D_EXPERT = 512

ROPE_BASE = 10000.0
EPS = 1e-6
F32 = jnp.float32
BF16 = jnp.bfloat16

V7X_VMEM_LIMIT_BYTES = 48 * 1024 * 1024

NEG_BIG = -1e30
NT_DIMS = (((1,), (1,)), ((), ()))
TN_DIMS = (((0,), (0,)), ((), ()))
NN_DIMS = (((1,), (0,)), ((), ()))


def _split_bf16(x):
    hi = x.astype(BF16)
    lo = (x - hi.astype(F32)).astype(BF16)
    return hi, lo


def _mm(a, b, dims=NN_DIMS, passes=1):
    if passes == 1:
        return lax.dot_general(a.astype(BF16), b.astype(BF16), dims, preferred_element_type=F32)
    a_hi, a_lo = _split_bf16(a)
    b_hi, b_lo = _split_bf16(b)
    out = lax.dot_general(a_hi, b_hi, dims, preferred_element_type=F32)
    out = out + lax.dot_general(a_hi, b_lo, dims, preferred_element_type=F32)
    return out + lax.dot_general(a_lo, b_hi, dims, preferred_element_type=F32)


def _cumsum_rows(tri, x):
    hi, lo = _split_bf16(x)
    return jnp.dot(tri, hi, preferred_element_type=F32) + jnp.dot(tri, lo, preferred_element_type=F32)


def _mm_kernel(a_ref, b_ref, o_ref):
    o_ref[...] = jnp.dot(a_ref[...], b_ref[...], preferred_element_type=F32).astype(o_ref.dtype)


def matmul(a, b, *, tm=512, tn=512, out_dtype=F32):
    M, K = a.shape
    _, N = b.shape
    assert M % tm == 0 and N % tn == 0, (M, N, tm, tn)
    return pl.pallas_call(
        _mm_kernel,
        grid=(N // tn, M // tm),
        in_specs=[pl.BlockSpec((tm, K), lambda n, m: (m, 0)),
                  pl.BlockSpec((K, tn), lambda n, m: (0, n))],
        out_specs=pl.BlockSpec((tm, tn), lambda n, m: (m, n)),
        out_shape=jax.ShapeDtypeStruct((M, N), out_dtype),
        compiler_params=pltpu.CompilerParams(
            dimension_semantics=("arbitrary", "arbitrary"),
            vmem_limit_bytes=V7X_VMEM_LIMIT_BYTES),
        name="matmul",
    )(a.astype(BF16), b.astype(BF16))


def _moe_kernel(h_ref, comb_ref, wg_ref, wu_ref, wd_ref, x_ref, mod_ref, o_ref, acc_ref):
    e = pl.program_id(1)

    @pl.when(e == 0)
    def _():
        acc_ref[...] = jnp.zeros_like(acc_ref)

    h = h_ref[...]
    g = jnp.dot(h, wg_ref[0], preferred_element_type=F32)
    u = jnp.dot(h, wu_ref[0], preferred_element_type=F32)
    comb = comb_ref[...]
    lane = lax.broadcasted_iota(jnp.int32, comb.shape, 1)
    ce = jnp.sum(jnp.where(lane == e, comb, 0.0), axis=1, keepdims=True)
    act = (g * jax.nn.sigmoid(g)) * u * ce
    acc_ref[...] += jnp.dot(act.astype(BF16), wd_ref[0], preferred_element_type=F32)

    @pl.when(e == N_EXPERTS - 1)
    def _():
        o_ref[...] = x_ref[...] + mod_ref[0, N_MOD - 1:N_MOD, :] * acc_ref[...]


def moe_experts(h, comb, w_gate, w_up, w_down, x, mod, *, rows_per_cond=None, fixed_row=None, tm=512):
    M, D = h.shape
    assert M % tm == 0
    cond = _cond_of_tile(rows_per_cond, tm, fixed_row)
    return pl.pallas_call(
        _moe_kernel,
        grid=(M // tm, N_EXPERTS),
        in_specs=[pl.BlockSpec((tm, D), lambda m, e: (m, 0)),
                  pl.BlockSpec((tm, N_EXPERTS), lambda m, e: (m, 0)),
                  pl.BlockSpec((1, D, D_EXPERT), lambda m, e: (e, 0, 0)),
                  pl.BlockSpec((1, D, D_EXPERT), lambda m, e: (e, 0, 0)),
                  pl.BlockSpec((1, D_EXPERT, D), lambda m, e: (e, 0, 0)),
                  pl.BlockSpec((tm, D), lambda m, e: (m, 0)),
                  pl.BlockSpec((1, N_MOD, D), lambda m, e: (cond(m), 0, 0))],
        out_specs=pl.BlockSpec((tm, D), lambda m, e: (m, 0)),
        out_shape=jax.ShapeDtypeStruct((M, D), F32),
        scratch_shapes=[pltpu.VMEM((tm, D), F32)],
        compiler_params=pltpu.CompilerParams(
            dimension_semantics=("arbitrary", "arbitrary"),
            vmem_limit_bytes=V7X_VMEM_LIMIT_BYTES),
        name="moe_experts",
    )(h, comb, w_gate, w_up, w_down, x, mod)


N_COND = 8
N_MOD = 6
ROW_TM = 256


def _ada_kernel(c_ref, w_ref, b_ref, o_ref):
    c = c_ref[...]
    o_ref[0] = _mm(c * jax.nn.sigmoid(c), w_ref[0], NN_DIMS, 3) + b_ref[0]


def ada_modulation(c_rows, ada_w, ada_b, *, tn=1024):
    L, D, N = ada_w.shape
    return pl.pallas_call(
        _ada_kernel,
        grid=(L, N // tn),
        in_specs=[pl.BlockSpec((N_COND, D), lambda l, n: (0, 0)),
                  pl.BlockSpec((1, D, tn), lambda l, n: (l, 0, n)),
                  pl.BlockSpec((1, 1, tn), lambda l, n: (l, 0, n))],
        out_specs=pl.BlockSpec((1, N_COND, tn), lambda l, n: (l, 0, n)),
        out_shape=jax.ShapeDtypeStruct((L, N_COND, N), F32),
        compiler_params=pltpu.CompilerParams(
            dimension_semantics=("arbitrary", "arbitrary"), vmem_limit_bytes=V7X_VMEM_LIMIT_BYTES),
        name="ada_modulation",
    )(c_rows, ada_w, ada_b.reshape(L, 1, N))


def _norm_mod(x, g, shift, scale):
    y = x * lax.rsqrt(jnp.mean(x * x, axis=-1, keepdims=True) + EPS)
    return y * g * (1.0 + scale) + shift


def _prenorm_kernel(x_ref, c_ref, g_ref, mod_ref, o_ref, *, n_latent_tiles):
    def emit(src_ref):
        o_ref[...] = _norm_mod(src_ref[...], g_ref[...], mod_ref[0, 0:1, :], mod_ref[0, 1:2, :]).astype(o_ref.dtype)

    @pl.when(pl.program_id(0) < n_latent_tiles)
    def _():
        emit(x_ref)

    @pl.when(pl.program_id(0) >= n_latent_tiles)
    def _():
        emit(c_ref)


def _top2_sum(a, b, c, d):
    hi1, lo1 = jnp.maximum(a, b), jnp.minimum(a, b)
    hi2, lo2 = jnp.maximum(c, d), jnp.minimum(c, d)
    return jnp.maximum(hi1, hi2) + jnp.maximum(jnp.minimum(hi1, hi2), jnp.maximum(lo1, lo2))


def _prenorm_router_kernel(x_ref, g_ref, mod_ref, rw_ref, rb_ref, o_ref, comb_ref):
    h = _norm_mod(x_ref[...], g_ref[...], mod_ref[0, 3:4, :], mod_ref[0, 4:5, :])
    o_ref[...] = h.astype(o_ref.dtype)
    s = jax.nn.sigmoid(_mm(rw_ref[...], h, NT_DIMS, 3))
    biased = s + rb_ref[...]
    rows = [biased[e:e + 1, :] for e in range(N_EXPERTS)]
    G = EXPERTS_PER_GROUP
    score = [_top2_sum(*rows[g * G:(g + 1) * G]) for g in range(N_GROUPS)]
    picked = []
    for e in range(N_EXPERTS):
        g = e // G
        ok = None
        for g2 in range(N_GROUPS):
            if g2 != g:
                t = (score[g] > score[g2]) if g2 < g else (score[g] >= score[g2])
                ok = t if ok is None else jnp.logical_and(ok, t)
        rank = 0.0
        for e2 in range(g * G, (g + 1) * G):
            if e2 != e:
                ahead = (rows[e2] >= rows[e]) if e2 < e else (rows[e2] > rows[e])
                rank = rank + jnp.where(ahead, 1.0, 0.0)
        picked.append(jnp.where(jnp.logical_and(ok, rank < TOP_K), s[e:e + 1, :], 0.0))
    w = jnp.concatenate(picked, axis=0)
    comb_ref[...] = w / jnp.sum(w, axis=0, keepdims=True)


def _cond_of_tile(rows_per_cond, tile_rows, fixed_row):
    if fixed_row is not None:
        return lambda m: fixed_row
    assert rows_per_cond % tile_rows == 0
    return lambda m: m // (rows_per_cond // tile_rows)


def prenorm(x, ctx, gain, mod, *, seq_len):
    n_lat, n_ctx = x.shape[0] // ROW_TM, ctx.shape[0] // ROW_TM
    n_batch = x.shape[0] // seq_len
    cond = _cond_of_tile(seq_len, ROW_TM, None)
    return pl.pallas_call(
        functools.partial(_prenorm_kernel, n_latent_tiles=n_lat),
        grid=(n_lat + n_ctx,),
        in_specs=[pl.BlockSpec((ROW_TM, D_MODEL), lambda i: (jnp.minimum(i, n_lat - 1), 0)),
                  pl.BlockSpec((ROW_TM, D_MODEL), lambda i: (jnp.maximum(i - n_lat, 0), 0)),
                  pl.BlockSpec((1, D_MODEL), lambda i: (0, 0)),
                  pl.BlockSpec((1, N_MOD, D_MODEL), lambda i: (jnp.where(i < n_lat, cond(i), n_batch), 0, 0))],
        out_specs=pl.BlockSpec((ROW_TM, D_MODEL), lambda i: (i, 0)),
        out_shape=jax.ShapeDtypeStruct((x.shape[0] + ctx.shape[0], D_MODEL), BF16),
        compiler_params=pltpu.CompilerParams(dimension_semantics=("arbitrary",)),
        name="prenorm",
    )(x, ctx, gain, mod)


def prenorm_router(x, gain, mod, router_w_t, router_bias, *, rows_per_cond=None, fixed_row=None):
    M = x.shape[0]
    cond = _cond_of_tile(rows_per_cond, ROW_TM, fixed_row)
    x_spec = pl.BlockSpec((ROW_TM, D_MODEL), lambda i: (i, 0))
    return pl.pallas_call(
        _prenorm_router_kernel,
        grid=(M // ROW_TM,),
        in_specs=[x_spec,
                  pl.BlockSpec((1, D_MODEL), lambda i: (0, 0)),
                  pl.BlockSpec((1, N_MOD, D_MODEL), lambda i: (cond(i), 0, 0)),
                  pl.BlockSpec((N_EXPERTS, D_MODEL), lambda i: (0, 0)),
                  pl.BlockSpec((N_EXPERTS, 1), lambda i: (0, 0))],
        out_specs=[x_spec, pl.BlockSpec((N_EXPERTS, ROW_TM), lambda i: (0, i))],
        out_shape=[jax.ShapeDtypeStruct((M, D_MODEL), BF16),
                   jax.ShapeDtypeStruct((N_EXPERTS, M), F32)],
        compiler_params=pltpu.CompilerParams(dimension_semantics=("arbitrary",)),
        name="prenorm_router",
    )(x, gain, mod, router_w_t, router_bias)


def _merge_kernel(y0_ref, y1_ref, y2_ref, g0_ref, g1_ref, g2_ref, wb_ref, o_ref):
    acc = 0.0
    for i, (y_ref, g_ref) in enumerate(((y0_ref, g0_ref), (y1_ref, g1_ref), (y2_ref, g2_ref))):
        acc = acc + jax.nn.sigmoid(g_ref[...]) * jnp.dot(y_ref[...].astype(BF16), wb_ref[i],
                                                         preferred_element_type=F32)
    o_ref[...] = acc.astype(o_ref.dtype)


def merge_branches(ys, z_all, row0, w_branch, *, tm=512, tn=1024):
    M = ys[0].shape[0]
    assert M % tm == 0 and row0 % tm == 0 and C_GATE % tn == 0 and D_MODEL % tn == 0
    row_blk0 = row0 // tm
    y_spec = pl.BlockSpec((tm, BRANCH_W), lambda m, n: (m, 0))

    def gate_spec(i):
        return pl.BlockSpec((tm, tn), lambda m, n: (row_blk0 + m, (C_GATE + i * D_MODEL) // tn + n))

    return pl.pallas_call(
        _merge_kernel,
        grid=(M // tm, D_MODEL // tn),
        in_specs=[y_spec] * 3 + [gate_spec(i) for i in range(N_BRANCH)]
        + [pl.BlockSpec((N_BRANCH, BRANCH_W, tn), lambda m, n: (0, 0, n))],
        out_specs=pl.BlockSpec((tm, tn), lambda m, n: (m, n)),
        out_shape=jax.ShapeDtypeStruct((M, D_MODEL), BF16),
        compiler_params=pltpu.CompilerParams(
            dimension_semantics=("arbitrary", "arbitrary"), vmem_limit_bytes=V7X_VMEM_LIMIT_BYTES),
        name="merge_branches",
    )(*ys, z_all, z_all, z_all, w_branch.astype(BF16))


def _residual_mm_kernel(a_ref, w_ref, x_ref, mod_ref, o_ref, *, gate_row):
    y = jnp.dot(a_ref[...], w_ref[...], preferred_element_type=F32)
    o_ref[...] = x_ref[...] + mod_ref[0, gate_row:gate_row + 1, :] * y


def residual_matmul(a, w, x, mod, *, gate_row, rows_per_cond=None, fixed_row=None, tm=512, tn=512):
    M, K = a.shape
    N = w.shape[1]
    assert M % tm == 0 and N % tn == 0
    cond = _cond_of_tile(rows_per_cond, tm, fixed_row)
    return pl.pallas_call(
        functools.partial(_residual_mm_kernel, gate_row=gate_row),
        grid=(N // tn, M // tm),
        in_specs=[pl.BlockSpec((tm, K), lambda n, m: (m, 0)),
                  pl.BlockSpec((K, tn), lambda n, m: (0, n)),
                  pl.BlockSpec((tm, tn), lambda n, m: (m, n)),
                  pl.BlockSpec((1, N_MOD, tn), lambda n, m: (cond(m), 0, n))],
        out_specs=pl.BlockSpec((tm, tn), lambda n, m: (m, n)),
        out_shape=jax.ShapeDtypeStruct((M, N), F32),
        compiler_params=pltpu.CompilerParams(
            dimension_semantics=("arbitrary", "arbitrary"), vmem_limit_bytes=V7X_VMEM_LIMIT_BYTES),
        name="residual_matmul",
    )(a, w.astype(BF16), x, mod)


def _head_sum(x2):
    row = lax.broadcasted_iota(jnp.int32, (LANES, LANES), 0) // NA_HEAD_DIM
    col = lax.broadcasted_iota(jnp.int32, (LANES, LANES), 1) // NA_HEAD_DIM
    ones_bd = jnp.where(row == col, 1.0, 0.0).astype(BF16)
    return _cumsum_rows_right(x2, ones_bd)


def _cumsum_rows_right(x, m):
    hi, lo = _split_bf16(x)
    return jnp.dot(hi, m, preferred_element_type=F32) + jnp.dot(lo, m, preferred_element_type=F32)


def _na_qkv_kernel(x_ref, g_ref, o_ref):
    x = x_ref[...]

    @pl.when(pl.program_id(1) < 2)
    def _():
        ms = _heads_sum(x * x) * (1.0 / NA_HEAD_DIM)
        o_ref[...] = (x * lax.rsqrt(ms + EPS) * g_ref[...]).astype(o_ref.dtype)

    @pl.when(pl.program_id(1) == 2)
    def _():
        o_ref[...] = x.astype(o_ref.dtype)


def na_qkv(z, gains, *, tm=256):
    M = z.shape[0]
    return pl.pallas_call(
        _na_qkv_kernel,
        grid=(M // tm, 3),
        in_specs=[pl.BlockSpec((tm, NA_W), lambda i, j: (i, j)),
                  pl.BlockSpec((1, NA_W), lambda i, j: (0, jnp.minimum(j, 1)))],
        out_specs=pl.BlockSpec((tm, NA_W), lambda i, j: (i, j)),
        out_shape=jax.ShapeDtypeStruct((M, 3 * NA_W), BF16),
        compiler_params=pltpu.CompilerParams(dimension_semantics=("arbitrary", "arbitrary")),
        name="na_qkv",
    )(z, gains)


def _softmax_pv(s_list, v_list):
    m = s_list[0].max(axis=-1, keepdims=True)
    for s in s_list[1:]:
        m = jnp.maximum(m, s.max(axis=-1, keepdims=True))
    den = 0.0
    acc = 0.0
    for s, v in zip(s_list, v_list):
        p = jnp.exp(s - m)
        den = den + p.sum(axis=-1, keepdims=True)
        acc = acc + jnp.dot(p.astype(BF16), v, preferred_element_type=F32)
    return acc / den


def _na_kernel(*refs, n_rows, win_r, rows_per_step):
    q_ref, k_ref, v_ref, kc_ref, vc_ref = refs[:5]
    bias_refs = refs[5:5 + rows_per_step]
    o_ref = refs[-1]
    kc = kc_ref[...]
    vc = vc_ref[...]
    lane = lax.broadcasted_iota(jnp.int32, (GRID_W, LANES), 1)
    chains = [(i, h) for i in range(rows_per_step) for h in range(2)]
    kw, vw = [], []
    for i in range(rows_per_step):
        r = pl.program_id(2) * rows_per_step + i
        rs = jnp.clip(r - win_r // 2, 0, n_rows - win_r)
        start = pl.multiple_of(rs * GRID_W, GRID_W)
        kw.append(k_ref[pl.ds(start, win_r * GRID_W), :])
        vw.append(v_ref[pl.ds(start, win_r * GRID_W), :])
    qh = []
    for i, h in chains:
        q = q_ref[i * GRID_W:(i + 1) * GRID_W, :]
        qh.append(jnp.where((lane // NA_HEAD_DIM) == h, q, jnp.zeros_like(q)))
    s_loc = [lax.dot_general(qh[c], kw[i], NT_DIMS, preferred_element_type=F32) + bias_refs[i][h, 0]
             for c, (i, h) in enumerate(chains)]
    s_ctx = [lax.dot_general(qh[c], kc, NT_DIMS, preferred_element_type=F32) for c in range(len(chains))]
    m = [jnp.maximum(s_loc[c].max(axis=-1, keepdims=True), s_ctx[c].max(axis=-1, keepdims=True))
         for c in range(len(chains))]
    p_loc = [jnp.exp(s_loc[c] - m[c]) for c in range(len(chains))]
    p_ctx = [jnp.exp(s_ctx[c] - m[c]) for c in range(len(chains))]
    den = [p_loc[c].sum(axis=-1, keepdims=True) + p_ctx[c].sum(axis=-1, keepdims=True)
           for c in range(len(chains))]
    acc = [jnp.dot(p_loc[c].astype(BF16), vw[i], preferred_element_type=F32)
           + jnp.dot(p_ctx[c].astype(BF16), vc, preferred_element_type=F32)
           for c, (i, h) in enumerate(chains)]
    out = [acc[c] / den[c] for c in range(len(chains))]
    for i in range(rows_per_step):
        o_ref[i * GRID_W:(i + 1) * GRID_W, :] = jnp.where(lane < NA_HEAD_DIM, out[2 * i], out[2 * i + 1])


def _ctx_attn_kernel(q_ref, k_ref, v_ref, o_ref):
    q = q_ref[...]
    k = k_ref[...]
    v = v_ref[...]
    lane = lax.broadcasted_iota(jnp.int32, q.shape, 1)
    outs = []
    for h in range(2):
        qh = jnp.where((lane // NA_HEAD_DIM) == h, q, jnp.zeros_like(q))
        s = lax.dot_general(qh, k, NT_DIMS, preferred_element_type=F32)
        outs.append(_softmax_pv([s], [v]))
    o_ref[...] = jnp.where(lane < NA_HEAD_DIM, outs[0], outs[1]).astype(o_ref.dtype)


def na_bias_table(rpb, n_rows, win_r):
    col = np.arange(GRID_W)
    col_start = np.clip(col - NA_WIN_C // 2, 0, GRID_W - NA_WIN_C)
    kcol = np.arange(GRID_W)
    valid = (kcol[None, :] >= col_start[:, None]) & (kcol[None, :] < col_start[:, None] + NA_WIN_C)
    dc = np.clip(kcol[None, :] - col[:, None] + NA_WIN_C - 1, 0, 2 * NA_WIN_C - 2)
    dr = np.arange(NA_WIN_R_MAX)[:, None] + np.arange(win_r)[None, :]
    dr = np.clip(dr, 0, 2 * NA_WIN_R_MAX - 2)
    t = rpb.astype(F32)[:, dr[:, None, :, None], dc[None, :, None, :]]
    t = jnp.where(valid[None, None, :, None, :], t, NEG_BIG)
    return t.reshape(rpb.shape[0], NA_WIN_R_MAX, GRID_W, win_r * GRID_W)


def na_attention(qkv, rpb, B, S, Lc, with_ctx):
    n_rows = S // GRID_W
    win_r = min(NA_WIN_R_MAX, n_rows)
    assert win_r == NA_WIN_R_MAX and (B * S) % Lc == 0
    HP = NA_HEADS // 2
    bias = na_bias_table(rpb, n_rows, win_r)
    ctx_blk0 = (B * S) // Lc

    def variant(r):
        return jnp.clip(r - win_r // 2, 0, n_rows - win_r) - r + NA_WIN_R_MAX - 1

    R = NA_ROWS_PER_STEP
    assert n_rows % R == 0
    steps = n_rows // R
    bias_specs = [pl.BlockSpec((2, 1, GRID_W, win_r * GRID_W),
                               lambda hp, b, r, i=i: (hp, variant(r * R + i), 0, 0)) for i in range(R)]
    y = pl.pallas_call(
        functools.partial(_na_kernel, n_rows=n_rows, win_r=win_r, rows_per_step=R),
        grid=(HP, B, steps),
        in_specs=[pl.BlockSpec((R * GRID_W, LANES), lambda hp, b, r: (b * steps + r, hp)),
                  pl.BlockSpec((S, LANES), lambda hp, b, r: (b, HP + hp)),
                  pl.BlockSpec((S, LANES), lambda hp, b, r: (b, 2 * HP + hp)),
                  pl.BlockSpec((Lc, LANES), lambda hp, b, r: (ctx_blk0 + b, HP + hp)),
                  pl.BlockSpec((Lc, LANES), lambda hp, b, r: (ctx_blk0 + b, 2 * HP + hp))] + bias_specs,
        out_specs=pl.BlockSpec((R * GRID_W, LANES), lambda hp, b, r: (b * steps + r, hp)),
        out_shape=jax.ShapeDtypeStruct((B * S, NA_W), F32),
        compiler_params=pltpu.CompilerParams(
            dimension_semantics=("arbitrary", "arbitrary", "arbitrary"),
            vmem_limit_bytes=V7X_VMEM_LIMIT_BYTES),
        name="na_attention",
    )(qkv, qkv, qkv, qkv, qkv, *([bias] * R))
    yc = None
    if with_ctx:
        yc = pl.pallas_call(
            _ctx_attn_kernel,
            grid=(HP, B),
            in_specs=[pl.BlockSpec((Lc, LANES), lambda hp, b: (ctx_blk0 + b, hp)),
                      pl.BlockSpec((Lc, LANES), lambda hp, b: (ctx_blk0 + b, HP + hp)),
                      pl.BlockSpec((Lc, LANES), lambda hp, b: (ctx_blk0 + b, 2 * HP + hp))],
            out_specs=pl.BlockSpec((Lc, LANES), lambda hp, b: (b, hp)),
            out_shape=jax.ShapeDtypeStruct((B * Lc, NA_W), F32),
            compiler_params=pltpu.CompilerParams(dimension_semantics=("arbitrary", "arbitrary")),
            name="ctx_attention",
        )(qkv, qkv, qkv)
    return y, yc


def na_mixer(z_all, B, S, Lc, q_norm, k_norm, rpb, with_ctx):
    assert (C_NAQ, C_NAK, C_NAV) == (0, NA_W, 2 * NA_W)
    scale = NA_HEAD_DIM ** -0.5
    gains = jnp.concatenate([jnp.tile(q_norm.astype(F32) * scale, NA_HEADS),
                             jnp.tile(k_norm.astype(F32), NA_HEADS)])[None]
    y, yc = na_attention(na_qkv(z_all, gains), rpb, B, S, Lc, with_ctx)
    return y, yc


def _log_sigmoid(x):
    return jnp.minimum(x, 0.0) - jnp.log(1.0 + jnp.exp(-jnp.abs(x)))


def _gla_kernel(*refs, reverse, rope, finish, n_chunks, nb):
    it = iter(refs)

    def take(count):
        return [next(it) for _ in range(count)]

    q_refs, k_refs, v_refs, gd_refs = take(nb), take(nb), take(nb), take(nb)
    w2_ref, gb_ref, s0_ref = take(3)
    if rope:
        cos_ref, sin_ref = take(2)
    if finish:
        yo_ref = next(it)
        r_refs = take(nb)
        g_ref = next(it)
    y_ref, sfin_ref, s_scr = take(3)

    C = GLA_CHUNK
    n = pl.program_id(1)
    chains = [(i, h) for i in range(nb) for h in range(GLA_HEADS)]

    @pl.when(n == 0)
    def _():
        s_scr[...] = s0_ref[...]

    def rows(refs_):
        return jnp.concatenate([r[...] for r in refs_], axis=0)

    ti = lax.broadcasted_iota(jnp.int32, (nb * C, nb * C), 0)
    tj = lax.broadcasted_iota(jnp.int32, (nb * C, nb * C), 1)
    same_seq = (ti // C) == (tj // C)
    tri = jnp.where(same_seq & ((tj >= ti) if reverse else (tj <= ti)), 1.0, 0.0).astype(BF16)
    ci = lax.broadcasted_iota(jnp.int32, (C, C), 0)
    cj = lax.broadcasted_iota(jnp.int32, (C, C), 1)
    incl = (cj >= ci) if reverse else (cj <= ci)

    pre = _mm(rows(gd_refs), w2_ref[...], NN_DIMS, 3) + gb_ref[...]
    la = _log_sigmoid(pre) * (1.0 / GLA_GATE_TAU)
    cs = _cumsum_rows(tri, la)
    ref_i = C // 2 - 1 if reverse else C // 2
    last_i = 0 if reverse else C - 1

    def per_seq_row(idx):
        return jnp.concatenate([jnp.broadcast_to(cs[i * C + idx:i * C + idx + 1, :], (C, GLA_QK))
                                for i in range(nb)], axis=0)

    b_ref = per_seq_row(ref_i)
    b_last = per_seq_row(last_i)

    q = rows(q_refs) * (GLA_DK ** -0.5)
    k = rows(k_refs)
    if rope:
        lane = lax.broadcasted_iota(jnp.int32, q.shape, 1)
        first = (lane % (GLA_DK // 2)) < GLA_DK // 4
        cos = jnp.concatenate([jnp.concatenate([cos_ref[...]] * GLA_HEADS, axis=1)] * nb, axis=0)
        sin = jnp.concatenate([jnp.concatenate([sin_ref[...]] * GLA_HEADS, axis=1)] * nb, axis=0)

        def rot(x):
            partner = jnp.where(first, pltpu.roll(x, GLA_QK - GLA_DK // 4, axis=1),
                                pltpu.roll(x, GLA_DK // 4, axis=1))
            return x * cos + partner * sin

        q = rot(q)
        k = rot(k)
    qi = q * jnp.exp(cs - b_ref)
    kj = k * jnp.exp(b_ref - cs)
    qe = q * jnp.exp(cs)
    ke = k * jnp.exp(b_last - cs)
    dec = jnp.exp(b_last)
    v_all = rows(v_refs)

    def hk(x, i, h):
        return x[i * C:(i + 1) * C, h * GLA_DK:(h + 1) * GLA_DK]

    def hv(x, i, h):
        return x[i * C:(i + 1) * C, h * GLA_DV:(h + 1) * GLA_DV]

    St = [s_scr[i, h] for i, h in chains]
    V = [hv(v_all, i, h) for i, h in chains]
    att = [jnp.where(incl, _mm(hk(qi, i, h), hk(kj, i, h), NT_DIMS), 0.0) for i, h in chains]
    y = [_mm(att[c], V[c], NN_DIMS) + _mm(hk(qe, i, h), St[c], NT_DIMS) for c, (i, h) in enumerate(chains)]
    kvt = [_mm(V[c], hk(ke, i, h), TN_DIMS) for c, (i, h) in enumerate(chains)]
    for c, (i, h) in enumerate(chains):
        s_scr[i, h] = St[c] * hk(dec, i, h)[0:1, :] + kvt[c]
    if finish:
        r = rows(r_refs)
        g = g_ref[...]
        for c, (i, h) in enumerate(chains):
            ys = y[c] + yo_ref[i, :, h * GLA_DV:(h + 1) * GLA_DV]
            ms = jnp.mean(ys * ys, axis=-1, keepdims=True)
            rh = hv(r, i, h)
            y[c] = ys * lax.rsqrt(ms + EPS) * g * (rh * jax.nn.sigmoid(rh))
    for i in range(nb):
        y_ref[i] = jnp.concatenate(y[i * GLA_HEADS:(i + 1) * GLA_HEADS], axis=1)

    @pl.when(n == n_chunks - 1)
    def _():
        sfin_ref[...] = s_scr[...]


def gla_rope_tables(seq_len):
    t = np.arange(seq_len)
    quarter = GLA_DK // 4
    inv = ROPE_BASE ** (-np.arange(0, 2 * quarter, 2, dtype=np.float64) / (2 * quarter))
    ar = (t // GRID_W)[:, None] * inv[None, :]
    ac = (t % GRID_W)[:, None] * inv[None, :]
    cos = np.concatenate([np.cos(ar), np.cos(ar), np.cos(ac), np.cos(ac)], axis=1)
    sin = np.concatenate([-np.sin(ar), np.sin(ar), -np.sin(ac), np.sin(ac)], axis=1)
    return jnp.asarray(cos, F32), jnp.asarray(sin, F32)


def gla_scan(z, w2pad, gate_b, s0, *, batch, seq_len, row0, reverse, rope, finish_with=None):
    N = seq_len // GLA_CHUNK
    blk0 = row0 // GLA_CHUNK
    off_q, off_k, off_v, off_r, off_gd = C_GQ, C_GK, C_GV, C_GR, C_GGD
    assert off_q % GLA_QK == 0 and off_k % GLA_QK == 0 and off_v % GLA_V == 0 and off_r % GLA_V == 0
    assert off_gd % LANES == 0

    def tb(n):
        return N - 1 - n if reverse else n

    def rb(b, n):
        return blk0 + b * N + tb(n)

    nb = GLA_BATCH_PER_STEP if batch % GLA_BATCH_PER_STEP == 0 else 1

    def tok_specs(width, col_off):
        return [pl.BlockSpec((GLA_CHUNK, width), lambda bb, n, i=i: (rb(bb * nb + i, n), col_off // width))
                for i in range(nb)]

    st_spec = pl.BlockSpec((nb, GLA_HEADS, GLA_DV, GLA_DK), lambda bb, n: (bb, 0, 0, 0))
    y_spec = pl.BlockSpec((nb, GLA_CHUNK, GLA_V), lambda bb, n: (bb, tb(n), 0))
    in_specs = (tok_specs(GLA_QK, off_q) + tok_specs(GLA_QK, off_k) + tok_specs(GLA_V, off_v)
                + tok_specs(LANES, off_gd)
                + [pl.BlockSpec((LANES, GLA_QK), lambda bb, n: (0, 0)),
                   pl.BlockSpec((1, GLA_QK), lambda bb, n: (0, 0)),
                   st_spec])
    args = [z] * (4 * nb) + [w2pad, gate_b, s0]
    if rope:
        cos, sin = gla_rope_tables(seq_len)
        in_specs += [pl.BlockSpec((GLA_CHUNK, GLA_DK), lambda bb, n: (tb(n), 0))] * 2
        args += [cos, sin]
    if finish_with is not None:
        y_other, norm_g = finish_with
        in_specs += [y_spec] + tok_specs(GLA_V, off_r) + [pl.BlockSpec((1, GLA_DV), lambda bb, n: (0, 0))]
        args += [y_other.reshape(batch, seq_len, GLA_V)] + [z] * nb + [norm_g]
    y, s_fin = pl.pallas_call(
        functools.partial(_gla_kernel, reverse=reverse, rope=rope, finish=finish_with is not None,
                          n_chunks=N, nb=nb),
        grid=(batch // nb, N),
        in_specs=in_specs,
        out_specs=[y_spec, st_spec],
        out_shape=[jax.ShapeDtypeStruct((batch, seq_len, GLA_V), F32),
                   jax.ShapeDtypeStruct((batch, GLA_HEADS, GLA_DV, GLA_DK), F32)],
        scratch_shapes=[pltpu.VMEM((nb, GLA_HEADS, GLA_DV, GLA_DK), F32)],
        compiler_params=pltpu.CompilerParams(
            dimension_semantics=("arbitrary", "arbitrary"),
            vmem_limit_bytes=V7X_VMEM_LIMIT_BYTES),
        name="gla_scan_rev" if reverse else "gla_scan",
    )(*args)
    return y.reshape(batch * seq_len, GLA_V), s_fin


def gla_mixer(z_all, B, S, Lc, gate_w2, gate_b, norm_g, with_ctx):
    s0 = jnp.zeros((B, GLA_HEADS, GLA_DV, GLA_DK), F32)
    g = norm_g.astype(F32)[None]
    w2 = [jnp.zeros((LANES, GLA_QK), F32).at[d * GLA_GATE_RANK:(d + 1) * GLA_GATE_RANK].set(gate_w2[d])
          for d in range(2)]
    gb = [gate_b[d].astype(F32)[None] for d in range(2)]
    ctx = dict(batch=B, seq_len=Lc, row0=B * S, rope=False)
    lat = dict(batch=B, seq_len=S, row0=0, rope=True)
    yc_f, sc_f = gla_scan(z_all, w2[0], gb[0], s0, reverse=False, **ctx)
    yc, sc_b = gla_scan(z_all, w2[1], gb[1], s0, reverse=True, finish_with=(yc_f, g), **ctx)
    yl_f, _ = gla_scan(z_all, w2[0], gb[0], sc_f, reverse=False, **lat)
    yl, _ = gla_scan(z_all, w2[1], gb[1], sc_b, reverse=True, finish_with=(yl_f, g), **lat)
    return yl, (yc if with_ctx else None)


RW_PREP_TM = 256
RW_HALO = 8
RW_N_PIECES = 5


def _heads_sum(x):
    return jnp.concatenate([_head_sum(x[:, j * LANES:(j + 1) * LANES]) for j in range(x.shape[1] // LANES)],
                           axis=1)


def _rw_prep_kernel(*refs, tm, n_latent_tiles, tiles_per_seq):
    P = RW_N_PIECES
    z_refs, prev_refs, next_refs, mu_refs = refs[0:P], refs[P:2 * P], refs[2 * P:3 * P], refs[3 * P:4 * P]
    kk_ref, ka_ref, rk_ref, w0_ref, a0_ref, w2_ref, a2_ref, g2_ref = refs[4 * P:4 * P + 8]
    (r_o, v_o, kap_o, lw0_o, kd0_o, b0_o, lw1_o, kd1_o, b1_o, bonus_o, g_o) = refs[4 * P + 8:]
    i = pl.program_id(0)
    latent = i < n_latent_tiles
    first = jnp.logical_or(jnp.logical_not(latent), i % tiles_per_seq == 0)
    last = jnp.logical_or(jnp.logical_not(latent), i % tiles_per_seq == tiles_per_seq - 1)

    def shifted(j):
        z = z_refs[j][...]
        row = lax.broadcasted_iota(jnp.int32, z.shape, 0)
        prev_row = jnp.where(first, 0.0, prev_refs[j][RW_HALO - 1:RW_HALO, :])
        next_row = jnp.where(last, 0.0, next_refs[j][0:1, :])
        prev = jnp.where(row == 0, prev_row, pltpu.roll(z, 1, axis=0))
        nxt = jnp.where(row == tm - 1, next_row, pltpu.roll(z, tm - 1, axis=0))
        return z + mu_refs[j][...] * (0.5 * (prev + nxt) - z)

    r, k, v, lr, gd = (shifted(j) for j in range(P))
    kkr = k * kk_ref[...]
    kap = kkr * lax.rsqrt(_heads_sum(kkr * kkr) + EPS)
    r_o[...] = r
    v_o[...] = v
    kap_o[...] = kap
    bonus_o[...] = _heads_sum(r * k * rk_ref[...]) * v
    g_o[...] = _mm(jax.nn.sigmoid(gd), g2_ref[...], NN_DIMS, 3)
    ka = ka_ref[...]
    for d, (lw_o, kd_o, b_o) in enumerate(((lw0_o, kd0_o, b0_o), (lw1_o, kd1_o, b1_o))):
        wd = lr[:, d * LANES:(d + 1) * LANES]
        ad = lr[:, (2 + d) * LANES:(3 + d) * LANES]
        w_log = _log_sigmoid(w0_ref[d:d + 1, :] + _mm(jnp.tanh(wd), w2_ref[d], NN_DIMS, 3)) - 0.5
        lw_o[...] = -jnp.exp(w_log)
        a = jax.nn.sigmoid(a0_ref[d:d + 1, :] + _mm(ad, a2_ref[d], NN_DIMS, 3))
        kd_o[...] = k * (1.0 + (a - 1.0) * ka)
        b_o[...] = kap * a


def rwkv_prep(z_all, n_latent_rows, seq_len, mu_p, k_k, k_a, r_k, w0, a0, w2p, a2p, g2):
    M = z_all.shape[0]
    tm = RW_PREP_TM
    assert M % tm == 0 and n_latent_rows % tm == 0 and seq_len % tm == 0
    widths = (RW_W, RW_W, RW_W, RW_LR_W, RW_GATE_RANK)
    offs = (C_RR, C_RK, C_RV, C_RLR, C_RGD)
    n_halo_blocks = M // RW_HALO
    z_specs, prev_specs, next_specs, mu_specs = [], [], [], []
    for w, o in zip(widths, offs):
        assert o % w == 0
        cb = o // w
        z_specs.append(pl.BlockSpec((tm, w), lambda i, cb=cb: (i, cb)))
        prev_specs.append(pl.BlockSpec(
            (RW_HALO, w), lambda i, cb=cb: (jnp.maximum(i * (tm // RW_HALO) - 1, 0), cb)))
        next_specs.append(pl.BlockSpec(
            (RW_HALO, w), lambda i, cb=cb: (jnp.minimum((i + 1) * (tm // RW_HALO), n_halo_blocks - 1), cb)))
        mu_specs.append(pl.BlockSpec((1, w), lambda i: (0, 0)))

    def full(shape):
        return pl.BlockSpec(shape, lambda i: (0,) * len(shape))

    par_specs = [full((1, RW_W))] * 3 + [full((2, RW_W))] * 2 + [full((2, LANES, RW_W))] * 2 \
        + [full((RW_GATE_RANK, RW_W))]
    out_spec = pl.BlockSpec((tm, RW_W), lambda i: (i, 0))
    return pl.pallas_call(
        functools.partial(_rw_prep_kernel, tm=tm, n_latent_tiles=n_latent_rows // tm,
                          tiles_per_seq=seq_len // tm),
        grid=(M // tm,),
        in_specs=z_specs + prev_specs + next_specs + mu_specs + par_specs,
        out_specs=[out_spec] * 11,
        out_shape=[jax.ShapeDtypeStruct((M, RW_W), F32)] * 11,
        compiler_params=pltpu.CompilerParams(
            dimension_semantics=("arbitrary",), vmem_limit_bytes=V7X_VMEM_LIMIT_BYTES),
        name="rwkv_prep",
    )(*([z_all] * 15), *mu_p, k_k, k_a, r_k, w0, a0, w2p, a2p, g2)


def _rw_chunk_kernel(*refs, reverse, n_par, n_chunks, passes, finish, nb):
    it = iter(refs)

    def side_by_side():
        return jnp.concatenate([next(it)[...] for _ in range(nb)], axis=1)

    r_in, v_all, kap_in, lw_all, kd_all, bet_all = (side_by_side() for _ in range(6))
    s0_ref = next(it)
    if finish:
        yo_ref = next(it)
        bonus, gate = side_by_side(), side_by_side()
        gamma = jnp.concatenate([next(it)[...]] * nb, axis=1)
        beta = jnp.concatenate([next(it)[...]] * nb, axis=1)
    y_ref, sfin_ref, s_scr = next(it), next(it), next(it)

    C = RW_CHUNK
    n = pl.program_id(2)
    P = range(nb * n_par)

    @pl.when(n == 0)
    def _():
        s_scr[...] = s0_ref[:, 0]

    row = lax.broadcasted_iota(jnp.int32, (C, LANES), 0)
    lane = lax.broadcasted_iota(jnp.int32, (C, LANES), 1)
    col = lane % RW_HEAD
    head0 = lane < RW_HEAD
    strict = (col > row) if reverse else (col < row)
    incl = (col >= row) if reverse else (col <= row)
    ti = lax.broadcasted_iota(jnp.int32, (C, C), 0)
    tj = lax.broadcasted_iota(jnp.int32, (C, C), 1)
    tri = jnp.where((tj >= ti) if reverse else (tj <= ti), 1.0, 0.0).astype(BF16)
    eye = jnp.where(row == col, 1.0, 0.0)
    brow = lax.broadcasted_iota(jnp.int32, (LANES, LANES), 0) // RW_HEAD
    bcol = lax.broadcasted_iota(jnp.int32, (LANES, LANES), 1) // RW_HEAD
    same_head = brow == bcol

    def sl(x, p):
        return x[:, p * LANES:(p + 1) * LANES]

    def bd(x):
        return jnp.concatenate([jnp.where(head0, x, 0.0), jnp.where(head0, 0.0, x)], axis=0)

    def pp(a_pair, b_pair):
        return _mm(a_pair, bd(b_pair), NN_DIMS, passes)

    cs_all = _cumsum_rows(tri, lw_all)
    tot_all = cs_all[0:1, :] if reverse else cs_all[C - 1:C, :]
    inv_all = jnp.exp(-cs_all)
    fin_all = jnp.exp(tot_all - cs_all)
    kap_t = kap_in * jnp.exp(cs_all - lw_all)
    r_t = r_in * jnp.exp(cs_all)
    k_t = kd_all * inv_all
    b_t = bet_all * inv_all
    k_h = kd_all * fin_all
    b_h = bet_all * fin_all
    dec = jnp.exp(tot_all)

    S = [s_scr[p // n_par, p % n_par] for p in P]
    V = [sl(v_all, p) for p in P]
    lhs = [jnp.concatenate([sl(kap_t, p), sl(r_t, p)], axis=0) for p in P]
    a_k = [_mm(lhs[p], bd(sl(k_t, p)), NT_DIMS, passes) for p in P]
    a_b = [_mm(lhs[p], bd(sl(b_t, p)), NT_DIMS, passes) for p in P]
    s0_both = [_mm(lhs[p], S[p], NT_DIMS, passes) for p in P]
    lmat = [jnp.where(strict, a_b[p][:C], 0.0) for p in P]
    power = [-lmat[p] for p in P]
    t_inv = [eye + power[p] for p in P]
    s = 2
    while s < C:
        power = [pp(power[p], power[p]) for p in P]
        t_inv = [t_inv[p] + pp(t_inv[p], power[p]) for p in P]
        s *= 2
    x = [s0_both[p][:C] + pp(jnp.where(strict, a_k[p][:C], 0.0), V[p]) for p in P]
    u = [pp(t_inv[p], x[p]) for p in P]
    y = [s0_both[p][C:] + pp(jnp.where(incl, a_k[p][C:], 0.0), V[p])
         - pp(jnp.where(incl, a_b[p][C:], 0.0), u[p]) for p in P]
    upd = [_mm(V[p], sl(k_h, p), TN_DIMS, passes) - _mm(u[p], sl(b_h, p), TN_DIMS, passes) for p in P]
    for p in P:
        s_scr[p // n_par, p % n_par] = S[p] * sl(dec, p) + jnp.where(same_head, upd[p], 0.0)
    width = n_par * LANES
    if finish:
        yo = jnp.concatenate([yo_ref[i] for i in range(nb)], axis=1)
        ys = jnp.concatenate(y, axis=1) + yo
        cen = ys - _heads_sum(ys) * (1.0 / RW_HEAD)
        var = _heads_sum(cen * cen) * (1.0 / RW_HEAD)
        out = (cen * lax.rsqrt(var + RW_GN_EPS) * gamma + beta + bonus) * gate
    else:
        out = jnp.concatenate(y, axis=1)
    for i in range(nb):
        y_ref[i] = out[:, i * width:(i + 1) * width]

    @pl.when(n == n_chunks - 1)
    def _():
        sfin_ref[:, 0] = s_scr[...]


def rwkv_chunk_scan(r, v, kap, lw, kd, bet, s0, *, batch, seq_len, row0, reverse, finish_with=None):
    n_par = RW_PAIRS_PER_STEP
    nb = RW_BATCH_PER_STEP if batch % RW_BATCH_PER_STEP == 0 else 1
    W = r.shape[1]
    G = W // (n_par * LANES)
    N = seq_len // RW_CHUNK
    blk0 = row0 // RW_CHUNK

    def tb(n):
        return N - 1 - n if reverse else n

    def seq_in():
        return [pl.BlockSpec((RW_CHUNK, n_par * LANES),
                             lambda bb, g, n, i=i: (blk0 + (bb * nb + i) * N + tb(n), g)) for i in range(nb)]

    y_spec = pl.BlockSpec((nb, RW_CHUNK, n_par * LANES), lambda bb, g, n: (bb, tb(n), g))
    st_spec = pl.BlockSpec((nb, 1, n_par, LANES, LANES), lambda bb, g, n: (bb, g, 0, 0, 0))
    in_specs = seq_in() * 6 + [st_spec]
    args = [a for a in (r, v, kap, lw, kd, bet) for _ in range(nb)] + [s0]
    if finish_with is not None:
        y_other, bonus, gate, gamma, beta = finish_with
        vec_spec = pl.BlockSpec((1, n_par * LANES), lambda bb, g, n: (0, g))
        in_specs += [y_spec] + seq_in() * 2 + [vec_spec, vec_spec]
        args += [y_other.reshape(batch, seq_len, W)] + [bonus] * nb + [gate] * nb + [gamma, beta]
    y, s_fin = pl.pallas_call(
        functools.partial(_rw_chunk_kernel, reverse=reverse, n_par=n_par, n_chunks=N, passes=RW_PASSES,
                          finish=finish_with is not None, nb=nb),
        grid=(batch // nb, G, N),
        in_specs=in_specs,
        out_specs=[y_spec, st_spec],
        out_shape=[jax.ShapeDtypeStruct((batch, seq_len, W), F32),
                   jax.ShapeDtypeStruct((batch, G, n_par, LANES, LANES), F32)],
        scratch_shapes=[pltpu.VMEM((nb, n_par, LANES, LANES), F32)],
        compiler_params=pltpu.CompilerParams(
            dimension_semantics=("arbitrary", "arbitrary", "arbitrary"),
            vmem_limit_bytes=V7X_VMEM_LIMIT_BYTES),
        name="rwkv_chunk_scan_rev" if reverse else "rwkv_chunk_scan",
    )(*args)
    return y.reshape(batch * seq_len, W), s_fin


def _pad_rows(w, rows):
    return jnp.pad(w.astype(F32), ((0, 0), (0, rows - w.shape[1]), (0, 0)))


def rwkv_mixer(z_all, B, S, Lc, mu, w0, w2, a0, a2, g2, k_k, k_a, r_k, gn_g, gn_b, with_ctx):
    assert Lc == RW_PREP_TM
    mu = mu.astype(F32)
    o = np.cumsum((0,) + RW_SPLITS)
    lr_pad = jnp.zeros((LANES - RW_DECAY_RANK,), F32)
    mu_lr = jnp.concatenate([mu[o[3]:o[3] + RW_DECAY_RANK], lr_pad, mu[o[3] + RW_DECAY_RANK:o[4]], lr_pad,
                             mu[o[4]:o[4] + RW_A_RANK], lr_pad, mu[o[4] + RW_A_RANK:o[5]], lr_pad])
    mu_p = [mu[o[0]:o[1]][None], mu[o[1]:o[2]][None], mu[o[2]:o[3]][None], mu_lr[None], mu[o[5]:o[6]][None]]
    row = lambda t: t.astype(F32).reshape(1, RW_W)
    prep = rwkv_prep(z_all, B * S, S, mu_p, row(k_k), row(k_a), row(r_k), w0.astype(F32), a0.astype(F32),
                     _pad_rows(w2, LANES), _pad_rows(a2, LANES), g2.astype(F32))
    r, v, kap, lw0, kd0, b0, lw1, kd1, b1, bonus, gate = prep
    fin = (bonus, gate, row(gn_g), row(gn_b))
    s0 = jnp.zeros((B, RW_W // (RW_PAIRS_PER_STEP * LANES), RW_PAIRS_PER_STEP, LANES, LANES), F32)
    ctx = dict(batch=B, seq_len=Lc, row0=B * S)
    lat = dict(batch=B, seq_len=S, row0=0)
    yc_f, sc_f = rwkv_chunk_scan(r, v, kap, lw0, kd0, b0, s0, reverse=False, **ctx)
    yc, sc_b = rwkv_chunk_scan(r, v, kap, lw1, kd1, b1, s0, reverse=True, finish_with=(yc_f,) + fin, **ctx)
    yl_f, _ = rwkv_chunk_scan(r, v, kap, lw0, kd0, b0, sc_f, reverse=False, **lat)
    yl, _ = rwkv_chunk_scan(r, v, kap, lw1, kd1, b1, sc_b, reverse=True, finish_with=(yl_f,) + fin, **lat)
    return yl, (yc if with_ctx else None)


def relayout_w_in(w):
    o = IN_OFFSETS
    rw = o[8] + np.cumsum((0,) + RW_SPLITS)
    zeros = lambda n: jnp.zeros((w.shape[0], n), w.dtype)
    seg = lambda lo, n: w[:, lo:lo + n]
    lr_pad = LANES - RW_DECAY_RANK
    pieces = [
        seg(o[0], 3 * NA_W),
        seg(o[5], GLA_V), seg(o[6], GLA_V),
        seg(rw[0], 3 * RW_W),
        seg(o[9], N_BRANCH * D_MODEL),
        seg(o[3], GLA_QK), seg(o[4], GLA_QK),
        seg(rw[3], RW_DECAY_RANK), zeros(lr_pad), seg(rw[3] + RW_DECAY_RANK, RW_DECAY_RANK), zeros(lr_pad),
        seg(rw[4], RW_A_RANK), zeros(lr_pad), seg(rw[4] + RW_A_RANK, RW_A_RANK), zeros(lr_pad),
        seg(rw[5], RW_GATE_RANK),
        seg(o[7], 2 * GLA_GATE_RANK),
    ]
    used = sum(x.shape[1] for x in pieces)
    assert used == C_GGD + 2 * GLA_GATE_RANK
    return jnp.concatenate(pieces + [zeros(Z_W - used)], axis=1).astype(BF16)


def token_mixers(x, ctx, mod, p, B, S, Lc, with_ctx):
    h_all = prenorm(x, ctx, p['norm1_g'], mod, seq_len=S)
    z_all = matmul(h_all, relayout_w_in(p['w_in']), tm=1024, tn=1024)
    y_na, yc_na = na_mixer(z_all, B, S, Lc, p['na_q_norm'], p['na_k_norm'], p['na_rpb'], with_ctx)
    y_gla, yc_gla = gla_mixer(z_all, B, S, Lc, p['gla_gate_w2'], p['gla_gate_b'], p['gla_norm_g'], with_ctx)
    y_rw, yc_rw = rwkv_mixer(z_all, B, S, Lc, p['rw_mu'], p['rw_w0'], p['rw_w2'], p['rw_a0'], p['rw_a2'],
                             p['rw_g2'], p['rw_k_k'], p['rw_k_a'], p['rw_r_k'], p['rw_gn_g'],
                             p['rw_gn_b'], with_ctx)
    merged = merge_branches((y_na, y_gla, y_rw), z_all, 0, p['w_branch'])
    x = residual_matmul(merged, p['w_out'], x, mod, gate_row=2, rows_per_cond=S)
    if with_ctx:
        merged_c = merge_branches((yc_na, yc_gla, yc_rw), z_all, B * S, p['w_branch'])
        ctx = residual_matmul(merged_c, p['w_out'], ctx, mod, gate_row=2, fixed_row=B)
    return x, ctx


def moe_sublayer(x, mod, p, *, rows_per_cond=None, fixed_row=None):
    h2, comb_t = prenorm_router(x, p['norm2_g'], mod, p['router_w_t'], p['router_bias'],
                                rows_per_cond=rows_per_cond, fixed_row=fixed_row)
    return moe_experts(h2, comb_t.T, p['moe_w_gate'], p['moe_w_up'], p['moe_w_down'], x, mod,
                       rows_per_cond=rows_per_cond, fixed_row=fixed_row)


def kernel(x, c, ctx, c_ctx, ada_w, ada_b, norm1_g, norm2_g, w_in, na_q_norm, na_k_norm, na_rpb,
           gla_gate_w2, gla_gate_b, gla_norm_g, rw_mu, rw_w0, rw_w2, rw_a0, rw_a2, rw_g2,
           rw_k_k, rw_k_a, rw_r_k, rw_gn_g, rw_gn_b, w_branch, w_out, router_w, router_bias,
           moe_w_gate, moe_w_up, moe_w_down):
    B, S, D = x.shape
    Lc = ctx.shape[1]
    assert B + 1 <= N_COND and S % 512 == 0 and (B * Lc) % 512 == 0 and Lc % ROW_TM == 0
    c_rows = jnp.concatenate([c, c_ctx[None], jnp.zeros((N_COND - B - 1, D), c.dtype)], axis=0).astype(F32)
    mods = ada_modulation(c_rows, ada_w, ada_b).reshape(DEPTH, N_COND, N_MOD, D)
    x = x.reshape(B * S, D)
    ctx = ctx.reshape(B * Lc, D)
    router_w_t = router_w.astype(F32).T
    router_b = router_bias.astype(F32).reshape(N_EXPERTS, 1)
    for l in range(DEPTH):
        with_ctx = l < DEPTH - 1
        p = {
            'w_in': w_in[l], 'na_q_norm': na_q_norm[l], 'na_k_norm': na_k_norm[l], 'na_rpb': na_rpb[l],
            'gla_gate_w2': gla_gate_w2[l], 'gla_gate_b': gla_gate_b[l], 'gla_norm_g': gla_norm_g[l],
            'rw_mu': rw_mu[l], 'rw_w0': rw_w0[l], 'rw_w2': rw_w2[l], 'rw_a0': rw_a0[l], 'rw_a2': rw_a2[l],
            'rw_g2': rw_g2[l], 'rw_k_k': rw_k_k[l], 'rw_k_a': rw_k_a[l], 'rw_r_k': rw_r_k[l],
            'rw_gn_g': rw_gn_g[l], 'rw_gn_b': rw_gn_b[l], 'w_branch': w_branch[l], 'w_out': w_out[l],
            'norm1_g': norm1_g[l].astype(F32)[None], 'norm2_g': norm2_g[l].astype(F32)[None],
            'router_w_t': router_w_t, 'router_bias': router_b,
            'moe_w_gate': moe_w_gate[l].astype(BF16), 'moe_w_up': moe_w_up[l].astype(BF16),
            'moe_w_down': moe_w_down[l].astype(BF16),
        }
        x, ctx = token_mixers(x, ctx, mods[l], p, B, S, Lc, with_ctx)
        x = moe_sublayer(x, mods[l], p, rows_per_cond=S)
        if with_ctx:
            ctx = moe_sublayer(ctx, mods[l], p, fixed_row=B)
    return x.reshape(B, S, D)
```

```python
import functools

import jax
import jax.numpy as jnp
from jax import lax
import numpy as np
from jax.experimental import pallas as pl
from jax.experimental.pallas import tpu as pltpu

D_MODEL = 2048
DEPTH = 2
GRID_W = 64

NA_HEADS = 16
NA_HEAD_DIM = 64
NA_WIN_R_MAX = 8
NA_WIN_C = 16
NA_W = NA_HEADS * NA_HEAD_DIM
NA_ROWS_PER_STEP = 8

GLA_HEADS = 4
GLA_DK = 128
GLA_DV = 256
GLA_QK = GLA_HEADS * GLA_DK
GLA_V = GLA_HEADS * GLA_DV
GLA_GATE_RANK = 16
GLA_GATE_TAU = 16.0
GLA_CHUNK = 64
GLA_BATCH_PER_STEP = 4

RW_HEADS = 16
RW_HEAD = 64
RW_W = RW_HEADS * RW_HEAD
RW_DECAY_RANK = 96
RW_A_RANK = 96
RW_GATE_RANK = 256
RW_SPLITS = (RW_W, RW_W, RW_W, 2 * RW_DECAY_RANK, 2 * RW_A_RANK, RW_GATE_RANK)
RW_IN = sum(RW_SPLITS)
RW_GN_EPS = 64e-5
RW_CHUNK = 64
RW_PASSES = 1
RW_PAIRS_PER_STEP = 8
RW_BATCH_PER_STEP = 4

N_BRANCH = 3
BRANCH_W = 1024
IN_SPLITS = (NA_W, NA_W, NA_W, GLA_QK, GLA_QK, GLA_V, GLA_V, 2 * GLA_GATE_RANK, RW_IN, N_BRANCH * D_MODEL)
D_IN = sum(IN_SPLITS)
IN_OFFSETS = tuple(int(o) for o in np.cumsum((0,) + IN_SPLITS[:-1]))

LANES = 128
RW_LR_W = 4 * LANES
C_NAQ, C_NAK, C_NAV = 0, 1024, 2048
C_GV, C_GR = 3072, 4096
C_RR, C_RK, C_RV = 5120, 6144, 7168
C_GATE = 8192
C_GQ, C_GK = 14336, 14848
C_RLR, C_RGD, C_GGD = 15360, 15872, 16128
Z_W = 16384

N_EXPERTS = 16
N_GROUPS = 4
EXPERTS_PER_GROUP = N_EXPERTS // N_GROUPS
TOP_K = 2
D_EXPERT = 512

ROPE_BASE = 10000.0
EPS = 1e-6
F32 = jnp.float32
BF16 = jnp.bfloat16

V7X_VMEM_LIMIT_BYTES = 48 * 1024 * 1024

NEG_BIG = -1e30
NT_DIMS = (((1,), (1,)), ((), ()))
TN_DIMS = (((0,), (0,)), ((), ()))
NN_DIMS = (((1,), (0,)), ((), ()))


def _split_bf16(x):
    hi = x.astype(BF16)
    lo = (x - hi.astype(F32)).astype(BF16)
    return hi, lo


def _mm(a, b, dims=NN_DIMS, passes=1):
    if passes == 1:
        return lax.dot_general(a.astype(BF16), b.astype(BF16), dims, preferred_element_type=F32)
    a_hi, a_lo = _split_bf16(a)
    b_hi, b_lo = _split_bf16(b)
    out = lax.dot_general(a_hi, b_hi, dims, preferred_element_type=F32)
    out = out + lax.dot_general(a_hi, b_lo, dims, preferred_element_type=F32)
    return out + lax.dot_general(a_lo, b_hi, dims, preferred_element_type=F32)


def _cumsum_rows(tri, x):
    hi, lo = _split_bf16(x)
    return jnp.dot(tri, hi, preferred_element_type=F32) + jnp.dot(tri, lo, preferred_element_type=F32)


def _mm_kernel(a_ref, b_ref, o_ref):
    o_ref[...] = jnp.dot(a_ref[...], b_ref[...], preferred_element_type=F32).astype(o_ref.dtype)


def matmul(a, b, *, tm=512, tn=512, out_dtype=F32):
    M, K = a.shape
    _, N = b.shape
    assert M % tm == 0 and N % tn == 0, (M, N, tm, tn)
    return pl.pallas_call(
        _mm_kernel,
        grid=(N // tn, M // tm),
        in_specs=[pl.BlockSpec((tm, K), lambda n, m: (m, 0)),
                  pl.BlockSpec((K, tn), lambda n, m: (0, n))],
        out_specs=pl.BlockSpec((tm, tn), lambda n, m: (m, n)),
        out_shape=jax.ShapeDtypeStruct((M, N), out_dtype),
        compiler_params=pltpu.CompilerParams(
            dimension_semantics=("arbitrary", "arbitrary"),
            vmem_limit_bytes=V7X_VMEM_LIMIT_BYTES),
        name="matmul",
    )(a.astype(BF16), b.astype(BF16))


def _moe_kernel(h_ref, comb_ref, wg_ref, wu_ref, wd_ref, x_ref, mod_ref, o_ref, acc_ref):
    e = pl.program_id(1)

    @pl.when(e == 0)
    def _():
        acc_ref[...] = jnp.zeros_like(acc_ref)

    h = h_ref[...]
    g = jnp.dot(h, wg_ref[0], preferred_element_type=F32)
    u = jnp.dot(h, wu_ref[0], preferred_element_type=F32)
    comb = comb_ref[...]
    lane = lax.broadcasted_iota(jnp.int32, comb.shape, 1)
    ce = jnp.sum(jnp.where(lane == e, comb, 0.0), axis=1, keepdims=True)
    act = (g * jax.nn.sigmoid(g)) * u * ce
    acc_ref[...] += jnp.dot(act.astype(BF16), wd_ref[0], preferred_element_type=F32)

    @pl.when(e == N_EXPERTS - 1)
    def _():
        o_ref[...] = x_ref[...] + mod_ref[0, N_MOD - 1:N_MOD, :] * acc_ref[...]


def moe_experts(h, comb, w_gate, w_up, w_down, x, mod, *, rows_per_cond=None, fixed_row=None, tm=512):
    M, D = h.shape
    assert M % tm == 0
    cond = _cond_of_tile(rows_per_cond, tm, fixed_row)
    return pl.pallas_call(
        _moe_kernel,
        grid=(M // tm, N_EXPERTS),
        in_specs=[pl.BlockSpec((tm, D), lambda m, e: (m, 0)),
                  pl.BlockSpec((tm, N_EXPERTS), lambda m, e: (m, 0)),
                  pl.BlockSpec((1, D, D_EXPERT), lambda m, e: (e, 0, 0)),
                  pl.BlockSpec((1, D, D_EXPERT), lambda m, e: (e, 0, 0)),
                  pl.BlockSpec((1, D_EXPERT, D), lambda m, e: (e, 0, 0)),
                  pl.BlockSpec((tm, D), lambda m, e: (m, 0)),
                  pl.BlockSpec((1, N_MOD, D), lambda m, e: (cond(m), 0, 0))],
        out_specs=pl.BlockSpec((tm, D), lambda m, e: (m, 0)),
        out_shape=jax.ShapeDtypeStruct((M, D), F32),
        scratch_shapes=[pltpu.VMEM((tm, D), F32)],
        compiler_params=pltpu.CompilerParams(
            dimension_semantics=("arbitrary", "arbitrary"),
            vmem_limit_bytes=V7X_VMEM_LIMIT_BYTES),
        name="moe_experts",
    )(h, comb, w_gate, w_up, w_down, x, mod)


N_COND = 8
N_MOD = 6
ROW_TM = 256


def _ada_kernel(c_ref, w_ref, b_ref, o_ref):
    c = c_ref[...]
    o_ref[0] = _mm(c * jax.nn.sigmoid(c), w_ref[0], NN_DIMS, 3) + b_ref[0]


def ada_modulation(c_rows, ada_w, ada_b, *, tn=1024):
    L, D, N = ada_w.shape
    return pl.pallas_call(
        _ada_kernel,
        grid=(L, N // tn),
        in_specs=[pl.BlockSpec((N_COND, D), lambda l, n: (0, 0)),
                  pl.BlockSpec((1, D, tn), lambda l, n: (l, 0, n)),
                  pl.BlockSpec((1, 1, tn), lambda l, n: (l, 0, n))],
        out_specs=pl.BlockSpec((1, N_COND, tn), lambda l, n: (l, 0, n)),
        out_shape=jax.ShapeDtypeStruct((L, N_COND, N), F32),
        compiler_params=pltpu.CompilerParams(
            dimension_semantics=("arbitrary", "arbitrary"), vmem_limit_bytes=V7X_VMEM_LIMIT_BYTES),
        name="ada_modulation",
    )(c_rows, ada_w, ada_b.reshape(L, 1, N))


def _norm_mod(x, g, shift, scale):
    y = x * lax.rsqrt(jnp.mean(x * x, axis=-1, keepdims=True) + EPS)
    return y * g * (1.0 + scale) + shift


def _prenorm_kernel(x_ref, c_ref, g_ref, mod_ref, o_ref, *, n_latent_tiles):
    def emit(src_ref):
        o_ref[...] = _norm_mod(src_ref[...], g_ref[...], mod_ref[0, 0:1, :], mod_ref[0, 1:2, :]).astype(o_ref.dtype)

    @pl.when(pl.program_id(0) < n_latent_tiles)
    def _():
        emit(x_ref)

    @pl.when(pl.program_id(0) >= n_latent_tiles)
    def _():
        emit(c_ref)


def _top2_sum(a, b, c, d):
    hi1, lo1 = jnp.maximum(a, b), jnp.minimum(a, b)
    hi2, lo2 = jnp.maximum(c, d), jnp.minimum(c, d)
    return jnp.maximum(hi1, hi2) + jnp.maximum(jnp.minimum(hi1, hi2), jnp.maximum(lo1, lo2))


def _prenorm_router_kernel(x_ref, g_ref, mod_ref, rw_ref, rb_ref, o_ref, comb_ref):
    h = _norm_mod(x_ref[...], g_ref[...], mod_ref[0, 3:4, :], mod_ref[0, 4:5, :])
    o_ref[...] = h.astype(o_ref.dtype)
    s = jax.nn.sigmoid(_mm(rw_ref[...], h, NT_DIMS, 3))
    biased = s + rb_ref[...]
    rows = [biased[e:e + 1, :] for e in range(N_EXPERTS)]
    G = EXPERTS_PER_GROUP
    score = [_top2_sum(*rows[g * G:(g + 1) * G]) for g in range(N_GROUPS)]
    picked = []
    for e in range(N_EXPERTS):
        g = e // G
        ok = None
        for g2 in range(N_GROUPS):
            if g2 != g:
                t = (score[g] > score[g2]) if g2 < g else (score[g] >= score[g2])
                ok = t if ok is None else jnp.logical_and(ok, t)
        rank = 0.0
        for e2 in range(g * G, (g + 1) * G):
            if e2 != e:
                ahead = (rows[e2] >= rows[e]) if e2 < e else (rows[e2] > rows[e])
                rank = rank + jnp.where(ahead, 1.0, 0.0)
        picked.append(jnp.where(jnp.logical_and(ok, rank < TOP_K), s[e:e + 1, :], 0.0))
    w = jnp.concatenate(picked, axis=0)
    comb_ref[...] = w / jnp.sum(w, axis=0, keepdims=True)


def _cond_of_tile(rows_per_cond, tile_rows, fixed_row):
    if fixed_row is not None:
        return lambda m: fixed_row
    assert rows_per_cond % tile_rows == 0
    return lambda m: m // (rows_per_cond // tile_rows)


def prenorm(x, ctx, gain, mod, *, seq_len):
    n_lat, n_ctx = x.shape[0] // ROW_TM, ctx.shape[0] // ROW_TM
    n_batch = x.shape[0] // seq_len
    cond = _cond_of_tile(seq_len, ROW_TM, None)
    return pl.pallas_call(
        functools.partial(_prenorm_kernel, n_latent_tiles=n_lat),
        grid=(n_lat + n_ctx,),
        in_specs=[pl.BlockSpec((ROW_TM, D_MODEL), lambda i: (jnp.minimum(i, n_lat - 1), 0)),
                  pl.BlockSpec((ROW_TM, D_MODEL), lambda i: (jnp.maximum(i - n_lat, 0), 0)),
                  pl.BlockSpec((1, D_MODEL), lambda i: (0, 0)),
                  pl.BlockSpec((1, N_MOD, D_MODEL), lambda i: (jnp.where(i < n_lat, cond(i), n_batch), 0, 0))],
        out_specs=pl.BlockSpec((ROW_TM, D_MODEL), lambda i: (i, 0)),
        out_shape=jax.ShapeDtypeStruct((x.shape[0] + ctx.shape[0], D_MODEL), BF16),
        compiler_params=pltpu.CompilerParams(dimension_semantics=("arbitrary",)),
        name="prenorm",
    )(x, ctx, gain, mod)


def prenorm_router(x, gain, mod, router_w_t, router_bias, *, rows_per_cond=None, fixed_row=None):
    M = x.shape[0]
    cond = _cond_of_tile(rows_per_cond, ROW_TM, fixed_row)
    x_spec = pl.BlockSpec((ROW_TM, D_MODEL), lambda i: (i, 0))
    return pl.pallas_call(
        _prenorm_router_kernel,
        grid=(M // ROW_TM,),
        in_specs=[x_spec,
                  pl.BlockSpec((1, D_MODEL), lambda i: (0, 0)),
                  pl.BlockSpec((1, N_MOD, D_MODEL), lambda i: (cond(i), 0, 0)),
                  pl.BlockSpec((N_EXPERTS, D_MODEL), lambda i: (0, 0)),
                  pl.BlockSpec((N_EXPERTS, 1), lambda i: (0, 0))],
        out_specs=[x_spec, pl.BlockSpec((N_EXPERTS, ROW_TM), lambda i: (0, i))],
        out_shape=[jax.ShapeDtypeStruct((M, D_MODEL), BF16),
                   jax.ShapeDtypeStruct((N_EXPERTS, M), F32)],
        compiler_params=pltpu.CompilerParams(dimension_semantics=("arbitrary",)),
        name="prenorm_router",
    )(x, gain, mod, router_w_t, router_bias)


def _merge_kernel(y0_ref, y1_ref, y2_ref, g0_ref, g1_ref, g2_ref, wb_ref, o_ref):
    acc = 0.0
    for i, (y_ref, g_ref) in enumerate(((y0_ref, g0_ref), (y1_ref, g1_ref), (y2_ref, g2_ref))):
        acc = acc + jax.nn.sigmoid(g_ref[...]) * jnp.dot(y_ref[...].astype(BF16), wb_ref[i],
                                                         preferred_element_type=F32)
    o_ref[...] = acc.astype(o_ref.dtype)


def merge_branches(ys, z_all, row0, w_branch, *, tm=512, tn=1024):
    M = ys[0].shape[0]
    assert M % tm == 0 and row0 % tm == 0 and C_GATE % tn == 0 and D_MODEL % tn == 0
    row_blk0 = row0 // tm
    y_spec = pl.BlockSpec((tm, BRANCH_W), lambda m, n: (m, 0))

    def gate_spec(i):
        return pl.BlockSpec((tm, tn), lambda m, n: (row_blk0 + m, (C_GATE + i * D_MODEL) // tn + n))

    return pl.pallas_call(
        _merge_kernel,
        grid=(M // tm, D_MODEL // tn),
        in_specs=[y_spec] * 3 + [gate_spec(i) for i in range(N_BRANCH)]
        + [pl.BlockSpec((N_BRANCH, BRANCH_W, tn), lambda m, n: (0, 0, n))],
        out_specs=pl.BlockSpec((tm, tn), lambda m, n: (m, n)),
        out_shape=jax.ShapeDtypeStruct((M, D_MODEL), BF16),
        compiler_params=pltpu.CompilerParams(
            dimension_semantics=("arbitrary", "arbitrary"), vmem_limit_bytes=V7X_VMEM_LIMIT_BYTES),
        name="merge_branches",
    )(*ys, z_all, z_all, z_all, w_branch.astype(BF16))


def _residual_mm_kernel(a_ref, w_ref, x_ref, mod_ref, o_ref, *, gate_row):
    y = jnp.dot(a_ref[...], w_ref[...], preferred_element_type=F32)
    o_ref[...] = x_ref[...] + mod_ref[0, gate_row:gate_row + 1, :] * y


def residual_matmul(a, w, x, mod, *, gate_row, rows_per_cond=None, fixed_row=None, tm=512, tn=512):
    M, K = a.shape
    N = w.shape[1]
    assert M % tm == 0 and N % tn == 0
    cond = _cond_of_tile(rows_per_cond, tm, fixed_row)
    return pl.pallas_call(
        functools.partial(_residual_mm_kernel, gate_row=gate_row),
        grid=(N // tn, M // tm),
        in_specs=[pl.BlockSpec((tm, K), lambda n, m: (m, 0)),
                  pl.BlockSpec((K, tn), lambda n, m: (0, n)),
                  pl.BlockSpec((tm, tn), lambda n, m: (m, n)),
                  pl.BlockSpec((1, N_MOD, tn), lambda n, m: (cond(m), 0, n))],
        out_specs=pl.BlockSpec((tm, tn), lambda n, m: (m, n)),
        out_shape=jax.ShapeDtypeStruct((M, N), F32),
        compiler_params=pltpu.CompilerParams(
            dimension_semantics=("arbitrary", "arbitrary"), vmem_limit_bytes=V7X_VMEM_LIMIT_BYTES),
        name="residual_matmul",
    )(a, w.astype(BF16), x, mod)


def _head_sum(x2):
    row = lax.broadcasted_iota(jnp.int32, (LANES, LANES), 0) // NA_HEAD_DIM
    col = lax.broadcasted_iota(jnp.int32, (LANES, LANES), 1) // NA_HEAD_DIM
    ones_bd = jnp.where(row == col, 1.0, 0.0).astype(BF16)
    return _cumsum_rows_right(x2, ones_bd)


def _cumsum_rows_right(x, m):
    hi, lo = _split_bf16(x)
    return jnp.dot(hi, m, preferred_element_type=F32) + jnp.dot(lo, m, preferred_element_type=F32)


def _na_qkv_kernel(x_ref, g_ref, o_ref):
    x = x_ref[...]

    @pl.when(pl.program_id(1) < 2)
    def _():
        ms = _heads_sum(x * x) * (1.0 / NA_HEAD_DIM)
        o_ref[...] = (x * lax.rsqrt(ms + EPS) * g_ref[...]).astype(o_ref.dtype)

    @pl.when(pl.program_id(1) == 2)
    def _():
        o_ref[...] = x.astype(o_ref.dtype)


def na_qkv(z, gains, *, tm=256):
    M = z.shape[0]
    return pl.pallas_call(
        _na_qkv_kernel,
        grid=(M // tm, 3),
        in_specs=[pl.BlockSpec((tm, NA_W), lambda i, j: (i, j)),
                  pl.BlockSpec((1, NA_W), lambda i, j: (0, jnp.minimum(j, 1)))],
        out_specs=pl.BlockSpec((tm, NA_W), lambda i, j: (i, j)),
        out_shape=jax.ShapeDtypeStruct((M, 3 * NA_W), BF16),
        compiler_params=pltpu.CompilerParams(dimension_semantics=("arbitrary", "arbitrary")),
        name="na_qkv",
    )(z, gains)


def _softmax_pv(s_list, v_list):
    m = s_list[0].max(axis=-1, keepdims=True)
    for s in s_list[1:]:
        m = jnp.maximum(m, s.max(axis=-1, keepdims=True))
    den = 0.0
    acc = 0.0
    for s, v in zip(s_list, v_list):
        p = jnp.exp(s - m)
        den = den + p.sum(axis=-1, keepdims=True)
        acc = acc + jnp.dot(p.astype(BF16), v, preferred_element_type=F32)
    return acc / den


def _na_kernel(*refs, n_rows, win_r, rows_per_step):
    q_ref, k_ref, v_ref, kc_ref, vc_ref = refs[:5]
    bias_refs = refs[5:5 + rows_per_step]
    o_ref = refs[-1]
    kc = kc_ref[...]
    vc = vc_ref[...]
    lane = lax.broadcasted_iota(jnp.int32, (GRID_W, LANES), 1)
    chains = [(i, h) for i in range(rows_per_step) for h in range(2)]
    kw, vw = [], []
    for i in range(rows_per_step):
        r = pl.program_id(2) * rows_per_step + i
        rs = jnp.clip(r - win_r // 2, 0, n_rows - win_r)
        start = pl.multiple_of(rs * GRID_W, GRID_W)
        kw.append(k_ref[pl.ds(start, win_r * GRID_W), :])
        vw.append(v_ref[pl.ds(start, win_r * GRID_W), :])
    qh = []
    for i, h in chains:
        q = q_ref[i * GRID_W:(i + 1) * GRID_W, :]
        qh.append(jnp.where((lane // NA_HEAD_DIM) == h, q, jnp.zeros_like(q)))
    s_loc = [lax.dot_general(qh[c], kw[i], NT_DIMS, preferred_element_type=F32) + bias_refs[i][h, 0]
             for c, (i, h) in enumerate(chains)]
    s_ctx = [lax.dot_general(qh[c], kc, NT_DIMS, preferred_element_type=F32) for c in range(len(chains))]
    m = [jnp.maximum(s_loc[c].max(axis=-1, keepdims=True), s_ctx[c].max(axis=-1, keepdims=True))
         for c in range(len(chains))]
    p_loc = [jnp.exp(s_loc[c] - m[c]) for c in range(len(chains))]
    p_ctx = [jnp.exp(s_ctx[c] - m[c]) for c in range(len(chains))]
    den = [p_loc[c].sum(axis=-1, keepdims=True) + p_ctx[c].sum(axis=-1, keepdims=True)
           for c in range(len(chains))]
    acc = [jnp.dot(p_loc[c].astype(BF16), vw[i], preferred_element_type=F32)
           + jnp.dot(p_ctx[c].astype(BF16), vc, preferred_element_type=F32)
           for c, (i, h) in enumerate(chains)]
    out = [acc[c] / den[c] for c in range(len(chains))]
    for i in range(rows_per_step):
        o_ref[i * GRID_W:(i + 1) * GRID_W, :] = jnp.where(lane < NA_HEAD_DIM, out[2 * i], out[2 * i + 1])


def _ctx_attn_kernel(q_ref, k_ref, v_ref, o_ref):
    q = q_ref[...]
    k = k_ref[...]
    v = v_ref[...]
    lane = lax.broadcasted_iota(jnp.int32, q.shape, 1)
    outs = []
    for h in range(2):
        qh = jnp.where((lane // NA_HEAD_DIM) == h, q, jnp.zeros_like(q))
        s = lax.dot_general(qh, k, NT_DIMS, preferred_element_type=F32)
        outs.append(_softmax_pv([s], [v]))
    o_ref[...] = jnp.where(lane < NA_HEAD_DIM, outs[0], outs[1]).astype(o_ref.dtype)


def na_bias_table(rpb, n_rows, win_r):
    col = np.arange(GRID_W)
    col_start = np.clip(col - NA_WIN_C // 2, 0, GRID_W - NA_WIN_C)
    kcol = np.arange(GRID_W)
    valid = (kcol[None, :] >= col_start[:, None]) & (kcol[None, :] < col_start[:, None] + NA_WIN_C)
    dc = np.clip(kcol[None, :] - col[:, None] + NA_WIN_C - 1, 0, 2 * NA_WIN_C - 2)
    dr = np.arange(NA_WIN_R_MAX)[:, None] + np.arange(win_r)[None, :]
    dr = np.clip(dr, 0, 2 * NA_WIN_R_MAX - 2)
    t = rpb.astype(F32)[:, dr][:, :, :, dc]
    t = jnp.where(valid[None, None, None], t, NEG_BIG)
    t = t.transpose(0, 1, 3, 2, 4)
    return t.reshape(rpb.shape[0], NA_WIN_R_MAX, GRID_W, win_r * GRID_W)


def na_attention(qkv, rpb, B, S, Lc, with_ctx):
    n_rows = S // GRID_W
    win_r = min(NA_WIN_R_MAX, n_rows)
    assert win_r == NA_WIN_R_MAX and (B * S) % Lc == 0
    HP = NA_HEADS // 2
    bias = na_bias_table(rpb, n_rows, win_r)
    ctx_blk0 = (B * S) // Lc

    def variant(r):
        return jnp.clip(r - win_r // 2, 0, n_rows - win_r) - r + NA_WIN_R_MAX - 1

    R = NA_ROWS_PER_STEP
    assert n_rows % R == 0
    steps = n_rows // R
    bias_specs = [pl.BlockSpec((2, 1, GRID_W, win_r * GRID_W),
                               lambda hp, b, r, i=i: (hp, variant(r * R + i), 0, 0)) for i in range(R)]
    y = pl.pallas_call(
        functools.partial(_na_kernel, n_rows=n_rows, win_r=win_r, rows_per_step=R),
        grid=(HP, B, steps),
        in_specs=[pl.BlockSpec((R * GRID_W, LANES), lambda hp, b, r: (b * steps + r, hp)),
                  pl.BlockSpec((S, LANES), lambda hp, b, r: (b, HP + hp)),
                  pl.BlockSpec((S, LANES), lambda hp, b, r: (b, 2 * HP + hp)),
                  pl.BlockSpec((Lc, LANES), lambda hp, b, r: (ctx_blk0 + b, HP + hp)),
                  pl.BlockSpec((Lc, LANES), lambda hp, b, r: (ctx_blk0 + b, 2 * HP + hp))] + bias_specs,
        out_specs=pl.BlockSpec((R * GRID_W, LANES), lambda hp, b, r: (b * steps + r, hp)),
        out_shape=jax.ShapeDtypeStruct((B * S, NA_W), F32),
        compiler_params=pltpu.CompilerParams(
            dimension_semantics=("arbitrary", "arbitrary", "arbitrary"),
            vmem_limit_bytes=V7X_VMEM_LIMIT_BYTES),
        name="na_attention",
    )(qkv, qkv, qkv, qkv, qkv, *([bias] * R))
    yc = None
    if with_ctx:
        yc = pl.pallas_call(
            _ctx_attn_kernel,
            grid=(HP, B),
            in_specs=[pl.BlockSpec((Lc, LANES), lambda hp, b: (ctx_blk0 + b, hp)),
                      pl.BlockSpec((Lc, LANES), lambda hp, b: (ctx_blk0 + b, HP + hp)),
                      pl.BlockSpec((Lc, LANES), lambda hp, b: (ctx_blk0 + b, 2 * HP + hp))],
            out_specs=pl.BlockSpec((Lc, LANES), lambda hp, b: (b, hp)),
            out_shape=jax.ShapeDtypeStruct((B * Lc, NA_W), F32),
            compiler_params=pltpu.CompilerParams(dimension_semantics=("arbitrary", "arbitrary")),
            name="ctx_attention",
        )(qkv, qkv, qkv)
    return y, yc


def na_mixer(z_all, B, S, Lc, q_norm, k_norm, rpb, with_ctx):
    assert (C_NAQ, C_NAK, C_NAV) == (0, NA_W, 2 * NA_W)
    scale = NA_HEAD_DIM ** -0.5
    gains = jnp.concatenate([jnp.tile(q_norm.astype(F32) * scale, NA_HEADS),
                             jnp.tile(k_norm.astype(F32), NA_HEADS)])[None]
    y, yc = na_attention(na_qkv(z_all, gains), rpb, B, S, Lc, with_ctx)
    return y, yc


def _log_sigmoid(x):
    return jnp.minimum(x, 0.0) - jnp.log(1.0 + jnp.exp(-jnp.abs(x)))


def _gla_kernel(*refs, reverse, rope, finish, n_chunks, nb):
    it = iter(refs)

    def take(count):
        return [next(it) for _ in range(count)]

    q_refs, k_refs, v_refs, gd_refs = take(nb), take(nb), take(nb), take(nb)
    w2_ref, gb_ref, s0_ref = take(3)
    if rope:
        cos_ref, sin_ref = take(2)
    if finish:
        yo_ref = next(it)
        r_refs = take(nb)
        g_ref = next(it)
    y_ref, sfin_ref, s_scr = take(3)

    C = GLA_CHUNK
    n = pl.program_id(1)
    chains = [(i, h) for i in range(nb) for h in range(GLA_HEADS)]

    @pl.when(n == 0)
    def _():
        s_scr[...] = s0_ref[...]

    def rows(refs_):
        return jnp.concatenate([r[...] for r in refs_], axis=0)

    ti = lax.broadcasted_iota(jnp.int32, (nb * C, nb * C), 0)
    tj = lax.broadcasted_iota(jnp.int32, (nb * C, nb * C), 1)
    same_seq = (ti // C) == (tj // C)
    tri = jnp.where(same_seq & ((tj >= ti) if reverse else (tj <= ti)), 1.0, 0.0).astype(BF16)
    ci = lax.broadcasted_iota(jnp.int32, (C, C), 0)
    cj = lax.broadcasted_iota(jnp.int32, (C, C), 1)
    incl = (cj >= ci) if reverse else (cj <= ci)

    pre = _mm(rows(gd_refs), w2_ref[...], NN_DIMS, 3) + gb_ref[...]
    la = _log_sigmoid(pre) * (1.0 / GLA_GATE_TAU)
    cs = _cumsum_rows(tri, la)
    ref_i = C // 2 - 1 if reverse else C // 2
    last_i = 0 if reverse else C - 1

    def per_seq_row(idx):
        return jnp.concatenate([jnp.broadcast_to(cs[i * C + idx:i * C + idx + 1, :], (C, GLA_QK))
                                for i in range(nb)], axis=0)

    b_ref = per_seq_row(ref_i)
    b_last = per_seq_row(last_i)

    q = rows(q_refs) * (GLA_DK ** -0.5)
    k = rows(k_refs)
    if rope:
        lane = lax.broadcasted_iota(jnp.int32, q.shape, 1)
        first = (lane % (GLA_DK // 2)) < GLA_DK // 4
        cos = jnp.concatenate([jnp.concatenate([cos_ref[...]] * GLA_HEADS, axis=1)] * nb, axis=0)
        sin = jnp.concatenate([jnp.concatenate([sin_ref[...]] * GLA_HEADS, axis=1)] * nb, axis=0)

        def rot(x):
            partner = jnp.where(first, pltpu.roll(x, GLA_QK - GLA_DK // 4, axis=1),
                                pltpu.roll(x, GLA_DK // 4, axis=1))
            return x * cos + partner * sin

        q = rot(q)
        k = rot(k)
    qi = q * jnp.exp(cs - b_ref)
    kj = k * jnp.exp(b_ref - cs)
    qe = q * jnp.exp(cs)
    ke = k * jnp.exp(b_last - cs)
    dec = jnp.exp(b_last)
    v_all = rows(v_refs)

    def hk(x, i, h):
        return x[i * C:(i + 1) * C, h * GLA_DK:(h + 1) * GLA_DK]

    def hv(x, i, h):
        return x[i * C:(i + 1) * C, h * GLA_DV:(h + 1) * GLA_DV]

    St = [s_scr[i, h] for i, h in chains]
    V = [hv(v_all, i, h) for i, h in chains]
    att = [jnp.where(incl, _mm(hk(qi, i, h), hk(kj, i, h), NT_DIMS), 0.0) for i, h in chains]
    y = [_mm(att[c], V[c], NN_DIMS) + _mm(hk(qe, i, h), St[c], NT_DIMS) for c, (i, h) in enumerate(chains)]
    kvt = [_mm(V[c], hk(ke, i, h), TN_DIMS) for c, (i, h) in enumerate(chains)]
    for c, (i, h) in enumerate(chains):
        s_scr[i, h] = St[c] * hk(dec, i, h)[0:1, :] + kvt[c]
    if finish:
        r = rows(r_refs)
        g = g_ref[...]
        for c, (i, h) in enumerate(chains):
            ys = y[c] + yo_ref[i, :, h * GLA_DV:(h + 1) * GLA_DV]
            ms = jnp.mean(ys * ys, axis=-1, keepdims=True)
            rh = hv(r, i, h)
            y[c] = ys * lax.rsqrt(ms + EPS) * g * (rh * jax.nn.sigmoid(rh))
    for i in range(nb):
        y_ref[i] = jnp.concatenate(y[i * GLA_HEADS:(i + 1) * GLA_HEADS], axis=1)

    @pl.when(n == n_chunks - 1)
    def _():
        sfin_ref[...] = s_scr[...]


def gla_rope_tables(seq_len):
    t = np.arange(seq_len)
    quarter = GLA_DK // 4
    inv = ROPE_BASE ** (-np.arange(0, 2 * quarter, 2, dtype=np.float64) / (2 * quarter))
    ar = (t // GRID_W)[:, None] * inv[None, :]
    ac = (t % GRID_W)[:, None] * inv[None, :]
    cos = np.concatenate([np.cos(ar), np.cos(ar), np.cos(ac), np.cos(ac)], axis=1)
    sin = np.concatenate([-np.sin(ar), np.sin(ar), -np.sin(ac), np.sin(ac)], axis=1)
    return jnp.asarray(cos, F32), jnp.asarray(sin, F32)


def gla_scan(z, w2pad, gate_b, s0, *, batch, seq_len, row0, reverse, rope, finish_with=None):
    N = seq_len // GLA_CHUNK
    blk0 = row0 // GLA_CHUNK
    off_q, off_k, off_v, off_r, off_gd = C_GQ, C_GK, C_GV, C_GR, C_GGD
    assert off_q % GLA_QK == 0 and off_k % GLA_QK == 0 and off_v % GLA_V == 0 and off_r % GLA_V == 0
    assert off_gd % LANES == 0

    def tb(n):
        return N - 1 - n if reverse else n

    def rb(b, n):
        return blk0 + b * N + tb(n)

    nb = GLA_BATCH_PER_STEP if batch % GLA_BATCH_PER_STEP == 0 else 1

    def tok_specs(width, col_off):
        return [pl.BlockSpec((GLA_CHUNK, width), lambda bb, n, i=i: (rb(bb * nb + i, n), col_off // width))
                for i in range(nb)]

    st_spec = pl.BlockSpec((nb, GLA_HEADS, GLA_DV, GLA_DK), lambda bb, n: (bb, 0, 0, 0))
    y_spec = pl.BlockSpec((nb, GLA_CHUNK, GLA_V), lambda bb, n: (bb, tb(n), 0))
    in_specs = (tok_specs(GLA_QK, off_q) + tok_specs(GLA_QK, off_k) + tok_specs(GLA_V, off_v)
                + tok_specs(LANES, off_gd)
                + [pl.BlockSpec((LANES, GLA_QK), lambda bb, n: (0, 0)),
                   pl.BlockSpec((1, GLA_QK), lambda bb, n: (0, 0)),
                   st_spec])
    args = [z] * (4 * nb) + [w2pad, gate_b, s0]
    if rope:
        cos, sin = gla_rope_tables(seq_len)
        in_specs += [pl.BlockSpec((GLA_CHUNK, GLA_DK), lambda bb, n: (tb(n), 0))] * 2
        args += [cos, sin]
    if finish_with is not None:
        y_other, norm_g = finish_with
        in_specs += [y_spec] + tok_specs(GLA_V, off_r) + [pl.BlockSpec((1, GLA_DV), lambda bb, n: (0, 0))]
        args += [y_other.reshape(batch, seq_len, GLA_V)] + [z] * nb + [norm_g]
    y, s_fin = pl.pallas_call(
        functools.partial(_gla_kernel, reverse=reverse, rope=rope, finish=finish_with is not None,
                          n_chunks=N, nb=nb),
        grid=(batch // nb, N),
        in_specs=in_specs,
        out_specs=[y_spec, st_spec],
        out_shape=[jax.ShapeDtypeStruct((batch, seq_len, GLA_V), F32),
                   jax.ShapeDtypeStruct((batch, GLA_HEADS, GLA_DV, GLA_DK), F32)],
        scratch_shapes=[pltpu.VMEM((nb, GLA_HEADS, GLA_DV, GLA_DK), F32)],
        compiler_params=pltpu.CompilerParams(
            dimension_semantics=("arbitrary", "arbitrary"),
            vmem_limit_bytes=V7X_VMEM_LIMIT_BYTES),
        name="gla_scan_rev" if reverse else "gla_scan",
    )(*args)
    return y.reshape(batch * seq_len, GLA_V), s_fin


def gla_mixer(z_all, B, S, Lc, gate_w2, gate_b, norm_g, with_ctx):
    s0 = jnp.zeros((B, GLA_HEADS, GLA_DV, GLA_DK), F32)
    g = norm_g.astype(F32)[None]
    w2 = [jnp.zeros((LANES, GLA_QK), F32).at[d * GLA_GATE_RANK:(d + 1) * GLA_GATE_RANK].set(gate_w2[d])
          for d in range(2)]
    gb = [gate_b[d].astype(F32)[None] for d in range(2)]
    ctx = dict(batch=B, seq_len=Lc, row0=B * S, rope=False)
    lat = dict(batch=B, seq_len=S, row0=0, rope=True)
    yc_f, sc_f = gla_scan(z_all, w2[0], gb[0], s0, reverse=False, **ctx)
    yc, sc_b = gla_scan(z_all, w2[1], gb[1], s0, reverse=True, finish_with=(yc_f, g), **ctx)
    yl_f, _ = gla_scan(z_all, w2[0], gb[0], sc_f, reverse=False, **lat)
    yl, _ = gla_scan(z_all, w2[1], gb[1], sc_b, reverse=True, finish_with=(yl_f, g), **lat)
    return yl, (yc if with_ctx else None)


RW_PREP_TM = 256
RW_HALO = 8
RW_N_PIECES = 5


def _heads_sum(x):
    return jnp.concatenate([_head_sum(x[:, j * LANES:(j + 1) * LANES]) for j in range(x.shape[1] // LANES)],
                           axis=1)


def _rw_prep_kernel(*refs, tm, n_latent_tiles, tiles_per_seq):
    P = RW_N_PIECES
    z_refs, prev_refs, next_refs, mu_refs = refs[0:P], refs[P:2 * P], refs[2 * P:3 * P], refs[3 * P:4 * P]
    kk_ref, ka_ref, rk_ref, w0_ref, a0_ref, w2_ref, a2_ref, g2_ref = refs[4 * P:4 * P + 8]
    (r_o, v_o, kap_o, lw0_o, kd0_o, b0_o, lw1_o, kd1_o, b1_o, bonus_o, g_o) = refs[4 * P + 8:]
    i = pl.program_id(0)
    latent = i < n_latent_tiles
    first = jnp.logical_or(jnp.logical_not(latent), i % tiles_per_seq == 0)
    last = jnp.logical_or(jnp.logical_not(latent), i % tiles_per_seq == tiles_per_seq - 1)

    def shifted(j):
        z = z_refs[j][...]
        row = lax.broadcasted_iota(jnp.int32, z.shape, 0)
        prev_row = jnp.where(first, 0.0, prev_refs[j][RW_HALO - 1:RW_HALO, :])
        next_row = jnp.where(last, 0.0, next_refs[j][0:1, :])
        prev = jnp.where(row == 0, prev_row, pltpu.roll(z, 1, axis=0))
        nxt = jnp.where(row == tm - 1, next_row, pltpu.roll(z, tm - 1, axis=0))
        return z + mu_refs[j][...] * (0.5 * (prev + nxt) - z)

    r, k, v, lr, gd = (shifted(j) for j in range(P))
    kkr = k * kk_ref[...]
    kap = kkr * lax.rsqrt(_heads_sum(kkr * kkr) + EPS)
    r_o[...] = r
    v_o[...] = v
    kap_o[...] = kap
    bonus_o[...] = _heads_sum(r * k * rk_ref[...]) * v
    g_o[...] = _mm(jax.nn.sigmoid(gd), g2_ref[...], NN_DIMS, 3)
    ka = ka_ref[...]
    for d, (lw_o, kd_o, b_o) in enumerate(((lw0_o, kd0_o, b0_o), (lw1_o, kd1_o, b1_o))):
        wd = lr[:, d * LANES:(d + 1) * LANES]
        ad = lr[:, (2 + d) * LANES:(3 + d) * LANES]
        w_log = _log_sigmoid(w0_ref[d:d + 1, :] + _mm(jnp.tanh(wd), w2_ref[d], NN_DIMS, 3)) - 0.5
        lw_o[...] = -jnp.exp(w_log)
        a = jax.nn.sigmoid(a0_ref[d:d + 1, :] + _mm(ad, a2_ref[d], NN_DIMS, 3))
        kd_o[...] = k * (1.0 + (a - 1.0) * ka)
        b_o[...] = kap * a


def rwkv_prep(z_all, n_latent_rows, seq_len, mu_p, k_k, k_a, r_k, w0, a0, w2p, a2p, g2):
    M = z_all.shape[0]
    tm = RW_PREP_TM
    assert M % tm == 0 and n_latent_rows % tm == 0 and seq_len % tm == 0
    widths = (RW_W, RW_W, RW_W, RW_LR_W, RW_GATE_RANK)
    offs = (C_RR, C_RK, C_RV, C_RLR, C_RGD)
    n_halo_blocks = M // RW_HALO
    z_specs, prev_specs, next_specs, mu_specs = [], [], [], []
    for w, o in zip(widths, offs):
        assert o % w == 0
        cb = o // w
        z_specs.append(pl.BlockSpec((tm, w), lambda i, cb=cb: (i, cb)))
        prev_specs.append(pl.BlockSpec(
            (RW_HALO, w), lambda i, cb=cb: (jnp.maximum(i * (tm // RW_HALO) - 1, 0), cb)))
        next_specs.append(pl.BlockSpec(
            (RW_HALO, w), lambda i, cb=cb: (jnp.minimum((i + 1) * (tm // RW_HALO), n_halo_blocks - 1), cb)))
        mu_specs.append(pl.BlockSpec((1, w), lambda i: (0, 0)))

    def full(shape):
        return pl.BlockSpec(shape, lambda i: (0,) * len(shape))

    par_specs = [full((1, RW_W))] * 3 + [full((2, RW_W))] * 2 + [full((2, LANES, RW_W))] * 2 \
        + [full((RW_GATE_RANK, RW_W))]
    out_spec = pl.BlockSpec((tm, RW_W), lambda i: (i, 0))
    return pl.pallas_call(
        functools.partial(_rw_prep_kernel, tm=tm, n_latent_tiles=n_latent_rows // tm,
                          tiles_per_seq=seq_len // tm),
        grid=(M // tm,),
        in_specs=z_specs + prev_specs + next_specs + mu_specs + par_specs,
        out_specs=[out_spec] * 11,
        out_shape=[jax.ShapeDtypeStruct((M, RW_W), F32)] * 11,
        compiler_params=pltpu.CompilerParams(
            dimension_semantics=("arbitrary",), vmem_limit_bytes=V7X_VMEM_LIMIT_BYTES),
        name="rwkv_prep",
    )(*([z_all] * 15), *mu_p, k_k, k_a, r_k, w0, a0, w2p, a2p, g2)


def _rw_chunk_kernel(*refs, reverse, n_par, n_chunks, passes, finish, nb):
    it = iter(refs)

    def side_by_side():
        return jnp.concatenate([next(it)[...] for _ in range(nb)], axis=1)

    r_in, v_all, kap_in, lw_all, kd_all, bet_all = (side_by_side() for _ in range(6))
    s0_ref = next(it)
    if finish:
        yo_ref = next(it)
        bonus, gate = side_by_side(), side_by_side()
        gamma = jnp.concatenate([next(it)[...]] * nb, axis=1)
        beta = jnp.concatenate([next(it)[...]] * nb, axis=1)
    y_ref, sfin_ref, s_scr = next(it), next(it), next(it)

    C = RW_CHUNK
    n = pl.program_id(2)
    P = range(nb * n_par)

    @pl.when(n == 0)
    def _():
        s_scr[...] = s0_ref[:, 0]

    row = lax.broadcasted_iota(jnp.int32, (C, LANES), 0)
    lane = lax.broadcasted_iota(jnp.int32, (C, LANES), 1)
    col = lane % RW_HEAD
    head0 = lane < RW_HEAD
    strict = (col > row) if reverse else (col < row)
    incl = (col >= row) if reverse else (col <= row)
    ti = lax.broadcasted_iota(jnp.int32, (C, C), 0)
    tj = lax.broadcasted_iota(jnp.int32, (C, C), 1)
    tri = jnp.where((tj >= ti) if reverse else (tj <= ti), 1.0, 0.0).astype(BF16)
    eye = jnp.where(row == col, 1.0, 0.0)
    brow = lax.broadcasted_iota(jnp.int32, (LANES, LANES), 0) // RW_HEAD
    bcol = lax.broadcasted_iota(jnp.int32, (LANES, LANES), 1) // RW_HEAD
    same_head = brow == bcol

    def sl(x, p):
        return x[:, p * LANES:(p + 1) * LANES]

    def bd(x):
        return jnp.concatenate([jnp.where(head0, x, 0.0), jnp.where(head0, 0.0, x)], axis=0)

    def pp(a_pair, b_pair):
        return _mm(a_pair, bd(b_pair), NN_DIMS, passes)

    cs_all = _cumsum_rows(tri, lw_all)
    tot_all = cs_all[0:1, :] if reverse else cs_all[C - 1:C, :]
    inv_all = jnp.exp(-cs_all)
    fin_all = jnp.exp(tot_all - cs_all)
    kap_t = kap_in * jnp.exp(cs_all - lw_all)
    r_t = r_in * jnp.exp(cs_all)
    k_t = kd_all * inv_all
    b_t = bet_all * inv_all
    k_h = kd_all * fin_all
    b_h = bet_all * fin_all
    dec = jnp.exp(tot_all)

    S = [s_scr[p // n_par, p % n_par] for p in P]
    V = [sl(v_all, p) for p in P]
    lhs = [jnp.concatenate([sl(kap_t, p), sl(r_t, p)], axis=0) for p in P]
    a_k = [_mm(lhs[p], bd(sl(k_t, p)), NT_DIMS, passes) for p in P]
    a_b = [_mm(lhs[p], bd(sl(b_t, p)), NT_DIMS, passes) for p in P]
    s0_both = [_mm(lhs[p], S[p], NT_DIMS, passes) for p in P]
    lmat = [jnp.where(strict, a_b[p][:C], 0.0) for p in P]
    power = [-lmat[p] for p in P]
    t_inv = [eye + power[p] for p in P]
    s = 2
    while s < C:
        power = [pp(power[p], power[p]) for p in P]
        t_inv = [t_inv[p] + pp(t_inv[p], power[p]) for p in P]
        s *= 2
    x = [s0_both[p][:C] + pp(jnp.where(strict, a_k[p][:C], 0.0), V[p]) for p in P]
    u = [pp(t_inv[p], x[p]) for p in P]
    y = [s0_both[p][C:] + pp(jnp.where(incl, a_k[p][C:], 0.0), V[p])
         - pp(jnp.where(incl, a_b[p][C:], 0.0), u[p]) for p in P]
    upd = [_mm(V[p], sl(k_h, p), TN_DIMS, passes) - _mm(u[p], sl(b_h, p), TN_DIMS, passes) for p in P]
    for p in P:
        s_scr[p // n_par, p % n_par] = S[p] * sl(dec, p) + jnp.where(same_head, upd[p], 0.0)
    width = n_par * LANES
    if finish:
        yo = jnp.concatenate([yo_ref[i] for i in range(nb)], axis=1)
        ys = jnp.concatenate(y, axis=1) + yo
        cen = ys - _heads_sum(ys) * (1.0 / RW_HEAD)
        var = _heads_sum(cen * cen) * (1.0 / RW_HEAD)
        out = (cen * lax.rsqrt(var + RW_GN_EPS) * gamma + beta + bonus) * gate
    else:
        out = jnp.concatenate(y, axis=1)
    for i in range(nb):
        y_ref[i] = out[:, i * width:(i + 1) * width]

    @pl.when(n == n_chunks - 1)
    def _():
        sfin_ref[:, 0] = s_scr[...]


def rwkv_chunk_scan(r, v, kap, lw, kd, bet, s0, *, batch, seq_len, row0, reverse, finish_with=None):
    n_par = RW_PAIRS_PER_STEP
    nb = RW_BATCH_PER_STEP if batch % RW_BATCH_PER_STEP == 0 else 1
    W = r.shape[1]
    G = W // (n_par * LANES)
    N = seq_len // RW_CHUNK
    blk0 = row0 // RW_CHUNK

    def tb(n):
        return N - 1 - n if reverse else n

    def seq_in():
        return [pl.BlockSpec((RW_CHUNK, n_par * LANES),
                             lambda bb, g, n, i=i: (blk0 + (bb * nb + i) * N + tb(n), g)) for i in range(nb)]

    y_spec = pl.BlockSpec((nb, RW_CHUNK, n_par * LANES), lambda bb, g, n: (bb, tb(n), g))
    st_spec = pl.BlockSpec((nb, 1, n_par, LANES, LANES), lambda bb, g, n: (bb, g, 0, 0, 0))
    in_specs = seq_in() * 6 + [st_spec]
    args = [a for a in (r, v, kap, lw, kd, bet) for _ in range(nb)] + [s0]
    if finish_with is not None:
        y_other, bonus, gate, gamma, beta = finish_with
        vec_spec = pl.BlockSpec((1, n_par * LANES), lambda bb, g, n: (0, g))
        in_specs += [y_spec] + seq_in() * 2 + [vec_spec, vec_spec]
        args += [y_other.reshape(batch, seq_len, W)] + [bonus] * nb + [gate] * nb + [gamma, beta]
    y, s_fin = pl.pallas_call(
        functools.partial(_rw_chunk_kernel, reverse=reverse, n_par=n_par, n_chunks=N, passes=RW_PASSES,
                          finish=finish_with is not None, nb=nb),
        grid=(batch // nb, G, N),
        in_specs=in_specs,
        out_specs=[y_spec, st_spec],
        out_shape=[jax.ShapeDtypeStruct((batch, seq_len, W), F32),
                   jax.ShapeDtypeStruct((batch, G, n_par, LANES, LANES), F32)],
        scratch_shapes=[pltpu.VMEM((nb, n_par, LANES, LANES), F32)],
        compiler_params=pltpu.CompilerParams(
            dimension_semantics=("arbitrary", "arbitrary", "arbitrary"),
            vmem_limit_bytes=V7X_VMEM_LIMIT_BYTES),
        name="rwkv_chunk_scan_rev" if reverse else "rwkv_chunk_scan",
    )(*args)
    return y.reshape(batch * seq_len, W), s_fin


def _pad_rows(w, rows):
    return jnp.pad(w.astype(F32), ((0, 0), (0, rows - w.shape[1]), (0, 0)))


def rwkv_mixer(z_all, B, S, Lc, mu, w0, w2, a0, a2, g2, k_k, k_a, r_k, gn_g, gn_b, with_ctx):
    assert Lc == RW_PREP_TM
    mu = mu.astype(F32)
    o = np.cumsum((0,) + RW_SPLITS)
    lr_pad = jnp.zeros((LANES - RW_DECAY_RANK,), F32)
    mu_lr = jnp.concatenate([mu[o[3]:o[3] + RW_DECAY_RANK], lr_pad, mu[o[3] + RW_DECAY_RANK:o[4]], lr_pad,
                             mu[o[4]:o[4] + RW_A_RANK], lr_pad, mu[o[4] + RW_A_RANK:o[5]], lr_pad])
    mu_p = [mu[o[0]:o[1]][None], mu[o[1]:o[2]][None], mu[o[2]:o[3]][None], mu_lr[None], mu[o[5]:o[6]][None]]
    row = lambda t: t.astype(F32).reshape(1, RW_W)
    prep = rwkv_prep(z_all, B * S, S, mu_p, row(k_k), row(k_a), row(r_k), w0.astype(F32), a0.astype(F32),
                     _pad_rows(w2, LANES), _pad_rows(a2, LANES), g2.astype(F32))
    r, v, kap, lw0, kd0, b0, lw1, kd1, b1, bonus, gate = prep
    fin = (bonus, gate, row(gn_g), row(gn_b))
    s0 = jnp.zeros((B, RW_W // (RW_PAIRS_PER_STEP * LANES), RW_PAIRS_PER_STEP, LANES, LANES), F32)
    ctx = dict(batch=B, seq_len=Lc, row0=B * S)
    lat = dict(batch=B, seq_len=S, row0=0)
    yc_f, sc_f = rwkv_chunk_scan(r, v, kap, lw0, kd0, b0, s0, reverse=False, **ctx)
    yc, sc_b = rwkv_chunk_scan(r, v, kap, lw1, kd1, b1, s0, reverse=True, finish_with=(yc_f,) + fin, **ctx)
    yl_f, _ = rwkv_chunk_scan(r, v, kap, lw0, kd0, b0, sc_f, reverse=False, **lat)
    yl, _ = rwkv_chunk_scan(r, v, kap, lw1, kd1, b1, sc_b, reverse=True, finish_with=(yl_f,) + fin, **lat)
    return yl, (yc if with_ctx else None)


def relayout_w_in(w):
    o = IN_OFFSETS
    rw = o[8] + np.cumsum((0,) + RW_SPLITS)
    zeros = lambda n: jnp.zeros((w.shape[0], n), w.dtype)
    seg = lambda lo, n: w[:, lo:lo + n]
    lr_pad = LANES - RW_DECAY_RANK
    pieces = [
        seg(o[0], 3 * NA_W),
        seg(o[5], GLA_V), seg(o[6], GLA_V),
        seg(rw[0], 3 * RW_W),
        seg(o[9], N_BRANCH * D_MODEL),
        seg(o[3], GLA_QK), seg(o[4], GLA_QK),
        seg(rw[3], RW_DECAY_RANK), zeros(lr_pad), seg(rw[3] + RW_DECAY_RANK, RW_DECAY_RANK), zeros(lr_pad),
        seg(rw[4], RW_A_RANK), zeros(lr_pad), seg(rw[4] + RW_A_RANK, RW_A_RANK), zeros(lr_pad),
        seg(rw[5], RW_GATE_RANK),
        seg(o[7], 2 * GLA_GATE_RANK),
    ]
    used = sum(x.shape[1] for x in pieces)
    assert used == C_GGD + 2 * GLA_GATE_RANK
    return jnp.concatenate(pieces + [zeros(Z_W - used)], axis=1).astype(BF16)


def token_mixers(x, ctx, mod, p, B, S, Lc, with_ctx):
    h_all = prenorm(x, ctx, p['norm1_g'], mod, seq_len=S)
    z_all = matmul(h_all, relayout_w_in(p['w_in']), tm=1024, tn=1024)
    y_na, yc_na = na_mixer(z_all, B, S, Lc, p['na_q_norm'], p['na_k_norm'], p['na_rpb'], with_ctx)
    y_gla, yc_gla = gla_mixer(z_all, B, S, Lc, p['gla_gate_w2'], p['gla_gate_b'], p['gla_norm_g'], with_ctx)
    y_rw, yc_rw = rwkv_mixer(z_all, B, S, Lc, p['rw_mu'], p['rw_w0'], p['rw_w2'], p['rw_a0'], p['rw_a2'],
                             p['rw_g2'], p['rw_k_k'], p['rw_k_a'], p['rw_r_k'], p['rw_gn_g'],
                             p['rw_gn_b'], with_ctx)
    merged = merge_branches((y_na, y_gla, y_rw), z_all, 0, p['w_branch'])
    x = residual_matmul(merged, p['w_out'], x, mod, gate_row=2, rows_per_cond=S)
    if with_ctx:
        merged_c = merge_branches((yc_na, yc_gla, yc_rw), z_all, B * S, p['w_branch'])
        ctx = residual_matmul(merged_c, p['w_out'], ctx, mod, gate_row=2, fixed_row=B)
    return x, ctx


def moe_sublayer(x, mod, p, *, rows_per_cond=None, fixed_row=None):
    h2, comb_t = prenorm_router(x, p['norm2_g'], mod, p['router_w_t'], p['router_bias'],
                                rows_per_cond=rows_per_cond, fixed_row=fixed_row)
    return moe_experts(h2, comb_t.T, p['moe_w_gate'], p['moe_w_up'], p['moe_w_down'], x, mod,
                       rows_per_cond=rows_per_cond, fixed_row=fixed_row)


def kernel(x, c, ctx, c_ctx, ada_w, ada_b, norm1_g, norm2_g, w_in, na_q_norm, na_k_norm, na_rpb,
           gla_gate_w2, gla_gate_b, gla_norm_g, rw_mu, rw_w0, rw_w2, rw_a0, rw_a2, rw_g2,
           rw_k_k, rw_k_a, rw_r_k, rw_gn_g, rw_gn_b, w_branch, w_out, router_w, router_bias,
           moe_w_gate, moe_w_up, moe_w_down):
    B, S, D = x.shape
    Lc = ctx.shape[1]
    assert B + 1 <= N_COND and S % 512 == 0 and (B * Lc) % 512 == 0 and Lc % ROW_TM == 0
    c_rows = jnp.concatenate([c, c_ctx[None], jnp.zeros((N_COND - B - 1, D), c.dtype)], axis=0).astype(F32)
    mods = ada_modulation(c_rows, ada_w, ada_b).reshape(DEPTH, N_COND, N_MOD, D)
    x = x.reshape(B * S, D)
    ctx = ctx.reshape(B * Lc, D)
    router_w_t = router_w.astype(F32).T
    router_b = router_bias.astype(F32).reshape(N_EXPERTS, 1)
    for l in range(DEPTH):
        with_ctx = l < DEPTH - 1
        p = {
            'w_in': w_in[l], 'na_q_norm': na_q_norm[l], 'na_k_norm': na_k_norm[l], 'na_rpb': na_rpb[l],
            'gla_gate_w2': gla_gate_w2[l], 'gla_gate_b': gla_gate_b[l], 'gla_norm_g': gla_norm_g[l],
            'rw_mu': rw_mu[l], 'rw_w0': rw_w0[l], 'rw_w2': rw_w2[l], 'rw_a0': rw_a0[l], 'rw_a2': rw_a2[l],
            'rw_g2': rw_g2[l], 'rw_k_k': rw_k_k[l], 'rw_k_a': rw_k_a[l], 'rw_r_k': rw_r_k[l],
            'rw_gn_g': rw_gn_g[l], 'rw_gn_b': rw_gn_b[l], 'w_branch': w_branch[l], 'w_out': w_out[l],
            'norm1_g': norm1_g[l].astype(F32)[None], 'norm2_g': norm2_g[l].astype(F32)[None],
            'router_w_t': router_w_t, 'router_bias': router_b,
            'moe_w_gate': moe_w_gate[l].astype(BF16), 'moe_w_up': moe_w_up[l].astype(BF16),
            'moe_w_down': moe_w_down[l].astype(BF16),
        }
        x, ctx = token_mixers(x, ctx, mods[l], p, B, S, Lc, with_ctx)
        x = moe_sublayer(x, mods[l], p, rows_per_cond=S)
        if with_ctx:
            ctx = moe_sublayer(ctx, mods[l], p, fixed_row=B)
    return x.reshape(B, S, D)
```

```python
import functools

import jax
import jax.numpy as jnp
from jax import lax
import numpy as np
from jax.experimental import pallas as pl
from jax.experimental.pallas import tpu as pltpu

D_MODEL = 2048
DEPTH = 2
GRID_W = 64

NA_HEADS = 16
NA_HEAD_DIM = 64
NA_WIN_R_MAX = 8
NA_WIN_C = 16
NA_W = NA_HEADS * NA_HEAD_DIM
NA_ROWS_PER_STEP = 8

GLA_HEADS = 4
GLA_DK = 128
GLA_DV = 256
GLA_QK = GLA_HEADS * GLA_DK
GLA_V = GLA_HEADS * GLA_DV
GLA_GATE_RANK = 16
GLA_GATE_TAU = 16.0
GLA_CHUNK = 64
GLA_BATCH_PER_STEP = 4

RW_HEADS = 16
RW_HEAD = 64
RW_W = RW_HEADS * RW_HEAD
RW_DECAY_RANK = 96
RW_A_RANK = 96
RW_GATE_RANK = 256
RW_SPLITS = (RW_W, RW_W, RW_W, 2 * RW_DECAY_RANK, 2 * RW_A_RANK, RW_GATE_RANK)
RW_IN = sum(RW_SPLITS)
RW_GN_EPS = 64e-5
RW_CHUNK = 64
RW_PASSES = 1
RW_PAIRS_PER_STEP = 8
RW_BATCH_PER_STEP = 4

N_BRANCH = 3
BRANCH_W = 1024
IN_SPLITS = (NA_W, NA_W, NA_W, GLA_QK, GLA_QK, GLA_V, GLA_V, 2 * GLA_GATE_RANK, RW_IN, N_BRANCH * D_MODEL)
D_IN = sum(IN_SPLITS)
IN_OFFSETS = tuple(int(o) for o in np.cumsum((0,) + IN_SPLITS[:-1]))

LANES = 128
RW_LR_W = 4 * LANES
C_NAQ, C_NAK, C_NAV = 0, 1024, 2048
C_GV, C_GR = 3072, 4096
C_RR, C_RK, C_RV = 5120, 6144, 7168
C_GATE = 8192
C_GQ, C_GK = 14336, 14848
C_RLR, C_RGD, C_GGD = 15360, 15872, 16128
Z_W = 16384

N_EXPERTS = 16
N_GROUPS = 4
EXPERTS_PER_GROUP = N_EXPERTS // N_GROUPS
TOP_K = 2
D_EXPERT = 512

ROPE_BASE = 10000.0
EPS = 1e-6
F32 = jnp.float32
BF16 = jnp.bfloat16

V7X_VMEM_LIMIT_BYTES = 48 * 1024 * 1024

NEG_BIG = -1e30
NT_DIMS = (((1,), (1,)), ((), ()))
TN_DIMS = (((0,), (0,)), ((), ()))
NN_DIMS = (((1,), (0,)), ((), ()))


def _split_bf16(x):
    hi = x.astype(BF16)
    lo = (x - hi.astype(F32)).astype(BF16)
    return hi, lo


def _mm(a, b, dims=NN_DIMS, passes=1):
    if passes == 1:
        return lax.dot_general(a.astype(BF16), b.astype(BF16), dims, preferred_element_type=F32)
    a_hi, a_lo = _split_bf16(a)
    b_hi, b_lo = _split_bf16(b)
    out = lax.dot_general(a_hi, b_hi, dims, preferred_element_type=F32)
    out = out + lax.dot_general(a_hi, b_lo, dims, preferred_element_type=F32)
    return out + lax.dot_general(a_lo, b_hi, dims, preferred_element_type=F32)


def _cumsum_rows(tri, x):
    hi, lo = _split_bf16(x)
    return jnp.dot(tri, hi, preferred_element_type=F32) + jnp.dot(tri, lo, preferred_element_type=F32)


def _mm_kernel(a_ref, b_ref, o_ref):
    o_ref[...] = jnp.dot(a_ref[...], b_ref[...], preferred_element_type=F32).astype(o_ref.dtype)


def matmul(a, b, *, tm=512, tn=512, out_dtype=F32):
    M, K = a.shape
    _, N = b.shape
    assert M % tm == 0 and N % tn == 0, (M, N, tm, tn)
    return pl.pallas_call(
        _mm_kernel,
        grid=(N // tn, M // tm),
        in_specs=[pl.BlockSpec((tm, K), lambda n, m: (m, 0)),
                  pl.BlockSpec((K, tn), lambda n, m: (0, n))],
        out_specs=pl.BlockSpec((tm, tn), lambda n, m: (m, n)),
        out_shape=jax.ShapeDtypeStruct((M, N), out_dtype),
        compiler_params=pltpu.CompilerParams(
            dimension_semantics=("arbitrary", "arbitrary"),
            vmem_limit_bytes=V7X_VMEM_LIMIT_BYTES),
        name="matmul",
    )(a.astype(BF16), b.astype(BF16))


def _moe_kernel(h_ref, comb_ref, wg_ref, wu_ref, wd_ref, x_ref, mod_ref, o_ref, acc_ref):
    e = pl.program_id(1)

    @pl.when(e == 0)
    def _():
        acc_ref[...] = jnp.zeros_like(acc_ref)

    h = h_ref[...]
    g = jnp.dot(h, wg_ref[0], preferred_element_type=F32)
    u = jnp.dot(h, wu_ref[0], preferred_element_type=F32)
    comb = comb_ref[...]
    lane = lax.broadcasted_iota(jnp.int32, comb.shape, 1)
    ce = jnp.sum(jnp.where(lane == e, comb, 0.0), axis=1, keepdims=True)
    act = (g * jax.nn.sigmoid(g)) * u * ce
    acc_ref[...] += jnp.dot(act.astype(BF16), wd_ref[0], preferred_element_type=F32)

    @pl.when(e == N_EXPERTS - 1)
    def _():
        o_ref[...] = x_ref[...] + mod_ref[0, N_MOD - 1:N_MOD, :] * acc_ref[...]


def moe_experts(h, comb, w_gate, w_up, w_down, x, mod, *, rows_per_cond=None, fixed_row=None, tm=512):
    M, D = h.shape
    assert M % tm == 0
    cond = _cond_of_tile(rows_per_cond, tm, fixed_row)
    return pl.pallas_call(
        _moe_kernel,
        grid=(M // tm, N_EXPERTS),
        in_specs=[pl.BlockSpec((tm, D), lambda m, e: (m, 0)),
                  pl.BlockSpec((tm, N_EXPERTS), lambda m, e: (m, 0)),
                  pl.BlockSpec((1, D, D_EXPERT), lambda m, e: (e, 0, 0)),
                  pl.BlockSpec((1, D, D_EXPERT), lambda m, e: (e, 0, 0)),
                  pl.BlockSpec((1, D_EXPERT, D), lambda m, e: (e, 0, 0)),
                  pl.BlockSpec((tm, D), lambda m, e: (m, 0)),
                  pl.BlockSpec((1, N_MOD, D), lambda m, e: (cond(m), 0, 0))],
        out_specs=pl.BlockSpec((tm, D), lambda m, e: (m, 0)),
        out_shape=jax.ShapeDtypeStruct((M, D), F32),
        scratch_shapes=[pltpu.VMEM((tm, D), F32)],
        compiler_params=pltpu.CompilerParams(
            dimension_semantics=("arbitrary", "arbitrary"),
            vmem_limit_bytes=V7X_VMEM_LIMIT_BYTES),
        name="moe_experts",
    )(h, comb, w_gate, w_up, w_down, x, mod)


N_COND = 8
N_MOD = 6
ROW_TM = 256


def _ada_kernel(c_ref, w_ref, b_ref, o_ref):
    c = c_ref[...]
    o_ref[0] = _mm(c * jax.nn.sigmoid(c), w_ref[0], NN_DIMS, 3) + b_ref[0]


def ada_modulation(c_rows, ada_w, ada_b, *, tn=2048):
    L, D, N = ada_w.shape
    return pl.pallas_call(
        _ada_kernel,
        grid=(L, N // tn),
        in_specs=[pl.BlockSpec((N_COND, D), lambda l, n: (0, 0)),
                  pl.BlockSpec((1, D, tn), lambda l, n: (l, 0, n)),
                  pl.BlockSpec((1, 1, tn), lambda l, n: (l, 0, n))],
        out_specs=pl.BlockSpec((1, N_COND, tn), lambda l, n: (l, 0, n)),
        out_shape=jax.ShapeDtypeStruct((L, N_COND, N), F32),
        compiler_params=pltpu.CompilerParams(
            dimension_semantics=("arbitrary", "arbitrary"), vmem_limit_bytes=V7X_VMEM_LIMIT_BYTES),
        name="ada_modulation",
    )(c_rows, ada_w, ada_b.reshape(L, 1, N))


def _norm_mod(x, g, shift, scale):
    y = x * lax.rsqrt(jnp.mean(x * x, axis=-1, keepdims=True) + EPS)
    return y * g * (1.0 + scale) + shift


def _prenorm_kernel(x_ref, c_ref, g_ref, mod_ref, o_ref, *, n_latent_tiles):
    def emit(src_ref):
        o_ref[...] = _norm_mod(src_ref[...], g_ref[...], mod_ref[0, 0:1, :], mod_ref[0, 1:2, :]).astype(o_ref.dtype)

    @pl.when(pl.program_id(0) < n_latent_tiles)
    def _():
        emit(x_ref)

    @pl.when(pl.program_id(0) >= n_latent_tiles)
    def _():
        emit(c_ref)


def _top2_sum(a, b, c, d):
    hi1, lo1 = jnp.maximum(a, b), jnp.minimum(a, b)
    hi2, lo2 = jnp.maximum(c, d), jnp.minimum(c, d)
    return jnp.maximum(hi1, hi2) + jnp.maximum(jnp.minimum(hi1, hi2), jnp.maximum(lo1, lo2))


def _prenorm_router_kernel(x_ref, g_ref, mod_ref, rw_ref, rb_ref, o_ref, comb_ref):
    h = _norm_mod(x_ref[...], g_ref[...], mod_ref[0, 3:4, :], mod_ref[0, 4:5, :])
    o_ref[...] = h.astype(o_ref.dtype)
    s = jax.nn.sigmoid(_mm(rw_ref[...], h, NT_DIMS, 3))
    biased = s + rb_ref[...]
    rows = [biased[e:e + 1, :] for e in range(N_EXPERTS)]
    G = EXPERTS_PER_GROUP
    score = [_top2_sum(*rows[g * G:(g + 1) * G]) for g in range(N_GROUPS)]
    picked = []
    for e in range(N_EXPERTS):
        g = e // G
        ok = None
        for g2 in range(N_GROUPS):
            if g2 != g:
                t = (score[g] > score[g2]) if g2 < g else (score[g] >= score[g2])
                ok = t if ok is None else jnp.logical_and(ok, t)
        rank = 0.0
        for e2 in range(g * G, (g + 1) * G):
            if e2 != e:
                ahead = (rows[e2] >= rows[e]) if e2 < e else (rows[e2] > rows[e])
                rank = rank + jnp.where(ahead, 1.0, 0.0)
        picked.append(jnp.where(jnp.logical_and(ok, rank < TOP_K), s[e:e + 1, :], 0.0))
    w = jnp.concatenate(picked, axis=0)
    comb_ref[...] = w / jnp.sum(w, axis=0, keepdims=True)


def _cond_of_tile(rows_per_cond, tile_rows, fixed_row):
    if fixed_row is not None:
        return lambda m: fixed_row
    assert rows_per_cond % tile_rows == 0
    return lambda m: m // (rows_per_cond // tile_rows)


def prenorm(x, ctx, gain, mod, *, seq_len):
    n_lat, n_ctx = x.shape[0] // ROW_TM, ctx.shape[0] // ROW_TM
    n_batch = x.shape[0] // seq_len
    cond = _cond_of_tile(seq_len, ROW_TM, None)
    return pl.pallas_call(
        functools.partial(_prenorm_kernel, n_latent_tiles=n_lat),
        grid=(n_lat + n_ctx,),
        in_specs=[pl.BlockSpec((ROW_TM, D_MODEL), lambda i: (jnp.minimum(i, n_lat - 1), 0)),
                  pl.BlockSpec((ROW_TM, D_MODEL), lambda i: (jnp.maximum(i - n_lat, 0), 0)),
                  pl.BlockSpec((1, D_MODEL), lambda i: (0, 0)),
                  pl.BlockSpec((1, N_MOD, D_MODEL), lambda i: (jnp.where(i < n_lat, cond(i), n_batch), 0, 0))],
        out_specs=pl.BlockSpec((ROW_TM, D_MODEL), lambda i: (i, 0)),
        out_shape=jax.ShapeDtypeStruct((x.shape[0] + ctx.shape[0], D_MODEL), BF16),
        compiler_params=pltpu.CompilerParams(dimension_semantics=("arbitrary",)),
        name="prenorm",
    )(x, ctx, gain, mod)


def prenorm_router(x, gain, mod, router_w_t, router_bias, *, rows_per_cond=None, fixed_row=None):
    M = x.shape[0]
    cond = _cond_of_tile(rows_per_cond, ROW_TM, fixed_row)
    x_spec = pl.BlockSpec((ROW_TM, D_MODEL), lambda i: (i, 0))
    return pl.pallas_call(
        _prenorm_router_kernel,
        grid=(M // ROW_TM,),
        in_specs=[x_spec,
                  pl.BlockSpec((1, D_MODEL), lambda i: (0, 0)),
                  pl.BlockSpec((1, N_MOD, D_MODEL), lambda i: (cond(i), 0, 0)),
                  pl.BlockSpec((N_EXPERTS, D_MODEL), lambda i: (0, 0)),
                  pl.BlockSpec((N_EXPERTS, 1), lambda i: (0, 0))],
        out_specs=[x_spec, pl.BlockSpec((N_EXPERTS, ROW_TM), lambda i: (0, i))],
        out_shape=[jax.ShapeDtypeStruct((M, D_MODEL), BF16),
                   jax.ShapeDtypeStruct((N_EXPERTS, M), F32)],
        compiler_params=pltpu.CompilerParams(dimension_semantics=("arbitrary",)),
        name="prenorm_router",
    )(x, gain, mod, router_w_t, router_bias)


def _merge_kernel(y0_ref, y1_ref, y2_ref, g0_ref, g1_ref, g2_ref, wb_ref, o_ref):
    acc = 0.0
    for i, (y_ref, g_ref) in enumerate(((y0_ref, g0_ref), (y1_ref, g1_ref), (y2_ref, g2_ref))):
        acc = acc + jax.nn.sigmoid(g_ref[...]) * jnp.dot(y_ref[...].astype(BF16), wb_ref[i],
                                                         preferred_element_type=F32)
    o_ref[...] = acc.astype(o_ref.dtype)


def merge_branches(ys, z_all, row0, w_branch, *, tm=512, tn=1024):
    M = ys[0].shape[0]
    assert M % tm == 0 and row0 % tm == 0 and C_GATE % tn == 0 and D_MODEL % tn == 0
    row_blk0 = row0 // tm
    y_spec = pl.BlockSpec((tm, BRANCH_W), lambda m, n: (m, 0))

    def gate_spec(i):
        return pl.BlockSpec((tm, tn), lambda m, n: (row_blk0 + m, (C_GATE + i * D_MODEL) // tn + n))

    return pl.pallas_call(
        _merge_kernel,
        grid=(M // tm, D_MODEL // tn),
        in_specs=[y_spec] * 3 + [gate_spec(i) for i in range(N_BRANCH)]
        + [pl.BlockSpec((N_BRANCH, BRANCH_W, tn), lambda m, n: (0, 0, n))],
        out_specs=pl.BlockSpec((tm, tn), lambda m, n: (m, n)),
        out_shape=jax.ShapeDtypeStruct((M, D_MODEL), BF16),
        compiler_params=pltpu.CompilerParams(
            dimension_semantics=("arbitrary", "arbitrary"), vmem_limit_bytes=V7X_VMEM_LIMIT_BYTES),
        name="merge_branches",
    )(*ys, z_all, z_all, z_all, w_branch.astype(BF16))


def _residual_mm_kernel(a_ref, w_ref, x_ref, mod_ref, o_ref, *, gate_row):
    y = jnp.dot(a_ref[...], w_ref[...], preferred_element_type=F32)
    o_ref[...] = x_ref[...] + mod_ref[0, gate_row:gate_row + 1, :] * y


def residual_matmul(a, w, x, mod, *, gate_row, rows_per_cond=None, fixed_row=None, tm=512, tn=512):
    M, K = a.shape
    N = w.shape[1]
    assert M % tm == 0 and N % tn == 0
    cond = _cond_of_tile(rows_per_cond, tm, fixed_row)
    return pl.pallas_call(
        functools.partial(_residual_mm_kernel, gate_row=gate_row),
        grid=(N // tn, M // tm),
        in_specs=[pl.BlockSpec((tm, K), lambda n, m: (m, 0)),
                  pl.BlockSpec((K, tn), lambda n, m: (0, n)),
                  pl.BlockSpec((tm, tn), lambda n, m: (m, n)),
                  pl.BlockSpec((1, N_MOD, tn), lambda n, m: (cond(m), 0, n))],
        out_specs=pl.BlockSpec((tm, tn), lambda n, m: (m, n)),
        out_shape=jax.ShapeDtypeStruct((M, N), F32),
        compiler_params=pltpu.CompilerParams(
            dimension_semantics=("arbitrary", "arbitrary"), vmem_limit_bytes=V7X_VMEM_LIMIT_BYTES),
        name="residual_matmul",
    )(a, w.astype(BF16), x, mod)


def _head_sum(x2):
    row = lax.broadcasted_iota(jnp.int32, (LANES, LANES), 0) // NA_HEAD_DIM
    col = lax.broadcasted_iota(jnp.int32, (LANES, LANES), 1) // NA_HEAD_DIM
    ones_bd = jnp.where(row == col, 1.0, 0.0).astype(BF16)
    return _cumsum_rows_right(x2, ones_bd)


def _cumsum_rows_right(x, m):
    hi, lo = _split_bf16(x)
    return jnp.dot(hi, m, preferred_element_type=F32) + jnp.dot(lo, m, preferred_element_type=F32)


def _na_qkv_kernel(x_ref, g_ref, o_ref):
    x = x_ref[...]

    @pl.when(pl.program_id(1) < 2)
    def _():
        ms = _heads_sum(x * x) * (1.0 / NA_HEAD_DIM)
        o_ref[...] = (x * lax.rsqrt(ms + EPS) * g_ref[...]).astype(o_ref.dtype)

    @pl.when(pl.program_id(1) == 2)
    def _():
        o_ref[...] = x.astype(o_ref.dtype)


def na_qkv(z, gains, *, tm=512):
    M = z.shape[0]
    return pl.pallas_call(
        _na_qkv_kernel,
        grid=(M // tm, 3),
        in_specs=[pl.BlockSpec((tm, NA_W), lambda i, j: (i, j)),
                  pl.BlockSpec((1, NA_W), lambda i, j: (0, jnp.minimum(j, 1)))],
        out_specs=pl.BlockSpec((tm, NA_W), lambda i, j: (i, j)),
        out_shape=jax.ShapeDtypeStruct((M, 3 * NA_W), BF16),
        compiler_params=pltpu.CompilerParams(dimension_semantics=("arbitrary", "arbitrary")),
        name="na_qkv",
    )(z, gains)


def _softmax_pv(s_list, v_list):
    m = s_list[0].max(axis=-1, keepdims=True)
    for s in s_list[1:]:
        m = jnp.maximum(m, s.max(axis=-1, keepdims=True))
    den = 0.0
    acc = 0.0
    for s, v in zip(s_list, v_list):
        p = jnp.exp(s - m)
        den = den + p.sum(axis=-1, keepdims=True)
        acc = acc + jnp.dot(p.astype(BF16), v, preferred_element_type=F32)
    return acc / den


def _na_kernel(*refs, n_rows, win_r, rows_per_step):
    q_ref, k_ref, v_ref, kc_ref, vc_ref = refs[:5]
    bias_refs = refs[5:5 + rows_per_step]
    o_ref = refs[-1]
    kc = kc_ref[...]
    vc = vc_ref[...]
    lane = lax.broadcasted_iota(jnp.int32, (GRID_W, LANES), 1)
    chains = [(i, h) for i in range(rows_per_step) for h in range(2)]
    kw, vw = [], []
    for i in range(rows_per_step):
        r = pl.program_id(2) * rows_per_step + i
        rs = jnp.clip(r - win_r // 2, 0, n_rows - win_r)
        start = pl.multiple_of(rs * GRID_W, GRID_W)
        kw.append(k_ref[pl.ds(start, win_r * GRID_W), :])
        vw.append(v_ref[pl.ds(start, win_r * GRID_W), :])
    qh = []
    for i, h in chains:
        q = q_ref[i * GRID_W:(i + 1) * GRID_W, :]
        qh.append(jnp.where((lane // NA_HEAD_DIM) == h, q, jnp.zeros_like(q)))
    s_loc = [lax.dot_general(qh[c], kw[i], NT_DIMS, preferred_element_type=F32) + bias_refs[i][h, 0]
             for c, (i, h) in enumerate(chains)]
    s_ctx = [lax.dot_general(qh[c], kc, NT_DIMS, preferred_element_type=F32) for c in range(len(chains))]
    m = [jnp.maximum(s_loc[c].max(axis=-1, keepdims=True), s_ctx[c].max(axis=-1, keepdims=True))
         for c in range(len(chains))]
    p_loc = [jnp.exp(s_loc[c] - m[c]) for c in range(len(chains))]
    p_ctx = [jnp.exp(s_ctx[c] - m[c]) for c in range(len(chains))]
    den = [p_loc[c].sum(axis=-1, keepdims=True) + p_ctx[c].sum(axis=-1, keepdims=True)
           for c in range(len(chains))]
    acc = [jnp.dot(p_loc[c].astype(BF16), vw[i], preferred_element_type=F32)
           + jnp.dot(p_ctx[c].astype(BF16), vc, preferred_element_type=F32)
           for c, (i, h) in enumerate(chains)]
    out = [acc[c] / den[c] for c in range(len(chains))]
    for i in range(rows_per_step):
        o_ref[i * GRID_W:(i + 1) * GRID_W, :] = jnp.where(lane < NA_HEAD_DIM, out[2 * i], out[2 * i + 1])


def _ctx_attn_kernel(q_ref, k_ref, v_ref, o_ref):
    q = q_ref[...]
    k = k_ref[...]
    v = v_ref[...]
    lane = lax.broadcasted_iota(jnp.int32, q.shape, 1)
    outs = []
    for h in range(2):
        qh = jnp.where((lane // NA_HEAD_DIM) == h, q, jnp.zeros_like(q))
        s = lax.dot_general(qh, k, NT_DIMS, preferred_element_type=F32)
        outs.append(_softmax_pv([s], [v]))
    o_ref[...] = jnp.where(lane < NA_HEAD_DIM, outs[0], outs[1]).astype(o_ref.dtype)


def na_bias_table(rpb, n_rows, win_r):
    col = np.arange(GRID_W)
    col_start = np.clip(col - NA_WIN_C // 2, 0, GRID_W - NA_WIN_C)
    kcol = np.arange(GRID_W)
    valid = (kcol[None, :] >= col_start[:, None]) & (kcol[None, :] < col_start[:, None] + NA_WIN_C)
    dc = np.clip(kcol[None, :] - col[:, None] + NA_WIN_C - 1, 0, 2 * NA_WIN_C - 2)
    n_dr = 2 * NA_WIN_R_MAX - 1
    per_dr = jnp.where(valid[None, None], rpb.astype(F32)[:, :, dc], NEG_BIG)

    def assemble(t_ref, o_ref):
        v = pl.program_id(1)
        o_ref[0, 0] = jnp.concatenate([t_ref[0, v + n] for n in range(win_r)], axis=1)

    return pl.pallas_call(
        assemble,
        grid=(rpb.shape[0], NA_WIN_R_MAX),
        in_specs=[pl.BlockSpec((1, n_dr, GRID_W, GRID_W), lambda h, v: (h, 0, 0, 0))],
        out_specs=pl.BlockSpec((1, 1, GRID_W, win_r * GRID_W), lambda h, v: (h, v, 0, 0)),
        out_shape=jax.ShapeDtypeStruct((rpb.shape[0], NA_WIN_R_MAX, GRID_W, win_r * GRID_W), F32),
        compiler_params=pltpu.CompilerParams(dimension_semantics=("arbitrary", "arbitrary")),
        name="na_bias_table",
    )(per_dr)


def na_attention(qkv, rpb, B, S, Lc, with_ctx):
    n_rows = S // GRID_W
    win_r = min(NA_WIN_R_MAX, n_rows)
    assert win_r == NA_WIN_R_MAX and (B * S) % Lc == 0
    HP = NA_HEADS // 2
    bias = na_bias_table(rpb, n_rows, win_r)
    ctx_blk0 = (B * S) // Lc

    def variant(r):
        return jnp.clip(r - win_r // 2, 0, n_rows - win_r) - r + NA_WIN_R_MAX - 1

    R = NA_ROWS_PER_STEP
    assert n_rows % R == 0
    steps = n_rows // R
    bias_specs = [pl.BlockSpec((2, 1, GRID_W, win_r * GRID_W),
                               lambda hp, b, r, i=i: (hp, variant(r * R + i), 0, 0)) for i in range(R)]
    y = pl.pallas_call(
        functools.partial(_na_kernel, n_rows=n_rows, win_r=win_r, rows_per_step=R),
        grid=(HP, B, steps),
        in_specs=[pl.BlockSpec((R * GRID_W, LANES), lambda hp, b, r: (b * steps + r, hp)),
                  pl.BlockSpec((S, LANES), lambda hp, b, r: (b, HP + hp)),
                  pl.BlockSpec((S, LANES), lambda hp, b, r: (b, 2 * HP + hp)),
                  pl.BlockSpec((Lc, LANES), lambda hp, b, r: (ctx_blk0 + b, HP + hp)),
                  pl.BlockSpec((Lc, LANES), lambda hp, b, r: (ctx_blk0 + b, 2 * HP + hp))] + bias_specs,
        out_specs=pl.BlockSpec((R * GRID_W, LANES), lambda hp, b, r: (b * steps + r, hp)),
        out_shape=jax.ShapeDtypeStruct((B * S, NA_W), F32),
        compiler_params=pltpu.CompilerParams(
            dimension_semantics=("arbitrary", "arbitrary", "arbitrary"),
            vmem_limit_bytes=V7X_VMEM_LIMIT_BYTES),
        name="na_attention",
    )(qkv, qkv, qkv, qkv, qkv, *([bias] * R))
    yc = None
    if with_ctx:
        yc = pl.pallas_call(
            _ctx_attn_kernel,
            grid=(HP, B),
            in_specs=[pl.BlockSpec((Lc, LANES), lambda hp, b: (ctx_blk0 + b, hp)),
                      pl.BlockSpec((Lc, LANES), lambda hp, b: (ctx_blk0 + b, HP + hp)),
                      pl.BlockSpec((Lc, LANES), lambda hp, b: (ctx_blk0 + b, 2 * HP + hp))],
            out_specs=pl.BlockSpec((Lc, LANES), lambda hp, b: (b, hp)),
            out_shape=jax.ShapeDtypeStruct((B * Lc, NA_W), F32),
            compiler_params=pltpu.CompilerParams(dimension_semantics=("arbitrary", "arbitrary")),
            name="ctx_attention",
        )(qkv, qkv, qkv)
    return y, yc


def na_mixer(z_all, B, S, Lc, q_norm, k_norm, rpb, with_ctx):
    assert (C_NAQ, C_NAK, C_NAV) == (0, NA_W, 2 * NA_W)
    scale = NA_HEAD_DIM ** -0.5
    gains = jnp.concatenate([jnp.tile(q_norm.astype(F32) * scale, NA_HEADS),
                             jnp.tile(k_norm.astype(F32), NA_HEADS)])[None]
    y, yc = na_attention(na_qkv(z_all, gains), rpb, B, S, Lc, with_ctx)
    return y, yc


def _log_sigmoid(x):
    return jnp.minimum(x, 0.0) - jnp.log(1.0 + jnp.exp(-jnp.abs(x)))


def _gla_kernel(*refs, reverse, rope, finish, n_chunks, nb):
    it = iter(refs)

    def take(count):
        return [next(it) for _ in range(count)]

    q_refs, k_refs, v_refs, gd_refs = take(nb), take(nb), take(nb), take(nb)
    w2_ref, gb_ref, s0_ref = take(3)
    if rope:
        cos_ref, sin_ref = take(2)
    if finish:
        yo_ref = next(it)
        r_refs = take(nb)
        g_ref = next(it)
    y_ref, sfin_ref, s_scr = take(3)

    C = GLA_CHUNK
    n = pl.program_id(1)
    chains = [(i, h) for i in range(nb) for h in range(GLA_HEADS)]

    @pl.when(n == 0)
    def _():
        s_scr[...] = s0_ref[...]

    def rows(refs_):
        return jnp.concatenate([r[...] for r in refs_], axis=0)

    ti = lax.broadcasted_iota(jnp.int32, (nb * C, nb * C), 0)
    tj = lax.broadcasted_iota(jnp.int32, (nb * C, nb * C), 1)
    same_seq = (ti // C) == (tj // C)
    tri = jnp.where(same_seq & ((tj >= ti) if reverse else (tj <= ti)), 1.0, 0.0).astype(BF16)
    ci = lax.broadcasted_iota(jnp.int32, (C, C), 0)
    cj = lax.broadcasted_iota(jnp.int32, (C, C), 1)
    incl = (cj >= ci) if reverse else (cj <= ci)

    pre = _mm(rows(gd_refs), w2_ref[...], NN_DIMS, 3) + gb_ref[...]
    la = _log_sigmoid(pre) * (1.0 / GLA_GATE_TAU)
    cs = _cumsum_rows(tri, la)
    ref_i = C // 2 - 1 if reverse else C // 2
    last_i = 0 if reverse else C - 1

    def per_seq_row(idx):
        return jnp.concatenate([jnp.broadcast_to(cs[i * C + idx:i * C + idx + 1, :], (C, GLA_QK))
                                for i in range(nb)], axis=0)

    b_ref = per_seq_row(ref_i)
    b_last = per_seq_row(last_i)

    q = rows(q_refs) * (GLA_DK ** -0.5)
    k = rows(k_refs)
    if rope:
        lane = lax.broadcasted_iota(jnp.int32, q.shape, 1)
        first = (lane % (GLA_DK // 2)) < GLA_DK // 4
        cos = jnp.concatenate([jnp.concatenate([cos_ref[...]] * GLA_HEADS, axis=1)] * nb, axis=0)
        sin = jnp.concatenate([jnp.concatenate([sin_ref[...]] * GLA_HEADS, axis=1)] * nb, axis=0)

        def rot(x):
            partner = jnp.where(first, pltpu.roll(x, GLA_QK - GLA_DK // 4, axis=1),
                                pltpu.roll(x, GLA_DK // 4, axis=1))
            return x * cos + partner * sin

        q = rot(q)
        k = rot(k)
    qi = q * jnp.exp(cs - b_ref)
    kj = k * jnp.exp(b_ref - cs)
    qe = q * jnp.exp(cs)
    ke = k * jnp.exp(b_last - cs)
    dec = jnp.exp(b_last)
    v_all = rows(v_refs)

    def hk(x, i, h):
        return x[i * C:(i + 1) * C, h * GLA_DK:(h + 1) * GLA_DK]

    def hv(x, i, h):
        return x[i * C:(i + 1) * C, h * GLA_DV:(h + 1) * GLA_DV]

    St = [s_scr[i, h] for i, h in chains]
    V = [hv(v_all, i, h) for i, h in chains]
    att = [jnp.where(incl, _mm(hk(qi, i, h), hk(kj, i, h), NT_DIMS), 0.0) for i, h in chains]
    y = [_mm(att[c], V[c], NN_DIMS) + _mm(hk(qe, i, h), St[c], NT_DIMS) for c, (i, h) in enumerate(chains)]
    kvt = [_mm(V[c], hk(ke, i, h), TN_DIMS) for c, (i, h) in enumerate(chains)]
    for c, (i, h) in enumerate(chains):
        s_scr[i, h] = St[c] * hk(dec, i, h)[0:1, :] + kvt[c]
    if finish:
        r = rows(r_refs)
        g = g_ref[...]
        for c, (i, h) in enumerate(chains):
            ys = y[c] + yo_ref[i, :, h * GLA_DV:(h + 1) * GLA_DV]
            ms = jnp.mean(ys * ys, axis=-1, keepdims=True)
            rh = hv(r, i, h)
            y[c] = ys * lax.rsqrt(ms + EPS) * g * (rh * jax.nn.sigmoid(rh))
    for i in range(nb):
        y_ref[i] = jnp.concatenate(y[i * GLA_HEADS:(i + 1) * GLA_HEADS], axis=1)

    @pl.when(n == n_chunks - 1)
    def _():
        sfin_ref[...] = s_scr[...]


def gla_rope_tables(seq_len):
    t = np.arange(seq_len)
    quarter = GLA_DK // 4
    inv = ROPE_BASE ** (-np.arange(0, 2 * quarter, 2, dtype=np.float64) / (2 * quarter))
    ar = (t // GRID_W)[:, None] * inv[None, :]
    ac = (t % GRID_W)[:, None] * inv[None, :]
    cos = np.concatenate([np.cos(ar), np.cos(ar), np.cos(ac), np.cos(ac)], axis=1)
    sin = np.concatenate([-np.sin(ar), np.sin(ar), -np.sin(ac), np.sin(ac)], axis=1)
    return jnp.asarray(cos, F32), jnp.asarray(sin, F32)


def gla_scan(z, w2pad, gate_b, s0, *, batch, seq_len, row0, reverse, rope, finish_with=None):
    N = seq_len // GLA_CHUNK
    blk0 = row0 // GLA_CHUNK
    off_q, off_k, off_v, off_r, off_gd = C_GQ, C_GK, C_GV, C_GR, C_GGD
    assert off_q % GLA_QK == 0 and off_k % GLA_QK == 0 and off_v % GLA_V == 0 and off_r % GLA_V == 0
    assert off_gd % LANES == 0

    def tb(n):
        return N - 1 - n if reverse else n

    def rb(b, n):
        return blk0 + b * N + tb(n)

    nb = GLA_BATCH_PER_STEP if batch % GLA_BATCH_PER_STEP == 0 else 1

    def tok_specs(width, col_off):
        return [pl.BlockSpec((GLA_CHUNK, width), lambda bb, n, i=i: (rb(bb * nb + i, n), col_off // width))
                for i in range(nb)]

    st_spec = pl.BlockSpec((nb, GLA_HEADS, GLA_DV, GLA_DK), lambda bb, n: (bb, 0, 0, 0))
    y_spec = pl.BlockSpec((nb, GLA_CHUNK, GLA_V), lambda bb, n: (bb, tb(n), 0))
    in_specs = (tok_specs(GLA_QK, off_q) + tok_specs(GLA_QK, off_k) + tok_specs(GLA_V, off_v)
                + tok_specs(LANES, off_gd)
                + [pl.BlockSpec((LANES, GLA_QK), lambda bb, n: (0, 0)),
                   pl.BlockSpec((1, GLA_QK), lambda bb, n: (0, 0)),
                   st_spec])
    args = [z] * (4 * nb) + [w2pad, gate_b, s0]
    if rope:
        cos, sin = gla_rope_tables(seq_len)
        in_specs += [pl.BlockSpec((GLA_CHUNK, GLA_DK), lambda bb, n: (tb(n), 0))] * 2
        args += [cos, sin]
    if finish_with is not None:
        y_other, norm_g = finish_with
        in_specs += [y_spec] + tok_specs(GLA_V, off_r) + [pl.BlockSpec((1, GLA_DV), lambda bb, n: (0, 0))]
        args += [y_other.reshape(batch, seq_len, GLA_V)] + [z] * nb + [norm_g]
    y, s_fin = pl.pallas_call(
        functools.partial(_gla_kernel, reverse=reverse, rope=rope, finish=finish_with is not None,
                          n_chunks=N, nb=nb),
        grid=(batch // nb, N),
        in_specs=in_specs,
        out_specs=[y_spec, st_spec],
        out_shape=[jax.ShapeDtypeStruct((batch, seq_len, GLA_V), F32),
                   jax.ShapeDtypeStruct((batch, GLA_HEADS, GLA_DV, GLA_DK), F32)],
        scratch_shapes=[pltpu.VMEM((nb, GLA_HEADS, GLA_DV, GLA_DK), F32)],
        compiler_params=pltpu.CompilerParams(
            dimension_semantics=("arbitrary", "arbitrary"),
            vmem_limit_bytes=V7X_VMEM_LIMIT_BYTES),
        name="gla_scan_rev" if reverse else "gla_scan",
    )(*args)
    return y.reshape(batch * seq_len, GLA_V), s_fin


def gla_mixer(z_all, B, S, Lc, gate_w2, gate_b, norm_g, with_ctx):
    s0 = jnp.zeros((B, GLA_HEADS, GLA_DV, GLA_DK), F32)
    g = norm_g.astype(F32)[None]
    w2 = [jnp.zeros((LANES, GLA_QK), F32).at[d * GLA_GATE_RANK:(d + 1) * GLA_GATE_RANK].set(gate_w2[d])
          for d in range(2)]
    gb = [gate_b[d].astype(F32)[None] for d in range(2)]
    ctx = dict(batch=B, seq_len=Lc, row0=B * S, rope=False)
    lat = dict(batch=B, seq_len=S, row0=0, rope=True)
    yc_f, sc_f = gla_scan(z_all, w2[0], gb[0], s0, reverse=False, **ctx)
    yc, sc_b = gla_scan(z_all, w2[1], gb[1], s0, reverse=True, finish_with=(yc_f, g), **ctx)
    yl_f, _ = gla_scan(z_all, w2[0], gb[0], sc_f, reverse=False, **lat)
    yl, _ = gla_scan(z_all, w2[1], gb[1], sc_b, reverse=True, finish_with=(yl_f, g), **lat)
    return yl, (yc if with_ctx else None)


RW_PREP_TM = 256
RW_HALO = 8
RW_N_PIECES = 5


def _heads_sum(x):
    return jnp.concatenate([_head_sum(x[:, j * LANES:(j + 1) * LANES]) for j in range(x.shape[1] // LANES)],
                           axis=1)


def _rw_prep_kernel(*refs, tm, n_latent_tiles, tiles_per_seq):
    P = RW_N_PIECES
    z_refs, prev_refs, next_refs, mu_refs = refs[0:P], refs[P:2 * P], refs[2 * P:3 * P], refs[3 * P:4 * P]
    kk_ref, ka_ref, rk_ref, w0_ref, a0_ref, w2_ref, a2_ref, g2_ref = refs[4 * P:4 * P + 8]
    (r_o, v_o, kap_o, lw0_o, kd0_o, b0_o, lw1_o, kd1_o, b1_o, bonus_o, g_o) = refs[4 * P + 8:]
    i = pl.program_id(0)
    latent = i < n_latent_tiles
    first = jnp.logical_or(jnp.logical_not(latent), i % tiles_per_seq == 0)
    last = jnp.logical_or(jnp.logical_not(latent), i % tiles_per_seq == tiles_per_seq - 1)

    def shifted(j):
        z = z_refs[j][...]
        row = lax.broadcasted_iota(jnp.int32, z.shape, 0)
        prev_row = jnp.where(first, 0.0, prev_refs[j][RW_HALO - 1:RW_HALO, :])
        next_row = jnp.where(last, 0.0, next_refs[j][0:1, :])
        prev = jnp.where(row == 0, prev_row, pltpu.roll(z, 1, axis=0))
        nxt = jnp.where(row == tm - 1, next_row, pltpu.roll(z, tm - 1, axis=0))
        return z + mu_refs[j][...] * (0.5 * (prev + nxt) - z)

    r, k, v, lr, gd = (shifted(j) for j in range(P))
    kkr = k * kk_ref[...]
    kap = kkr * lax.rsqrt(_heads_sum(kkr * kkr) + EPS)
    r_o[...] = r
    v_o[...] = v
    kap_o[...] = kap
    bonus_o[...] = _heads_sum(r * k * rk_ref[...]) * v
    g_o[...] = _mm(jax.nn.sigmoid(gd), g2_ref[...], NN_DIMS, 3)
    ka = ka_ref[...]
    for d, (lw_o, kd_o, b_o) in enumerate(((lw0_o, kd0_o, b0_o), (lw1_o, kd1_o, b1_o))):
        wd = lr[:, d * LANES:(d + 1) * LANES]
        ad = lr[:, (2 + d) * LANES:(3 + d) * LANES]
        w_log = _log_sigmoid(w0_ref[d:d + 1, :] + _mm(jnp.tanh(wd), w2_ref[d], NN_DIMS, 3)) - 0.5
        lw_o[...] = -jnp.exp(w_log)
        a = jax.nn.sigmoid(a0_ref[d:d + 1, :] + _mm(ad, a2_ref[d], NN_DIMS, 3))
        kd_o[...] = k * (1.0 + (a - 1.0) * ka)
        b_o[...] = kap * a


def rwkv_prep(z_all, n_latent_rows, seq_len, mu_p, k_k, k_a, r_k, w0, a0, w2p, a2p, g2):
    M = z_all.shape[0]
    tm = RW_PREP_TM
    assert M % tm == 0 and n_latent_rows % tm == 0 and seq_len % tm == 0
    widths = (RW_W, RW_W, RW_W, RW_LR_W, RW_GATE_RANK)
    offs = (C_RR, C_RK, C_RV, C_RLR, C_RGD)
    n_halo_blocks = M // RW_HALO
    z_specs, prev_specs, next_specs, mu_specs = [], [], [], []
    for w, o in zip(widths, offs):
        assert o % w == 0
        cb = o // w
        z_specs.append(pl.BlockSpec((tm, w), lambda i, cb=cb: (i, cb)))
        prev_specs.append(pl.BlockSpec(
            (RW_HALO, w), lambda i, cb=cb: (jnp.maximum(i * (tm // RW_HALO) - 1, 0), cb)))
        next_specs.append(pl.BlockSpec(
            (RW_HALO, w), lambda i, cb=cb: (jnp.minimum((i + 1) * (tm // RW_HALO), n_halo_blocks - 1), cb)))
        mu_specs.append(pl.BlockSpec((1, w), lambda i: (0, 0)))

    def full(shape):
        return pl.BlockSpec(shape, lambda i: (0,) * len(shape))

    par_specs = [full((1, RW_W))] * 3 + [full((2, RW_W))] * 2 + [full((2, LANES, RW_W))] * 2 \
        + [full((RW_GATE_RANK, RW_W))]
    out_spec = pl.BlockSpec((tm, RW_W), lambda i: (i, 0))
    return pl.pallas_call(
        functools.partial(_rw_prep_kernel, tm=tm, n_latent_tiles=n_latent_rows // tm,
                          tiles_per_seq=seq_len // tm),
        grid=(M // tm,),
        in_specs=z_specs + prev_specs + next_specs + mu_specs + par_specs,
        out_specs=[out_spec] * 11,
        out_shape=[jax.ShapeDtypeStruct((M, RW_W), F32)] * 11,
        compiler_params=pltpu.CompilerParams(
            dimension_semantics=("arbitrary",), vmem_limit_bytes=V7X_VMEM_LIMIT_BYTES),
        name="rwkv_prep",
    )(*([z_all] * 15), *mu_p, k_k, k_a, r_k, w0, a0, w2p, a2p, g2)


def _rw_chunk_kernel(*refs, reverse, n_par, n_chunks, passes, finish, nb):
    it = iter(refs)

    def side_by_side():
        return jnp.concatenate([next(it)[...] for _ in range(nb)], axis=1)

    r_in, v_all, kap_in, lw_all, kd_all, bet_all = (side_by_side() for _ in range(6))
    s0_ref = next(it)
    if finish:
        yo_ref = next(it)
        bonus, gate = side_by_side(), side_by_side()
        gamma = jnp.concatenate([next(it)[...]] * nb, axis=1)
        beta = jnp.concatenate([next(it)[...]] * nb, axis=1)
    y_ref, sfin_ref, s_scr = next(it), next(it), next(it)

    C = RW_CHUNK
    n = pl.program_id(2)
    P = range(nb * n_par)

    @pl.when(n == 0)
    def _():
        s_scr[...] = s0_ref[:, 0]

    row = lax.broadcasted_iota(jnp.int32, (C, LANES), 0)
    lane = lax.broadcasted_iota(jnp.int32, (C, LANES), 1)
    col = lane % RW_HEAD
    head0 = lane < RW_HEAD
    strict = (col > row) if reverse else (col < row)
    incl = (col >= row) if reverse else (col <= row)
    ti = lax.broadcasted_iota(jnp.int32, (C, C), 0)
    tj = lax.broadcasted_iota(jnp.int32, (C, C), 1)
    tri = jnp.where((tj >= ti) if reverse else (tj <= ti), 1.0, 0.0).astype(BF16)
    eye = jnp.where(row == col, 1.0, 0.0)
    brow = lax.broadcasted_iota(jnp.int32, (LANES, LANES), 0) // RW_HEAD
    bcol = lax.broadcasted_iota(jnp.int32, (LANES, LANES), 1) // RW_HEAD
    same_head = brow == bcol

    def sl(x, p):
        return x[:, p * LANES:(p + 1) * LANES]

    def bd(x):
        return jnp.concatenate([jnp.where(head0, x, 0.0), jnp.where(head0, 0.0, x)], axis=0)

    def pp(a_pair, b_pair):
        return _mm(a_pair, bd(b_pair), NN_DIMS, passes)

    cs_all = _cumsum_rows(tri, lw_all)
    tot_all = cs_all[0:1, :] if reverse else cs_all[C - 1:C, :]
    inv_all = jnp.exp(-cs_all)
    fin_all = jnp.exp(tot_all - cs_all)
    kap_t = kap_in * jnp.exp(cs_all - lw_all)
    r_t = r_in * jnp.exp(cs_all)
    k_t = kd_all * inv_all
    b_t = bet_all * inv_all
    k_h = kd_all * fin_all
    b_h = bet_all * fin_all
    dec = jnp.exp(tot_all)

    S = [s_scr[p // n_par, p % n_par] for p in P]
    V = [sl(v_all, p) for p in P]
    lhs = [jnp.concatenate([sl(kap_t, p), sl(r_t, p)], axis=0) for p in P]
    a_k = [_mm(lhs[p], bd(sl(k_t, p)), NT_DIMS, passes) for p in P]
    a_b = [_mm(lhs[p], bd(sl(b_t, p)), NT_DIMS, passes) for p in P]
    s0_both = [_mm(lhs[p], S[p], NT_DIMS, passes) for p in P]
    lmat = [jnp.where(strict, a_b[p][:C], 0.0) for p in P]
    power = [-lmat[p] for p in P]
    t_inv = [eye + power[p] for p in P]
    s = 2
    while s < C:
        power = [pp(power[p], power[p]) for p in P]
        t_inv = [t_inv[p] + pp(t_inv[p], power[p]) for p in P]
        s *= 2
    x = [s0_both[p][:C] + pp(jnp.where(strict, a_k[p][:C], 0.0), V[p]) for p in P]
    u = [pp(t_inv[p], x[p]) for p in P]
    y = [s0_both[p][C:] + pp(jnp.where(incl, a_k[p][C:], 0.0), V[p])
         - pp(jnp.where(incl, a_b[p][C:], 0.0), u[p]) for p in P]
    upd = [_mm(V[p], sl(k_h, p), TN_DIMS, passes) - _mm(u[p], sl(b_h, p), TN_DIMS, passes) for p in P]
    for p in P:
        s_scr[p // n_par, p % n_par] = S[p] * sl(dec, p) + jnp.where(same_head, upd[p], 0.0)
    width = n_par * LANES
    if finish:
        yo = jnp.concatenate([yo_ref[i] for i in range(nb)], axis=1)
        ys = jnp.concatenate(y, axis=1) + yo
        cen = ys - _heads_sum(ys) * (1.0 / RW_HEAD)
        var = _heads_sum(cen * cen) * (1.0 / RW_HEAD)
        out = (cen * lax.rsqrt(var + RW_GN_EPS) * gamma + beta + bonus) * gate
    else:
        out = jnp.concatenate(y, axis=1)
    for i in range(nb):
        y_ref[i] = out[:, i * width:(i + 1) * width]

    @pl.when(n == n_chunks - 1)
    def _():
        sfin_ref[:, 0] = s_scr[...]


def rwkv_chunk_scan(r, v, kap, lw, kd, bet, s0, *, batch, seq_len, row0, reverse, finish_with=None):
    n_par = RW_PAIRS_PER_STEP
    nb = RW_BATCH_PER_STEP if batch % RW_BATCH_PER_STEP == 0 else 1
    W = r.shape[1]
    G = W // (n_par * LANES)
    N = seq_len // RW_CHUNK
    blk0 = row0 // RW_CHUNK

    def tb(n):
        return N - 1 - n if reverse else n

    def seq_in():
        return [pl.BlockSpec((RW_CHUNK, n_par * LANES),
                             lambda bb, g, n, i=i: (blk0 + (bb * nb + i) * N + tb(n), g)) for i in range(nb)]

    y_spec = pl.BlockSpec((nb, RW_CHUNK, n_par * LANES), lambda bb, g, n: (bb, tb(n), g))
    st_spec = pl.BlockSpec((nb, 1, n_par, LANES, LANES), lambda bb, g, n: (bb, g, 0, 0, 0))
    in_specs = seq_in() * 6 + [st_spec]
    args = [a for a in (r, v, kap, lw, kd, bet) for _ in range(nb)] + [s0]
    if finish_with is not None:
        y_other, bonus, gate, gamma, beta = finish_with
        vec_spec = pl.BlockSpec((1, n_par * LANES), lambda bb, g, n: (0, g))
        in_specs += [y_spec] + seq_in() * 2 + [vec_spec, vec_spec]
        args += [y_other.reshape(batch, seq_len, W)] + [bonus] * nb + [gate] * nb + [gamma, beta]
    y, s_fin = pl.pallas_call(
        functools.partial(_rw_chunk_kernel, reverse=reverse, n_par=n_par, n_chunks=N, passes=RW_PASSES,
                          finish=finish_with is not None, nb=nb),
        grid=(batch // nb, G, N),
        in_specs=in_specs,
        out_specs=[y_spec, st_spec],
        out_shape=[jax.ShapeDtypeStruct((batch, seq_len, W), F32),
                   jax.ShapeDtypeStruct((batch, G, n_par, LANES, LANES), F32)],
        scratch_shapes=[pltpu.VMEM((nb, n_par, LANES, LANES), F32)],
        compiler_params=pltpu.CompilerParams(
            dimension_semantics=("arbitrary", "arbitrary", "arbitrary"),
            vmem_limit_bytes=V7X_VMEM_LIMIT_BYTES),
        name="rwkv_chunk_scan_rev" if reverse else "rwkv_chunk_scan",
    )(*args)
    return y.reshape(batch * seq_len, W), s_fin


def _pad_rows(w, rows):
    return jnp.pad(w.astype(F32), ((0, 0), (0, rows - w.shape[1]), (0, 0)))


def rwkv_mixer(z_all, B, S, Lc, mu, w0, w2, a0, a2, g2, k_k, k_a, r_k, gn_g, gn_b, with_ctx):
    assert Lc == RW_PREP_TM
    mu = mu.astype(F32)
    o = np.cumsum((0,) + RW_SPLITS)
    lr_pad = jnp.zeros((LANES - RW_DECAY_RANK,), F32)
    mu_lr = jnp.concatenate([mu[o[3]:o[3] + RW_DECAY_RANK], lr_pad, mu[o[3] + RW_DECAY_RANK:o[4]], lr_pad,
                             mu[o[4]:o[4] + RW_A_RANK], lr_pad, mu[o[4] + RW_A_RANK:o[5]], lr_pad])
    mu_p = [mu[o[0]:o[1]][None], mu[o[1]:o[2]][None], mu[o[2]:o[3]][None], mu_lr[None], mu[o[5]:o[6]][None]]
    row = lambda t: t.astype(F32).reshape(1, RW_W)
    prep = rwkv_prep(z_all, B * S, S, mu_p, row(k_k), row(k_a), row(r_k), w0.astype(F32), a0.astype(F32),
                     _pad_rows(w2, LANES), _pad_rows(a2, LANES), g2.astype(F32))
    r, v, kap, lw0, kd0, b0, lw1, kd1, b1, bonus, gate = prep
    fin = (bonus, gate, row(gn_g), row(gn_b))
    s0 = jnp.zeros((B, RW_W // (RW_PAIRS_PER_STEP * LANES), RW_PAIRS_PER_STEP, LANES, LANES), F32)
    ctx = dict(batch=B, seq_len=Lc, row0=B * S)
    lat = dict(batch=B, seq_len=S, row0=0)
    yc_f, sc_f = rwkv_chunk_scan(r, v, kap, lw0, kd0, b0, s0, reverse=False, **ctx)
    yc, sc_b = rwkv_chunk_scan(r, v, kap, lw1, kd1, b1, s0, reverse=True, finish_with=(yc_f,) + fin, **ctx)
    yl_f, _ = rwkv_chunk_scan(r, v, kap, lw0, kd0, b0, sc_f, reverse=False, **lat)
    yl, _ = rwkv_chunk_scan(r, v, kap, lw1, kd1, b1, sc_b, reverse=True, finish_with=(yl_f,) + fin, **lat)
    return yl, (yc if with_ctx else None)


def relayout_w_in(w):
    o = IN_OFFSETS
    rw = o[8] + np.cumsum((0,) + RW_SPLITS)
    w = w.astype(BF16)
    zeros = lambda n: jnp.zeros((w.shape[0], n), w.dtype)
    seg = lambda lo, n: w[:, lo:lo + n]
    lr_pad = LANES - RW_DECAY_RANK
    pieces = [
        seg(o[0], 3 * NA_W),
        seg(o[5], GLA_V), seg(o[6], GLA_V),
        seg(rw[0], 3 * RW_W),
        seg(o[9], N_BRANCH * D_MODEL),
        seg(o[3], GLA_QK), seg(o[4], GLA_QK),
        seg(rw[3], RW_DECAY_RANK), zeros(lr_pad), seg(rw[3] + RW_DECAY_RANK, RW_DECAY_RANK), zeros(lr_pad),
        seg(rw[4], RW_A_RANK), zeros(lr_pad), seg(rw[4] + RW_A_RANK, RW_A_RANK), zeros(lr_pad),
        seg(rw[5], RW_GATE_RANK),
        seg(o[7], 2 * GLA_GATE_RANK),
    ]
    used = sum(x.shape[1] for x in pieces)
    assert used == C_GGD + 2 * GLA_GATE_RANK
    return jnp.concatenate(pieces + [zeros(Z_W - used)], axis=1).astype(BF16)


def token_mixers(x, ctx, mod, p, B, S, Lc, with_ctx):
    h_all = prenorm(x, ctx, p['norm1_g'], mod, seq_len=S)
    z_all = matmul(h_all, relayout_w_in(p['w_in']), tm=1024, tn=1024)
    y_na, yc_na = na_mixer(z_all, B, S, Lc, p['na_q_norm'], p['na_k_norm'], p['na_rpb'], with_ctx)
    y_gla, yc_gla = gla_mixer(z_all, B, S, Lc, p['gla_gate_w2'], p['gla_gate_b'], p['gla_norm_g'], with_ctx)
    y_rw, yc_rw = rwkv_mixer(z_all, B, S, Lc, p['rw_mu'], p['rw_w0'], p['rw_w2'], p['rw_a0'], p['rw_a2'],
                             p['rw_g2'], p['rw_k_k'], p['rw_k_a'], p['rw_r_k'], p['rw_gn_g'],
                             p['rw_gn_b'], with_ctx)
    merged = merge_branches((y_na, y_gla, y_rw), z_all, 0, p['w_branch'])
    x = residual_matmul(merged, p['w_out'], x, mod, gate_row=2, rows_per_cond=S)
    if with_ctx:
        merged_c = merge_branches((yc_na, yc_gla, yc_rw), z_all, B * S, p['w_branch'])
        ctx = residual_matmul(merged_c, p['w_out'], ctx, mod, gate_row=2, fixed_row=B)
    return x, ctx


def moe_sublayer(x, mod, p, *, rows_per_cond=None, fixed_row=None):
    h2, comb_t = prenorm_router(x, p['norm2_g'], mod, p['router_w_t'], p['router_bias'],
                                rows_per_cond=rows_per_cond, fixed_row=fixed_row)
    return moe_experts(h2, comb_t.T, p['moe_w_gate'], p['moe_w_up'], p['moe_w_down'], x, mod,
                       rows_per_cond=rows_per_cond, fixed_row=fixed_row)


def kernel(x, c, ctx, c_ctx, ada_w, ada_b, norm1_g, norm2_g, w_in, na_q_norm, na_k_norm, na_rpb,
           gla_gate_w2, gla_gate_b, gla_norm_g, rw_mu, rw_w0, rw_w2, rw_a0, rw_a2, rw_g2,
           rw_k_k, rw_k_a, rw_r_k, rw_gn_g, rw_gn_b, w_branch, w_out, router_w, router_bias,
           moe_w_gate, moe_w_up, moe_w_down):
    B, S, D = x.shape
    Lc = ctx.shape[1]
    assert B + 1 <= N_COND and S % 512 == 0 and (B * Lc) % 512 == 0 and Lc % ROW_TM == 0
    c_rows = jnp.concatenate([c, c_ctx[None], jnp.zeros((N_COND - B - 1, D), c.dtype)], axis=0).astype(F32)
    mods = ada_modulation(c_rows, ada_w, ada_b).reshape(DEPTH, N_COND, N_MOD, D)
    x = x.reshape(B * S, D)
    ctx = ctx.reshape(B * Lc, D)
    router_w_t = router_w.astype(F32).T
    router_b = router_bias.astype(F32).reshape(N_EXPERTS, 1)
    for l in range(DEPTH):
        with_ctx = l < DEPTH - 1
        p = {
            'w_in': w_in[l], 'na_q_norm': na_q_norm[l], 'na_k_norm': na_k_norm[l], 'na_rpb': na_rpb[l],
            'gla_gate_w2': gla_gate_w2[l], 'gla_gate_b': gla_gate_b[l], 'gla_norm_g': gla_norm_g[l],
            'rw_mu': rw_mu[l], 'rw_w0': rw_w0[l], 'rw_w2': rw_w2[l], 'rw_a0': rw_a0[l], 'rw_a2': rw_a2[l],
            'rw_g2': rw_g2[l], 'rw_k_k': rw_k_k[l], 'rw_k_a': rw_k_a[l], 'rw_r_k': rw_r_k[l],
            'rw_gn_g': rw_gn_g[l], 'rw_gn_b': rw_gn_b[l], 'w_branch': w_branch[l], 'w_out': w_out[l],
            'norm1_g': norm1_g[l].astype(F32)[None], 'norm2_g': norm2_g[l].astype(F32)[None],
            'router_w_t': router_w_t, 'router_bias': router_b,
            'moe_w_gate': moe_w_gate[l].astype(BF16), 'moe_w_up': moe_w_up[l].astype(BF16),
            'moe_w_down': moe_w_down[l].astype(BF16),
        }
        x, ctx = token_mixers(x, ctx, mods[l], p, B, S, Lc, with_ctx)
        x = moe_sublayer(x, mods[l], p, rows_per_cond=S)
        if with_ctx:
            ctx = moe_sublayer(ctx, mods[l], p, fixed_row=B)
    return x.reshape(B, S, D)
```

```python
import functools

import jax
import jax.numpy as jnp
from jax import lax
import numpy as np
from jax.experimental import pallas as pl
from jax.experimental.pallas import tpu as pltpu

D_MODEL = 2048
DEPTH = 2
GRID_W = 64

NA_HEADS = 16
NA_HEAD_DIM = 64
NA_WIN_R_MAX = 8
NA_WIN_C = 16
NA_W = NA_HEADS * NA_HEAD_DIM
NA_ROWS_PER_STEP = 8

GLA_HEADS = 4
GLA_DK = 128
GLA_DV = 256
GLA_QK = GLA_HEADS * GLA_DK
GLA_V = GLA_HEADS * GLA_DV
GLA_GATE_RANK = 16
GLA_GATE_TAU = 16.0
GLA_CHUNK = 64
GLA_BATCH_PER_STEP = 4

RW_HEADS = 16
RW_HEAD = 64
RW_W = RW_HEADS * RW_HEAD
RW_DECAY_RANK = 96
RW_A_RANK = 96
RW_GATE_RANK = 256
RW_SPLITS = (RW_W, RW_W, RW_W, 2 * RW_DECAY_RANK, 2 * RW_A_RANK, RW_GATE_RANK)
RW_IN = sum(RW_SPLITS)
RW_GN_EPS = 64e-5
RW_CHUNK = 64
RW_PASSES = 1
RW_PAIRS_PER_STEP = 8
RW_BATCH_PER_STEP = 4

N_BRANCH = 3
BRANCH_W = 1024
IN_SPLITS = (NA_W, NA_W, NA_W, GLA_QK, GLA_QK, GLA_V, GLA_V, 2 * GLA_GATE_RANK, RW_IN, N_BRANCH * D_MODEL)
D_IN = sum(IN_SPLITS)
IN_OFFSETS = tuple(int(o) for o in np.cumsum((0,) + IN_SPLITS[:-1]))

LANES = 128
RW_LR_W = 4 * LANES
C_NAQ, C_NAK, C_NAV = 0, 1024, 2048
C_GV, C_GR = 3072, 4096
C_RR, C_RK, C_RV = 5120, 6144, 7168
C_GATE = 8192
C_GQ, C_GK = 14336, 14848
C_RLR, C_RGD, C_GGD = 15360, 15872, 16128
Z_W = 16384

N_EXPERTS = 16
N_GROUPS = 4
EXPERTS_PER_GROUP = N_EXPERTS // N_GROUPS
TOP_K = 2
D_EXPERT = 512

ROPE_BASE = 10000.0
EPS = 1e-6
F32 = jnp.float32
BF16 = jnp.bfloat16

V7X_VMEM_LIMIT_BYTES = 48 * 1024 * 1024

NEG_BIG = -1e30
NT_DIMS = (((1,), (1,)), ((), ()))
TN_DIMS = (((0,), (0,)), ((), ()))
NN_DIMS = (((1,), (0,)), ((), ()))


def _split_bf16(x):
    hi = x.astype(BF16)
    lo = (x - hi.astype(F32)).astype(BF16)
    return hi, lo


def _mm(a, b, dims=NN_DIMS, passes=1):
    if passes == 1:
        return lax.dot_general(a.astype(BF16), b.astype(BF16), dims, preferred_element_type=F32)
    a_hi, a_lo = _split_bf16(a)
    b_hi, b_lo = _split_bf16(b)
    out = lax.dot_general(a_hi, b_hi, dims, preferred_element_type=F32)
    out = out + lax.dot_general(a_hi, b_lo, dims, preferred_element_type=F32)
    return out + lax.dot_general(a_lo, b_hi, dims, preferred_element_type=F32)


def _cumsum_rows(tri, x):
    hi, lo = _split_bf16(x)
    return jnp.dot(tri, hi, preferred_element_type=F32) + jnp.dot(tri, lo, preferred_element_type=F32)


def _mm_kernel(a_ref, b_ref, o_ref):
    o_ref[...] = jnp.dot(a_ref[...], b_ref[...], preferred_element_type=F32).astype(o_ref.dtype)


def matmul(a, b, *, tm=512, tn=512, out_dtype=F32):
    M, K = a.shape
    _, N = b.shape
    assert M % tm == 0 and N % tn == 0, (M, N, tm, tn)
    return pl.pallas_call(
        _mm_kernel,
        grid=(N // tn, M // tm),
        in_specs=[pl.BlockSpec((tm, K), lambda n, m: (m, 0)),
                  pl.BlockSpec((K, tn), lambda n, m: (0, n))],
        out_specs=pl.BlockSpec((tm, tn), lambda n, m: (m, n)),
        out_shape=jax.ShapeDtypeStruct((M, N), out_dtype),
        compiler_params=pltpu.CompilerParams(
            dimension_semantics=("arbitrary", "arbitrary"),
            vmem_limit_bytes=V7X_VMEM_LIMIT_BYTES),
        name="matmul",
    )(a.astype(BF16), b.astype(BF16))


def _moe_kernel(h_ref, comb_ref, wg_ref, wu_ref, wd_ref, x_ref, mod_ref, o_ref, acc_ref):
    e = pl.program_id(1)

    @pl.when(e == 0)
    def _():
        acc_ref[...] = jnp.zeros_like(acc_ref)

    h = h_ref[...]
    g = jnp.dot(h, wg_ref[0], preferred_element_type=F32)
    u = jnp.dot(h, wu_ref[0], preferred_element_type=F32)
    comb = comb_ref[...]
    lane = lax.broadcasted_iota(jnp.int32, comb.shape, 1)
    ce = jnp.sum(jnp.where(lane == e, comb, 0.0), axis=1, keepdims=True)
    act = (g * jax.nn.sigmoid(g)) * u * ce
    acc_ref[...] += jnp.dot(act.astype(BF16), wd_ref[0], preferred_element_type=F32)

    @pl.when(e == N_EXPERTS - 1)
    def _():
        o_ref[...] = x_ref[...] + mod_ref[0, N_MOD - 1:N_MOD, :] * acc_ref[...]


def moe_experts(h, comb, w_gate, w_up, w_down, x, mod, *, rows_per_cond=None, fixed_row=None, tm=512):
    M, D = h.shape
    assert M % tm == 0
    cond = _cond_of_tile(rows_per_cond, tm, fixed_row)
    return pl.pallas_call(
        _moe_kernel,
        grid=(M // tm, N_EXPERTS),
        in_specs=[pl.BlockSpec((tm, D), lambda m, e: (m, 0)),
                  pl.BlockSpec((tm, N_EXPERTS), lambda m, e: (m, 0)),
                  pl.BlockSpec((1, D, D_EXPERT), lambda m, e: (e, 0, 0)),
                  pl.BlockSpec((1, D, D_EXPERT), lambda m, e: (e, 0, 0)),
                  pl.BlockSpec((1, D_EXPERT, D), lambda m, e: (e, 0, 0)),
                  pl.BlockSpec((tm, D), lambda m, e: (m, 0)),
                  pl.BlockSpec((1, N_MOD, D), lambda m, e: (cond(m), 0, 0))],
        out_specs=pl.BlockSpec((tm, D), lambda m, e: (m, 0)),
        out_shape=jax.ShapeDtypeStruct((M, D), F32),
        scratch_shapes=[pltpu.VMEM((tm, D), F32)],
        compiler_params=pltpu.CompilerParams(
            dimension_semantics=("arbitrary", "arbitrary"),
            vmem_limit_bytes=V7X_VMEM_LIMIT_BYTES),
        name="moe_experts",
    )(h, comb, w_gate, w_up, w_down, x, mod)


N_COND = 8
N_MOD = 6
ROW_TM = 256


def _ada_kernel(c_ref, w_ref, b_ref, o_ref):
    c = c_ref[...]
    o_ref[0] = _mm(c * jax.nn.sigmoid(c), w_ref[0], NN_DIMS, 3) + b_ref[0]


def ada_modulation(c_rows, ada_w, ada_b, *, tn=2048):
    L, D, N = ada_w.shape
    return pl.pallas_call(
        _ada_kernel,
        grid=(L, N // tn),
        in_specs=[pl.BlockSpec((N_COND, D), lambda l, n: (0, 0)),
                  pl.BlockSpec((1, D, tn), lambda l, n: (l, 0, n)),
                  pl.BlockSpec((1, 1, tn), lambda l, n: (l, 0, n))],
        out_specs=pl.BlockSpec((1, N_COND, tn), lambda l, n: (l, 0, n)),
        out_shape=jax.ShapeDtypeStruct((L, N_COND, N), F32),
        compiler_params=pltpu.CompilerParams(
            dimension_semantics=("arbitrary", "arbitrary"), vmem_limit_bytes=V7X_VMEM_LIMIT_BYTES),
        name="ada_modulation",
    )(c_rows, ada_w, ada_b.reshape(L, 1, N))


def _norm_mod(x, g, shift, scale):
    y = x * lax.rsqrt(jnp.mean(x * x, axis=-1, keepdims=True) + EPS)
    return y * g * (1.0 + scale) + shift


def _prenorm_kernel(x_ref, c_ref, g_ref, mod_ref, o_ref, *, n_latent_tiles):
    def emit(src_ref):
        o_ref[...] = _norm_mod(src_ref[...], g_ref[...], mod_ref[0, 0:1, :], mod_ref[0, 1:2, :]).astype(o_ref.dtype)

    @pl.when(pl.program_id(0) < n_latent_tiles)
    def _():
        emit(x_ref)

    @pl.when(pl.program_id(0) >= n_latent_tiles)
    def _():
        emit(c_ref)


def _top2_sum(a, b, c, d):
    hi1, lo1 = jnp.maximum(a, b), jnp.minimum(a, b)
    hi2, lo2 = jnp.maximum(c, d), jnp.minimum(c, d)
    return jnp.maximum(hi1, hi2) + jnp.maximum(jnp.minimum(hi1, hi2), jnp.maximum(lo1, lo2))


def _prenorm_router_kernel(x_ref, g_ref, mod_ref, rw_ref, rb_ref, o_ref, comb_ref):
    h = _norm_mod(x_ref[...], g_ref[...], mod_ref[0, 3:4, :], mod_ref[0, 4:5, :])
    o_ref[...] = h.astype(o_ref.dtype)
    s = jax.nn.sigmoid(_mm(rw_ref[...], h, NT_DIMS, 3))
    biased = s + rb_ref[...]
    rows = [biased[e:e + 1, :] for e in range(N_EXPERTS)]
    G = EXPERTS_PER_GROUP
    score = [_top2_sum(*rows[g * G:(g + 1) * G]) for g in range(N_GROUPS)]
    picked = []
    for e in range(N_EXPERTS):
        g = e // G
        ok = None
        for g2 in range(N_GROUPS):
            if g2 != g:
                t = (score[g] > score[g2]) if g2 < g else (score[g] >= score[g2])
                ok = t if ok is None else jnp.logical_and(ok, t)
        rank = 0.0
        for e2 in range(g * G, (g + 1) * G):
            if e2 != e:
                ahead = (rows[e2] >= rows[e]) if e2 < e else (rows[e2] > rows[e])
                rank = rank + jnp.where(ahead, 1.0, 0.0)
        picked.append(jnp.where(jnp.logical_and(ok, rank < TOP_K), s[e:e + 1, :], 0.0))
    w = jnp.concatenate(picked, axis=0)
    comb_ref[...] = w / jnp.sum(w, axis=0, keepdims=True)


def _cond_of_tile(rows_per_cond, tile_rows, fixed_row):
    if fixed_row is not None:
        return lambda m: fixed_row
    assert rows_per_cond % tile_rows == 0
    return lambda m: m // (rows_per_cond // tile_rows)


def prenorm(x, ctx, gain, mod, *, seq_len):
    n_lat, n_ctx = x.shape[0] // ROW_TM, ctx.shape[0] // ROW_TM
    n_batch = x.shape[0] // seq_len
    cond = _cond_of_tile(seq_len, ROW_TM, None)
    return pl.pallas_call(
        functools.partial(_prenorm_kernel, n_latent_tiles=n_lat),
        grid=(n_lat + n_ctx,),
        in_specs=[pl.BlockSpec((ROW_TM, D_MODEL), lambda i: (jnp.minimum(i, n_lat - 1), 0)),
                  pl.BlockSpec((ROW_TM, D_MODEL), lambda i: (jnp.maximum(i - n_lat, 0), 0)),
                  pl.BlockSpec((1, D_MODEL), lambda i: (0, 0)),
                  pl.BlockSpec((1, N_MOD, D_MODEL), lambda i: (jnp.where(i < n_lat, cond(i), n_batch), 0, 0))],
        out_specs=pl.BlockSpec((ROW_TM, D_MODEL), lambda i: (i, 0)),
        out_shape=jax.ShapeDtypeStruct((x.shape[0] + ctx.shape[0], D_MODEL), BF16),
        compiler_params=pltpu.CompilerParams(dimension_semantics=("arbitrary",)),
        name="prenorm",
    )(x, ctx, gain, mod)


def prenorm_router(x, gain, mod, router_w_t, router_bias, *, rows_per_cond=None, fixed_row=None):
    M = x.shape[0]
    cond = _cond_of_tile(rows_per_cond, ROW_TM, fixed_row)
    x_spec = pl.BlockSpec((ROW_TM, D_MODEL), lambda i: (i, 0))
    return pl.pallas_call(
        _prenorm_router_kernel,
        grid=(M // ROW_TM,),
        in_specs=[x_spec,
                  pl.BlockSpec((1, D_MODEL), lambda i: (0, 0)),
                  pl.BlockSpec((1, N_MOD, D_MODEL), lambda i: (cond(i), 0, 0)),
                  pl.BlockSpec((N_EXPERTS, D_MODEL), lambda i: (0, 0)),
                  pl.BlockSpec((N_EXPERTS, 1), lambda i: (0, 0))],
        out_specs=[x_spec, pl.BlockSpec((N_EXPERTS, ROW_TM), lambda i: (0, i))],
        out_shape=[jax.ShapeDtypeStruct((M, D_MODEL), BF16),
                   jax.ShapeDtypeStruct((N_EXPERTS, M), F32)],
        compiler_params=pltpu.CompilerParams(dimension_semantics=("arbitrary",)),
        name="prenorm_router",
    )(x, gain, mod, router_w_t, router_bias)


def _merge_kernel(y0_ref, y1_ref, y2_ref, g0_ref, g1_ref, g2_ref, wb_ref, o_ref):
    acc = 0.0
    for i, (y_ref, g_ref) in enumerate(((y0_ref, g0_ref), (y1_ref, g1_ref), (y2_ref, g2_ref))):
        acc = acc + jax.nn.sigmoid(g_ref[...]) * jnp.dot(y_ref[...].astype(BF16), wb_ref[i],
                                                         preferred_element_type=F32)
    o_ref[...] = acc.astype(o_ref.dtype)


def merge_branches(ys, z_all, row0, w_branch, *, tm=512, tn=1024):
    M = ys[0].shape[0]
    assert M % tm == 0 and row0 % tm == 0 and C_GATE % tn == 0 and D_MODEL % tn == 0
    row_blk0 = row0 // tm
    y_spec = pl.BlockSpec((tm, BRANCH_W), lambda m, n: (m, 0))

    def gate_spec(i):
        return pl.BlockSpec((tm, tn), lambda m, n: (row_blk0 + m, (C_GATE + i * D_MODEL) // tn + n))

    return pl.pallas_call(
        _merge_kernel,
        grid=(M // tm, D_MODEL // tn),
        in_specs=[y_spec] * 3 + [gate_spec(i) for i in range(N_BRANCH)]
        + [pl.BlockSpec((N_BRANCH, BRANCH_W, tn), lambda m, n: (0, 0, n))],
        out_specs=pl.BlockSpec((tm, tn), lambda m, n: (m, n)),
        out_shape=jax.ShapeDtypeStruct((M, D_MODEL), BF16),
        compiler_params=pltpu.CompilerParams(
            dimension_semantics=("arbitrary", "arbitrary"), vmem_limit_bytes=V7X_VMEM_LIMIT_BYTES),
        name="merge_branches",
    )(*ys, z_all, z_all, z_all, w_branch.astype(BF16))


def _residual_mm_kernel(a_ref, w_ref, x_ref, mod_ref, o_ref, *, gate_row):
    y = jnp.dot(a_ref[...], w_ref[...], preferred_element_type=F32)
    o_ref[...] = x_ref[...] + mod_ref[0, gate_row:gate_row + 1, :] * y


def residual_matmul(a, w, x, mod, *, gate_row, rows_per_cond=None, fixed_row=None, tm=512, tn=512):
    M, K = a.shape
    N = w.shape[1]
    assert M % tm == 0 and N % tn == 0
    cond = _cond_of_tile(rows_per_cond, tm, fixed_row)
    return pl.pallas_call(
        functools.partial(_residual_mm_kernel, gate_row=gate_row),
        grid=(N // tn, M // tm),
        in_specs=[pl.BlockSpec((tm, K), lambda n, m: (m, 0)),
                  pl.BlockSpec((K, tn), lambda n, m: (0, n)),
                  pl.BlockSpec((tm, tn), lambda n, m: (m, n)),
                  pl.BlockSpec((1, N_MOD, tn), lambda n, m: (cond(m), 0, n))],
        out_specs=pl.BlockSpec((tm, tn), lambda n, m: (m, n)),
        out_shape=jax.ShapeDtypeStruct((M, N), F32),
        compiler_params=pltpu.CompilerParams(
            dimension_semantics=("arbitrary", "arbitrary"), vmem_limit_bytes=V7X_VMEM_LIMIT_BYTES),
        name="residual_matmul",
    )(a, w.astype(BF16), x, mod)


def _head_sum(x2):
    row = lax.broadcasted_iota(jnp.int32, (LANES, LANES), 0) // NA_HEAD_DIM
    col = lax.broadcasted_iota(jnp.int32, (LANES, LANES), 1) // NA_HEAD_DIM
    ones_bd = jnp.where(row == col, 1.0, 0.0).astype(BF16)
    return _cumsum_rows_right(x2, ones_bd)


def _cumsum_rows_right(x, m):
    hi, lo = _split_bf16(x)
    return jnp.dot(hi, m, preferred_element_type=F32) + jnp.dot(lo, m, preferred_element_type=F32)


def _na_qkv_kernel(x_ref, g_ref, o_ref):
    x = x_ref[...]

    @pl.when(pl.program_id(1) < 2)
    def _():
        ms = _heads_sum(x * x) * (1.0 / NA_HEAD_DIM)
        o_ref[...] = (x * lax.rsqrt(ms + EPS) * g_ref[...]).astype(o_ref.dtype)

    @pl.when(pl.program_id(1) == 2)
    def _():
        o_ref[...] = x.astype(o_ref.dtype)


def na_qkv(z, gains, *, tm=512):
    M = z.shape[0]
    return pl.pallas_call(
        _na_qkv_kernel,
        grid=(M // tm, 3),
        in_specs=[pl.BlockSpec((tm, NA_W), lambda i, j: (i, j)),
                  pl.BlockSpec((1, NA_W), lambda i, j: (0, jnp.minimum(j, 1)))],
        out_specs=pl.BlockSpec((tm, NA_W), lambda i, j: (i, j)),
        out_shape=jax.ShapeDtypeStruct((M, 3 * NA_W), BF16),
        compiler_params=pltpu.CompilerParams(dimension_semantics=("arbitrary", "arbitrary")),
        name="na_qkv",
    )(z, gains)


def _softmax_pv(s_list, v_list):
    m = s_list[0].max(axis=-1, keepdims=True)
    for s in s_list[1:]:
        m = jnp.maximum(m, s.max(axis=-1, keepdims=True))
    den = 0.0
    acc = 0.0
    for s, v in zip(s_list, v_list):
        p = jnp.exp(s - m)
        den = den + p.sum(axis=-1, keepdims=True)
        acc = acc + jnp.dot(p.astype(BF16), v, preferred_element_type=F32)
    return acc / den


def _na_kernel(*refs, n_rows, win_r, rows_per_step):
    q_ref, k_ref, v_ref, kc_ref, vc_ref = refs[:5]
    bias_refs = refs[5:5 + rows_per_step]
    o_ref = refs[-1]
    kc = kc_ref[...]
    vc = vc_ref[...]
    lane = lax.broadcasted_iota(jnp.int32, (GRID_W, LANES), 1)
    chains = [(i, h) for i in range(rows_per_step) for h in range(2)]
    kw, vw = [], []
    for i in range(rows_per_step):
        r = pl.program_id(2) * rows_per_step + i
        rs = jnp.clip(r - win_r // 2, 0, n_rows - win_r)
        start = pl.multiple_of(rs * GRID_W, GRID_W)
        kw.append(k_ref[pl.ds(start, win_r * GRID_W), :])
        vw.append(v_ref[pl.ds(start, win_r * GRID_W), :])
    qh = []
    for i, h in chains:
        q = q_ref[i * GRID_W:(i + 1) * GRID_W, :]
        qh.append(jnp.where((lane // NA_HEAD_DIM) == h, q, jnp.zeros_like(q)))
    s_loc = [lax.dot_general(qh[c], kw[i], NT_DIMS, preferred_element_type=F32) + bias_refs[i][h, 0]
             for c, (i, h) in enumerate(chains)]
    s_ctx = [lax.dot_general(qh[c], kc, NT_DIMS, preferred_element_type=F32) for c in range(len(chains))]
    m = [jnp.maximum(s_loc[c].max(axis=-1, keepdims=True), s_ctx[c].max(axis=-1, keepdims=True))
         for c in range(len(chains))]
    p_loc = [jnp.exp(s_loc[c] - m[c]) for c in range(len(chains))]
    p_ctx = [jnp.exp(s_ctx[c] - m[c]) for c in range(len(chains))]
    den = [p_loc[c].sum(axis=-1, keepdims=True) + p_ctx[c].sum(axis=-1, keepdims=True)
           for c in range(len(chains))]
    acc = [jnp.dot(p_loc[c].astype(BF16), vw[i], preferred_element_type=F32)
           + jnp.dot(p_ctx[c].astype(BF16), vc, preferred_element_type=F32)
           for c, (i, h) in enumerate(chains)]
    out = [acc[c] / den[c] for c in range(len(chains))]
    for i in range(rows_per_step):
        o_ref[i * GRID_W:(i + 1) * GRID_W, :] = jnp.where(lane < NA_HEAD_DIM, out[2 * i], out[2 * i + 1])


def _ctx_attn_kernel(q_ref, k_ref, v_ref, o_ref):
    q = q_ref[...]
    k = k_ref[...]
    v = v_ref[...]
    lane = lax.broadcasted_iota(jnp.int32, q.shape, 1)
    outs = []
    for h in range(2):
        qh = jnp.where((lane // NA_HEAD_DIM) == h, q, jnp.zeros_like(q))
        s = lax.dot_general(qh, k, NT_DIMS, preferred_element_type=F32)
        outs.append(_softmax_pv([s], [v]))
    o_ref[...] = jnp.where(lane < NA_HEAD_DIM, outs[0], outs[1]).astype(o_ref.dtype)


def na_bias_table(rpb, n_rows, win_r):
    col = np.arange(GRID_W)
    col_start = np.clip(col - NA_WIN_C // 2, 0, GRID_W - NA_WIN_C)
    kcol = np.arange(GRID_W)
    valid = (kcol[None, :] >= col_start[:, None]) & (kcol[None, :] < col_start[:, None] + NA_WIN_C)
    dc = np.clip(kcol[None, :] - col[:, None] + NA_WIN_C - 1, 0, 2 * NA_WIN_C - 2)
    n_dr = 2 * NA_WIN_R_MAX - 1
    per_dr = jnp.where(valid[None, None], rpb.astype(F32)[:, :, dc], NEG_BIG)

    def assemble(t_ref, o_ref):
        v = pl.program_id(1)
        o_ref[0, 0] = jnp.concatenate([t_ref[0, v + n] for n in range(win_r)], axis=1)

    return pl.pallas_call(
        assemble,
        grid=(rpb.shape[0], NA_WIN_R_MAX),
        in_specs=[pl.BlockSpec((1, n_dr, GRID_W, GRID_W), lambda h, v: (h, 0, 0, 0))],
        out_specs=pl.BlockSpec((1, 1, GRID_W, win_r * GRID_W), lambda h, v: (h, v, 0, 0)),
        out_shape=jax.ShapeDtypeStruct((rpb.shape[0], NA_WIN_R_MAX, GRID_W, win_r * GRID_W), F32),
        compiler_params=pltpu.CompilerParams(dimension_semantics=("arbitrary", "arbitrary")),
        name="na_bias_table",
    )(per_dr)


def na_attention(qkv, rpb, B, S, Lc, with_ctx):
    n_rows = S // GRID_W
    win_r = min(NA_WIN_R_MAX, n_rows)
    assert win_r == NA_WIN_R_MAX and (B * S) % Lc == 0
    HP = NA_HEADS // 2
    bias = na_bias_table(rpb, n_rows, win_r)
    ctx_blk0 = (B * S) // Lc

    def variant(r):
        return jnp.clip(r - win_r // 2, 0, n_rows - win_r) - r + NA_WIN_R_MAX - 1

    R = NA_ROWS_PER_STEP
    assert n_rows % R == 0
    steps = n_rows // R
    bias_specs = [pl.BlockSpec((2, 1, GRID_W, win_r * GRID_W),
                               lambda hp, b, r, i=i: (hp, variant(r * R + i), 0, 0)) for i in range(R)]
    y = pl.pallas_call(
        functools.partial(_na_kernel, n_rows=n_rows, win_r=win_r, rows_per_step=R),
        grid=(HP, B, steps),
        in_specs=[pl.BlockSpec((R * GRID_W, LANES), lambda hp, b, r: (b * steps + r, hp)),
                  pl.BlockSpec((S, LANES), lambda hp, b, r: (b, HP + hp)),
                  pl.BlockSpec((S, LANES), lambda hp, b, r: (b, 2 * HP + hp)),
                  pl.BlockSpec((Lc, LANES), lambda hp, b, r: (ctx_blk0 + b, HP + hp)),
                  pl.BlockSpec((Lc, LANES), lambda hp, b, r: (ctx_blk0 + b, 2 * HP + hp))] + bias_specs,
        out_specs=pl.BlockSpec((R * GRID_W, LANES), lambda hp, b, r: (b * steps + r, hp)),
        out_shape=jax.ShapeDtypeStruct((B * S, NA_W), F32),
        compiler_params=pltpu.CompilerParams(
            dimension_semantics=("arbitrary", "arbitrary", "arbitrary"),
            vmem_limit_bytes=V7X_VMEM_LIMIT_BYTES),
        name="na_attention",
    )(qkv, qkv, qkv, qkv, qkv, *([bias] * R))
    yc = None
    if with_ctx:
        yc = pl.pallas_call(
            _ctx_attn_kernel,
            grid=(HP, B),
            in_specs=[pl.BlockSpec((Lc, LANES), lambda hp, b: (ctx_blk0 + b, hp)),
                      pl.BlockSpec((Lc, LANES), lambda hp, b: (ctx_blk0 + b, HP + hp)),
                      pl.BlockSpec((Lc, LANES), lambda hp, b: (ctx_blk0 + b, 2 * HP + hp))],
            out_specs=pl.BlockSpec((Lc, LANES), lambda hp, b: (b, hp)),
            out_shape=jax.ShapeDtypeStruct((B * Lc, NA_W), F32),
            compiler_params=pltpu.CompilerParams(dimension_semantics=("arbitrary", "arbitrary")),
            name="ctx_attention",
        )(qkv, qkv, qkv)
    return y, yc


def na_mixer(z_all, B, S, Lc, q_norm, k_norm, rpb, with_ctx):
    assert (C_NAQ, C_NAK, C_NAV) == (0, NA_W, 2 * NA_W)
    scale = NA_HEAD_DIM ** -0.5
    gains = jnp.concatenate([jnp.tile(q_norm.astype(F32) * scale, NA_HEADS),
                             jnp.tile(k_norm.astype(F32), NA_HEADS)])[None]
    y, yc = na_attention(na_qkv(z_all, gains), rpb, B, S, Lc, with_ctx)
    return y, yc


def _log_sigmoid(x):
    return jnp.minimum(x, 0.0) - jnp.log(1.0 + jnp.exp(-jnp.abs(x)))


def _gla_kernel(*refs, reverse, rope, finish, n_chunks, nb):
    it = iter(refs)

    def take(count):
        return [next(it) for _ in range(count)]

    q_refs, k_refs, v_refs, gd_refs = take(nb), take(nb), take(nb), take(nb)
    w2_ref, gb_ref, s0_ref = take(3)
    if rope:
        cos_ref, sin_ref = take(2)
    if finish:
        yo_ref = next(it)
        r_refs = take(nb)
        g_ref = next(it)
    y_ref, sfin_ref, s_scr = take(3)

    C = GLA_CHUNK
    n = pl.program_id(1)
    chains = [(i, h) for i in range(nb) for h in range(GLA_HEADS)]

    @pl.when(n == 0)
    def _():
        s_scr[...] = s0_ref[...]

    def rows(refs_):
        return jnp.concatenate([r[...] for r in refs_], axis=0)

    ti = lax.broadcasted_iota(jnp.int32, (nb * C, nb * C), 0)
    tj = lax.broadcasted_iota(jnp.int32, (nb * C, nb * C), 1)
    same_seq = (ti // C) == (tj // C)
    tri = jnp.where(same_seq & ((tj >= ti) if reverse else (tj <= ti)), 1.0, 0.0).astype(BF16)
    ci = lax.broadcasted_iota(jnp.int32, (C, C), 0)
    cj = lax.broadcasted_iota(jnp.int32, (C, C), 1)
    incl = (cj >= ci) if reverse else (cj <= ci)

    pre = _mm(rows(gd_refs), w2_ref[...], NN_DIMS, 3) + gb_ref[...]
    la = _log_sigmoid(pre) * (1.0 / GLA_GATE_TAU)
    cs = _cumsum_rows(tri, la)
    ref_i = C // 2 - 1 if reverse else C // 2
    last_i = 0 if reverse else C - 1

    def per_seq_row(idx):
        return jnp.concatenate([jnp.broadcast_to(cs[i * C + idx:i * C + idx + 1, :], (C, GLA_QK))
                                for i in range(nb)], axis=0)

    b_ref = per_seq_row(ref_i)
    b_last = per_seq_row(last_i)

    q = rows(q_refs) * (GLA_DK ** -0.5)
    k = rows(k_refs)
    if rope:
        lane = lax.broadcasted_iota(jnp.int32, q.shape, 1)
        first = (lane % (GLA_DK // 2)) < GLA_DK // 4
        cos = jnp.concatenate([jnp.concatenate([cos_ref[...]] * GLA_HEADS, axis=1)] * nb, axis=0)
        sin = jnp.concatenate([jnp.concatenate([sin_ref[...]] * GLA_HEADS, axis=1)] * nb, axis=0)

        def rot(x):
            partner = jnp.where(first, pltpu.roll(x, GLA_QK - GLA_DK // 4, axis=1),
                                pltpu.roll(x, GLA_DK // 4, axis=1))
            return x * cos + partner * sin

        q = rot(q)
        k = rot(k)
    qi = q * jnp.exp(cs - b_ref)
    kj = k * jnp.exp(b_ref - cs)
    qe = q * jnp.exp(cs)
    ke = k * jnp.exp(b_last - cs)
    dec = jnp.exp(b_last)
    v_all = rows(v_refs)

    def hk(x, i, h):
        return x[i * C:(i + 1) * C, h * GLA_DK:(h + 1) * GLA_DK]

    def hv(x, i, h):
        return x[i * C:(i + 1) * C, h * GLA_DV:(h + 1) * GLA_DV]

    St = [s_scr[i, h] for i, h in chains]
    V = [hv(v_all, i, h) for i, h in chains]
    att = [jnp.where(incl, _mm(hk(qi, i, h), hk(kj, i, h), NT_DIMS), 0.0) for i, h in chains]
    y = [_mm(att[c], V[c], NN_DIMS) + _mm(hk(qe, i, h), St[c], NT_DIMS) for c, (i, h) in enumerate(chains)]
    kvt = [_mm(V[c], hk(ke, i, h), TN_DIMS) for c, (i, h) in enumerate(chains)]
    for c, (i, h) in enumerate(chains):
        s_scr[i, h] = St[c] * hk(dec, i, h)[0:1, :] + kvt[c]
    if finish:
        r = rows(r_refs)
        g = g_ref[...]
        for c, (i, h) in enumerate(chains):
            ys = y[c] + yo_ref[i, :, h * GLA_DV:(h + 1) * GLA_DV]
            ms = jnp.mean(ys * ys, axis=-1, keepdims=True)
            rh = hv(r, i, h)
            y[c] = ys * lax.rsqrt(ms + EPS) * g * (rh * jax.nn.sigmoid(rh))
    for i in range(nb):
        y_ref[i] = jnp.concatenate(y[i * GLA_HEADS:(i + 1) * GLA_HEADS], axis=1)

    @pl.when(n == n_chunks - 1)
    def _():
        sfin_ref[...] = s_scr[...]


def gla_rope_tables(seq_len):
    t = np.arange(seq_len)
    quarter = GLA_DK // 4
    inv = ROPE_BASE ** (-np.arange(0, 2 * quarter, 2, dtype=np.float64) / (2 * quarter))
    ar = (t // GRID_W)[:, None] * inv[None, :]
    ac = (t % GRID_W)[:, None] * inv[None, :]
    cos = np.concatenate([np.cos(ar), np.cos(ar), np.cos(ac), np.cos(ac)], axis=1)
    sin = np.concatenate([-np.sin(ar), np.sin(ar), -np.sin(ac), np.sin(ac)], axis=1)
    return jnp.asarray(cos, F32), jnp.asarray(sin, F32)


def gla_scan(z, w2pad, gate_b, s0, *, batch, seq_len, row0, reverse, rope, finish_with=None):
    N = seq_len // GLA_CHUNK
    blk0 = row0 // GLA_CHUNK
    off_q, off_k, off_v, off_r, off_gd = C_GQ, C_GK, C_GV, C_GR, C_GGD
    assert off_q % GLA_QK == 0 and off_k % GLA_QK == 0 and off_v % GLA_V == 0 and off_r % GLA_V == 0
    assert off_gd % LANES == 0

    def tb(n):
        return N - 1 - n if reverse else n

    def rb(b, n):
        return blk0 + b * N + tb(n)

    nb = GLA_BATCH_PER_STEP if batch % GLA_BATCH_PER_STEP == 0 else 1

    def tok_specs(width, col_off):
        return [pl.BlockSpec((GLA_CHUNK, width), lambda bb, n, i=i: (rb(bb * nb + i, n), col_off // width))
                for i in range(nb)]

    st_spec = pl.BlockSpec((nb, GLA_HEADS, GLA_DV, GLA_DK), lambda bb, n: (bb, 0, 0, 0))
    y_spec = pl.BlockSpec((nb, GLA_CHUNK, GLA_V), lambda bb, n: (bb, tb(n), 0))
    in_specs = (tok_specs(GLA_QK, off_q) + tok_specs(GLA_QK, off_k) + tok_specs(GLA_V, off_v)
                + tok_specs(LANES, off_gd)
                + [pl.BlockSpec((LANES, GLA_QK), lambda bb, n: (0, 0)),
                   pl.BlockSpec((1, GLA_QK), lambda bb, n: (0, 0)),
                   st_spec])
    args = [z] * (4 * nb) + [w2pad, gate_b, s0]
    if rope:
        cos, sin = gla_rope_tables(seq_len)
        in_specs += [pl.BlockSpec((GLA_CHUNK, GLA_DK), lambda bb, n: (tb(n), 0))] * 2
        args += [cos, sin]
    if finish_with is not None:
        y_other, norm_g = finish_with
        in_specs += [y_spec] + tok_specs(GLA_V, off_r) + [pl.BlockSpec((1, GLA_DV), lambda bb, n: (0, 0))]
        args += [y_other.reshape(batch, seq_len, GLA_V)] + [z] * nb + [norm_g]
    y, s_fin = pl.pallas_call(
        functools.partial(_gla_kernel, reverse=reverse, rope=rope, finish=finish_with is not None,
                          n_chunks=N, nb=nb),
        grid=(batch // nb, N),
        in_specs=in_specs,
        out_specs=[y_spec, st_spec],
        out_shape=[jax.ShapeDtypeStruct((batch, seq_len, GLA_V), F32),
                   jax.ShapeDtypeStruct((batch, GLA_HEADS, GLA_DV, GLA_DK), F32)],
        scratch_shapes=[pltpu.VMEM((nb, GLA_HEADS, GLA_DV, GLA_DK), F32)],
        compiler_params=pltpu.CompilerParams(
            dimension_semantics=("arbitrary", "arbitrary"),
            vmem_limit_bytes=V7X_VMEM_LIMIT_BYTES),
        name="gla_scan_rev" if reverse else "gla_scan",
    )(*args)
    return y.reshape(batch * seq_len, GLA_V), s_fin


def gla_mixer(z_all, B, S, Lc, gate_w2, gate_b, norm_g, with_ctx):
    s0 = jnp.zeros((B, GLA_HEADS, GLA_DV, GLA_DK), F32)
    g = norm_g.astype(F32)[None]
    w2 = [jnp.zeros((LANES, GLA_QK), F32).at[d * GLA_GATE_RANK:(d + 1) * GLA_GATE_RANK].set(gate_w2[d])
          for d in range(2)]
    gb = [gate_b[d].astype(F32)[None] for d in range(2)]
    ctx = dict(batch=B, seq_len=Lc, row0=B * S, rope=False)
    lat = dict(batch=B, seq_len=S, row0=0, rope=True)
    yc_f, sc_f = gla_scan(z_all, w2[0], gb[0], s0, reverse=False, **ctx)
    yc, sc_b = gla_scan(z_all, w2[1], gb[1], s0, reverse=True, finish_with=(yc_f, g), **ctx)
    yl_f, _ = gla_scan(z_all, w2[0], gb[0], sc_f, reverse=False, **lat)
    yl, _ = gla_scan(z_all, w2[1], gb[1], sc_b, reverse=True, finish_with=(yl_f, g), **lat)
    return yl, (yc if with_ctx else None)


RW_PREP_TM = 256
RW_HALO = 8
RW_N_PIECES = 5


def _heads_sum(x):
    return jnp.concatenate([_head_sum(x[:, j * LANES:(j + 1) * LANES]) for j in range(x.shape[1] // LANES)],
                           axis=1)


def _rw_prep_kernel(*refs, tm, n_latent_tiles, tiles_per_seq):
    P = RW_N_PIECES
    z_refs, prev_refs, next_refs, mu_refs = refs[0:P], refs[P:2 * P], refs[2 * P:3 * P], refs[3 * P:4 * P]
    kk_ref, ka_ref, rk_ref, w0_ref, a0_ref, w2_ref, a2_ref, g2_ref = refs[4 * P:4 * P + 8]
    (r_o, v_o, kap_o, lw0_o, kd0_o, b0_o, lw1_o, kd1_o, b1_o, bonus_o, g_o) = refs[4 * P + 8:]
    i = pl.program_id(0)
    latent = i < n_latent_tiles
    first = jnp.logical_or(jnp.logical_not(latent), i % tiles_per_seq == 0)
    last = jnp.logical_or(jnp.logical_not(latent), i % tiles_per_seq == tiles_per_seq - 1)

    def shifted(j):
        z = z_refs[j][...]
        row = lax.broadcasted_iota(jnp.int32, z.shape, 0)
        prev_row = jnp.where(first, 0.0, prev_refs[j][RW_HALO - 1:RW_HALO, :])
        next_row = jnp.where(last, 0.0, next_refs[j][0:1, :])
        prev = jnp.where(row == 0, prev_row, pltpu.roll(z, 1, axis=0))
        nxt = jnp.where(row == tm - 1, next_row, pltpu.roll(z, tm - 1, axis=0))
        return z + mu_refs[j][...] * (0.5 * (prev + nxt) - z)

    r, k, v, lr, gd = (shifted(j) for j in range(P))
    kkr = k * kk_ref[...]
    kap = kkr * lax.rsqrt(_heads_sum(kkr * kkr) + EPS)
    r_o[...] = r
    v_o[...] = v
    kap_o[...] = kap
    bonus_o[...] = _heads_sum(r * k * rk_ref[...]) * v
    g_o[...] = _mm(jax.nn.sigmoid(gd), g2_ref[...], NN_DIMS, 3)
    ka = ka_ref[...]
    for d, (lw_o, kd_o, b_o) in enumerate(((lw0_o, kd0_o, b0_o), (lw1_o, kd1_o, b1_o))):
        wd = lr[:, d * LANES:(d + 1) * LANES]
        ad = lr[:, (2 + d) * LANES:(3 + d) * LANES]
        w_log = _log_sigmoid(w0_ref[d:d + 1, :] + _mm(jnp.tanh(wd), w2_ref[d], NN_DIMS, 3)) - 0.5
        lw_o[...] = -jnp.exp(w_log)
        a = jax.nn.sigmoid(a0_ref[d:d + 1, :] + _mm(ad, a2_ref[d], NN_DIMS, 3))
        kd_o[...] = k * (1.0 + (a - 1.0) * ka)
        b_o[...] = kap * a


def rwkv_prep(z_all, n_latent_rows, seq_len, mu_p, k_k, k_a, r_k, w0, a0, w2p, a2p, g2):
    M = z_all.shape[0]
    tm = RW_PREP_TM
    assert M % tm == 0 and n_latent_rows % tm == 0 and seq_len % tm == 0
    widths = (RW_W, RW_W, RW_W, RW_LR_W, RW_GATE_RANK)
    offs = (C_RR, C_RK, C_RV, C_RLR, C_RGD)
    n_halo_blocks = M // RW_HALO
    z_specs, prev_specs, next_specs, mu_specs = [], [], [], []
    for w, o in zip(widths, offs):
        assert o % w == 0
        cb = o // w
        z_specs.append(pl.BlockSpec((tm, w), lambda i, cb=cb: (i, cb)))
        prev_specs.append(pl.BlockSpec(
            (RW_HALO, w), lambda i, cb=cb: (jnp.maximum(i * (tm // RW_HALO) - 1, 0), cb)))
        next_specs.append(pl.BlockSpec(
            (RW_HALO, w), lambda i, cb=cb: (jnp.minimum((i + 1) * (tm // RW_HALO), n_halo_blocks - 1), cb)))
        mu_specs.append(pl.BlockSpec((1, w), lambda i: (0, 0)))

    def full(shape):
        return pl.BlockSpec(shape, lambda i: (0,) * len(shape))

    par_specs = [full((1, RW_W))] * 3 + [full((2, RW_W))] * 2 + [full((2, LANES, RW_W))] * 2 \
        + [full((RW_GATE_RANK, RW_W))]
    out_spec = pl.BlockSpec((tm, RW_W), lambda i: (i, 0))
    return pl.pallas_call(
        functools.partial(_rw_prep_kernel, tm=tm, n_latent_tiles=n_latent_rows // tm,
                          tiles_per_seq=seq_len // tm),
        grid=(M // tm,),
        in_specs=z_specs + prev_specs + next_specs + mu_specs + par_specs,
        out_specs=[out_spec] * 11,
        out_shape=[jax.ShapeDtypeStruct((M, RW_W), F32)] * 11,
        compiler_params=pltpu.CompilerParams(
            dimension_semantics=("arbitrary",), vmem_limit_bytes=V7X_VMEM_LIMIT_BYTES),
        name="rwkv_prep",
    )(*([z_all] * 15), *mu_p, k_k, k_a, r_k, w0, a0, w2p, a2p, g2)


def _rw_chunk_kernel(*refs, reverse, n_par, n_chunks, passes, finish, nb):
    it = iter(refs)

    def side_by_side():
        return jnp.concatenate([next(it)[...] for _ in range(nb)], axis=1)

    r_in, v_all, kap_in, lw_all, kd_all, bet_all = (side_by_side() for _ in range(6))
    s0_ref = next(it)
    if finish:
        yo_ref = next(it)
        bonus, gate = side_by_side(), side_by_side()
        gamma = jnp.concatenate([next(it)[...]] * nb, axis=1)
        beta = jnp.concatenate([next(it)[...]] * nb, axis=1)
    y_ref, sfin_ref, s_scr = next(it), next(it), next(it)

    C = RW_CHUNK
    n = pl.program_id(2)
    P = range(nb * n_par)

    @pl.when(n == 0)
    def _():
        s_scr[...] = s0_ref[:, 0]

    row = lax.broadcasted_iota(jnp.int32, (C, LANES), 0)
    lane = lax.broadcasted_iota(jnp.int32, (C, LANES), 1)
    col = lane % RW_HEAD
    head0 = lane < RW_HEAD
    strict = (col > row) if reverse else (col < row)
    incl = (col >= row) if reverse else (col <= row)
    ti = lax.broadcasted_iota(jnp.int32, (C, C), 0)
    tj = lax.broadcasted_iota(jnp.int32, (C, C), 1)
    tri = jnp.where((tj >= ti) if reverse else (tj <= ti), 1.0, 0.0).astype(BF16)
    eye = jnp.where(row == col, 1.0, 0.0)
    brow = lax.broadcasted_iota(jnp.int32, (LANES, LANES), 0) // RW_HEAD
    bcol = lax.broadcasted_iota(jnp.int32, (LANES, LANES), 1) // RW_HEAD
    same_head = brow == bcol

    def sl(x, p):
        return x[:, p * LANES:(p + 1) * LANES]

    def bd(x):
        return jnp.concatenate([jnp.where(head0, x, 0.0), jnp.where(head0, 0.0, x)], axis=0)

    def pp(a_pair, b_pair):
        return _mm(a_pair, bd(b_pair), NN_DIMS, passes)

    cs_all = _cumsum_rows(tri, lw_all)
    tot_all = cs_all[0:1, :] if reverse else cs_all[C - 1:C, :]
    inv_all = jnp.exp(-cs_all)
    fin_all = jnp.exp(tot_all - cs_all)
    kap_t = kap_in * jnp.exp(cs_all - lw_all)
    r_t = r_in * jnp.exp(cs_all)
    k_t = kd_all * inv_all
    b_t = bet_all * inv_all
    k_h = kd_all * fin_all
    b_h = bet_all * fin_all
    dec = jnp.exp(tot_all)

    S = [s_scr[p // n_par, p % n_par] for p in P]
    V = [sl(v_all, p) for p in P]
    lhs = [jnp.concatenate([sl(kap_t, p), sl(r_t, p)], axis=0) for p in P]
    a_k = [_mm(lhs[p], bd(sl(k_t, p)), NT_DIMS, passes) for p in P]
    a_b = [_mm(lhs[p], bd(sl(b_t, p)), NT_DIMS, passes) for p in P]
    s0_both = [_mm(lhs[p], S[p], NT_DIMS, passes) for p in P]
    lmat = [jnp.where(strict, a_b[p][:C], 0.0) for p in P]
    power = [-lmat[p] for p in P]
    t_inv = [eye + power[p] for p in P]
    s = 2
    while s < C:
        power = [pp(power[p], power[p]) for p in P]
        t_inv = [t_inv[p] + pp(t_inv[p], power[p]) for p in P]
        s *= 2
    x = [s0_both[p][:C] + pp(jnp.where(strict, a_k[p][:C], 0.0), V[p]) for p in P]
    u = [pp(t_inv[p], x[p]) for p in P]
    y = [s0_both[p][C:] + pp(jnp.where(incl, a_k[p][C:], 0.0), V[p])
         - pp(jnp.where(incl, a_b[p][C:], 0.0), u[p]) for p in P]
    upd = [_mm(V[p], sl(k_h, p), TN_DIMS, passes) - _mm(u[p], sl(b_h, p), TN_DIMS, passes) for p in P]
    for p in P:
        s_scr[p // n_par, p % n_par] = S[p] * sl(dec, p) + jnp.where(same_head, upd[p], 0.0)
    width = n_par * LANES
    if finish:
        yo = jnp.concatenate([yo_ref[i] for i in range(nb)], axis=1)
        ys = jnp.concatenate(y, axis=1) + yo
        cen = ys - _heads_sum(ys) * (1.0 / RW_HEAD)
        var = _heads_sum(cen * cen) * (1.0 / RW_HEAD)
        out = (cen * lax.rsqrt(var + RW_GN_EPS) * gamma + beta + bonus) * gate
    else:
        out = jnp.concatenate(y, axis=1)
    for i in range(nb):
        y_ref[i] = out[:, i * width:(i + 1) * width]

    @pl.when(n == n_chunks - 1)
    def _():
        sfin_ref[:, 0] = s_scr[...]


def rwkv_chunk_scan(r, v, kap, lw, kd, bet, s0, *, batch, seq_len, row0, reverse, finish_with=None):
    n_par = RW_PAIRS_PER_STEP
    nb = RW_BATCH_PER_STEP if batch % RW_BATCH_PER_STEP == 0 else 1
    W = r.shape[1]
    G = W // (n_par * LANES)
    N = seq_len // RW_CHUNK
    blk0 = row0 // RW_CHUNK

    def tb(n):
        return N - 1 - n if reverse else n

    def seq_in():
        return [pl.BlockSpec((RW_CHUNK, n_par * LANES),
                             lambda bb, g, n, i=i: (blk0 + (bb * nb + i) * N + tb(n), g)) for i in range(nb)]

    y_spec = pl.BlockSpec((nb, RW_CHUNK, n_par * LANES), lambda bb, g, n: (bb, tb(n), g))
    st_spec = pl.BlockSpec((nb, 1, n_par, LANES, LANES), lambda bb, g, n: (bb, g, 0, 0, 0))
    in_specs = seq_in() * 6 + [st_spec]
    args = [a for a in (r, v, kap, lw, kd, bet) for _ in range(nb)] + [s0]
    if finish_with is not None:
        y_other, bonus, gate, gamma, beta = finish_with
        vec_spec = pl.BlockSpec((1, n_par * LANES), lambda bb, g, n: (0, g))
        in_specs += [y_spec] + seq_in() * 2 + [vec_spec, vec_spec]
        args += [y_other.reshape(batch, seq_len, W)] + [bonus] * nb + [gate] * nb + [gamma, beta]
    y, s_fin = pl.pallas_call(
        functools.partial(_rw_chunk_kernel, reverse=reverse, n_par=n_par, n_chunks=N, passes=RW_PASSES,
                          finish=finish_with is not None, nb=nb),
        grid=(batch // nb, G, N),
        in_specs=in_specs,
        out_specs=[y_spec, st_spec],
        out_shape=[jax.ShapeDtypeStruct((batch, seq_len, W), F32),
                   jax.ShapeDtypeStruct((batch, G, n_par, LANES, LANES), F32)],
        scratch_shapes=[pltpu.VMEM((nb, n_par, LANES, LANES), F32)],
        compiler_params=pltpu.CompilerParams(
            dimension_semantics=("arbitrary", "arbitrary", "arbitrary"),
            vmem_limit_bytes=V7X_VMEM_LIMIT_BYTES),
        name="rwkv_chunk_scan_rev" if reverse else "rwkv_chunk_scan",
    )(*args)
    return y.reshape(batch * seq_len, W), s_fin


def _pad_rows(w, rows):
    return jnp.pad(w.astype(F32), ((0, 0), (0, rows - w.shape[1]), (0, 0)))


def rwkv_mixer(z_all, B, S, Lc, mu, w0, w2, a0, a2, g2, k_k, k_a, r_k, gn_g, gn_b, with_ctx):
    assert Lc == RW_PREP_TM
    mu = mu.astype(F32)
    o = np.cumsum((0,) + RW_SPLITS)
    lr_pad = jnp.zeros((LANES - RW_DECAY_RANK,), F32)
    mu_lr = jnp.concatenate([mu[o[3]:o[3] + RW_DECAY_RANK], lr_pad, mu[o[3] + RW_DECAY_RANK:o[4]], lr_pad,
                             mu[o[4]:o[4] + RW_A_RANK], lr_pad, mu[o[4] + RW_A_RANK:o[5]], lr_pad])
    mu_p = [mu[o[0]:o[1]][None], mu[o[1]:o[2]][None], mu[o[2]:o[3]][None], mu_lr[None], mu[o[5]:o[6]][None]]
    row = lambda t: t.astype(F32).reshape(1, RW_W)
    prep = rwkv_prep(z_all, B * S, S, mu_p, row(k_k), row(k_a), row(r_k), w0.astype(F32), a0.astype(F32),
                     _pad_rows(w2, LANES), _pad_rows(a2, LANES), g2.astype(F32))
    r, v, kap, lw0, kd0, b0, lw1, kd1, b1, bonus, gate = prep
    fin = (bonus, gate, row(gn_g), row(gn_b))
    s0 = jnp.zeros((B, RW_W // (RW_PAIRS_PER_STEP * LANES), RW_PAIRS_PER_STEP, LANES, LANES), F32)
    ctx = dict(batch=B, seq_len=Lc, row0=B * S)
    lat = dict(batch=B, seq_len=S, row0=0)
    yc_f, sc_f = rwkv_chunk_scan(r, v, kap, lw0, kd0, b0, s0, reverse=False, **ctx)
    yc, sc_b = rwkv_chunk_scan(r, v, kap, lw1, kd1, b1, s0, reverse=True, finish_with=(yc_f,) + fin, **ctx)
    yl_f, _ = rwkv_chunk_scan(r, v, kap, lw0, kd0, b0, sc_f, reverse=False, **lat)
    yl, _ = rwkv_chunk_scan(r, v, kap, lw1, kd1, b1, sc_b, reverse=True, finish_with=(yl_f,) + fin, **lat)
    return yl, (yc if with_ctx else None)


def relayout_w_in(w):
    o = IN_OFFSETS
    rw = o[8] + np.cumsum((0,) + RW_SPLITS)
    zeros = lambda n: jnp.zeros((w.shape[0], n), w.dtype)
    seg = lambda lo, n: w[:, lo:lo + n]
    lr_pad = LANES - RW_DECAY_RANK
    pieces = [
        seg(o[0], 3 * NA_W),
        seg(o[5], GLA_V), seg(o[6], GLA_V),
        seg(rw[0], 3 * RW_W),
        seg(o[9], N_BRANCH * D_MODEL),
        seg(o[3], GLA_QK), seg(o[4], GLA_QK),
        seg(rw[3], RW_DECAY_RANK), zeros(lr_pad), seg(rw[3] + RW_DECAY_RANK, RW_DECAY_RANK), zeros(lr_pad),
        seg(rw[4], RW_A_RANK), zeros(lr_pad), seg(rw[4] + RW_A_RANK, RW_A_RANK), zeros(lr_pad),
        seg(rw[5], RW_GATE_RANK),
        seg(o[7], 2 * GLA_GATE_RANK),
    ]
    used = sum(x.shape[1] for x in pieces)
    assert used == C_GGD + 2 * GLA_GATE_RANK
    return jnp.concatenate(pieces + [zeros(Z_W - used)], axis=1).astype(BF16)


def token_mixers(x, ctx, mod, p, B, S, Lc, with_ctx):
    h_all = prenorm(x, ctx, p['norm1_g'], mod, seq_len=S)
    z_all = matmul(h_all, relayout_w_in(p['w_in']), tm=1024, tn=1024)
    y_na, yc_na = na_mixer(z_all, B, S, Lc, p['na_q_norm'], p['na_k_norm'], p['na_rpb'], with_ctx)
    y_gla, yc_gla = gla_mixer(z_all, B, S, Lc, p['gla_gate_w2'], p['gla_gate_b'], p['gla_norm_g'], with_ctx)
    y_rw, yc_rw = rwkv_mixer(z_all, B, S, Lc, p['rw_mu'], p['rw_w0'], p['rw_w2'], p['rw_a0'], p['rw_a2'],
                             p['rw_g2'], p['rw_k_k'], p['rw_k_a'], p['rw_r_k'], p['rw_gn_g'],
                             p['rw_gn_b'], with_ctx)
    merged = merge_branches((y_na, y_gla, y_rw), z_all, 0, p['w_branch'])
    x = residual_matmul(merged, p['w_out'], x, mod, gate_row=2, rows_per_cond=S)
    if with_ctx:
        merged_c = merge_branches((yc_na, yc_gla, yc_rw), z_all, B * S, p['w_branch'])
        ctx = residual_matmul(merged_c, p['w_out'], ctx, mod, gate_row=2, fixed_row=B)
    return x, ctx


def moe_sublayer(x, mod, p, *, rows_per_cond=None, fixed_row=None):
    h2, comb_t = prenorm_router(x, p['norm2_g'], mod, p['router_w_t'], p['router_bias'],
                                rows_per_cond=rows_per_cond, fixed_row=fixed_row)
    return moe_experts(h2, comb_t.T, p['moe_w_gate'], p['moe_w_up'], p['moe_w_down'], x, mod,
                       rows_per_cond=rows_per_cond, fixed_row=fixed_row)


def kernel(x, c, ctx, c_ctx, ada_w, ada_b, norm1_g, norm2_g, w_in, na_q_norm, na_k_norm, na_rpb,
           gla_gate_w2, gla_gate_b, gla_norm_g, rw_mu, rw_w0, rw_w2, rw_a0, rw_a2, rw_g2,
           rw_k_k, rw_k_a, rw_r_k, rw_gn_g, rw_gn_b, w_branch, w_out, router_w, router_bias,
           moe_w_gate, moe_w_up, moe_w_down):
    B, S, D = x.shape
    Lc = ctx.shape[1]
    assert B + 1 <= N_COND and S % 512 == 0 and (B * Lc) % 512 == 0 and Lc % ROW_TM == 0
    c_rows = jnp.concatenate([c, c_ctx[None], jnp.zeros((N_COND - B - 1, D), c.dtype)], axis=0).astype(F32)
    mods = ada_modulation(c_rows, ada_w, ada_b).reshape(DEPTH, N_COND, N_MOD, D)
    x = x.reshape(B * S, D)
    ctx = ctx.reshape(B * Lc, D)
    router_w_t = router_w.astype(F32).T
    router_b = router_bias.astype(F32).reshape(N_EXPERTS, 1)
    for l in range(DEPTH):
        with_ctx = l < DEPTH - 1
        p = {
            'w_in': w_in[l], 'na_q_norm': na_q_norm[l], 'na_k_norm': na_k_norm[l], 'na_rpb': na_rpb[l],
            'gla_gate_w2': gla_gate_w2[l], 'gla_gate_b': gla_gate_b[l], 'gla_norm_g': gla_norm_g[l],
            'rw_mu': rw_mu[l], 'rw_w0': rw_w0[l], 'rw_w2': rw_w2[l], 'rw_a0': rw_a0[l], 'rw_a2': rw_a2[l],
            'rw_g2': rw_g2[l], 'rw_k_k': rw_k_k[l], 'rw_k_a': rw_k_a[l], 'rw_r_k': rw_r_k[l],
            'rw_gn_g': rw_gn_g[l], 'rw_gn_b': rw_gn_b[l], 'w_branch': w_branch[l], 'w_out': w_out[l],
            'norm1_g': norm1_g[l].astype(F32)[None], 'norm2_g': norm2_g[l].astype(F32)[None],
            'router_w_t': router_w_t, 'router_bias': router_b,
            'moe_w_gate': moe_w_gate[l].astype(BF16), 'moe_w_up': moe_w_up[l].astype(BF16),
            'moe_w_down': moe_w_down[l].astype(BF16),
        }
        x, ctx = token_mixers(x, ctx, mods[l], p, B, S, Lc, with_ctx)
        x = moe_sublayer(x, mods[l], p, rows_per_cond=S)
        if with_ctx:
            ctx = moe_sublayer(ctx, mods[l], p, fixed_row=B)
    return x.reshape(B, S, D)
```

```python
import functools

import jax
import jax.numpy as jnp
from jax import lax
import numpy as np
from jax.experimental import pallas as pl
from jax.experimental.pallas import tpu as pltpu

D_MODEL = 2048
DEPTH = 2
GRID_W = 64

NA_HEADS = 16
NA_HEAD_DIM = 64
NA_WIN_R_MAX = 8
NA_WIN_C = 16
NA_W = NA_HEADS * NA_HEAD_DIM
NA_ROWS_PER_STEP = 16

GLA_HEADS = 4
GLA_DK = 128
GLA_DV = 256
GLA_QK = GLA_HEADS * GLA_DK
GLA_V = GLA_HEADS * GLA_DV
GLA_GATE_RANK = 16
GLA_GATE_TAU = 16.0
GLA_CHUNK = 64
GLA_BATCH_PER_STEP = 4

RW_HEADS = 16
RW_HEAD = 64
RW_W = RW_HEADS * RW_HEAD
RW_DECAY_RANK = 96
RW_A_RANK = 96
RW_GATE_RANK = 256
RW_SPLITS = (RW_W, RW_W, RW_W, 2 * RW_DECAY_RANK, 2 * RW_A_RANK, RW_GATE_RANK)
RW_IN = sum(RW_SPLITS)
RW_GN_EPS = 64e-5
RW_CHUNK = 64
RW_PASSES = 1
RW_PAIRS_PER_STEP = 8
RW_BATCH_PER_STEP = 4

N_BRANCH = 3
BRANCH_W = 1024
IN_SPLITS = (NA_W, NA_W, NA_W, GLA_QK, GLA_QK, GLA_V, GLA_V, 2 * GLA_GATE_RANK, RW_IN, N_BRANCH * D_MODEL)
D_IN = sum(IN_SPLITS)
IN_OFFSETS = tuple(int(o) for o in np.cumsum((0,) + IN_SPLITS[:-1]))

LANES = 128
RW_LR_W = 4 * LANES
C_NAQ, C_NAK, C_NAV = 0, 1024, 2048
C_GV, C_GR = 3072, 4096
C_RR, C_RK, C_RV = 5120, 6144, 7168
C_GATE = 8192
C_GQ, C_GK = 14336, 14848
C_RLR, C_RGD, C_GGD = 15360, 15872, 16128
Z_W = 16384

N_EXPERTS = 16
N_GROUPS = 4
EXPERTS_PER_GROUP = N_EXPERTS // N_GROUPS
TOP_K = 2
D_EXPERT = 512

ROPE_BASE = 10000.0
EPS = 1e-6
F32 = jnp.float32
BF16 = jnp.bfloat16

V7X_VMEM_LIMIT_BYTES = 48 * 1024 * 1024

NEG_BIG = -1e30
NT_DIMS = (((1,), (1,)), ((), ()))
TN_DIMS = (((0,), (0,)), ((), ()))
NN_DIMS = (((1,), (0,)), ((), ()))


def _split_bf16(x):
    hi = x.astype(BF16)
    lo = (x - hi.astype(F32)).astype(BF16)
    return hi, lo


def _mm(a, b, dims=NN_DIMS, passes=1):
    if passes == 1:
        return lax.dot_general(a.astype(BF16), b.astype(BF16), dims, preferred_element_type=F32)
    a_hi, a_lo = _split_bf16(a)
    b_hi, b_lo = _split_bf16(b)
    out = lax.dot_general(a_hi, b_hi, dims, preferred_element_type=F32)
    out = out + lax.dot_general(a_hi, b_lo, dims, preferred_element_type=F32)
    return out + lax.dot_general(a_lo, b_hi, dims, preferred_element_type=F32)


def _cumsum_rows(tri, x):
    hi, lo = _split_bf16(x)
    return jnp.dot(tri, hi, preferred_element_type=F32) + jnp.dot(tri, lo, preferred_element_type=F32)


def _mm_kernel(a_ref, b_ref, o_ref):
    o_ref[...] = jnp.dot(a_ref[...], b_ref[...], preferred_element_type=F32).astype(o_ref.dtype)


def matmul(a, b, *, tm=512, tn=512, out_dtype=F32):
    M, K = a.shape
    _, N = b.shape
    assert M % tm == 0 and N % tn == 0, (M, N, tm, tn)
    return pl.pallas_call(
        _mm_kernel,
        grid=(N // tn, M // tm),
        in_specs=[pl.BlockSpec((tm, K), lambda n, m: (m, 0)),
                  pl.BlockSpec((K, tn), lambda n, m: (0, n))],
        out_specs=pl.BlockSpec((tm, tn), lambda n, m: (m, n)),
        out_shape=jax.ShapeDtypeStruct((M, N), out_dtype),
        compiler_params=pltpu.CompilerParams(
            dimension_semantics=("arbitrary", "arbitrary"),
            vmem_limit_bytes=V7X_VMEM_LIMIT_BYTES),
        name="matmul",
    )(a.astype(BF16), b.astype(BF16))


def _moe_kernel(h_ref, comb_ref, wg_ref, wu_ref, wd_ref, x_ref, mod_ref, o_ref, acc_ref):
    e = pl.program_id(1)

    @pl.when(e == 0)
    def _():
        acc_ref[...] = jnp.zeros_like(acc_ref)

    h = h_ref[...]
    g = jnp.dot(h, wg_ref[0], preferred_element_type=F32)
    u = jnp.dot(h, wu_ref[0], preferred_element_type=F32)
    comb = comb_ref[...]
    lane = lax.broadcasted_iota(jnp.int32, comb.shape, 1)
    ce = jnp.sum(jnp.where(lane == e, comb, 0.0), axis=1, keepdims=True)
    act = (g * jax.nn.sigmoid(g)) * u * ce
    acc_ref[...] += jnp.dot(act.astype(BF16), wd_ref[0], preferred_element_type=F32)

    @pl.when(e == N_EXPERTS - 1)
    def _():
        o_ref[...] = x_ref[...] + mod_ref[0, N_MOD - 1:N_MOD, :] * acc_ref[...]


def moe_experts(h, comb, w_gate, w_up, w_down, x, mod, *, rows_per_cond=None, fixed_row=None, tm=512):
    M, D = h.shape
    assert M % tm == 0
    cond = _cond_of_tile(rows_per_cond, tm, fixed_row)
    return pl.pallas_call(
        _moe_kernel,
        grid=(M // tm, N_EXPERTS),
        in_specs=[pl.BlockSpec((tm, D), lambda m, e: (m, 0)),
                  pl.BlockSpec((tm, N_EXPERTS), lambda m, e: (m, 0)),
                  pl.BlockSpec((1, D, D_EXPERT), lambda m, e: (e, 0, 0)),
                  pl.BlockSpec((1, D, D_EXPERT), lambda m, e: (e, 0, 0)),
                  pl.BlockSpec((1, D_EXPERT, D), lambda m, e: (e, 0, 0)),
                  pl.BlockSpec((tm, D), lambda m, e: (m, 0)),
                  pl.BlockSpec((1, N_MOD, D), lambda m, e: (cond(m), 0, 0))],
        out_specs=pl.BlockSpec((tm, D), lambda m, e: (m, 0)),
        out_shape=jax.ShapeDtypeStruct((M, D), F32),
        scratch_shapes=[pltpu.VMEM((tm, D), F32)],
        compiler_params=pltpu.CompilerParams(
            dimension_semantics=("arbitrary", "arbitrary"),
            vmem_limit_bytes=V7X_VMEM_LIMIT_BYTES),
        name="moe_experts",
    )(h, comb, w_gate, w_up, w_down, x, mod)


N_COND = 8
N_MOD = 6
ROW_TM = 256


def _ada_kernel(c_ref, w_ref, b_ref, o_ref):
    c = c_ref[...]
    o_ref[0] = _mm(c * jax.nn.sigmoid(c), w_ref[0], NN_DIMS, 3) + b_ref[0]


def ada_modulation(c_rows, ada_w, ada_b, *, tn=2048):
    L, D, N = ada_w.shape
    return pl.pallas_call(
        _ada_kernel,
        grid=(L, N // tn),
        in_specs=[pl.BlockSpec((N_COND, D), lambda l, n: (0, 0)),
                  pl.BlockSpec((1, D, tn), lambda l, n: (l, 0, n)),
                  pl.BlockSpec((1, 1, tn), lambda l, n: (l, 0, n))],
        out_specs=pl.BlockSpec((1, N_COND, tn), lambda l, n: (l, 0, n)),
        out_shape=jax.ShapeDtypeStruct((L, N_COND, N), F32),
        compiler_params=pltpu.CompilerParams(
            dimension_semantics=("arbitrary", "arbitrary"), vmem_limit_bytes=V7X_VMEM_LIMIT_BYTES),
        name="ada_modulation",
    )(c_rows, ada_w, ada_b.reshape(L, 1, N))


def _norm_mod(x, g, shift, scale):
    y = x * lax.rsqrt(jnp.mean(x * x, axis=-1, keepdims=True) + EPS)
    return y * g * (1.0 + scale) + shift


def _prenorm_kernel(x_ref, c_ref, g_ref, mod_ref, o_ref, *, n_latent_tiles):
    def emit(src_ref):
        o_ref[...] = _norm_mod(src_ref[...], g_ref[...], mod_ref[0, 0:1, :], mod_ref[0, 1:2, :]).astype(o_ref.dtype)

    @pl.when(pl.program_id(0) < n_latent_tiles)
    def _():
        emit(x_ref)

    @pl.when(pl.program_id(0) >= n_latent_tiles)
    def _():
        emit(c_ref)


def _top2_sum(a, b, c, d):
    hi1, lo1 = jnp.maximum(a, b), jnp.minimum(a, b)
    hi2, lo2 = jnp.maximum(c, d), jnp.minimum(c, d)
    return jnp.maximum(hi1, hi2) + jnp.maximum(jnp.minimum(hi1, hi2), jnp.maximum(lo1, lo2))


def _prenorm_router_kernel(x_ref, g_ref, mod_ref, rw_ref, rb_ref, o_ref, comb_ref):
    h = _norm_mod(x_ref[...], g_ref[...], mod_ref[0, 3:4, :], mod_ref[0, 4:5, :])
    o_ref[...] = h.astype(o_ref.dtype)
    s = jax.nn.sigmoid(_mm(rw_ref[...], h, NT_DIMS, 3))
    biased = s + rb_ref[...]
    rows = [biased[e:e + 1, :] for e in range(N_EXPERTS)]
    G = EXPERTS_PER_GROUP
    score = [_top2_sum(*rows[g * G:(g + 1) * G]) for g in range(N_GROUPS)]
    picked = []
    for e in range(N_EXPERTS):
        g = e // G
        ok = None
        for g2 in range(N_GROUPS):
            if g2 != g:
                t = (score[g] > score[g2]) if g2 < g else (score[g] >= score[g2])
                ok = t if ok is None else jnp.logical_and(ok, t)
        rank = 0.0
        for e2 in range(g * G, (g + 1) * G):
            if e2 != e:
                ahead = (rows[e2] >= rows[e]) if e2 < e else (rows[e2] > rows[e])
                rank = rank + jnp.where(ahead, 1.0, 0.0)
        picked.append(jnp.where(jnp.logical_and(ok, rank < TOP_K), s[e:e + 1, :], 0.0))
    w = jnp.concatenate(picked, axis=0)
    comb_ref[...] = w / jnp.sum(w, axis=0, keepdims=True)


def _cond_of_tile(rows_per_cond, tile_rows, fixed_row):
    if fixed_row is not None:
        return lambda m: fixed_row
    assert rows_per_cond % tile_rows == 0
    return lambda m: m // (rows_per_cond // tile_rows)


def prenorm(x, ctx, gain, mod, *, seq_len):
    n_lat, n_ctx = x.shape[0] // ROW_TM, ctx.shape[0] // ROW_TM
    n_batch = x.shape[0] // seq_len
    cond = _cond_of_tile(seq_len, ROW_TM, None)
    return pl.pallas_call(
        functools.partial(_prenorm_kernel, n_latent_tiles=n_lat),
        grid=(n_lat + n_ctx,),
        in_specs=[pl.BlockSpec((ROW_TM, D_MODEL), lambda i: (jnp.minimum(i, n_lat - 1), 0)),
                  pl.BlockSpec((ROW_TM, D_MODEL), lambda i: (jnp.maximum(i - n_lat, 0), 0)),
                  pl.BlockSpec((1, D_MODEL), lambda i: (0, 0)),
                  pl.BlockSpec((1, N_MOD, D_MODEL), lambda i: (jnp.where(i < n_lat, cond(i), n_batch), 0, 0))],
        out_specs=pl.BlockSpec((ROW_TM, D_MODEL), lambda i: (i, 0)),
        out_shape=jax.ShapeDtypeStruct((x.shape[0] + ctx.shape[0], D_MODEL), BF16),
        compiler_params=pltpu.CompilerParams(dimension_semantics=("arbitrary",)),
        name="prenorm",
    )(x, ctx, gain, mod)


def prenorm_router(x, gain, mod, router_w_t, router_bias, *, rows_per_cond=None, fixed_row=None):
    M = x.shape[0]
    cond = _cond_of_tile(rows_per_cond, ROW_TM, fixed_row)
    x_spec = pl.BlockSpec((ROW_TM, D_MODEL), lambda i: (i, 0))
    return pl.pallas_call(
        _prenorm_router_kernel,
        grid=(M // ROW_TM,),
        in_specs=[x_spec,
                  pl.BlockSpec((1, D_MODEL), lambda i: (0, 0)),
                  pl.BlockSpec((1, N_MOD, D_MODEL), lambda i: (cond(i), 0, 0)),
                  pl.BlockSpec((N_EXPERTS, D_MODEL), lambda i: (0, 0)),
                  pl.BlockSpec((N_EXPERTS, 1), lambda i: (0, 0))],
        out_specs=[x_spec, pl.BlockSpec((N_EXPERTS, ROW_TM), lambda i: (0, i))],
        out_shape=[jax.ShapeDtypeStruct((M, D_MODEL), BF16),
                   jax.ShapeDtypeStruct((N_EXPERTS, M), F32)],
        compiler_params=pltpu.CompilerParams(dimension_semantics=("arbitrary",)),
        name="prenorm_router",
    )(x, gain, mod, router_w_t, router_bias)


def _merge_kernel(y0_ref, y1_ref, y2_ref, g0_ref, g1_ref, g2_ref, wb_ref, o_ref):
    acc = 0.0
    for i, (y_ref, g_ref) in enumerate(((y0_ref, g0_ref), (y1_ref, g1_ref), (y2_ref, g2_ref))):
        acc = acc + jax.nn.sigmoid(g_ref[...]) * jnp.dot(y_ref[...].astype(BF16), wb_ref[i],
                                                         preferred_element_type=F32)
    o_ref[...] = acc.astype(o_ref.dtype)


def merge_branches(ys, z_all, row0, w_branch, *, tm=512, tn=1024):
    M = ys[0].shape[0]
    assert M % tm == 0 and row0 % tm == 0 and C_GATE % tn == 0 and D_MODEL % tn == 0
    row_blk0 = row0 // tm
    y_spec = pl.BlockSpec((tm, BRANCH_W), lambda m, n: (m, 0))

    def gate_spec(i):
        return pl.BlockSpec((tm, tn), lambda m, n: (row_blk0 + m, (C_GATE + i * D_MODEL) // tn + n))

    return pl.pallas_call(
        _merge_kernel,
        grid=(M // tm, D_MODEL // tn),
        in_specs=[y_spec] * 3 + [gate_spec(i) for i in range(N_BRANCH)]
        + [pl.BlockSpec((N_BRANCH, BRANCH_W, tn), lambda m, n: (0, 0, n))],
        out_specs=pl.BlockSpec((tm, tn), lambda m, n: (m, n)),
        out_shape=jax.ShapeDtypeStruct((M, D_MODEL), BF16),
        compiler_params=pltpu.CompilerParams(
            dimension_semantics=("arbitrary", "arbitrary"), vmem_limit_bytes=V7X_VMEM_LIMIT_BYTES),
        name="merge_branches",
    )(*ys, z_all, z_all, z_all, w_branch.astype(BF16))


def _residual_mm_kernel(a_ref, w_ref, x_ref, mod_ref, o_ref, *, gate_row):
    y = jnp.dot(a_ref[...], w_ref[...], preferred_element_type=F32)
    o_ref[...] = x_ref[...] + mod_ref[0, gate_row:gate_row + 1, :] * y


def residual_matmul(a, w, x, mod, *, gate_row, rows_per_cond=None, fixed_row=None, tm=512, tn=512):
    M, K = a.shape
    N = w.shape[1]
    assert M % tm == 0 and N % tn == 0
    cond = _cond_of_tile(rows_per_cond, tm, fixed_row)
    return pl.pallas_call(
        functools.partial(_residual_mm_kernel, gate_row=gate_row),
        grid=(N // tn, M // tm),
        in_specs=[pl.BlockSpec((tm, K), lambda n, m: (m, 0)),
                  pl.BlockSpec((K, tn), lambda n, m: (0, n)),
                  pl.BlockSpec((tm, tn), lambda n, m: (m, n)),
                  pl.BlockSpec((1, N_MOD, tn), lambda n, m: (cond(m), 0, n))],
        out_specs=pl.BlockSpec((tm, tn), lambda n, m: (m, n)),
        out_shape=jax.ShapeDtypeStruct((M, N), F32),
        compiler_params=pltpu.CompilerParams(
            dimension_semantics=("arbitrary", "arbitrary"), vmem_limit_bytes=V7X_VMEM_LIMIT_BYTES),
        name="residual_matmul",
    )(a, w.astype(BF16), x, mod)


def _head_sum(x2):
    row = lax.broadcasted_iota(jnp.int32, (LANES, LANES), 0) // NA_HEAD_DIM
    col = lax.broadcasted_iota(jnp.int32, (LANES, LANES), 1) // NA_HEAD_DIM
    ones_bd = jnp.where(row == col, 1.0, 0.0).astype(BF16)
    return _cumsum_rows_right(x2, ones_bd)


def _cumsum_rows_right(x, m):
    hi, lo = _split_bf16(x)
    return jnp.dot(hi, m, preferred_element_type=F32) + jnp.dot(lo, m, preferred_element_type=F32)


def _na_qkv_kernel(x_ref, g_ref, o_ref):
    x = x_ref[...]

    @pl.when(pl.program_id(1) < 2)
    def _():
        ms = _heads_sum(x * x) * (1.0 / NA_HEAD_DIM)
        o_ref[...] = (x * lax.rsqrt(ms + EPS) * g_ref[...]).astype(o_ref.dtype)

    @pl.when(pl.program_id(1) == 2)
    def _():
        o_ref[...] = x.astype(o_ref.dtype)


def na_qkv(z, gains, *, tm=512):
    M = z.shape[0]
    return pl.pallas_call(
        _na_qkv_kernel,
        grid=(M // tm, 3),
        in_specs=[pl.BlockSpec((tm, NA_W), lambda i, j: (i, j)),
                  pl.BlockSpec((1, NA_W), lambda i, j: (0, jnp.minimum(j, 1)))],
        out_specs=pl.BlockSpec((tm, NA_W), lambda i, j: (i, j)),
        out_shape=jax.ShapeDtypeStruct((M, 3 * NA_W), BF16),
        compiler_params=pltpu.CompilerParams(dimension_semantics=("arbitrary", "arbitrary")),
        name="na_qkv",
    )(z, gains)


def _softmax_pv(s_list, v_list):
    m = s_list[0].max(axis=-1, keepdims=True)
    for s in s_list[1:]:
        m = jnp.maximum(m, s.max(axis=-1, keepdims=True))
    den = 0.0
    acc = 0.0
    for s, v in zip(s_list, v_list):
        p = jnp.exp(s - m)
        den = den + p.sum(axis=-1, keepdims=True)
        acc = acc + jnp.dot(p.astype(BF16), v, preferred_element_type=F32)
    return acc / den


def _na_kernel(*refs, n_rows, win_r, rows_per_step):
    q_ref, k_ref, v_ref, kc_ref, vc_ref = refs[:5]
    bias_refs = refs[5:5 + rows_per_step]
    o_ref = refs[-1]
    kc = kc_ref[...]
    vc = vc_ref[...]
    lane = lax.broadcasted_iota(jnp.int32, (GRID_W, LANES), 1)
    chains = [(i, h) for i in range(rows_per_step) for h in range(2)]
    kw, vw = [], []
    for i in range(rows_per_step):
        r = pl.program_id(2) * rows_per_step + i
        rs = jnp.clip(r - win_r // 2, 0, n_rows - win_r)
        start = pl.multiple_of(rs * GRID_W, GRID_W)
        kw.append(k_ref[pl.ds(start, win_r * GRID_W), :])
        vw.append(v_ref[pl.ds(start, win_r * GRID_W), :])
    qh = []
    for i, h in chains:
        q = q_ref[i * GRID_W:(i + 1) * GRID_W, :]
        qh.append(jnp.where((lane // NA_HEAD_DIM) == h, q, jnp.zeros_like(q)))
    s_loc = [lax.dot_general(qh[c], kw[i], NT_DIMS, preferred_element_type=F32) + bias_refs[i][h, 0]
             for c, (i, h) in enumerate(chains)]
    s_ctx = [lax.dot_general(qh[c], kc, NT_DIMS, preferred_element_type=F32) for c in range(len(chains))]
    m = [jnp.maximum(s_loc[c].max(axis=-1, keepdims=True), s_ctx[c].max(axis=-1, keepdims=True))
         for c in range(len(chains))]
    p_loc = [jnp.exp(s_loc[c] - m[c]) for c in range(len(chains))]
    p_ctx = [jnp.exp(s_ctx[c] - m[c]) for c in range(len(chains))]
    den = [p_loc[c].sum(axis=-1, keepdims=True) + p_ctx[c].sum(axis=-1, keepdims=True)
           for c in range(len(chains))]
    acc = [jnp.dot(p_loc[c].astype(BF16), vw[i], preferred_element_type=F32)
           + jnp.dot(p_ctx[c].astype(BF16), vc, preferred_element_type=F32)
           for c, (i, h) in enumerate(chains)]
    out = [acc[c] / den[c] for c in range(len(chains))]
    for i in range(rows_per_step):
        o_ref[i * GRID_W:(i + 1) * GRID_W, :] = jnp.where(lane < NA_HEAD_DIM, out[2 * i], out[2 * i + 1])


def _ctx_attn_kernel(q_ref, k_ref, v_ref, o_ref):
    q = q_ref[...]
    k = k_ref[...]
    v = v_ref[...]
    lane = lax.broadcasted_iota(jnp.int32, q.shape, 1)
    outs = []
    for h in range(2):
        qh = jnp.where((lane // NA_HEAD_DIM) == h, q, jnp.zeros_like(q))
        s = lax.dot_general(qh, k, NT_DIMS, preferred_element_type=F32)
        outs.append(_softmax_pv([s], [v]))
    o_ref[...] = jnp.where(lane < NA_HEAD_DIM, outs[0], outs[1]).astype(o_ref.dtype)


def na_bias_table(rpb, n_rows, win_r):
    col = np.arange(GRID_W)
    col_start = np.clip(col - NA_WIN_C // 2, 0, GRID_W - NA_WIN_C)
    kcol = np.arange(GRID_W)
    valid = (kcol[None, :] >= col_start[:, None]) & (kcol[None, :] < col_start[:, None] + NA_WIN_C)
    dc = np.clip(kcol[None, :] - col[:, None] + NA_WIN_C - 1, 0, 2 * NA_WIN_C - 2)
    n_dr = 2 * NA_WIN_R_MAX - 1
    per_dr = jnp.where(valid[None, None], rpb.astype(F32)[:, :, dc], NEG_BIG)

    def assemble(t_ref, o_ref):
        v = pl.program_id(1)
        o_ref[0, 0] = jnp.concatenate([t_ref[0, v + n] for n in range(win_r)], axis=1)

    return pl.pallas_call(
        assemble,
        grid=(rpb.shape[0], NA_WIN_R_MAX),
        in_specs=[pl.BlockSpec((1, n_dr, GRID_W, GRID_W), lambda h, v: (h, 0, 0, 0))],
        out_specs=pl.BlockSpec((1, 1, GRID_W, win_r * GRID_W), lambda h, v: (h, v, 0, 0)),
        out_shape=jax.ShapeDtypeStruct((rpb.shape[0], NA_WIN_R_MAX, GRID_W, win_r * GRID_W), F32),
        compiler_params=pltpu.CompilerParams(dimension_semantics=("arbitrary", "arbitrary")),
        name="na_bias_table",
    )(per_dr)


def na_attention(qkv, rpb, B, S, Lc, with_ctx):
    n_rows = S // GRID_W
    win_r = min(NA_WIN_R_MAX, n_rows)
    assert win_r == NA_WIN_R_MAX and (B * S) % Lc == 0
    HP = NA_HEADS // 2
    bias = na_bias_table(rpb, n_rows, win_r)
    ctx_blk0 = (B * S) // Lc

    def variant(r):
        return jnp.clip(r - win_r // 2, 0, n_rows - win_r) - r + NA_WIN_R_MAX - 1

    R = NA_ROWS_PER_STEP
    assert n_rows % R == 0
    steps = n_rows // R
    bias_specs = [pl.BlockSpec((2, 1, GRID_W, win_r * GRID_W),
                               lambda hp, b, r, i=i: (hp, variant(r * R + i), 0, 0)) for i in range(R)]
    y = pl.pallas_call(
        functools.partial(_na_kernel, n_rows=n_rows, win_r=win_r, rows_per_step=R),
        grid=(HP, B, steps),
        in_specs=[pl.BlockSpec((R * GRID_W, LANES), lambda hp, b, r: (b * steps + r, hp)),
                  pl.BlockSpec((S, LANES), lambda hp, b, r: (b, HP + hp)),
                  pl.BlockSpec((S, LANES), lambda hp, b, r: (b, 2 * HP + hp)),
                  pl.BlockSpec((Lc, LANES), lambda hp, b, r: (ctx_blk0 + b, HP + hp)),
                  pl.BlockSpec((Lc, LANES), lambda hp, b, r: (ctx_blk0 + b, 2 * HP + hp))] + bias_specs,
        out_specs=pl.BlockSpec((R * GRID_W, LANES), lambda hp, b, r: (b * steps + r, hp)),
        out_shape=jax.ShapeDtypeStruct((B * S, NA_W), F32),
        compiler_params=pltpu.CompilerParams(
            dimension_semantics=("arbitrary", "arbitrary", "arbitrary"),
            vmem_limit_bytes=V7X_VMEM_LIMIT_BYTES),
        name="na_attention",
    )(qkv, qkv, qkv, qkv, qkv, *([bias] * R))
    yc = None
    if with_ctx:
        yc = pl.pallas_call(
            _ctx_attn_kernel,
            grid=(HP, B),
            in_specs=[pl.BlockSpec((Lc, LANES), lambda hp, b: (ctx_blk0 + b, hp)),
                      pl.BlockSpec((Lc, LANES), lambda hp, b: (ctx_blk0 + b, HP + hp)),
                      pl.BlockSpec((Lc, LANES), lambda hp, b: (ctx_blk0 + b, 2 * HP + hp))],
            out_specs=pl.BlockSpec((Lc, LANES), lambda hp, b: (b, hp)),
            out_shape=jax.ShapeDtypeStruct((B * Lc, NA_W), F32),
            compiler_params=pltpu.CompilerParams(dimension_semantics=("arbitrary", "arbitrary")),
            name="ctx_attention",
        )(qkv, qkv, qkv)
    return y, yc


def na_mixer(z_all, B, S, Lc, q_norm, k_norm, rpb, with_ctx):
    assert (C_NAQ, C_NAK, C_NAV) == (0, NA_W, 2 * NA_W)
    scale = NA_HEAD_DIM ** -0.5
    gains = jnp.concatenate([jnp.tile(q_norm.astype(F32) * scale, NA_HEADS),
                             jnp.tile(k_norm.astype(F32), NA_HEADS)])[None]
    y, yc = na_attention(na_qkv(z_all, gains), rpb, B, S, Lc, with_ctx)
    return y, yc


def _log_sigmoid(x):
    return jnp.minimum(x, 0.0) - jnp.log(1.0 + jnp.exp(-jnp.abs(x)))


def _gla_kernel(*refs, reverse, rope, finish, n_chunks, nb):
    it = iter(refs)

    def take(count):
        return [next(it) for _ in range(count)]

    q_refs, k_refs, v_refs, gd_refs = take(nb), take(nb), take(nb), take(nb)
    w2_ref, gb_ref, s0_ref = take(3)
    if rope:
        cos_ref, sin_ref = take(2)
    if finish:
        yo_ref = next(it)
        r_refs = take(nb)
        g_ref = next(it)
    y_ref, sfin_ref, s_scr = take(3)

    C = GLA_CHUNK
    n = pl.program_id(1)
    chains = [(i, h) for i in range(nb) for h in range(GLA_HEADS)]

    @pl.when(n == 0)
    def _():
        s_scr[...] = s0_ref[...]

    def rows(refs_):
        return jnp.concatenate([r[...] for r in refs_], axis=0)

    ti = lax.broadcasted_iota(jnp.int32, (nb * C, nb * C), 0)
    tj = lax.broadcasted_iota(jnp.int32, (nb * C, nb * C), 1)
    same_seq = (ti // C) == (tj // C)
    tri = jnp.where(same_seq & ((tj >= ti) if reverse else (tj <= ti)), 1.0, 0.0).astype(BF16)
    ci = lax.broadcasted_iota(jnp.int32, (C, C), 0)
    cj = lax.broadcasted_iota(jnp.int32, (C, C), 1)
    incl = (cj >= ci) if reverse else (cj <= ci)

    pre = _mm(rows(gd_refs), w2_ref[...], NN_DIMS, 3) + gb_ref[...]
    la = _log_sigmoid(pre) * (1.0 / GLA_GATE_TAU)
    cs = _cumsum_rows(tri, la)
    ref_i = C // 2 - 1 if reverse else C // 2
    last_i = 0 if reverse else C - 1

    def per_seq_row(idx):
        return jnp.concatenate([jnp.broadcast_to(cs[i * C + idx:i * C + idx + 1, :], (C, GLA_QK))
                                for i in range(nb)], axis=0)

    b_ref = per_seq_row(ref_i)
    b_last = per_seq_row(last_i)

    q = rows(q_refs) * (GLA_DK ** -0.5)
    k = rows(k_refs)
    if rope:
        lane = lax.broadcasted_iota(jnp.int32, q.shape, 1)
        first = (lane % (GLA_DK // 2)) < GLA_DK // 4
        cos = jnp.concatenate([jnp.concatenate([cos_ref[...]] * GLA_HEADS, axis=1)] * nb, axis=0)
        sin = jnp.concatenate([jnp.concatenate([sin_ref[...]] * GLA_HEADS, axis=1)] * nb, axis=0)

        def rot(x):
            partner = jnp.where(first, pltpu.roll(x, GLA_QK - GLA_DK // 4, axis=1),
                                pltpu.roll(x, GLA_DK // 4, axis=1))
            return x * cos + partner * sin

        q = rot(q)
        k = rot(k)
    qi = q * jnp.exp(cs - b_ref)
    kj = k * jnp.exp(b_ref - cs)
    qe = q * jnp.exp(cs)
    ke = k * jnp.exp(b_last - cs)
    dec = jnp.exp(b_last)
    v_all = rows(v_refs)

    def hk(x, i, h):
        return x[i * C:(i + 1) * C, h * GLA_DK:(h + 1) * GLA_DK]

    def hv(x, i, h):
        return x[i * C:(i + 1) * C, h * GLA_DV:(h + 1) * GLA_DV]

    St = [s_scr[i, h] for i, h in chains]
    V = [hv(v_all, i, h) for i, h in chains]
    att = [jnp.where(incl, _mm(hk(qi, i, h), hk(kj, i, h), NT_DIMS), 0.0) for i, h in chains]
    y = [_mm(att[c], V[c], NN_DIMS) + _mm(hk(qe, i, h), St[c], NT_DIMS) for c, (i, h) in enumerate(chains)]
    kvt = [_mm(V[c], hk(ke, i, h), TN_DIMS) for c, (i, h) in enumerate(chains)]
    for c, (i, h) in enumerate(chains):
        s_scr[i, h] = St[c] * hk(dec, i, h)[0:1, :] + kvt[c]
    if finish:
        r = rows(r_refs)
        g = g_ref[...]
        for c, (i, h) in enumerate(chains):
            ys = y[c] + yo_ref[i, :, h * GLA_DV:(h + 1) * GLA_DV]
            ms = jnp.mean(ys * ys, axis=-1, keepdims=True)
            rh = hv(r, i, h)
            y[c] = ys * lax.rsqrt(ms + EPS) * g * (rh * jax.nn.sigmoid(rh))
    for i in range(nb):
        y_ref[i] = jnp.concatenate(y[i * GLA_HEADS:(i + 1) * GLA_HEADS], axis=1)

    @pl.when(n == n_chunks - 1)
    def _():
        sfin_ref[...] = s_scr[...]


def gla_rope_tables(seq_len):
    t = np.arange(seq_len)
    quarter = GLA_DK // 4
    inv = ROPE_BASE ** (-np.arange(0, 2 * quarter, 2, dtype=np.float64) / (2 * quarter))
    ar = (t // GRID_W)[:, None] * inv[None, :]
    ac = (t % GRID_W)[:, None] * inv[None, :]
    cos = np.concatenate([np.cos(ar), np.cos(ar), np.cos(ac), np.cos(ac)], axis=1)
    sin = np.concatenate([-np.sin(ar), np.sin(ar), -np.sin(ac), np.sin(ac)], axis=1)
    return jnp.asarray(cos, F32), jnp.asarray(sin, F32)


def gla_scan(z, w2pad, gate_b, s0, *, batch, seq_len, row0, reverse, rope, finish_with=None):
    N = seq_len // GLA_CHUNK
    blk0 = row0 // GLA_CHUNK
    off_q, off_k, off_v, off_r, off_gd = C_GQ, C_GK, C_GV, C_GR, C_GGD
    assert off_q % GLA_QK == 0 and off_k % GLA_QK == 0 and off_v % GLA_V == 0 and off_r % GLA_V == 0
    assert off_gd % LANES == 0

    def tb(n):
        return N - 1 - n if reverse else n

    def rb(b, n):
        return blk0 + b * N + tb(n)

    nb = GLA_BATCH_PER_STEP if batch % GLA_BATCH_PER_STEP == 0 else 1

    def tok_specs(width, col_off):
        return [pl.BlockSpec((GLA_CHUNK, width), lambda bb, n, i=i: (rb(bb * nb + i, n), col_off // width))
                for i in range(nb)]

    st_spec = pl.BlockSpec((nb, GLA_HEADS, GLA_DV, GLA_DK), lambda bb, n: (bb, 0, 0, 0))
    y_spec = pl.BlockSpec((nb, GLA_CHUNK, GLA_V), lambda bb, n: (bb, tb(n), 0))
    in_specs = (tok_specs(GLA_QK, off_q) + tok_specs(GLA_QK, off_k) + tok_specs(GLA_V, off_v)
                + tok_specs(LANES, off_gd)
                + [pl.BlockSpec((LANES, GLA_QK), lambda bb, n: (0, 0)),
                   pl.BlockSpec((1, GLA_QK), lambda bb, n: (0, 0)),
                   st_spec])
    args = [z] * (4 * nb) + [w2pad, gate_b, s0]
    if rope:
        cos, sin = gla_rope_tables(seq_len)
        in_specs += [pl.BlockSpec((GLA_CHUNK, GLA_DK), lambda bb, n: (tb(n), 0))] * 2
        args += [cos, sin]
    if finish_with is not None:
        y_other, norm_g = finish_with
        in_specs += [y_spec] + tok_specs(GLA_V, off_r) + [pl.BlockSpec((1, GLA_DV), lambda bb, n: (0, 0))]
        args += [y_other.reshape(batch, seq_len, GLA_V)] + [z] * nb + [norm_g]
    y, s_fin = pl.pallas_call(
        functools.partial(_gla_kernel, reverse=reverse, rope=rope, finish=finish_with is not None,
                          n_chunks=N, nb=nb),
        grid=(batch // nb, N),
        in_specs=in_specs,
        out_specs=[y_spec, st_spec],
        out_shape=[jax.ShapeDtypeStruct((batch, seq_len, GLA_V), F32),
                   jax.ShapeDtypeStruct((batch, GLA_HEADS, GLA_DV, GLA_DK), F32)],
        scratch_shapes=[pltpu.VMEM((nb, GLA_HEADS, GLA_DV, GLA_DK), F32)],
        compiler_params=pltpu.CompilerParams(
            dimension_semantics=("arbitrary", "arbitrary"),
            vmem_limit_bytes=V7X_VMEM_LIMIT_BYTES),
        name="gla_scan_rev" if reverse else "gla_scan",
    )(*args)
    return y.reshape(batch * seq_len, GLA_V), s_fin


def gla_mixer(z_all, B, S, Lc, gate_w2, gate_b, norm_g, with_ctx):
    s0 = jnp.zeros((B, GLA_HEADS, GLA_DV, GLA_DK), F32)
    g = norm_g.astype(F32)[None]
    w2 = [jnp.zeros((LANES, GLA_QK), F32).at[d * GLA_GATE_RANK:(d + 1) * GLA_GATE_RANK].set(gate_w2[d])
          for d in range(2)]
    gb = [gate_b[d].astype(F32)[None] for d in range(2)]
    ctx = dict(batch=B, seq_len=Lc, row0=B * S, rope=False)
    lat = dict(batch=B, seq_len=S, row0=0, rope=True)
    yc_f, sc_f = gla_scan(z_all, w2[0], gb[0], s0, reverse=False, **ctx)
    yc, sc_b = gla_scan(z_all, w2[1], gb[1], s0, reverse=True, finish_with=(yc_f, g), **ctx)
    yl_f, _ = gla_scan(z_all, w2[0], gb[0], sc_f, reverse=False, **lat)
    yl, _ = gla_scan(z_all, w2[1], gb[1], sc_b, reverse=True, finish_with=(yl_f, g), **lat)
    return yl, (yc if with_ctx else None)


RW_PREP_TM = 256
RW_HALO = 8
RW_N_PIECES = 5


def _heads_sum(x):
    return jnp.concatenate([_head_sum(x[:, j * LANES:(j + 1) * LANES]) for j in range(x.shape[1] // LANES)],
                           axis=1)


def _rw_prep_kernel(*refs, tm, n_latent_tiles, tiles_per_seq):
    P = RW_N_PIECES
    z_refs, prev_refs, next_refs, mu_refs = refs[0:P], refs[P:2 * P], refs[2 * P:3 * P], refs[3 * P:4 * P]
    kk_ref, ka_ref, rk_ref, w0_ref, a0_ref, w2_ref, a2_ref, g2_ref = refs[4 * P:4 * P + 8]
    (r_o, v_o, kap_o, lw0_o, kd0_o, b0_o, lw1_o, kd1_o, b1_o, bonus_o, g_o) = refs[4 * P + 8:]
    i = pl.program_id(0)
    latent = i < n_latent_tiles
    first = jnp.logical_or(jnp.logical_not(latent), i % tiles_per_seq == 0)
    last = jnp.logical_or(jnp.logical_not(latent), i % tiles_per_seq == tiles_per_seq - 1)

    def shifted(j):
        z = z_refs[j][...]
        row = lax.broadcasted_iota(jnp.int32, z.shape, 0)
        prev_row = jnp.where(first, 0.0, prev_refs[j][RW_HALO - 1:RW_HALO, :])
        next_row = jnp.where(last, 0.0, next_refs[j][0:1, :])
        prev = jnp.where(row == 0, prev_row, pltpu.roll(z, 1, axis=0))
        nxt = jnp.where(row == tm - 1, next_row, pltpu.roll(z, tm - 1, axis=0))
        return z + mu_refs[j][...] * (0.5 * (prev + nxt) - z)

    r, k, v, lr, gd = (shifted(j) for j in range(P))
    kkr = k * kk_ref[...]
    kap = kkr * lax.rsqrt(_heads_sum(kkr * kkr) + EPS)
    r_o[...] = r
    v_o[...] = v
    kap_o[...] = kap
    bonus_o[...] = _heads_sum(r * k * rk_ref[...]) * v
    g_o[...] = _mm(jax.nn.sigmoid(gd), g2_ref[...], NN_DIMS, 3)
    ka = ka_ref[...]
    for d, (lw_o, kd_o, b_o) in enumerate(((lw0_o, kd0_o, b0_o), (lw1_o, kd1_o, b1_o))):
        wd = lr[:, d * LANES:(d + 1) * LANES]
        ad = lr[:, (2 + d) * LANES:(3 + d) * LANES]
        w_log = _log_sigmoid(w0_ref[d:d + 1, :] + _mm(jnp.tanh(wd), w2_ref[d], NN_DIMS, 3)) - 0.5
        lw_o[...] = -jnp.exp(w_log)
        a = jax.nn.sigmoid(a0_ref[d:d + 1, :] + _mm(ad, a2_ref[d], NN_DIMS, 3))
        kd_o[...] = k * (1.0 + (a - 1.0) * ka)
        b_o[...] = kap * a


def rwkv_prep(z_all, n_latent_rows, seq_len, mu_p, k_k, k_a, r_k, w0, a0, w2p, a2p, g2):
    M = z_all.shape[0]
    tm = RW_PREP_TM
    assert M % tm == 0 and n_latent_rows % tm == 0 and seq_len % tm == 0
    widths = (RW_W, RW_W, RW_W, RW_LR_W, RW_GATE_RANK)
    offs = (C_RR, C_RK, C_RV, C_RLR, C_RGD)
    n_halo_blocks = M // RW_HALO
    z_specs, prev_specs, next_specs, mu_specs = [], [], [], []
    for w, o in zip(widths, offs):
        assert o % w == 0
        cb = o // w
        z_specs.append(pl.BlockSpec((tm, w), lambda i, cb=cb: (i, cb)))
        prev_specs.append(pl.BlockSpec(
            (RW_HALO, w), lambda i, cb=cb: (jnp.maximum(i * (tm // RW_HALO) - 1, 0), cb)))
        next_specs.append(pl.BlockSpec(
            (RW_HALO, w), lambda i, cb=cb: (jnp.minimum((i + 1) * (tm // RW_HALO), n_halo_blocks - 1), cb)))
        mu_specs.append(pl.BlockSpec((1, w), lambda i: (0, 0)))

    def full(shape):
        return pl.BlockSpec(shape, lambda i: (0,) * len(shape))

    par_specs = [full((1, RW_W))] * 3 + [full((2, RW_W))] * 2 + [full((2, LANES, RW_W))] * 2 \
        + [full((RW_GATE_RANK, RW_W))]
    out_spec = pl.BlockSpec((tm, RW_W), lambda i: (i, 0))
    return pl.pallas_call(
        functools.partial(_rw_prep_kernel, tm=tm, n_latent_tiles=n_latent_rows // tm,
                          tiles_per_seq=seq_len // tm),
        grid=(M // tm,),
        in_specs=z_specs + prev_specs + next_specs + mu_specs + par_specs,
        out_specs=[out_spec] * 11,
        out_shape=[jax.ShapeDtypeStruct((M, RW_W), F32)] * 11,
        compiler_params=pltpu.CompilerParams(
            dimension_semantics=("arbitrary",), vmem_limit_bytes=V7X_VMEM_LIMIT_BYTES),
        name="rwkv_prep",
    )(*([z_all] * 15), *mu_p, k_k, k_a, r_k, w0, a0, w2p, a2p, g2)


def _rw_chunk_kernel(*refs, reverse, n_par, n_chunks, passes, finish, nb):
    it = iter(refs)

    def side_by_side():
        return jnp.concatenate([next(it)[...] for _ in range(nb)], axis=1)

    r_in, v_all, kap_in, lw_all, kd_all, bet_all = (side_by_side() for _ in range(6))
    s0_ref = next(it)
    if finish:
        yo_ref = next(it)
        bonus, gate = side_by_side(), side_by_side()
        gamma = jnp.concatenate([next(it)[...]] * nb, axis=1)
        beta = jnp.concatenate([next(it)[...]] * nb, axis=1)
    y_ref, sfin_ref, s_scr = next(it), next(it), next(it)

    C = RW_CHUNK
    n = pl.program_id(2)
    P = range(nb * n_par)

    @pl.when(n == 0)
    def _():
        s_scr[...] = s0_ref[:, 0]

    row = lax.broadcasted_iota(jnp.int32, (C, LANES), 0)
    lane = lax.broadcasted_iota(jnp.int32, (C, LANES), 1)
    col = lane % RW_HEAD
    head0 = lane < RW_HEAD
    strict = (col > row) if reverse else (col < row)
    incl = (col >= row) if reverse else (col <= row)
    ti = lax.broadcasted_iota(jnp.int32, (C, C), 0)
    tj = lax.broadcasted_iota(jnp.int32, (C, C), 1)
    tri = jnp.where((tj >= ti) if reverse else (tj <= ti), 1.0, 0.0).astype(BF16)
    eye = jnp.where(row == col, 1.0, 0.0)
    brow = lax.broadcasted_iota(jnp.int32, (LANES, LANES), 0) // RW_HEAD
    bcol = lax.broadcasted_iota(jnp.int32, (LANES, LANES), 1) // RW_HEAD
    same_head = brow == bcol

    def sl(x, p):
        return x[:, p * LANES:(p + 1) * LANES]

    def bd(x):
        return jnp.concatenate([jnp.where(head0, x, 0.0), jnp.where(head0, 0.0, x)], axis=0)

    def pp(a_pair, b_pair):
        return _mm(a_pair, bd(b_pair), NN_DIMS, passes)

    cs_all = _cumsum_rows(tri, lw_all)
    tot_all = cs_all[0:1, :] if reverse else cs_all[C - 1:C, :]
    inv_all = jnp.exp(-cs_all)
    fin_all = jnp.exp(tot_all - cs_all)
    kap_t = kap_in * jnp.exp(cs_all - lw_all)
    r_t = r_in * jnp.exp(cs_all)
    k_t = kd_all * inv_all
    b_t = bet_all * inv_all
    k_h = kd_all * fin_all
    b_h = bet_all * fin_all
    dec = jnp.exp(tot_all)

    S = [s_scr[p // n_par, p % n_par] for p in P]
    V = [sl(v_all, p) for p in P]
    lhs = [jnp.concatenate([sl(kap_t, p), sl(r_t, p)], axis=0) for p in P]
    a_k = [_mm(lhs[p], bd(sl(k_t, p)), NT_DIMS, passes) for p in P]
    a_b = [_mm(lhs[p], bd(sl(b_t, p)), NT_DIMS, passes) for p in P]
    s0_both = [_mm(lhs[p], S[p], NT_DIMS, passes) for p in P]
    lmat = [jnp.where(strict, a_b[p][:C], 0.0) for p in P]
    power = [-lmat[p] for p in P]
    t_inv = [eye + power[p] for p in P]
    s = 2
    while s < C:
        power = [pp(power[p], power[p]) for p in P]
        t_inv = [t_inv[p] + pp(t_inv[p], power[p]) for p in P]
        s *= 2
    x = [s0_both[p][:C] + pp(jnp.where(strict, a_k[p][:C], 0.0), V[p]) for p in P]
    u = [pp(t_inv[p], x[p]) for p in P]
    y = [s0_both[p][C:] + pp(jnp.where(incl, a_k[p][C:], 0.0), V[p])
         - pp(jnp.where(incl, a_b[p][C:], 0.0), u[p]) for p in P]
    upd = [_mm(V[p], sl(k_h, p), TN_DIMS, passes) - _mm(u[p], sl(b_h, p), TN_DIMS, passes) for p in P]
    for p in P:
        s_scr[p // n_par, p % n_par] = S[p] * sl(dec, p) + jnp.where(same_head, upd[p], 0.0)
    width = n_par * LANES
    if finish:
        yo = jnp.concatenate([yo_ref[i] for i in range(nb)], axis=1)
        ys = jnp.concatenate(y, axis=1) + yo
        cen = ys - _heads_sum(ys) * (1.0 / RW_HEAD)
        var = _heads_sum(cen * cen) * (1.0 / RW_HEAD)
        out = (cen * lax.rsqrt(var + RW_GN_EPS) * gamma + beta + bonus) * gate
    else:
        out = jnp.concatenate(y, axis=1)
    for i in range(nb):
        y_ref[i] = out[:, i * width:(i + 1) * width]

    @pl.when(n == n_chunks - 1)
    def _():
        sfin_ref[:, 0] = s_scr[...]


def rwkv_chunk_scan(r, v, kap, lw, kd, bet, s0, *, batch, seq_len, row0, reverse, finish_with=None):
    n_par = RW_PAIRS_PER_STEP
    nb = RW_BATCH_PER_STEP if batch % RW_BATCH_PER_STEP == 0 else 1
    W = r.shape[1]
    G = W // (n_par * LANES)
    N = seq_len // RW_CHUNK
    blk0 = row0 // RW_CHUNK

    def tb(n):
        return N - 1 - n if reverse else n

    def seq_in():
        return [pl.BlockSpec((RW_CHUNK, n_par * LANES),
                             lambda bb, g, n, i=i: (blk0 + (bb * nb + i) * N + tb(n), g)) for i in range(nb)]

    y_spec = pl.BlockSpec((nb, RW_CHUNK, n_par * LANES), lambda bb, g, n: (bb, tb(n), g))
    st_spec = pl.BlockSpec((nb, 1, n_par, LANES, LANES), lambda bb, g, n: (bb, g, 0, 0, 0))
    in_specs = seq_in() * 6 + [st_spec]
    args = [a for a in (r, v, kap, lw, kd, bet) for _ in range(nb)] + [s0]
    if finish_with is not None:
        y_other, bonus, gate, gamma, beta = finish_with
        vec_spec = pl.BlockSpec((1, n_par * LANES), lambda bb, g, n: (0, g))
        in_specs += [y_spec] + seq_in() * 2 + [vec_spec, vec_spec]
        args += [y_other.reshape(batch, seq_len, W)] + [bonus] * nb + [gate] * nb + [gamma, beta]
    y, s_fin = pl.pallas_call(
        functools.partial(_rw_chunk_kernel, reverse=reverse, n_par=n_par, n_chunks=N, passes=RW_PASSES,
                          finish=finish_with is not None, nb=nb),
        grid=(batch // nb, G, N),
        in_specs=in_specs,
        out_specs=[y_spec, st_spec],
        out_shape=[jax.ShapeDtypeStruct((batch, seq_len, W), F32),
                   jax.ShapeDtypeStruct((batch, G, n_par, LANES, LANES), F32)],
        scratch_shapes=[pltpu.VMEM((nb, n_par, LANES, LANES), F32)],
        compiler_params=pltpu.CompilerParams(
            dimension_semantics=("arbitrary", "arbitrary", "arbitrary"),
            vmem_limit_bytes=V7X_VMEM_LIMIT_BYTES),
        name="rwkv_chunk_scan_rev" if reverse else "rwkv_chunk_scan",
    )(*args)
    return y.reshape(batch * seq_len, W), s_fin


def _pad_rows(w, rows):
    return jnp.pad(w.astype(F32), ((0, 0), (0, rows - w.shape[1]), (0, 0)))


def rwkv_mixer(z_all, B, S, Lc, mu, w0, w2, a0, a2, g2, k_k, k_a, r_k, gn_g, gn_b, with_ctx):
    assert Lc == RW_PREP_TM
    mu = mu.astype(F32)
    o = np.cumsum((0,) + RW_SPLITS)
    lr_pad = jnp.zeros((LANES - RW_DECAY_RANK,), F32)
    mu_lr = jnp.concatenate([mu[o[3]:o[3] + RW_DECAY_RANK], lr_pad, mu[o[3] + RW_DECAY_RANK:o[4]], lr_pad,
                             mu[o[4]:o[4] + RW_A_RANK], lr_pad, mu[o[4] + RW_A_RANK:o[5]], lr_pad])
    mu_p = [mu[o[0]:o[1]][None], mu[o[1]:o[2]][None], mu[o[2]:o[3]][None], mu_lr[None], mu[o[5]:o[6]][None]]
    row = lambda t: t.astype(F32).reshape(1, RW_W)
    prep = rwkv_prep(z_all, B * S, S, mu_p, row(k_k), row(k_a), row(r_k), w0.astype(F32), a0.astype(F32),
                     _pad_rows(w2, LANES), _pad_rows(a2, LANES), g2.astype(F32))
    r, v, kap, lw0, kd0, b0, lw1, kd1, b1, bonus, gate = prep
    fin = (bonus, gate, row(gn_g), row(gn_b))
    s0 = jnp.zeros((B, RW_W // (RW_PAIRS_PER_STEP * LANES), RW_PAIRS_PER_STEP, LANES, LANES), F32)
    ctx = dict(batch=B, seq_len=Lc, row0=B * S)
    lat = dict(batch=B, seq_len=S, row0=0)
    yc_f, sc_f = rwkv_chunk_scan(r, v, kap, lw0, kd0, b0, s0, reverse=False, **ctx)
    yc, sc_b = rwkv_chunk_scan(r, v, kap, lw1, kd1, b1, s0, reverse=True, finish_with=(yc_f,) + fin, **ctx)
    yl_f, _ = rwkv_chunk_scan(r, v, kap, lw0, kd0, b0, sc_f, reverse=False, **lat)
    yl, _ = rwkv_chunk_scan(r, v, kap, lw1, kd1, b1, sc_b, reverse=True, finish_with=(yl_f,) + fin, **lat)
    return yl, (yc if with_ctx else None)


def relayout_w_in(w):
    o = IN_OFFSETS
    rw = o[8] + np.cumsum((0,) + RW_SPLITS)
    zeros = lambda n: jnp.zeros((w.shape[0], n), w.dtype)
    seg = lambda lo, n: w[:, lo:lo + n]
    lr_pad = LANES - RW_DECAY_RANK
    pieces = [
        seg(o[0], 3 * NA_W),
        seg(o[5], GLA_V), seg(o[6], GLA_V),
        seg(rw[0], 3 * RW_W),
        seg(o[9], N_BRANCH * D_MODEL),
        seg(o[3], GLA_QK), seg(o[4], GLA_QK),
        seg(rw[3], RW_DECAY_RANK), zeros(lr_pad), seg(rw[3] + RW_DECAY_RANK, RW_DECAY_RANK), zeros(lr_pad),
        seg(rw[4], RW_A_RANK), zeros(lr_pad), seg(rw[4] + RW_A_RANK, RW_A_RANK), zeros(lr_pad),
        seg(rw[5], RW_GATE_RANK),
        seg(o[7], 2 * GLA_GATE_RANK),
    ]
    used = sum(x.shape[1] for x in pieces)
    assert used == C_GGD + 2 * GLA_GATE_RANK
    return jnp.concatenate(pieces + [zeros(Z_W - used)], axis=1).astype(BF16)


def token_mixers(x, ctx, mod, p, B, S, Lc, with_ctx):
    h_all = prenorm(x, ctx, p['norm1_g'], mod, seq_len=S)
    z_all = matmul(h_all, relayout_w_in(p['w_in']), tm=1024, tn=1024)
    y_na, yc_na = na_mixer(z_all, B, S, Lc, p['na_q_norm'], p['na_k_norm'], p['na_rpb'], with_ctx)
    y_gla, yc_gla = gla_mixer(z_all, B, S, Lc, p['gla_gate_w2'], p['gla_gate_b'], p['gla_norm_g'], with_ctx)
    y_rw, yc_rw = rwkv_mixer(z_all, B, S, Lc, p['rw_mu'], p['rw_w0'], p['rw_w2'], p['rw_a0'], p['rw_a2'],
                             p['rw_g2'], p['rw_k_k'], p['rw_k_a'], p['rw_r_k'], p['rw_gn_g'],
                             p['rw_gn_b'], with_ctx)
    merged = merge_branches((y_na, y_gla, y_rw), z_all, 0, p['w_branch'])
    x = residual_matmul(merged, p['w_out'], x, mod, gate_row=2, rows_per_cond=S)
    if with_ctx:
        merged_c = merge_branches((yc_na, yc_gla, yc_rw), z_all, B * S, p['w_branch'])
        ctx = residual_matmul(merged_c, p['w_out'], ctx, mod, gate_row=2, fixed_row=B)
    return x, ctx


def moe_sublayer(x, mod, p, *, rows_per_cond=None, fixed_row=None):
    h2, comb_t = prenorm_router(x, p['norm2_g'], mod, p['router_w_t'], p['router_bias'],
                                rows_per_cond=rows_per_cond, fixed_row=fixed_row)
    return moe_experts(h2, comb_t.T, p['moe_w_gate'], p['moe_w_up'], p['moe_w_down'], x, mod,
                       rows_per_cond=rows_per_cond, fixed_row=fixed_row)


def kernel(x, c, ctx, c_ctx, ada_w, ada_b, norm1_g, norm2_g, w_in, na_q_norm, na_k_norm, na_rpb,
           gla_gate_w2, gla_gate_b, gla_norm_g, rw_mu, rw_w0, rw_w2, rw_a0, rw_a2, rw_g2,
           rw_k_k, rw_k_a, rw_r_k, rw_gn_g, rw_gn_b, w_branch, w_out, router_w, router_bias,
           moe_w_gate, moe_w_up, moe_w_down):
    B, S, D = x.shape
    Lc = ctx.shape[1]
    assert B + 1 <= N_COND and S % 512 == 0 and (B * Lc) % 512 == 0 and Lc % ROW_TM == 0
    c_rows = jnp.concatenate([c, c_ctx[None], jnp.zeros((N_COND - B - 1, D), c.dtype)], axis=0).astype(F32)
    mods = ada_modulation(c_rows, ada_w, ada_b).reshape(DEPTH, N_COND, N_MOD, D)
    x = x.reshape(B * S, D)
    ctx = ctx.reshape(B * Lc, D)
    router_w_t = router_w.astype(F32).T
    router_b = router_bias.astype(F32).reshape(N_EXPERTS, 1)
    for l in range(DEPTH):
        with_ctx = l < DEPTH - 1
        p = {
            'w_in': w_in[l], 'na_q_norm': na_q_norm[l], 'na_k_norm': na_k_norm[l], 'na_rpb': na_rpb[l],
            'gla_gate_w2': gla_gate_w2[l], 'gla_gate_b': gla_gate_b[l], 'gla_norm_g': gla_norm_g[l],
            'rw_mu': rw_mu[l], 'rw_w0': rw_w0[l], 'rw_w2': rw_w2[l], 'rw_a0': rw_a0[l], 'rw_a2': rw_a2[l],
            'rw_g2': rw_g2[l], 'rw_k_k': rw_k_k[l], 'rw_k_a': rw_k_a[l], 'rw_r_k': rw_r_k[l],
            'rw_gn_g': rw_gn_g[l], 'rw_gn_b': rw_gn_b[l], 'w_branch': w_branch[l], 'w_out': w_out[l],
            'norm1_g': norm1_g[l].astype(F32)[None], 'norm2_g': norm2_g[l].astype(F32)[None],
            'router_w_t': router_w_t, 'router_bias': router_b,
            'moe_w_gate': moe_w_gate[l].astype(BF16), 'moe_w_up': moe_w_up[l].astype(BF16),
            'moe_w_down': moe_w_down[l].astype(BF16),
        }
        x, ctx = token_mixers(x, ctx, mods[l], p, B, S, Lc, with_ctx)
        x = moe_sublayer(x, mods[l], p, rows_per_cond=S)
        if with_ctx:
            ctx = moe_sublayer(ctx, mods[l], p, fixed_row=B)
    return x.reshape(B, S, D)
```
